```python
import math
import jax, jax.numpy as jnp
from jax import lax
import numpy as np

D_MODEL = 1024
BATCH = 4
SEQ = 4096
DEPTH = 2
DEC_BATCH = 16
DEC_SEQ = 16
PAST_LEN = 4096

CHUNK = 64
N_A = DEPTH // 2
N_B = DEPTH - N_A
N_MEM = 256
DA_HEADS = 4
DA_QK = 64
DA_V = 2 * DA_QK
DA_W = DA_HEADS * DA_V
SW_HEADS = 8
SW_KV = 2
SW_GROUP = SW_HEADS // SW_KV
SW_HD = 64
SW_W = SW_HEADS * SW_HD
WINDOW = 128
MEM_HEADS = 4
MEM_HD = 128
MEM_W = MEM_HEADS * MEM_HD
MIX_W = DA_W + MEM_W
A_SIZES = (DA_HEADS * 2 * DA_QK, DA_HEADS * 2 * DA_QK, DA_W, DA_W, MEM_W, MEM_W)
B_SIZES = (SW_W, SW_W, MEM_W, MEM_W)
ROPE_THETA = 500000.0
ROT_FRAC = 4
Q_BLOCK = 128
DN_ALPHA = (2 * DEPTH) ** 0.25
DN_BETA = (8 * DEPTH) ** -0.25
LN_EPS = 1e-5
NEG = -1e30

kernel_name = "yoco_diffattn_swa_sink_stream_step"


def _split(h, sizes):
    return jnp.split(h, [int(c) for c in np.cumsum(sizes)[:-1]], axis=-1)


def rope_tables(pos):
    half = SW_HD // ROT_FRAC // 2
    inv = ROPE_THETA ** (-jnp.arange(half, dtype=jnp.float32) / half)
    ang = pos.astype(jnp.float32)[:, None] * inv[None, :]
    return jnp.cos(ang), jnp.sin(ang)


def apply_partial_rope(x, cos, sin):
    r = x.shape[-1] // ROT_FRAC
    half = r // 2
    shape = (1, cos.shape[0]) + (1,) * (x.ndim - 3) + (half,)
    c = cos.reshape(shape).astype(x.dtype)
    s = sin.reshape(shape).astype(x.dtype)
    x1, x2, xp = x[..., :half], x[..., half:r], x[..., r:]
    return jnp.concatenate([x1 * c - x2 * s, x2 * c + x1 * s, xp], axis=-1)


def layer_norm(x, g, b):
    xf = x.astype(jnp.float32)
    mu = jnp.mean(xf, -1, keepdims=True)
    var = jnp.mean(jnp.square(xf - mu), -1, keepdims=True)
    return ((xf - mu) * lax.rsqrt(var + LN_EPS) * g.astype(jnp.float32) + b.astype(jnp.float32)).astype(x.dtype)


def head_rms(o, g):
    of = o.astype(jnp.float32)
    return (of * lax.rsqrt(jnp.mean(of * of, -1, keepdims=True) + LN_EPS) * g.astype(jnp.float32)).astype(o.dtype)


def mem_kv(mem, w):
    B = mem.shape[0]
    k, v = jnp.split(mem @ w, 2, axis=-1)
    return k.reshape(B, N_MEM, MEM_HEADS, MEM_HD), v.reshape(B, N_MEM, MEM_HEADS, MEM_HD)


def mem_attend(q, mk, mv):
    s = jnp.einsum('bqhd,bkhd->bhqk', q, mk).astype(jnp.float32) * (MEM_HD ** -0.5)
    p = jax.nn.softmax(s, axis=-1)
    o = jnp.einsum('bhqk,bkhd->bqhd', p.astype(mv.dtype), mv)
    return o.reshape(q.shape[0], q.shape[1], MEM_W)


def diff_attend(q, k, v, mask, lam):
    s = jnp.einsum('bqhcd,bkhcd->bhcqk', q, k).astype(jnp.float32) * (DA_QK ** -0.5)
    if mask is not None:
        s = jnp.where(mask, s, NEG)
    p = jax.nn.softmax(s, axis=-1)
    a = p[:, :, 0] - lam * p[:, :, 1]
    return jnp.einsum('bhqk,bkhd->bqhd', a.astype(v.dtype), v)


def diff_prompt(q, k, v, lam):
    B, S = q.shape[:2]
    nb = S // Q_BLOCK
    qb = q.reshape((B, nb, Q_BLOCK) + q.shape[2:]).swapaxes(0, 1)
    kchunk = jnp.arange(S) // CHUNK

    def one(args):
        qi, i = args
        qchunk = (i * Q_BLOCK + jnp.arange(Q_BLOCK)) // CHUNK
        return diff_attend(qi, k, v, kchunk[None, :] <= qchunk[:, None], lam)

    out = lax.map(one, (qb, jnp.arange(nb)))
    return out.swapaxes(0, 1).reshape(B, S, DA_HEADS, DA_V)


def sink_softmax(s, sinks):
    sk = sinks.astype(jnp.float32).reshape(SW_KV, SW_GROUP)[:, :, None, None]
    sk = jnp.broadcast_to(sk, s.shape[:-1] + (1,))
    return jax.nn.softmax(jnp.concatenate([s, sk], axis=-1), axis=-1)[..., :-1]


def swa_prompt(q, k, v, sinks):
    B, S = q.shape[:2]
    nc = S // CHUNK
    nw = WINDOW // CHUNK
    qc = q.reshape(B, nc, CHUNK, SW_KV, SW_GROUP, SW_HD)
    pad = ((0, 0), (WINDOW, 0), (0, 0), (0, 0))
    kc = jnp.pad(k, pad).reshape(B, nc + nw, CHUNK, SW_KV, SW_HD)
    vc = jnp.pad(v, pad).reshape(B, nc + nw, CHUNK, SW_KV, SW_HD)
    kb = jnp.concatenate([kc[:, j:j + nc] for j in range(nw + 1)], axis=2)
    vb = jnp.concatenate([vc[:, j:j + nc] for j in range(nw + 1)], axis=2)
    s = jnp.einsum('bcqkgd,bcskd->bckgqs', qc, kb).astype(jnp.float32) * (SW_HD ** -0.5)
    blk = jnp.repeat(jnp.arange(nw + 1), CHUNK)[None, :]
    valid = (jnp.arange(nc)[:, None] + blk) >= nw
    s = jnp.where(valid[None, :, None, None, None, :], s, NEG)
    p = sink_softmax(s, sinks)
    o = jnp.einsum('bckgqs,bcskd->bcqkgd', p.astype(v.dtype), vb)
    return o.reshape(B, S, SW_W)


def swa_sample(q, k, v, sinks):
    B, T = q.shape[:2]
    s = jnp.einsum('bqkgd,bskd->bkgqs', q, k).astype(jnp.float32) * (SW_HD ** -0.5)
    p = sink_softmax(s, sinks)
    o = jnp.einsum('bkgqs,bskd->bqkgd', p.astype(v.dtype), v)
    return o.reshape(B, T, SW_W)


def a_project(x, w_in, cos, sin):
    B, T, _ = x.shape
    q, k, v, gd, mq, gm = _split(x @ w_in, A_SIZES)
    q = apply_partial_rope(q.reshape(B, T, DA_HEADS, 2, DA_QK), cos, sin)
    k = apply_partial_rope(k.reshape(B, T, DA_HEADS, 2, DA_QK), cos, sin)
    return q, k, v.reshape(B, T, DA_HEADS, DA_V), gd, mq.reshape(B, T, MEM_HEADS, MEM_HD), gm


def b_project(x, w_in, cos, sin):
    B, T, _ = x.shape
    q, gs, mq, gm = _split(x @ w_in, B_SIZES)
    q = apply_partial_rope(q.reshape(B, T, SW_KV, SW_GROUP, SW_HD), cos, sin)
    return q, gs, mq.reshape(B, T, MEM_HEADS, MEM_HD), gm


def shared_kv(h, w_kv, cos, sin):
    B, T, _ = h.shape
    k, v = jnp.split(h @ w_kv, 2, axis=-1)
    k = apply_partial_rope(k.reshape(B, T, SW_KV, SW_HD), cos, sin)
    return k, v.reshape(B, T, SW_KV, SW_HD)


def diff_finish(od, g, lam_init):
    B, T = od.shape[:2]
    return (head_rms(od, g) * (1.0 - lam_init)).reshape(B, T, DA_W)


def merge(x, o1, g1, o2, g2, w_out, ln_g, ln_b):
    mix = jnp.concatenate([o1 * jax.nn.silu(g1), o2 * jax.nn.silu(g2)], axis=-1)
    return layer_norm(DN_ALPHA * x + mix @ w_out, ln_g, ln_b)


def setup_inputs(seed: int = 0) -> dict:
    key = jax.random.key(seed)
    ks = jax.random.split(key, 24)

    def nrm(k, shape, scale=1.0):
        return jax.random.normal(k, shape, jnp.float32) * scale

    wr = min(WINDOW, PAST_LEN)
    return {
        "x_prompt": nrm(ks[0], (BATCH, SEQ, D_MODEL)),
        "x_sample": nrm(ks[1], (DEC_BATCH, DEC_SEQ, D_MODEL)),
        "mem_prompt": nrm(ks[2], (BATCH, N_MEM, D_MODEL)),
        "cache_diff_k": nrm(ks[3], (N_A, DEC_BATCH, PAST_LEN, DA_HEADS, 2, DA_QK)),
        "cache_diff_v": nrm(ks[4], (N_A, DEC_BATCH, PAST_LEN, DA_HEADS, DA_V)),
        "cache_swa_k": nrm(ks[5], (DEC_BATCH, wr, SW_KV, SW_HD)),
        "cache_swa_v": nrm(ks[6], (DEC_BATCH, wr, SW_KV, SW_HD)),
        "cache_mem_k": nrm(ks[7], (DEPTH, DEC_BATCH, N_MEM, MEM_HEADS, MEM_HD)),
        "cache_mem_v": nrm(ks[8], (DEPTH, DEC_BATCH, N_MEM, MEM_HEADS, MEM_HD)),
        "w_in_a": nrm(ks[9], (N_A, D_MODEL, sum(A_SIZES)), D_MODEL ** -0.5),
        "lam_q1": nrm(ks[10], (N_A, DA_QK), 0.1),
        "lam_k1": nrm(ks[11], (N_A, DA_QK), 0.1),
        "lam_q2": nrm(ks[12], (N_A, DA_QK), 0.1),
        "lam_k2": nrm(ks[13], (N_A, DA_QK), 0.1),
        "diff_norm_g": 1.0 + nrm(ks[14], (N_A, DA_V), 0.02),
        "w_in_b": nrm(ks[15], (N_B, D_MODEL, sum(B_SIZES)), D_MODEL ** -0.5),
        "sinks": nrm(ks[16], (N_B, SW_HEADS), 0.5),
        "w_kv_shared": nrm(ks[17], (D_MODEL, 2 * SW_KV * SW_HD), D_MODEL ** -0.5),
        "w_mem_kv": nrm(ks[18], (DEPTH, D_MODEL, 2 * MEM_W), D_MODEL ** -0.5),
        "w_out": nrm(ks[19], (DEPTH, MIX_W, D_MODEL), MIX_W ** -0.5 * DN_BETA),
        "ln_g": 1.0 + nrm(ks[20], (DEPTH, D_MODEL), 0.02),
        "ln_b": nrm(ks[21], (DEPTH, D_MODEL), 0.02),
    }


def reference(x_prompt, x_sample, mem_prompt, cache_diff_k, cache_diff_v, cache_swa_k, cache_swa_v,
              cache_mem_k, cache_mem_v, w_in_a, lam_q1, lam_k1, lam_q2, lam_k2, diff_norm_g,
              w_in_b, sinks, w_kv_shared, w_mem_kv, w_out, ln_g, ln_b):
    S = x_prompt.shape[1]
    T = x_sample.shape[1]
    P = cache_diff_k.shape[2]
    cos_p, sin_p = rope_tables(jnp.arange(S))
    cos_s, sin_s = rope_tables(P + jnp.arange(T))
    xp, xs = x_prompt, x_sample
    dkp, dvp, dks, dvs, mkp, mvp = [], [], [], [], [], []
    for l in range(DEPTH):
        mk_p, mv_p = mem_kv(mem_prompt, w_mem_kv[l])
        mkp.append(mk_p)
        mvp.append(mv_p)
        mk_s, mv_s = cache_mem_k[l], cache_mem_v[l]
        if l < N_A:
            lam_init = 0.8 - 0.6 * math.exp(-0.3 * l)
            lam = (jnp.exp(jnp.sum(lam_q1[l].astype(jnp.float32) * lam_k1[l].astype(jnp.float32)))
                   - jnp.exp(jnp.sum(lam_q2[l].astype(jnp.float32) * lam_k2[l].astype(jnp.float32)))
                   + lam_init)
            q, k, v, gd, mq, gm = a_project(xp, w_in_a[l], cos_p, sin_p)
            od = diff_finish(diff_prompt(q, k, v, lam), diff_norm_g[l], lam_init)
            om = mem_attend(mq, mk_p, mv_p)
            xp = merge(xp, od, gd, om, gm, w_out[l], ln_g[l], ln_b[l])
            dkp.append(k)
            dvp.append(v)
            q, k, v, gd, mq, gm = a_project(xs, w_in_a[l], cos_s, sin_s)
            kk = jnp.concatenate([cache_diff_k[l].astype(k.dtype), k], axis=1)
            vv = jnp.concatenate([cache_diff_v[l].astype(v.dtype), v], axis=1)
            od = diff_finish(diff_attend(q, kk, vv, None, lam), diff_norm_g[l], lam_init)
            om = mem_attend(mq, mk_s, mv_s)
            xs = merge(xs, od, gd, om, gm, w_out[l], ln_g[l], ln_b[l])
            dks.append(k)
            dvs.append(v)
        else:
            if l == N_A:
                skp, svp = shared_kv(xp, w_kv_shared, cos_p, sin_p)
                skn, svn = shared_kv(xs, w_kv_shared, cos_s, sin_s)
                sks = jnp.concatenate([cache_swa_k.astype(skn.dtype), skn], axis=1)
                svs = jnp.concatenate([cache_swa_v.astype(svn.dtype), svn], axis=1)
                wr_p = min(WINDOW, S)
                wr_s = cache_swa_k.shape[1]
                swa_kp, swa_vp = skp[:, S - wr_p:], svp[:, S - wr_p:]
                swa_ks, swa_vs = sks[:, sks.shape[1] - wr_s:], svs[:, svs.shape[1] - wr_s:]
            ib = l - N_A
            q, gs, mq, gm = b_project(xp, w_in_b[ib], cos_p, sin_p)
            os_ = swa_prompt(q, skp, svp, sinks[ib])
            om = mem_attend(mq, mk_p, mv_p)
            xp = merge(xp, os_, gs, om, gm, w_out[l], ln_g[l], ln_b[l])
            q, gs, mq, gm = b_project(xs, w_in_b[ib], cos_s, sin_s)
            os_ = swa_sample(q, sks, svs, sinks[ib])
            om = mem_attend(mq, mk_s, mv_s)
            xs = merge(xs, os_, gs, om, gm, w_out[l], ln_g[l], ln_b[l])
    diff_k_prompt = jnp.stack(dkp)
    diff_v_prompt = jnp.stack(dvp)
    diff_k_sample = jnp.stack(dks)
    diff_v_sample = jnp.stack(dvs)
    mem_k_prompt = jnp.stack(mkp)
    mem_v_prompt = jnp.stack(mvp)
    return (xp, xs, diff_k_prompt, diff_v_prompt, diff_k_sample, diff_v_sample,
            swa_kp, swa_vp, swa_ks, swa_vs, mem_k_prompt, mem_v_prompt)
```

```python
import functools
import math

import jax
import jax.numpy as jnp
from jax import lax
from jax.experimental import pallas as pl
from jax.experimental.pallas import tpu as pltpu

D_MODEL = 1024
CHUNK = 64
N_MEM = 256
DA_HEADS = 4
DA_QK = 64
DA_V = 128
DA_W = 512
SW_HEADS = 8
SW_KV = 2
SW_GROUP = 4
SW_HD = 64
SW_W = 512
WINDOW = 128
MEM_HEADS = 4
MEM_HD = 128
MEM_W = 512
MIX_W = 1024
ROPE_THETA = 500000.0
ROPE_HALF = 8
DEPTH = 2
DN_ALPHA = (2 * DEPTH) ** 0.25
LN_EPS = 1e-5
NEG = -1e30
LANES = 128

F32 = jnp.float32
BF16 = jnp.bfloat16
VMEM_LIMIT = 48 * 1024 * 1024

_NT = (((1,), (1,)), ((), ()))


def _params(*sem):
    return pltpu.CompilerParams(dimension_semantics=sem, vmem_limit_bytes=VMEM_LIMIT)


def _rope(x, c, a, b):
    outs = []
    for i in range(x.shape[1] // LANES):
        blk = x[:, i * LANES:(i + 1) * LANES]
        outs.append(blk * c + pltpu.roll(blk, LANES - ROPE_HALF, 1) * a + pltpu.roll(blk, ROPE_HALF, 1) * b)
    return outs[0] if len(outs) == 1 else jnp.concatenate(outs, axis=1)


def _silu(g):
    return g * (1.0 / (1.0 + jnp.exp(-g)))


def _layer_norm(z, g, b):
    mu = jnp.mean(z, axis=-1, keepdims=True)
    d = z - mu
    var = jnp.mean(d * d, axis=-1, keepdims=True)
    return d * lax.rsqrt(var + LN_EPS) * g + b


def _mem_attend(mq, mk, mv):
    outs = []
    for h in range(MEM_HEADS):
        sl = slice(h * MEM_HD, (h + 1) * MEM_HD)
        s = lax.dot_general(mq[:, sl], mk[:, sl], _NT, preferred_element_type=F32) * (MEM_HD ** -0.5)
        p = jnp.exp(s - jnp.max(s, axis=1, keepdims=True))
        l = jnp.sum(p, axis=1, keepdims=True)
        outs.append(jnp.dot(p.astype(BF16), mv[:, sl], preferred_element_type=F32) / l)
    return jnp.concatenate(outs, axis=1)


def _merge(x, mix, w_out, ln_g, ln_b):
    return _layer_norm(DN_ALPHA * x + jnp.dot(mix, w_out, preferred_element_type=F32), ln_g, ln_b)


def _lambda(lamv, lam_init):
    e1 = jnp.exp(jnp.sum(lamv[0:1, :] * lamv[1:2, :], axis=1, keepdims=True))
    e2 = jnp.exp(jnp.sum(lamv[2:3, :] * lamv[3:4, :], axis=1, keepdims=True))
    return e1 - e2 + lam_init


def _split_components(qh):
    lane = lax.broadcasted_iota(jnp.int32, qh.shape, 1)
    zero = jnp.zeros_like(qh)
    return jnp.concatenate([jnp.where(lane < DA_QK, qh, zero), jnp.where(lane >= DA_QK, qh, zero)], axis=0)


def _flash_update(qq, ks, vs, mask, m_ref, l_ref, acc_ref):
    s = lax.dot_general(qq, ks, _NT, preferred_element_type=F32)
    if mask is not None:
        s = jnp.where(mask, s, NEG)
    m_old = m_ref[...]
    m_new = jnp.maximum(m_old, jnp.max(s, axis=1, keepdims=True))
    alpha = jnp.exp(m_old - m_new)
    p = jnp.exp(s - m_new)
    l_ref[...] = alpha * l_ref[...] + jnp.sum(p, axis=1, keepdims=True)
    acc_ref[...] = alpha * acc_ref[...] + jnp.dot(p.astype(BF16), vs, preferred_element_type=F32)
    m_ref[...] = m_new


def _diff_finish(acc, l, lam, dng, lam_init):
    rows = acc.shape[0] // 2
    o = acc / l
    od = o[:rows] - lam * o[rows:]
    return od * lax.rsqrt(jnp.mean(od * od, axis=-1, keepdims=True) + LN_EPS) * dng * (1.0 - lam_init)


def _sink_attend(qq, ks, vs, sink, mask):
    s = lax.dot_general(qq, ks, _NT, preferred_element_type=F32)
    if mask is not None:
        s = jnp.where(mask, s, NEG)
    m = jnp.maximum(jnp.max(s, axis=1, keepdims=True), sink)
    p = jnp.exp(s - m)
    l = jnp.sum(p, axis=1, keepdims=True) + jnp.exp(sink - m)
    return jnp.dot(p.astype(BF16), vs, preferred_element_type=F32) / l


def _swa_heads(q, k, kr, v, vr, sink_of, mask):
    low = lax.broadcasted_iota(jnp.int32, (q.shape[0], LANES), 1) < SW_HD
    outs = []
    for pair in range(SW_HEADS // 2):
        kv = pair // 2
        slab = q[:, pair * LANES:(pair + 1) * LANES]
        zero = jnp.zeros_like(slab)
        k_even, k_odd = (k, kr) if kv == 0 else (kr, k)
        v_even, v_odd = (v, vr) if kv == 0 else (vr, v)
        o_even = _sink_attend(jnp.where(low, slab, zero), k_even, v_even, sink_of(2 * pair), mask)
        o_odd = _sink_attend(jnp.where(low, zero, slab), k_odd, v_odd, sink_of(2 * pair + 1), mask)
        outs.append(jnp.where(low, o_even, o_odd))
    return jnp.concatenate(outs, axis=1)


def _mem_kv_kernel(mem_ref, w_ref, kf_ref, vf_ref, kb_ref, vb_ref):
    h = jnp.dot(mem_ref[...], w_ref[0], preferred_element_type=F32)
    k, v = h[:, :MEM_W], h[:, MEM_W:]
    kf_ref[0] = k
    vf_ref[0] = v
    kb_ref[0] = k.astype(BF16)
    vb_ref[0] = v.astype(BF16)


def _mem_kv(mem_bf, w_bf):
    rows = mem_bf.shape[0]
    out = lambda dt: jax.ShapeDtypeStruct((DEPTH, rows, MEM_W), dt)
    ospec = pl.BlockSpec((1, rows, MEM_W), lambda l: (l, 0, 0))
    return pl.pallas_call(
        _mem_kv_kernel,
        grid=(DEPTH,),
        in_specs=[pl.BlockSpec((rows, D_MODEL), lambda l: (0, 0)),
                  pl.BlockSpec((1, D_MODEL, 2 * MEM_W), lambda l: (l, 0, 0))],
        out_specs=[ospec] * 4,
        out_shape=[out(F32), out(F32), out(BF16), out(BF16)],
        compiler_params=_params("arbitrary"),
        name="mem_kv",
    )(mem_bf, w_bf)


def _proj_a_kernel(fuse_mem, x_ref, w_ref, c_ref, a_ref, b_ref, *refs):
    if fuse_mem:
        mk_ref, mv_ref, q_ref, kf_ref, vf_ref, kb_ref, vb_ref, g_ref, om_ref = refs
    else:
        q_ref, kf_ref, vf_ref, kb_ref, vb_ref, g_ref, mq_ref, gm_ref = refs
    x = x_ref[...].astype(BF16)
    c, a, b = c_ref[...], a_ref[...], b_ref[...]

    def cols(i):
        return jnp.dot(x, w_ref[:, i * DA_W:(i + 1) * DA_W], preferred_element_type=F32)

    q_ref[...] = (_rope(cols(0), c, a, b) * (DA_QK ** -0.5)).astype(BF16)
    k = _rope(cols(1), c, a, b)
    kf_ref[...] = k
    kb_ref[...] = k.astype(BF16)
    v = cols(2)
    vf_ref[...] = v
    vb_ref[...] = v.astype(BF16)
    g_ref[...] = _silu(cols(3)).astype(BF16)
    mq = cols(4).astype(BF16)
    gm = _silu(cols(5))
    if fuse_mem:
        om_ref[...] = (_mem_attend(mq, mk_ref[0], mv_ref[0]) * gm).astype(BF16)
    else:
        mq_ref[...] = mq
        gm_ref[...] = gm.astype(BF16)


def _proj_b_kernel(fuse_mem, x_ref, w_ref, c_ref, a_ref, b_ref, *refs):
    if fuse_mem:
        mk_ref, mv_ref, q_ref, g_ref, om_ref, kf_ref, vf_ref, kb_ref, kr_ref, vb_ref, vr_ref = refs
    else:
        q_ref, g_ref, mq_ref, gm_ref, kf_ref, vf_ref, kb_ref, kr_ref, vb_ref, vr_ref = refs
    x = x_ref[...].astype(BF16)
    c, a, b = c_ref[...], a_ref[...], b_ref[...]

    def cols(i):
        return jnp.dot(x, w_ref[:, i * SW_W:(i + 1) * SW_W], preferred_element_type=F32)

    q_ref[...] = (_rope(cols(0), c, a, b) * (SW_HD ** -0.5)).astype(BF16)
    g_ref[...] = _silu(cols(1)).astype(BF16)
    mq = cols(2).astype(BF16)
    gm = _silu(cols(3))
    if fuse_mem:
        om_ref[...] = (_mem_attend(mq, mk_ref[0], mv_ref[0]) * gm).astype(BF16)
    else:
        mq_ref[...] = mq
        gm_ref[...] = gm.astype(BF16)
    kv = jnp.dot(x, w_ref[:, 4 * SW_W:], preferred_element_type=F32)
    k = _rope(kv[:, :LANES], c, a, b)
    v = kv[:, LANES:]
    kf_ref[...] = k
    vf_ref[...] = v
    kb_ref[...] = k.astype(BF16)
    kr_ref[...] = pltpu.roll(k, SW_HD, 1).astype(BF16)
    vb_ref[...] = v.astype(BF16)
    vr_ref[...] = pltpu.roll(v, SW_HD, 1).astype(BF16)


def _project(layer, x, w_bf, tabs, mem=None, *, tm, rows_per_batch=None):
    M = x.shape[0]
    fuse = mem is not None
    row = lambda w: pl.BlockSpec((tm, w), lambda i: (i, 0))
    in_specs = [row(D_MODEL), pl.BlockSpec(w_bf.shape, lambda i: (0, 0)), row(LANES), row(LANES), row(LANES)]
    args = [x, w_bf, *tabs]
    if fuse:
        per = rows_per_batch // tm
        mspec = pl.BlockSpec((1, N_MEM, MEM_W), lambda i: (i // per, 0, 0))
        in_specs += [mspec, mspec]
        args += list(mem)
    o = lambda w, dt: (row(w), jax.ShapeDtypeStruct((M, w), dt))
    if layer == "a":
        outs = [o(DA_W, BF16), o(DA_W, F32), o(DA_W, F32), o(DA_W, BF16), o(DA_W, BF16), o(DA_W, BF16)]
        body = _proj_a_kernel
    else:
        outs = [o(SW_W, BF16), o(SW_W, BF16)]
        body = _proj_b_kernel
    mem_outs = [o(MEM_W, BF16)] if fuse else [o(MEM_W, BF16), o(MEM_W, BF16)]
    if layer == "a":
        outs = outs + mem_outs
    else:
        outs = outs + mem_outs + [o(LANES, F32), o(LANES, F32)] + [o(LANES, BF16)] * 4
    return pl.pallas_call(
        functools.partial(body, fuse),
        grid=(M // tm,),
        in_specs=in_specs,
        out_specs=[s for s, _ in outs],
        out_shape=[t for _, t in outs],
        compiler_params=_params("arbitrary"),
        name="proj_" + layer,
    )(*args)


def _diff_prompt_kernel(lam_init, tq, lamv_ref, q_ref, k_ref, v_ref, g_ref, om_ref, x_ref, wout_ref,
                        lng_ref, lnb_ref, dng_ref, y_ref, m_ref, l_ref, acc_ref, mix_ref):
    i = pl.program_id(1)
    lam = _lambda(lamv_ref[...], lam_init)
    dng = dng_ref[...]
    qc = lax.broadcasted_iota(jnp.int32, (2 * tq, tq), 0) % tq // CHUNK
    kc = lax.broadcasted_iota(jnp.int32, (2 * tq, tq), 1) // CHUNK
    diag_mask = kc <= qc
    for h in range(DA_HEADS):
        hs = slice(h * DA_V, (h + 1) * DA_V)
        qq = _split_components(q_ref[:, hs])
        m_ref[...] = jnp.full(m_ref.shape, -jnp.inf, F32)
        l_ref[...] = jnp.zeros(l_ref.shape, F32)
        acc_ref[...] = jnp.zeros(acc_ref.shape, F32)

        def body(j, carry):
            rows = pl.ds(pl.multiple_of(j * tq, tq), tq)
            _flash_update(qq, k_ref[0, rows, hs], v_ref[0, rows, hs], None, m_ref, l_ref, acc_ref)
            return carry

        lax.fori_loop(0, i, body, 0)
        rows = pl.ds(pl.multiple_of(i * tq, tq), tq)
        _flash_update(qq, k_ref[0, rows, hs], v_ref[0, rows, hs], diag_mask, m_ref, l_ref, acc_ref)
        od = _diff_finish(acc_ref[...], l_ref[...], lam, dng, lam_init)
        mix_ref[:, hs] = (od * g_ref[:, hs].astype(F32)).astype(BF16)
    mix_ref[:, DA_W:] = om_ref[...]
    y_ref[...] = _merge(x_ref[...], mix_ref[...], wout_ref[...], lng_ref[...], lnb_ref[...])


def _diff_prompt(lam_init, lamv, q, k, v, g, om, x, w_out, ln_g, ln_b, dng, *, B, S, tq):
    nq = S // tq
    row = lambda w: pl.BlockSpec((tq, w), lambda b, i: (b * nq + i, 0))
    full = lambda shape: pl.BlockSpec(shape, lambda b, i: (0,) * len(shape))
    seq = pl.BlockSpec((1, S, DA_W), lambda b, i: (b, 0, 0))
    return pl.pallas_call(
        functools.partial(_diff_prompt_kernel, lam_init, tq),
        grid=(B, nq),
        in_specs=[full((4, DA_QK)), row(DA_W), seq, seq, row(DA_W), row(MEM_W), row(D_MODEL),
                  full((MIX_W, D_MODEL)), full((1, D_MODEL)), full((1, D_MODEL)), full((1, DA_V))],
        out_specs=row(D_MODEL),
        out_shape=jax.ShapeDtypeStruct((B * S, D_MODEL), F32),
        scratch_shapes=[pltpu.VMEM((2 * tq, 1), F32), pltpu.VMEM((2 * tq, 1), F32),
                        pltpu.VMEM((2 * tq, DA_V), F32), pltpu.VMEM((tq, MIX_W), BF16)],
        compiler_params=_params("arbitrary", "arbitrary"),
        name="diff_prompt",
    )(lamv, q, k.reshape(B, S, DA_W), v.reshape(B, S, DA_W), g, om, x, w_out, ln_g, ln_b, dng)


def _swa_prompt_kernel(tq, sinks_ref, q_ref, k_ref, kr_ref, v_ref, vr_ref, g_ref, om_ref, x_ref, wout_ref,
                       lng_ref, lnb_ref, y_ref):
    i = pl.program_id(1)
    nk = tq + WINDOW
    start = pl.multiple_of(jnp.maximum(i * tq - WINDOW, 0), CHUNK)
    rows = pl.ds(start, nk)
    shift = jnp.where(i == 0, 0, WINDOW // CHUNK)
    kc = lax.broadcasted_iota(jnp.int32, (tq, nk), 1) // CHUNK - shift
    qc = lax.broadcasted_iota(jnp.int32, (tq, nk), 0) // CHUNK
    mask = (kc <= qc) & (kc >= qc - WINDOW // CHUNK)
    o = _swa_heads(q_ref[...], k_ref[0, rows, :], kr_ref[0, rows, :], v_ref[0, rows, :], vr_ref[0, rows, :],
                   lambda h: sinks_ref[h], mask)
    mix = jnp.concatenate([(o * g_ref[...].astype(F32)).astype(BF16), om_ref[...]], axis=1)
    y_ref[...] = _merge(x_ref[...], mix, wout_ref[...], lng_ref[...], lnb_ref[...])


def _swa_prompt(sinks, q, k, kr, v, vr, g, om, x, w_out, ln_g, ln_b, *, B, S, tq):
    nq = S // tq
    row = lambda w: pl.BlockSpec((tq, w), lambda b, i: (b * nq + i, 0))
    full = lambda shape: pl.BlockSpec(shape, lambda b, i: (0,) * len(shape))
    seq = pl.BlockSpec((1, S, LANES), lambda b, i: (b, 0, 0))
    r3 = lambda t: t.reshape(B, S, LANES)
    return pl.pallas_call(
        functools.partial(_swa_prompt_kernel, tq),
        grid=(B, nq),
        in_specs=[pl.BlockSpec(memory_space=pltpu.SMEM), row(SW_W), seq, seq, seq, seq, row(SW_W), row(MEM_W),
                  row(D_MODEL), full((MIX_W, D_MODEL)), full((1, D_MODEL)), full((1, D_MODEL))],
        out_specs=row(D_MODEL),
        out_shape=jax.ShapeDtypeStruct((B * S, D_MODEL), F32),
        compiler_params=_params("arbitrary", "arbitrary"),
        name="swa_prompt",
    )(sinks, q, r3(k), r3(kr), r3(v), r3(vr), g, om, x, w_out, ln_g, ln_b)


def _sample_tail(b, last, o_first, g_ref, mq_ref, gm_ref, cmk_ref, cmv_ref, x_ref, wout_ref, lng_ref, lnb_ref,
                 y_ref, mix_ref, T):
    om = _mem_attend(mq_ref[...], cmk_ref[0].astype(BF16), cmv_ref[0].astype(BF16)) * gm_ref[...].astype(F32)
    mix = jnp.concatenate([(o_first * g_ref[...].astype(F32)).astype(BF16), om.astype(BF16)], axis=1)
    mix_ref[pl.ds(pl.multiple_of(b * T, T), T), :] = mix

    @pl.when(last)
    def _():
        y_ref[...] = _merge(x_ref[...], mix_ref[...], wout_ref[...], lng_ref[...], lnb_ref[...])


def _diff_sample_kernel(lam_init, T, lamv_ref, q_ref, kn_ref, vn_ref, ck_ref, cv_ref, g_ref, mq_ref, gm_ref,
                        cmk_ref, cmv_ref, x_ref, wout_ref, lng_ref, lnb_ref, dng_ref, y_ref,
                        m_ref, l_ref, acc_ref, mix_ref):
    b, j = pl.program_id(0), pl.program_id(1)
    nb, nj = pl.num_programs(0), pl.num_programs(1)
    qqs = [_split_components(q_ref[:, h * DA_V:(h + 1) * DA_V]) for h in range(DA_HEADS)]

    @pl.when(j == 0)
    def _():
        m_ref[...] = jnp.full(m_ref.shape, -jnp.inf, F32)
        l_ref[...] = jnp.zeros(l_ref.shape, F32)
        acc_ref[...] = jnp.zeros(acc_ref.shape, F32)
        for h in range(DA_HEADS):
            hs = slice(h * DA_V, (h + 1) * DA_V)
            _flash_update(qqs[h], kn_ref[:, hs], vn_ref[:, hs], None, m_ref.at[h], l_ref.at[h], acc_ref.at[h])

    for h in range(DA_HEADS):
        hs = slice(h * DA_V, (h + 1) * DA_V)
        _flash_update(qqs[h], ck_ref[0, :, hs].astype(BF16), cv_ref[0, :, hs].astype(BF16), None,
                      m_ref.at[h], l_ref.at[h], acc_ref.at[h])

    @pl.when(j == nj - 1)
    def _():
        lam = _lambda(lamv_ref[...], lam_init)
        od = jnp.concatenate([_diff_finish(acc_ref[h], l_ref[h], lam, dng_ref[...], lam_init)
                              for h in range(DA_HEADS)], axis=1)
        _sample_tail(b, b == nb - 1, od, g_ref, mq_ref, gm_ref, cmk_ref, cmv_ref, x_ref, wout_ref, lng_ref,
                     lnb_ref, y_ref, mix_ref, T)


def _diff_sample(lam_init, lamv, q, kn, vn, ck, cv, g, mq, gm, cmk, cmv, x, w_out, ln_g, ln_b, dng, *, NB, T, tk):
    P = ck.shape[1]
    row = lambda w: pl.BlockSpec((T, w), lambda b, j: (b, 0))
    full = lambda shape: pl.BlockSpec(shape, lambda b, j: (0,) * len(shape))
    cache = pl.BlockSpec((1, tk, DA_W), lambda b, j: (b, j, 0))
    memc = pl.BlockSpec((1, N_MEM, MEM_W), lambda b, j: (b, 0, 0))
    return pl.pallas_call(
        functools.partial(_diff_sample_kernel, lam_init, T),
        grid=(NB, P // tk),
        in_specs=[full((4, DA_QK)), row(DA_W), row(DA_W), row(DA_W), cache, cache, row(DA_W), row(MEM_W),
                  row(MEM_W), memc, memc, full((NB * T, D_MODEL)), full((MIX_W, D_MODEL)), full((1, D_MODEL)),
                  full((1, D_MODEL)), full((1, DA_V))],
        out_specs=full((NB * T, D_MODEL)),
        out_shape=jax.ShapeDtypeStruct((NB * T, D_MODEL), F32),
        scratch_shapes=[pltpu.VMEM((DA_HEADS, 2 * T, 1), F32), pltpu.VMEM((DA_HEADS, 2 * T, 1), F32),
                        pltpu.VMEM((DA_HEADS, 2 * T, DA_V), F32), pltpu.VMEM((NB * T, MIX_W), BF16)],
        compiler_params=_params("arbitrary", "arbitrary"),
        name="diff_sample",
    )(lamv, q, kn, vn, ck, cv, g, mq, gm, cmk, cmv, x, w_out, ln_g, ln_b, dng)


def _swa_sample_kernel(T, sinks_ref, q_ref, kn_ref, knr_ref, vn_ref, vnr_ref, ck_ref, cv_ref, g_ref, mq_ref, gm_ref,
                       cmk_ref, cmv_ref, x_ref, wout_ref, lng_ref, lnb_ref, y_ref, mix_ref):
    b = pl.program_id(0)
    ck, cv = ck_ref[0], cv_ref[0]
    cat = lambda c, n: jnp.concatenate([c.astype(BF16), n], axis=0)
    k = cat(ck, kn_ref[...])
    kr = cat(pltpu.roll(ck, SW_HD, 1), knr_ref[...])
    v = cat(cv, vn_ref[...])
    vr = cat(pltpu.roll(cv, SW_HD, 1), vnr_ref[...])
    o = _swa_heads(q_ref[...], k, kr, v, vr, lambda h: sinks_ref[h], None)
    _sample_tail(b, b == pl.num_programs(0) - 1, o, g_ref, mq_ref, gm_ref, cmk_ref, cmv_ref, x_ref, wout_ref,
                 lng_ref, lnb_ref, y_ref, mix_ref, T)


def _swa_sample(sinks, q, kn, knr, vn, vnr, ck, cv, g, mq, gm, cmk, cmv, x, w_out, ln_g, ln_b, *, NB, T):
    W = ck.shape[1]
    row = lambda w: pl.BlockSpec((T, w), lambda b: (b, 0))
    full = lambda shape: pl.BlockSpec(shape, lambda b: (0,) * len(shape))
    cache = pl.BlockSpec((1, W, LANES), lambda b: (b, 0, 0))
    memc = pl.BlockSpec((1, N_MEM, MEM_W), lambda b: (b, 0, 0))
    return pl.pallas_call(
        functools.partial(_swa_sample_kernel, T),
        grid=(NB,),
        in_specs=[pl.BlockSpec(memory_space=pltpu.SMEM), row(SW_W), row(LANES), row(LANES), row(LANES), row(LANES),
                  cache, cache, row(SW_W), row(MEM_W), row(MEM_W), memc, memc, full((NB * T, D_MODEL)),
                  full((MIX_W, D_MODEL)), full((1, D_MODEL)), full((1, D_MODEL))],
        out_specs=full((NB * T, D_MODEL)),
        out_shape=jax.ShapeDtypeStruct((NB * T, D_MODEL), F32),
        scratch_shapes=[pltpu.VMEM((NB * T, MIX_W), BF16)],
        compiler_params=_params("arbitrary"),
        name="swa_sample",
    )(sinks, q, kn, knr, vn, vnr, ck, cv, g, mq, gm, cmk, cmv, x, w_out, ln_g, ln_b)


def _rope_tables(pos):
    T = pos.shape[0]
    inv = ROPE_THETA ** (-jnp.arange(ROPE_HALF, dtype=F32) / ROPE_HALF)
    ang = pos.astype(F32)[:, None] * inv[None, :]
    cos, sin = jnp.cos(ang), jnp.sin(ang)
    rest = SW_HD - 2 * ROPE_HALF
    c = jnp.concatenate([cos, cos, jnp.ones((T, rest), F32)], axis=1)
    a = jnp.concatenate([-sin, jnp.zeros((T, SW_HD - ROPE_HALF), F32)], axis=1)
    b = jnp.concatenate([jnp.zeros((T, ROPE_HALF), F32), sin, jnp.zeros((T, rest), F32)], axis=1)
    return tuple(jnp.tile(t, (1, LANES // SW_HD)) for t in (c, a, b))


def kernel(x_prompt, x_sample, mem_prompt, cache_diff_k, cache_diff_v, cache_swa_k, cache_swa_v, cache_mem_k,
           cache_mem_v, w_in_a, lam_q1, lam_k1, lam_q2, lam_k2, diff_norm_g, w_in_b, sinks, w_kv_shared, w_mem_kv,
           w_out, ln_g, ln_b):
    B, S, _ = x_prompt.shape
    NB, T, _ = x_sample.shape
    P = cache_diff_k.shape[2]
    WR = cache_swa_k.shape[1]
    assert w_in_a.shape[0] == 1 and w_in_b.shape[0] == 1 and w_out.shape[0] == DEPTH

    tabs_p = _rope_tables(jnp.tile(jnp.arange(S), B))
    tabs_s = _rope_tables(jnp.tile(P + jnp.arange(T), NB))
    xp = x_prompt.reshape(B * S, D_MODEL)
    xs = x_sample.reshape(NB * T, D_MODEL)
    wa = w_in_a[0].astype(BF16)
    wb = jnp.concatenate([w_in_b[0], w_kv_shared], axis=1).astype(BF16)
    wo = w_out.astype(BF16)
    lng = ln_g.reshape(DEPTH, 1, D_MODEL)
    lnb = ln_b.reshape(DEPTH, 1, D_MODEL)
    lamv = jnp.concatenate([lam_q1, lam_k1, lam_q2, lam_k2], axis=0)
    dng = diff_norm_g.reshape(1, DA_V)
    lam_init = 0.8 - 0.6 * math.exp(-0.3 * 0)

    mkf, mvf, mkb, mvb = _mem_kv(mem_prompt.reshape(B * N_MEM, D_MODEL).astype(BF16), w_mem_kv.astype(BF16))
    mem_b = lambda l: (mkb[l].reshape(B, N_MEM, MEM_W), mvb[l].reshape(B, N_MEM, MEM_W))
    cmk = cache_mem_k.reshape(DEPTH, NB, N_MEM, MEM_W)
    cmv = cache_mem_v.reshape(DEPTH, NB, N_MEM, MEM_W)

    q, kf, vf, kb, vb, g, om = _project("a", xp, wa, tabs_p, mem_b(0), tm=512, rows_per_batch=S)
    xp1 = _diff_prompt(lam_init, lamv, q, kb, vb, g, om, xp, wo[0], lng[0], lnb[0], dng, B=B, S=S, tq=256)
    qs, kfs, vfs, kbs, vbs, gs_, mqs, gms = _project("a", xs, wa, tabs_s, tm=NB * T)
    xs1 = _diff_sample(lam_init, lamv, qs, kbs, vbs, cache_diff_k[0].reshape(NB, P, DA_W),
                       cache_diff_v[0].reshape(NB, P, DA_W), gs_, mqs, gms, cmk[0], cmv[0], xs, wo[0], lng[0],
                       lnb[0], dng, NB=NB, T=T, tk=1024)

    q, g, om, skf, svf, kb1, kr1, vb1, vr1 = _project("b", xp1, wb, tabs_p, mem_b(1), tm=512, rows_per_batch=S)
    yp = _swa_prompt(sinks[0], q, kb1, kr1, vb1, vr1, g, om, xp1, wo[1], lng[1], lnb[1], B=B, S=S, tq=256)
    qs, gs_, mqs, gms, skfs, svfs, kbs1, krs1, vbs1, vrs1 = _project("b", xs1, wb, tabs_s, tm=NB * T)
    ys = _swa_sample(sinks[0], qs, kbs1, krs1, vbs1, vrs1, cache_swa_k.reshape(NB, WR, LANES),
                     cache_swa_v.reshape(NB, WR, LANES), gs_, mqs, gms, cmk[1], cmv[1], xs1, wo[1], lng[1], lnb[1],
                     NB=NB, T=T)

    wr_p = min(WINDOW, S)
    swa_kp = skf.reshape(B, S, SW_KV, SW_HD)[:, S - wr_p:]
    swa_vp = svf.reshape(B, S, SW_KV, SW_HD)[:, S - wr_p:]
    roll = lambda c, n: jnp.concatenate([c, n.reshape(NB, T, SW_KV, SW_HD)], axis=1)[:, T:]
    return (yp.reshape(B, S, D_MODEL), ys.reshape(NB, T, D_MODEL),
            kf.reshape(1, B, S, DA_HEADS, 2, DA_QK), vf.reshape(1, B, S, DA_HEADS, DA_V),
            kfs.reshape(1, NB, T, DA_HEADS, 2, DA_QK), vfs.reshape(1, NB, T, DA_HEADS, DA_V),
            swa_kp, swa_vp, roll(cache_swa_k, skfs), roll(cache_swa_v, svfs),
            mkf.reshape(DEPTH, B, N_MEM, MEM_HEADS, MEM_HD), mvf.reshape(DEPTH, B, N_MEM, MEM_HEADS, MEM_HD))
```

```python
import functools
import math

import jax
import jax.numpy as jnp
from jax import lax
from jax.experimental import pallas as pl
from jax.experimental.pallas import tpu as pltpu

D_MODEL = 1024
CHUNK = 64
N_MEM = 256
DA_HEADS = 4
DA_QK = 64
DA_V = 128
DA_W = 512
SW_HEADS = 8
SW_KV = 2
SW_GROUP = 4
SW_HD = 64
SW_W = 512
WINDOW = 128
MEM_HEADS = 4
MEM_HD = 128
MEM_W = 512
MIX_W = 1024
ROPE_THETA = 500000.0
ROPE_HALF = 8
DEPTH = 2
DN_ALPHA = (2 * DEPTH) ** 0.25
LN_EPS = 1e-5
NEG = -1e30
LANES = 128
DIFF_TILE = 256

F32 = jnp.float32
BF16 = jnp.bfloat16
VMEM_LIMIT = 48 * 1024 * 1024

_NT = (((1,), (1,)), ((), ()))


def _params(*sem):
    return pltpu.CompilerParams(dimension_semantics=sem, vmem_limit_bytes=VMEM_LIMIT)


def _rope(x, c, a, b):
    outs = []
    for i in range(x.shape[1] // LANES):
        blk = x[:, i * LANES:(i + 1) * LANES]
        outs.append(blk * c + pltpu.roll(blk, LANES - ROPE_HALF, 1) * a + pltpu.roll(blk, ROPE_HALF, 1) * b)
    return outs[0] if len(outs) == 1 else jnp.concatenate(outs, axis=1)


def _silu(g):
    return g * (1.0 / (1.0 + jnp.exp(-g)))


def _layer_norm(z, g, b):
    mu = jnp.mean(z, axis=-1, keepdims=True)
    d = z - mu
    var = jnp.mean(d * d, axis=-1, keepdims=True)
    return d * lax.rsqrt(var + LN_EPS) * g + b


def _mem_attend(mq, mk, mv):
    outs = []
    for h in range(MEM_HEADS):
        sl = slice(h * MEM_HD, (h + 1) * MEM_HD)
        s = lax.dot_general(mq[:, sl], mk[:, sl], _NT, preferred_element_type=F32) * (MEM_HD ** -0.5)
        p = jnp.exp(s - jnp.max(s, axis=1, keepdims=True))
        l = jnp.sum(p, axis=1, keepdims=True)
        outs.append(jnp.dot(p.astype(BF16), mv[:, sl], preferred_element_type=F32) / l)
    return jnp.concatenate(outs, axis=1)


def _merge(x, mix, w_out, ln_g, ln_b):
    return _layer_norm(DN_ALPHA * x + jnp.dot(mix, w_out, preferred_element_type=F32), ln_g, ln_b)


def _lambda(lamv, lam_init):
    e1 = jnp.exp(jnp.sum(lamv[0:1, :] * lamv[1:2, :], axis=1, keepdims=True))
    e2 = jnp.exp(jnp.sum(lamv[2:3, :] * lamv[3:4, :], axis=1, keepdims=True))
    return e1 - e2 + lam_init


def _split_components(qh):
    lane = lax.broadcasted_iota(jnp.int32, qh.shape, 1)
    zero = jnp.zeros_like(qh)
    return jnp.concatenate([jnp.where(lane < DA_QK, qh, zero), jnp.where(lane >= DA_QK, qh, zero)], axis=0)


def _flash_update(qq, ks, vs, mask, m_ref, l_ref, acc_ref):
    s = lax.dot_general(qq, ks, _NT, preferred_element_type=F32)
    if mask is not None:
        s = jnp.where(mask, s, NEG)
    m_old = m_ref[...]
    m_new = jnp.maximum(m_old, jnp.max(s, axis=1, keepdims=True))
    alpha = jnp.exp(m_old - m_new)
    p = jnp.exp(s - m_new)
    l_ref[...] = alpha * l_ref[...] + jnp.sum(p, axis=1, keepdims=True)
    acc_ref[...] = alpha * acc_ref[...] + jnp.dot(p.astype(BF16), vs, preferred_element_type=F32)
    m_ref[...] = m_new


def _flash_update_t(qq, ks, vts, mask, m_ref, l_ref, acc_ref):
    s = lax.dot_general(ks, qq, _NT, preferred_element_type=F32)
    if mask is not None:
        s = jnp.where(mask, s, NEG)
    m_old = m_ref[...]
    m_new = jnp.maximum(m_old, jnp.max(s, axis=0, keepdims=True))
    alpha = jnp.exp(m_old - m_new)
    p = jnp.exp(s - m_new)
    l_ref[...] = alpha * l_ref[...] + jnp.sum(p, axis=0, keepdims=True)
    acc_ref[...] = alpha * acc_ref[...] + jnp.dot(vts, p.astype(BF16), preferred_element_type=F32)
    m_ref[...] = m_new


def _diff_finish(acc, l, lam, dng, lam_init):
    rows = acc.shape[0] // 2
    o = acc / l
    od = o[:rows] - lam * o[rows:]
    return od * lax.rsqrt(jnp.mean(od * od, axis=-1, keepdims=True) + LN_EPS) * dng * (1.0 - lam_init)


def _sink_attend(qq, ks, vs, sink, mask):
    s = lax.dot_general(qq, ks, _NT, preferred_element_type=F32)
    if mask is not None:
        s = jnp.where(mask, s, NEG)
    m = jnp.maximum(jnp.max(s, axis=1, keepdims=True), sink)
    p = jnp.exp(s - m)
    l = jnp.sum(p, axis=1, keepdims=True) + jnp.exp(sink - m)
    return jnp.dot(p.astype(BF16), vs, preferred_element_type=F32) / l


def _swa_heads(q, k, kr, v, vr, sink_of, mask):
    low = lax.broadcasted_iota(jnp.int32, (q.shape[0], LANES), 1) < SW_HD
    outs = []
    for pair in range(SW_HEADS // 2):
        kv = pair // 2
        slab = q[:, pair * LANES:(pair + 1) * LANES]
        zero = jnp.zeros_like(slab)
        k_even, k_odd = (k, kr) if kv == 0 else (kr, k)
        v_even, v_odd = (v, vr) if kv == 0 else (vr, v)
        o_even = _sink_attend(jnp.where(low, slab, zero), k_even, v_even, sink_of(2 * pair), mask)
        o_odd = _sink_attend(jnp.where(low, zero, slab), k_odd, v_odd, sink_of(2 * pair + 1), mask)
        outs.append(jnp.where(low, o_even, o_odd))
    return jnp.concatenate(outs, axis=1)


def _mem_kv_kernel(mem_ref, w_ref, kf_ref, vf_ref, kb_ref, vb_ref):
    h = jnp.dot(mem_ref[...], w_ref[0], preferred_element_type=F32)
    k, v = h[:, :MEM_W], h[:, MEM_W:]
    kf_ref[0] = k
    vf_ref[0] = v
    kb_ref[0] = k.astype(BF16)
    vb_ref[0] = v.astype(BF16)


def _mem_kv(mem_bf, w_bf):
    rows = mem_bf.shape[0]
    out = lambda dt: jax.ShapeDtypeStruct((DEPTH, rows, MEM_W), dt)
    ospec = pl.BlockSpec((1, rows, MEM_W), lambda l: (l, 0, 0))
    return pl.pallas_call(
        _mem_kv_kernel,
        grid=(DEPTH,),
        in_specs=[pl.BlockSpec((rows, D_MODEL), lambda l: (0, 0)),
                  pl.BlockSpec((1, D_MODEL, 2 * MEM_W), lambda l: (l, 0, 0))],
        out_specs=[ospec] * 4,
        out_shape=[out(F32), out(F32), out(BF16), out(BF16)],
        compiler_params=_params("arbitrary"),
        name="mem_kv",
    )(mem_bf, w_bf)


def _proj_a_kernel(fuse_mem, x_ref, w_ref, c_ref, a_ref, b_ref, *refs):
    if fuse_mem:
        mk_ref, mv_ref, q_ref, kf_ref, vf_ref, kb_ref, vb_ref, g_ref, om_ref = refs
    else:
        q_ref, kf_ref, vf_ref, kb_ref, vb_ref, g_ref, mq_ref, gm_ref = refs
    x = x_ref[...].astype(BF16)
    c, a, b = c_ref[...], a_ref[...], b_ref[...]

    def cols(i):
        return jnp.dot(x, w_ref[:, i * DA_W:(i + 1) * DA_W], preferred_element_type=F32)

    q_ref[...] = (_rope(cols(0), c, a, b) * (DA_QK ** -0.5)).astype(BF16)
    k = _rope(cols(1), c, a, b)
    kf_ref[...] = k
    kb_ref[...] = k.astype(BF16)
    v = cols(2)
    vf_ref[...] = v
    if fuse_mem:
        for blk in range(vb_ref.shape[0]):
            vb_ref[blk] = v[blk * DIFF_TILE:(blk + 1) * DIFF_TILE, :].T.astype(BF16)
    else:
        vb_ref[...] = v.astype(BF16)
    g_ref[...] = _silu(cols(3)).astype(BF16)
    mq = cols(4).astype(BF16)
    gm = _silu(cols(5))
    if fuse_mem:
        om_ref[...] = (_mem_attend(mq, mk_ref[0], mv_ref[0]) * gm).astype(BF16)
    else:
        mq_ref[...] = mq
        gm_ref[...] = gm.astype(BF16)


def _proj_b_kernel(fuse_mem, x_ref, w_ref, c_ref, a_ref, b_ref, *refs):
    if fuse_mem:
        mk_ref, mv_ref, q_ref, g_ref, om_ref, kf_ref, vf_ref, kb_ref, kr_ref, vb_ref, vr_ref = refs
    else:
        q_ref, g_ref, mq_ref, gm_ref, kf_ref, vf_ref, kb_ref, kr_ref, vb_ref, vr_ref = refs
    x = x_ref[...].astype(BF16)
    c, a, b = c_ref[...], a_ref[...], b_ref[...]

    def cols(i):
        return jnp.dot(x, w_ref[:, i * SW_W:(i + 1) * SW_W], preferred_element_type=F32)

    q_ref[...] = (_rope(cols(0), c, a, b) * (SW_HD ** -0.5)).astype(BF16)
    g_ref[...] = _silu(cols(1)).astype(BF16)
    mq = cols(2).astype(BF16)
    gm = _silu(cols(3))
    if fuse_mem:
        om_ref[...] = (_mem_attend(mq, mk_ref[0], mv_ref[0]) * gm).astype(BF16)
    else:
        mq_ref[...] = mq
        gm_ref[...] = gm.astype(BF16)
    kv = jnp.dot(x, w_ref[:, 4 * SW_W:], preferred_element_type=F32)
    k = _rope(kv[:, :LANES], c, a, b)
    v = kv[:, LANES:]
    kf_ref[...] = k
    vf_ref[...] = v
    kb_ref[...] = k.astype(BF16)
    kr_ref[...] = pltpu.roll(k, SW_HD, 1).astype(BF16)
    vb_ref[...] = v.astype(BF16)
    vr_ref[...] = pltpu.roll(v, SW_HD, 1).astype(BF16)


def _project(layer, x, w_bf, tabs, mem=None, *, tm, rows_per_batch=None):
    M = x.shape[0]
    fuse = mem is not None
    row = lambda w: pl.BlockSpec((tm, w), lambda i: (i, 0))
    in_specs = [row(D_MODEL), pl.BlockSpec(w_bf.shape, lambda i: (0, 0)), row(LANES), row(LANES), row(LANES)]
    args = [x, w_bf, *tabs]
    if fuse:
        per = rows_per_batch // tm
        mspec = pl.BlockSpec((1, N_MEM, MEM_W), lambda i: (i // per, 0, 0))
        in_specs += [mspec, mspec]
        args += list(mem)
    o = lambda w, dt: (row(w), jax.ShapeDtypeStruct((M, w), dt))
    if layer == "a":
        vt = (pl.BlockSpec((tm // DIFF_TILE, DA_W, DIFF_TILE), lambda i: (i, 0, 0)),
              jax.ShapeDtypeStruct((M // DIFF_TILE, DA_W, DIFF_TILE), BF16))
        outs = [o(DA_W, BF16), o(DA_W, F32), o(DA_W, F32), o(DA_W, BF16), vt if fuse else o(DA_W, BF16), o(DA_W, BF16)]
        body = _proj_a_kernel
    else:
        outs = [o(SW_W, BF16), o(SW_W, BF16)]
        body = _proj_b_kernel
    mem_outs = [o(MEM_W, BF16)] if fuse else [o(MEM_W, BF16), o(MEM_W, BF16)]
    if layer == "a":
        outs = outs + mem_outs
    else:
        outs = outs + mem_outs + [o(LANES, F32), o(LANES, F32)] + [o(LANES, BF16)] * 4
    return pl.pallas_call(
        functools.partial(body, fuse),
        grid=(M // tm,),
        in_specs=in_specs,
        out_specs=[s for s, _ in outs],
        out_shape=[t for _, t in outs],
        compiler_params=_params("arbitrary"),
        name="proj_" + layer,
    )(*args)


def _diff_prompt_kernel(lam_init, lamv_ref, q_ref, k_ref, vt_ref, g_ref, om_ref, x_ref, wout_ref,
                        lng_ref, lnb_ref, dng_ref, y_ref, m_ref, l_ref, acc_ref, mix_ref):
    t = DIFF_TILE
    i = pl.program_id(1)
    heads = [slice(h * DA_V, (h + 1) * DA_V) for h in range(DA_HEADS)]
    qqs = [_split_components(q_ref[:, hs]) for hs in heads]
    m_ref[...] = jnp.full(m_ref.shape, -jnp.inf, F32)
    l_ref[...] = jnp.zeros(l_ref.shape, F32)
    acc_ref[...] = jnp.zeros(acc_ref.shape, F32)

    def block(j, mask):
        rows = pl.ds(pl.multiple_of(j * t, t), t)
        for h, hs in enumerate(heads):
            _flash_update_t(qqs[h], k_ref[0, rows, hs], vt_ref[0, j, hs, :], mask, m_ref.at[h], l_ref.at[h],
                            acc_ref.at[h])

    def body(j, carry):
        block(j, None)
        return carry

    lax.fori_loop(0, i, body, 0)
    kc = lax.broadcasted_iota(jnp.int32, (t, 2 * t), 0) // CHUNK
    qc = lax.broadcasted_iota(jnp.int32, (t, 2 * t), 1) % t // CHUNK
    block(i, kc <= qc)

    lam = _lambda(lamv_ref[...], lam_init)
    for h, hs in enumerate(heads):
        o = acc_ref[h] / l_ref[h]
        od = (o[:, :t] - lam * o[:, t:]).T
        od = od * lax.rsqrt(jnp.mean(od * od, axis=-1, keepdims=True) + LN_EPS) * dng_ref[...] * (1.0 - lam_init)
        mix_ref[:, hs] = (od * g_ref[:, hs].astype(F32)).astype(BF16)
    mix_ref[:, DA_W:] = om_ref[...]
    y_ref[...] = _merge(x_ref[...], mix_ref[...], wout_ref[...], lng_ref[...], lnb_ref[...])


def _diff_prompt(lam_init, lamv, q, k, vt, g, om, x, w_out, ln_g, ln_b, dng, *, B, S):
    t = DIFF_TILE
    nq = S // t
    row = lambda w: pl.BlockSpec((t, w), lambda b, i: (b * nq + i, 0))
    full = lambda shape: pl.BlockSpec(shape, lambda b, i: (0,) * len(shape))
    return pl.pallas_call(
        functools.partial(_diff_prompt_kernel, lam_init),
        grid=(B, nq),
        in_specs=[full((4, DA_QK)), row(DA_W), pl.BlockSpec((1, S, DA_W), lambda b, i: (b, 0, 0)),
                  pl.BlockSpec((1, nq, DA_W, t), lambda b, i: (b, 0, 0, 0)), row(DA_W), row(MEM_W), row(D_MODEL),
                  full((MIX_W, D_MODEL)), full((1, D_MODEL)), full((1, D_MODEL)), full((1, DA_V))],
        out_specs=row(D_MODEL),
        out_shape=jax.ShapeDtypeStruct((B * S, D_MODEL), F32),
        scratch_shapes=[pltpu.VMEM((DA_HEADS, 1, 2 * t), F32), pltpu.VMEM((DA_HEADS, 1, 2 * t), F32),
                        pltpu.VMEM((DA_HEADS, DA_V, 2 * t), F32), pltpu.VMEM((t, MIX_W), BF16)],
        compiler_params=_params("arbitrary", "arbitrary"),
        name="diff_prompt",
    )(lamv, q, k.reshape(B, S, DA_W), vt.reshape(B, nq, DA_W, t), g, om, x, w_out, ln_g, ln_b, dng)


def _swa_prompt_kernel(tq, sinks_ref, q_ref, k_ref, kr_ref, v_ref, vr_ref, g_ref, om_ref, x_ref, wout_ref,
                       lng_ref, lnb_ref, y_ref):
    i = pl.program_id(1)
    nk = tq + WINDOW
    start = pl.multiple_of(jnp.maximum(i * tq - WINDOW, 0), CHUNK)
    rows = pl.ds(start, nk)
    shift = jnp.where(i == 0, 0, WINDOW // CHUNK)
    kc = lax.broadcasted_iota(jnp.int32, (tq, nk), 1) // CHUNK - shift
    qc = lax.broadcasted_iota(jnp.int32, (tq, nk), 0) // CHUNK
    mask = (kc <= qc) & (kc >= qc - WINDOW // CHUNK)
    o = _swa_heads(q_ref[...], k_ref[0, rows, :], kr_ref[0, rows, :], v_ref[0, rows, :], vr_ref[0, rows, :],
                   lambda h: sinks_ref[h], mask)
    mix = jnp.concatenate([(o * g_ref[...].astype(F32)).astype(BF16), om_ref[...]], axis=1)
    y_ref[...] = _merge(x_ref[...], mix, wout_ref[...], lng_ref[...], lnb_ref[...])


def _swa_prompt(sinks, q, k, kr, v, vr, g, om, x, w_out, ln_g, ln_b, *, B, S, tq):
    nq = S // tq
    row = lambda w: pl.BlockSpec((tq, w), lambda b, i: (b * nq + i, 0))
    full = lambda shape: pl.BlockSpec(shape, lambda b, i: (0,) * len(shape))
    seq = pl.BlockSpec((1, S, LANES), lambda b, i: (b, 0, 0))
    r3 = lambda t: t.reshape(B, S, LANES)
    return pl.pallas_call(
        functools.partial(_swa_prompt_kernel, tq),
        grid=(B, nq),
        in_specs=[pl.BlockSpec(memory_space=pltpu.SMEM), row(SW_W), seq, seq, seq, seq, row(SW_W), row(MEM_W),
                  row(D_MODEL), full((MIX_W, D_MODEL)), full((1, D_MODEL)), full((1, D_MODEL))],
        out_specs=row(D_MODEL),
        out_shape=jax.ShapeDtypeStruct((B * S, D_MODEL), F32),
        compiler_params=_params("arbitrary", "arbitrary"),
        name="swa_prompt",
    )(sinks, q, r3(k), r3(kr), r3(v), r3(vr), g, om, x, w_out, ln_g, ln_b)


def _sample_tail(b, last, o_first, g_ref, mq_ref, gm_ref, cmk_ref, cmv_ref, x_ref, wout_ref, lng_ref, lnb_ref,
                 y_ref, mix_ref, T):
    om = _mem_attend(mq_ref[...], cmk_ref[0].astype(BF16), cmv_ref[0].astype(BF16)) * gm_ref[...].astype(F32)
    mix = jnp.concatenate([(o_first * g_ref[...].astype(F32)).astype(BF16), om.astype(BF16)], axis=1)
    mix_ref[pl.ds(pl.multiple_of(b * T, T), T), :] = mix

    @pl.when(last)
    def _():
        y_ref[...] = _merge(x_ref[...], mix_ref[...], wout_ref[...], lng_ref[...], lnb_ref[...])


def _diff_sample_kernel(lam_init, T, lamv_ref, q_ref, kn_ref, vn_ref, ck_ref, cv_ref, g_ref, mq_ref, gm_ref,
                        cmk_ref, cmv_ref, x_ref, wout_ref, lng_ref, lnb_ref, dng_ref, y_ref,
                        m_ref, l_ref, acc_ref, mix_ref):
    b, j = pl.program_id(0), pl.program_id(1)
    nb, nj = pl.num_programs(0), pl.num_programs(1)
    qqs = [_split_components(q_ref[:, h * DA_V:(h + 1) * DA_V]) for h in range(DA_HEADS)]

    @pl.when(j == 0)
    def _():
        m_ref[...] = jnp.full(m_ref.shape, -jnp.inf, F32)
        l_ref[...] = jnp.zeros(l_ref.shape, F32)
        acc_ref[...] = jnp.zeros(acc_ref.shape, F32)
        for h in range(DA_HEADS):
            hs = slice(h * DA_V, (h + 1) * DA_V)
            _flash_update(qqs[h], kn_ref[:, hs], vn_ref[:, hs], None, m_ref.at[h], l_ref.at[h], acc_ref.at[h])

    for h in range(DA_HEADS):
        hs = slice(h * DA_V, (h + 1) * DA_V)
        _flash_update(qqs[h], ck_ref[0, :, hs].astype(BF16), cv_ref[0, :, hs].astype(BF16), None,
                      m_ref.at[h], l_ref.at[h], acc_ref.at[h])

    @pl.when(j == nj - 1)
    def _():
        lam = _lambda(lamv_ref[...], lam_init)
        od = jnp.concatenate([_diff_finish(acc_ref[h], l_ref[h], lam, dng_ref[...], lam_init)
                              for h in range(DA_HEADS)], axis=1)
        _sample_tail(b, b == nb - 1, od, g_ref, mq_ref, gm_ref, cmk_ref, cmv_ref, x_ref, wout_ref, lng_ref,
                     lnb_ref, y_ref, mix_ref, T)


def _diff_sample(lam_init, lamv, q, kn, vn, ck, cv, g, mq, gm, cmk, cmv, x, w_out, ln_g, ln_b, dng, *, NB, T, tk):
    P = ck.shape[1]
    row = lambda w: pl.BlockSpec((T, w), lambda b, j: (b, 0))
    full = lambda shape: pl.BlockSpec(shape, lambda b, j: (0,) * len(shape))
    cache = pl.BlockSpec((1, tk, DA_W), lambda b, j: (b, j, 0))
    memc = pl.BlockSpec((1, N_MEM, MEM_W), lambda b, j: (b, 0, 0))
    return pl.pallas_call(
        functools.partial(_diff_sample_kernel, lam_init, T),
        grid=(NB, P // tk),
        in_specs=[full((4, DA_QK)), row(DA_W), row(DA_W), row(DA_W), cache, cache, row(DA_W), row(MEM_W),
                  row(MEM_W), memc, memc, full((NB * T, D_MODEL)), full((MIX_W, D_MODEL)), full((1, D_MODEL)),
                  full((1, D_MODEL)), full((1, DA_V))],
        out_specs=full((NB * T, D_MODEL)),
        out_shape=jax.ShapeDtypeStruct((NB * T, D_MODEL), F32),
        scratch_shapes=[pltpu.VMEM((DA_HEADS, 2 * T, 1), F32), pltpu.VMEM((DA_HEADS, 2 * T, 1), F32),
                        pltpu.VMEM((DA_HEADS, 2 * T, DA_V), F32), pltpu.VMEM((NB * T, MIX_W), BF16)],
        compiler_params=_params("arbitrary", "arbitrary"),
        name="diff_sample",
    )(lamv, q, kn, vn, ck, cv, g, mq, gm, cmk, cmv, x, w_out, ln_g, ln_b, dng)


def _swa_sample_kernel(T, sinks_ref, q_ref, kn_ref, knr_ref, vn_ref, vnr_ref, ck_ref, cv_ref, g_ref, mq_ref, gm_ref,
                       cmk_ref, cmv_ref, x_ref, wout_ref, lng_ref, lnb_ref, y_ref, mix_ref):
    b = pl.program_id(0)
    ck, cv = ck_ref[0], cv_ref[0]
    cat = lambda c, n: jnp.concatenate([c.astype(BF16), n], axis=0)
    k = cat(ck, kn_ref[...])
    kr = cat(pltpu.roll(ck, SW_HD, 1), knr_ref[...])
    v = cat(cv, vn_ref[...])
    vr = cat(pltpu.roll(cv, SW_HD, 1), vnr_ref[...])
    o = _swa_heads(q_ref[...], k, kr, v, vr, lambda h: sinks_ref[h], None)
    _sample_tail(b, b == pl.num_programs(0) - 1, o, g_ref, mq_ref, gm_ref, cmk_ref, cmv_ref, x_ref, wout_ref,
                 lng_ref, lnb_ref, y_ref, mix_ref, T)


def _swa_sample(sinks, q, kn, knr, vn, vnr, ck, cv, g, mq, gm, cmk, cmv, x, w_out, ln_g, ln_b, *, NB, T):
    W = ck.shape[1]
    row = lambda w: pl.BlockSpec((T, w), lambda b: (b, 0))
    full = lambda shape: pl.BlockSpec(shape, lambda b: (0,) * len(shape))
    cache = pl.BlockSpec((1, W, LANES), lambda b: (b, 0, 0))
    memc = pl.BlockSpec((1, N_MEM, MEM_W), lambda b: (b, 0, 0))
    return pl.pallas_call(
        functools.partial(_swa_sample_kernel, T),
        grid=(NB,),
        in_specs=[pl.BlockSpec(memory_space=pltpu.SMEM), row(SW_W), row(LANES), row(LANES), row(LANES), row(LANES),
                  cache, cache, row(SW_W), row(MEM_W), row(MEM_W), memc, memc, full((NB * T, D_MODEL)),
                  full((MIX_W, D_MODEL)), full((1, D_MODEL)), full((1, D_MODEL))],
        out_specs=full((NB * T, D_MODEL)),
        out_shape=jax.ShapeDtypeStruct((NB * T, D_MODEL), F32),
        scratch_shapes=[pltpu.VMEM((NB * T, MIX_W), BF16)],
        compiler_params=_params("arbitrary"),
        name="swa_sample",
    )(sinks, q, kn, knr, vn, vnr, ck, cv, g, mq, gm, cmk, cmv, x, w_out, ln_g, ln_b)


def _rope_tables(pos):
    T = pos.shape[0]
    inv = ROPE_THETA ** (-jnp.arange(ROPE_HALF, dtype=F32) / ROPE_HALF)
    ang = pos.astype(F32)[:, None] * inv[None, :]
    cos, sin = jnp.cos(ang), jnp.sin(ang)
    rest = SW_HD - 2 * ROPE_HALF
    c = jnp.concatenate([cos, cos, jnp.ones((T, rest), F32)], axis=1)
    a = jnp.concatenate([-sin, jnp.zeros((T, SW_HD - ROPE_HALF), F32)], axis=1)
    b = jnp.concatenate([jnp.zeros((T, ROPE_HALF), F32), sin, jnp.zeros((T, rest), F32)], axis=1)
    return tuple(jnp.tile(t, (1, LANES // SW_HD)) for t in (c, a, b))


def kernel(x_prompt, x_sample, mem_prompt, cache_diff_k, cache_diff_v, cache_swa_k, cache_swa_v, cache_mem_k,
           cache_mem_v, w_in_a, lam_q1, lam_k1, lam_q2, lam_k2, diff_norm_g, w_in_b, sinks, w_kv_shared, w_mem_kv,
           w_out, ln_g, ln_b):
    B, S, _ = x_prompt.shape
    NB, T, _ = x_sample.shape
    P = cache_diff_k.shape[2]
    WR = cache_swa_k.shape[1]
    assert w_in_a.shape[0] == 1 and w_in_b.shape[0] == 1 and w_out.shape[0] == DEPTH

    tabs_p = _rope_tables(jnp.tile(jnp.arange(S), B))
    tabs_s = _rope_tables(jnp.tile(P + jnp.arange(T), NB))
    xp = x_prompt.reshape(B * S, D_MODEL)
    xs = x_sample.reshape(NB * T, D_MODEL)
    wa = w_in_a[0].astype(BF16)
    wb = jnp.concatenate([w_in_b[0], w_kv_shared], axis=1).astype(BF16)
    wo = w_out.astype(BF16)
    lng = ln_g.reshape(DEPTH, 1, D_MODEL)
    lnb = ln_b.reshape(DEPTH, 1, D_MODEL)
    lamv = jnp.concatenate([lam_q1, lam_k1, lam_q2, lam_k2], axis=0)
    dng = diff_norm_g.reshape(1, DA_V)
    lam_init = 0.8 - 0.6 * math.exp(-0.3 * 0)

    mkf, mvf, mkb, mvb = _mem_kv(mem_prompt.reshape(B * N_MEM, D_MODEL).astype(BF16), w_mem_kv.astype(BF16))
    mem_b = lambda l: (mkb[l].reshape(B, N_MEM, MEM_W), mvb[l].reshape(B, N_MEM, MEM_W))
    cmk = cache_mem_k.reshape(DEPTH, NB, N_MEM, MEM_W)
    cmv = cache_mem_v.reshape(DEPTH, NB, N_MEM, MEM_W)

    q, kf, vf, kb, vb, g, om = _project("a", xp, wa, tabs_p, mem_b(0), tm=512, rows_per_batch=S)
    xp1 = _diff_prompt(lam_init, lamv, q, kb, vb, g, om, xp, wo[0], lng[0], lnb[0], dng, B=B, S=S)
    qs, kfs, vfs, kbs, vbs, gs_, mqs, gms = _project("a", xs, wa, tabs_s, tm=NB * T)
    xs1 = _diff_sample(lam_init, lamv, qs, kbs, vbs, cache_diff_k[0].reshape(NB, P, DA_W),
                       cache_diff_v[0].reshape(NB, P, DA_W), gs_, mqs, gms, cmk[0], cmv[0], xs, wo[0], lng[0],
                       lnb[0], dng, NB=NB, T=T, tk=1024)

    q, g, om, skf, svf, kb1, kr1, vb1, vr1 = _project("b", xp1, wb, tabs_p, mem_b(1), tm=512, rows_per_batch=S)
    yp = _swa_prompt(sinks[0], q, kb1, kr1, vb1, vr1, g, om, xp1, wo[1], lng[1], lnb[1], B=B, S=S, tq=256)
    qs, gs_, mqs, gms, skfs, svfs, kbs1, krs1, vbs1, vrs1 = _project("b", xs1, wb, tabs_s, tm=NB * T)
    ys = _swa_sample(sinks[0], qs, kbs1, krs1, vbs1, vrs1, cache_swa_k.reshape(NB, WR, LANES),
                     cache_swa_v.reshape(NB, WR, LANES), gs_, mqs, gms, cmk[1], cmv[1], xs1, wo[1], lng[1], lnb[1],
                     NB=NB, T=T)

    wr_p = min(WINDOW, S)
    swa_kp = skf.reshape(B, S, SW_KV, SW_HD)[:, S - wr_p:]
    swa_vp = svf.reshape(B, S, SW_KV, SW_HD)[:, S - wr_p:]
    roll = lambda c, n: jnp.concatenate([c, n.reshape(NB, T, SW_KV, SW_HD)], axis=1)[:, T:]
    return (yp.reshape(B, S, D_MODEL), ys.reshape(NB, T, D_MODEL),
            kf.reshape(1, B, S, DA_HEADS, 2, DA_QK), vf.reshape(1, B, S, DA_HEADS, DA_V),
            kfs.reshape(1, NB, T, DA_HEADS, 2, DA_QK), vfs.reshape(1, NB, T, DA_HEADS, DA_V),
            swa_kp, swa_vp, roll(cache_swa_k, skfs), roll(cache_swa_v, svfs),
            mkf.reshape(DEPTH, B, N_MEM, MEM_HEADS, MEM_HD), mvf.reshape(DEPTH, B, N_MEM, MEM_HEADS, MEM_HD))
```

```python
import functools
import math

import jax
import jax.numpy as jnp
from jax import lax
from jax.experimental import pallas as pl
from jax.experimental.pallas import tpu as pltpu

D_MODEL = 1024
CHUNK = 64
N_MEM = 256
DA_HEADS = 4
DA_QK = 64
DA_V = 128
DA_W = 512
SW_HEADS = 8
SW_KV = 2
SW_GROUP = 4
SW_HD = 64
SW_W = 512
WINDOW = 128
MEM_HEADS = 4
MEM_HD = 128
MEM_W = 512
MIX_W = 1024
ROPE_THETA = 500000.0
ROPE_HALF = 8
DEPTH = 2
DN_ALPHA = (2 * DEPTH) ** 0.25
LN_EPS = 1e-5
NEG = -1e30
LANES = 128
DIFF_TILE = 256
PIPE_LEAD = 3
SUM_ROWS = 16
LOG2E = math.log2(math.e)

F32 = jnp.float32
BF16 = jnp.bfloat16
VMEM_LIMIT = 48 * 1024 * 1024

_NT = (((1,), (1,)), ((), ()))


def _params(*sem):
    return pltpu.CompilerParams(dimension_semantics=sem, vmem_limit_bytes=VMEM_LIMIT)


def _rope(x, c, a, b):
    outs = []
    for i in range(x.shape[1] // LANES):
        blk = x[:, i * LANES:(i + 1) * LANES]
        outs.append(blk * c + pltpu.roll(blk, LANES - ROPE_HALF, 1) * a + pltpu.roll(blk, ROPE_HALF, 1) * b)
    return outs[0] if len(outs) == 1 else jnp.concatenate(outs, axis=1)


def _silu(g):
    return g * (1.0 / (1.0 + jnp.exp(-g)))


def _layer_norm(z, g, b):
    mu = jnp.mean(z, axis=-1, keepdims=True)
    d = z - mu
    var = jnp.mean(d * d, axis=-1, keepdims=True)
    return d * lax.rsqrt(var + LN_EPS) * g + b


def _mem_attend(mq, mk, mv):
    outs = []
    for h in range(MEM_HEADS):
        sl = slice(h * MEM_HD, (h + 1) * MEM_HD)
        s = lax.dot_general(mq[:, sl], mk[:, sl], _NT, preferred_element_type=F32) * (MEM_HD ** -0.5)
        p = jnp.exp(s - jnp.max(s, axis=1, keepdims=True))
        l = jnp.sum(p, axis=1, keepdims=True)
        outs.append(jnp.dot(p.astype(BF16), mv[:, sl], preferred_element_type=F32) / l)
    return jnp.concatenate(outs, axis=1)


def _merge(x, mix, w_out, ln_g, ln_b):
    return _layer_norm(DN_ALPHA * x + jnp.dot(mix, w_out, preferred_element_type=F32), ln_g, ln_b)


def _lambda(lamv, lam_init):
    e1 = jnp.exp(jnp.sum(lamv[0:1, :] * lamv[1:2, :], axis=1, keepdims=True))
    e2 = jnp.exp(jnp.sum(lamv[2:3, :] * lamv[3:4, :], axis=1, keepdims=True))
    return e1 - e2 + lam_init


def _split_components(qh):
    lane = lax.broadcasted_iota(jnp.int32, qh.shape, 1)
    zero = jnp.zeros_like(qh)
    return jnp.concatenate([jnp.where(lane < DA_QK, qh, zero), jnp.where(lane >= DA_QK, qh, zero)], axis=0)


def _flash_update(qq, ks, vs, mask, m_ref, l_ref, acc_ref):
    s = lax.dot_general(qq, ks, _NT, preferred_element_type=F32)
    if mask is not None:
        s = jnp.where(mask, s, NEG)
    m_old = m_ref[...]
    m_new = jnp.maximum(m_old, jnp.max(s, axis=1, keepdims=True))
    alpha = jnp.exp2(m_old - m_new)
    p = jnp.exp2(s - m_new)
    l_ref[...] = alpha * l_ref[...] + jnp.sum(p, axis=1, keepdims=True)
    acc_ref[...] = alpha * acc_ref[...] + jnp.dot(p.astype(BF16), vs, preferred_element_type=F32)
    m_ref[...] = m_new


def _stage_scores(qq, ks, mask, s_ref, m_ref, alpha_ref):
    s = lax.dot_general(ks, qq, _NT, preferred_element_type=F32)
    if mask is not None:
        s = jnp.where(mask, s, NEG)
    s_ref[...] = s
    m_old = m_ref[...]
    m_new = jnp.maximum(m_old, jnp.max(s, axis=0, keepdims=True))
    alpha_ref[...] = jnp.exp2(m_old - m_new)
    m_ref[...] = m_new


def _stage_exp(s_ref, p_ref, m_ref):
    p_ref[...] = jnp.exp2(s_ref[...] - m_ref[...]).astype(BF16)


def _stage_pv(vts, p_ref, alpha_ref, acc_ref):
    ones = jnp.ones((SUM_ROWS, vts.shape[1]), BF16)
    pv = jnp.dot(jnp.concatenate([vts, ones], axis=0), p_ref[...], preferred_element_type=F32)
    acc_ref[...] = alpha_ref[...] * acc_ref[...] + pv


def _diff_finish(acc, l, lam, dng, lam_init):
    rows = acc.shape[0] // 2
    o = acc / l
    od = o[:rows] - lam * o[rows:]
    return od * lax.rsqrt(jnp.mean(od * od, axis=-1, keepdims=True) + LN_EPS) * dng * (1.0 - lam_init)


def _sink_attend(qq, ks, vs, sink, mask):
    s = lax.dot_general(qq, ks, _NT, preferred_element_type=F32)
    if mask is not None:
        s = jnp.where(mask, s, NEG)
    m = jnp.maximum(jnp.max(s, axis=1, keepdims=True), sink)
    p = jnp.exp(s - m)
    l = jnp.sum(p, axis=1, keepdims=True) + jnp.exp(sink - m)
    return jnp.dot(p.astype(BF16), vs, preferred_element_type=F32) / l


def _swa_heads(q, k, kr, v, vr, sink_of, mask):
    low = lax.broadcasted_iota(jnp.int32, (q.shape[0], LANES), 1) < SW_HD
    outs = []
    for pair in range(SW_HEADS // 2):
        kv = pair // 2
        slab = q[:, pair * LANES:(pair + 1) * LANES]
        zero = jnp.zeros_like(slab)
        k_even, k_odd = (k, kr) if kv == 0 else (kr, k)
        v_even, v_odd = (v, vr) if kv == 0 else (vr, v)
        o_even = _sink_attend(jnp.where(low, slab, zero), k_even, v_even, sink_of(2 * pair), mask)
        o_odd = _sink_attend(jnp.where(low, zero, slab), k_odd, v_odd, sink_of(2 * pair + 1), mask)
        outs.append(jnp.where(low, o_even, o_odd))
    return jnp.concatenate(outs, axis=1)


def _mem_kv_kernel(mem_ref, w_ref, kf_ref, vf_ref, kb_ref, vb_ref):
    h = jnp.dot(mem_ref[...], w_ref[0], preferred_element_type=F32)
    k, v = h[:, :MEM_W], h[:, MEM_W:]
    kf_ref[0] = k
    vf_ref[0] = v
    kb_ref[0] = k.astype(BF16)
    vb_ref[0] = v.astype(BF16)


def _mem_kv(mem_bf, w_bf):
    rows = mem_bf.shape[0]
    out = lambda dt: jax.ShapeDtypeStruct((DEPTH, rows, MEM_W), dt)
    ospec = pl.BlockSpec((1, rows, MEM_W), lambda l: (l, 0, 0))
    return pl.pallas_call(
        _mem_kv_kernel,
        grid=(DEPTH,),
        in_specs=[pl.BlockSpec((rows, D_MODEL), lambda l: (0, 0)),
                  pl.BlockSpec((1, D_MODEL, 2 * MEM_W), lambda l: (l, 0, 0))],
        out_specs=[ospec] * 4,
        out_shape=[out(F32), out(F32), out(BF16), out(BF16)],
        compiler_params=_params("arbitrary"),
        name="mem_kv",
    )(mem_bf, w_bf)


def _proj_a_kernel(fuse_mem, x_ref, w_ref, c_ref, a_ref, b_ref, *refs):
    if fuse_mem:
        mk_ref, mv_ref, q_ref, kf_ref, vf_ref, kb_ref, vb_ref, g_ref, om_ref = refs
    else:
        q_ref, kf_ref, vf_ref, kb_ref, vb_ref, g_ref, mq_ref, gm_ref = refs
    x = x_ref[...].astype(BF16)
    c, a, b = c_ref[...], a_ref[...], b_ref[...]

    def cols(i):
        return jnp.dot(x, w_ref[:, i * DA_W:(i + 1) * DA_W], preferred_element_type=F32)

    q_ref[...] = (_rope(cols(0), c, a, b) * (DA_QK ** -0.5 * LOG2E)).astype(BF16)
    k = _rope(cols(1), c, a, b)
    kf_ref[...] = k
    kb_ref[...] = k.astype(BF16)
    v = cols(2)
    vf_ref[...] = v
    if fuse_mem:
        for blk in range(vb_ref.shape[0]):
            vb_ref[blk] = v[blk * DIFF_TILE:(blk + 1) * DIFF_TILE, :].T.astype(BF16)
    else:
        vb_ref[...] = v.astype(BF16)
    g_ref[...] = _silu(cols(3)).astype(BF16)
    mq = cols(4).astype(BF16)
    gm = _silu(cols(5))
    if fuse_mem:
        om_ref[...] = (_mem_attend(mq, mk_ref[0], mv_ref[0]) * gm).astype(BF16)
    else:
        mq_ref[...] = mq
        gm_ref[...] = gm.astype(BF16)


def _proj_b_kernel(fuse_mem, x_ref, w_ref, c_ref, a_ref, b_ref, *refs):
    if fuse_mem:
        mk_ref, mv_ref, q_ref, g_ref, om_ref, kf_ref, vf_ref, kb_ref, kr_ref, vb_ref, vr_ref = refs
    else:
        q_ref, g_ref, mq_ref, gm_ref, kf_ref, vf_ref, kb_ref, kr_ref, vb_ref, vr_ref = refs
    x = x_ref[...].astype(BF16)
    c, a, b = c_ref[...], a_ref[...], b_ref[...]

    def cols(i):
        return jnp.dot(x, w_ref[:, i * SW_W:(i + 1) * SW_W], preferred_element_type=F32)

    q_ref[...] = (_rope(cols(0), c, a, b) * (SW_HD ** -0.5)).astype(BF16)
    g_ref[...] = _silu(cols(1)).astype(BF16)
    mq = cols(2).astype(BF16)
    gm = _silu(cols(3))
    if fuse_mem:
        om_ref[...] = (_mem_attend(mq, mk_ref[0], mv_ref[0]) * gm).astype(BF16)
    else:
        mq_ref[...] = mq
        gm_ref[...] = gm.astype(BF16)
    kv = jnp.dot(x, w_ref[:, 4 * SW_W:], preferred_element_type=F32)
    k = _rope(kv[:, :LANES], c, a, b)
    v = kv[:, LANES:]
    kf_ref[...] = k
    vf_ref[...] = v
    kb_ref[...] = k.astype(BF16)
    kr_ref[...] = pltpu.roll(k, SW_HD, 1).astype(BF16)
    vb_ref[...] = v.astype(BF16)
    vr_ref[...] = pltpu.roll(v, SW_HD, 1).astype(BF16)


def _project(layer, x, w_bf, tabs, mem=None, *, tm, rows_per_batch=None):
    M = x.shape[0]
    fuse = mem is not None
    row = lambda w: pl.BlockSpec((tm, w), lambda i: (i, 0))
    in_specs = [row(D_MODEL), pl.BlockSpec(w_bf.shape, lambda i: (0, 0)), row(LANES), row(LANES), row(LANES)]
    args = [x, w_bf, *tabs]
    if fuse:
        per = rows_per_batch // tm
        mspec = pl.BlockSpec((1, N_MEM, MEM_W), lambda i: (i // per, 0, 0))
        in_specs += [mspec, mspec]
        args += list(mem)
    o = lambda w, dt: (row(w), jax.ShapeDtypeStruct((M, w), dt))
    if layer == "a":
        vt = (pl.BlockSpec((tm // DIFF_TILE, DA_W, DIFF_TILE), lambda i: (i, 0, 0)),
              jax.ShapeDtypeStruct((M // DIFF_TILE, DA_W, DIFF_TILE), BF16))
        outs = [o(DA_W, BF16), o(DA_W, F32), o(DA_W, F32), o(DA_W, BF16), vt if fuse else o(DA_W, BF16), o(DA_W, BF16)]
        body = _proj_a_kernel
    else:
        outs = [o(SW_W, BF16), o(SW_W, BF16)]
        body = _proj_b_kernel
    mem_outs = [o(MEM_W, BF16)] if fuse else [o(MEM_W, BF16), o(MEM_W, BF16)]
    if layer == "a":
        outs = outs + mem_outs
    else:
        outs = outs + mem_outs + [o(LANES, F32), o(LANES, F32)] + [o(LANES, BF16)] * 4
    return pl.pallas_call(
        functools.partial(body, fuse),
        grid=(M // tm,),
        in_specs=in_specs,
        out_specs=[s for s, _ in outs],
        out_shape=[t for _, t in outs],
        compiler_params=_params("arbitrary"),
        name="proj_" + layer,
    )(*args)


def _diff_prompt_kernel(lam_init, lamv_ref, q_ref, k_ref, vt_ref, g_ref, om_ref, x_ref, wout_ref,
                        lng_ref, lnb_ref, dng_ref, y_ref, m_ref, alpha_ref, acc_ref, s_ref, p_ref, mix_ref):
    t = DIFF_TILE
    SLOTS = s_ref.shape[0]
    LEAD = PIPE_LEAD
    i = pl.program_id(1)
    heads = [slice(h * DA_V, (h + 1) * DA_V) for h in range(DA_HEADS)]
    lane = lax.broadcasted_iota(jnp.int32, (t, DA_V), 1)
    units = []
    for hs in heads:
        qh = q_ref[:, hs]
        zero = jnp.zeros_like(qh)
        units += [(hs, jnp.where(lane < DA_QK, qh, zero)), (hs, jnp.where(lane >= DA_QK, qh, zero))]
    m_ref[...] = jnp.full(m_ref.shape, -jnp.inf, F32)
    acc_ref[...] = jnp.zeros(acc_ref.shape, F32)

    n = len(units)
    assert n == SLOTS and LEAD + 1 < n

    def exp_stage(u):
        _stage_exp(s_ref.at[u], p_ref.at[u], m_ref.at[u])

    def pv_stage(j, u):
        _stage_pv(vt_ref[0, j, units[u][0], :], p_ref.at[u], alpha_ref.at[u], acc_ref.at[u])

    def block(j, mask, first):
        rows = pl.ds(pl.multiple_of(j * t, t), t)
        for u, (hs, qq) in enumerate(units):
            _stage_scores(qq, k_ref[0, rows, hs], mask, s_ref.at[u], m_ref.at[u], alpha_ref.at[u])
            if u >= LEAD:
                exp_stage(u - LEAD)
            elif not first:
                exp_stage(u - LEAD + n)
            if u >= LEAD + 1:
                pv_stage(j, u - LEAD - 1)
            elif not first:
                pv_stage(j - 1, u - LEAD - 1 + n)

    def body(j, carry):
        block(j, None, False)
        return carry

    kc = lax.broadcasted_iota(jnp.int32, (t, t), 0) // CHUNK
    qc = lax.broadcasted_iota(jnp.int32, (t, t), 1) // CHUNK
    diag_mask = kc <= qc
    block(0, diag_mask | (i > 0), True)
    lax.fori_loop(1, i, body, 0)

    @pl.when(i >= 1)
    def _():
        block(i, diag_mask, False)

    for u in range(n - LEAD, n):
        exp_stage(u)
        pv_stage(i, u - 1)
    pv_stage(i, n - 1)

    lam = _lambda(lamv_ref[...], lam_init)
    for h, hs in enumerate(heads):
        o1 = acc_ref[2 * h, :DA_V, :] / acc_ref[2 * h, DA_V:DA_V + 1, :]
        o2 = acc_ref[2 * h + 1, :DA_V, :] / acc_ref[2 * h + 1, DA_V:DA_V + 1, :]
        od = (o1 - lam * o2).T
        od = od * lax.rsqrt(jnp.mean(od * od, axis=-1, keepdims=True) + LN_EPS) * dng_ref[...] * (1.0 - lam_init)
        mix_ref[:, hs] = (od * g_ref[:, hs].astype(F32)).astype(BF16)
    mix_ref[:, DA_W:] = om_ref[...]
    y_ref[...] = _merge(x_ref[...], mix_ref[...], wout_ref[...], lng_ref[...], lnb_ref[...])


def _diff_prompt(lam_init, lamv, q, k, vt, g, om, x, w_out, ln_g, ln_b, dng, *, B, S):
    t = DIFF_TILE
    nq = S // t
    row = lambda w: pl.BlockSpec((t, w), lambda b, i: (b * nq + i, 0))
    full = lambda shape: pl.BlockSpec(shape, lambda b, i: (0,) * len(shape))
    return pl.pallas_call(
        functools.partial(_diff_prompt_kernel, lam_init),
        grid=(B, nq),
        in_specs=[full((4, DA_QK)), row(DA_W), pl.BlockSpec((1, S, DA_W), lambda b, i: (b, 0, 0)),
                  pl.BlockSpec((1, nq, DA_W, t), lambda b, i: (b, 0, 0, 0)), row(DA_W), row(MEM_W), row(D_MODEL),
                  full((MIX_W, D_MODEL)), full((1, D_MODEL)), full((1, D_MODEL)), full((1, DA_V))],
        out_specs=row(D_MODEL),
        out_shape=jax.ShapeDtypeStruct((B * S, D_MODEL), F32),
        scratch_shapes=[pltpu.VMEM((2 * DA_HEADS, 1, t), F32), pltpu.VMEM((2 * DA_HEADS, 1, t), F32),
                        pltpu.VMEM((2 * DA_HEADS, DA_V + SUM_ROWS, t), F32),
                        pltpu.VMEM((2 * DA_HEADS, t, t), F32), pltpu.VMEM((2 * DA_HEADS, t, t), BF16),
                        pltpu.VMEM((t, MIX_W), BF16)],
        compiler_params=_params("arbitrary", "arbitrary"),
        name="diff_prompt",
    )(lamv, q, k.reshape(B, S, DA_W), vt.reshape(B, nq, DA_W, t), g, om, x, w_out, ln_g, ln_b, dng)


def _swa_prompt_kernel(tq, sinks_ref, q_ref, k_ref, kr_ref, v_ref, vr_ref, g_ref, om_ref, x_ref, wout_ref,
                       lng_ref, lnb_ref, y_ref):
    i = pl.program_id(1)
    nk = tq + WINDOW
    start = pl.multiple_of(jnp.maximum(i * tq - WINDOW, 0), CHUNK)
    rows = pl.ds(start, nk)
    shift = jnp.where(i == 0, 0, WINDOW // CHUNK)
    kc = lax.broadcasted_iota(jnp.int32, (tq, nk), 1) // CHUNK - shift
    qc = lax.broadcasted_iota(jnp.int32, (tq, nk), 0) // CHUNK
    mask = (kc <= qc) & (kc >= qc - WINDOW // CHUNK)
    o = _swa_heads(q_ref[...], k_ref[0, rows, :], kr_ref[0, rows, :], v_ref[0, rows, :], vr_ref[0, rows, :],
                   lambda h: sinks_ref[h], mask)
    mix = jnp.concatenate([(o * g_ref[...].astype(F32)).astype(BF16), om_ref[...]], axis=1)
    y_ref[...] = _merge(x_ref[...], mix, wout_ref[...], lng_ref[...], lnb_ref[...])


def _swa_prompt(sinks, q, k, kr, v, vr, g, om, x, w_out, ln_g, ln_b, *, B, S, tq):
    nq = S // tq
    row = lambda w: pl.BlockSpec((tq, w), lambda b, i: (b * nq + i, 0))
    full = lambda shape: pl.BlockSpec(shape, lambda b, i: (0,) * len(shape))
    seq = pl.BlockSpec((1, S, LANES), lambda b, i: (b, 0, 0))
    r3 = lambda t: t.reshape(B, S, LANES)
    return pl.pallas_call(
        functools.partial(_swa_prompt_kernel, tq),
        grid=(B, nq),
        in_specs=[pl.BlockSpec(memory_space=pltpu.SMEM), row(SW_W), seq, seq, seq, seq, row(SW_W), row(MEM_W),
                  row(D_MODEL), full((MIX_W, D_MODEL)), full((1, D_MODEL)), full((1, D_MODEL))],
        out_specs=row(D_MODEL),
        out_shape=jax.ShapeDtypeStruct((B * S, D_MODEL), F32),
        compiler_params=_params("arbitrary", "arbitrary"),
        name="swa_prompt",
    )(sinks, q, r3(k), r3(kr), r3(v), r3(vr), g, om, x, w_out, ln_g, ln_b)


def _sample_tail(b, last, o_first, g_ref, mq_ref, gm_ref, cmk_ref, cmv_ref, x_ref, wout_ref, lng_ref, lnb_ref,
                 y_ref, mix_ref, T):
    om = _mem_attend(mq_ref[...], cmk_ref[0].astype(BF16), cmv_ref[0].astype(BF16)) * gm_ref[...].astype(F32)
    mix = jnp.concatenate([(o_first * g_ref[...].astype(F32)).astype(BF16), om.astype(BF16)], axis=1)
    mix_ref[pl.ds(pl.multiple_of(b * T, T), T), :] = mix

    @pl.when(last)
    def _():
        y_ref[...] = _merge(x_ref[...], mix_ref[...], wout_ref[...], lng_ref[...], lnb_ref[...])


def _diff_sample_kernel(lam_init, T, lamv_ref, q_ref, kn_ref, vn_ref, ck_ref, cv_ref, g_ref, mq_ref, gm_ref,
                        cmk_ref, cmv_ref, x_ref, wout_ref, lng_ref, lnb_ref, dng_ref, y_ref,
                        m_ref, l_ref, acc_ref, mix_ref):
    b, j = pl.program_id(0), pl.program_id(1)
    nb, nj = pl.num_programs(0), pl.num_programs(1)
    qqs = [_split_components(q_ref[:, h * DA_V:(h + 1) * DA_V]) for h in range(DA_HEADS)]

    @pl.when(j == 0)
    def _():
        m_ref[...] = jnp.full(m_ref.shape, -jnp.inf, F32)
        l_ref[...] = jnp.zeros(l_ref.shape, F32)
        acc_ref[...] = jnp.zeros(acc_ref.shape, F32)
        for h in range(DA_HEADS):
            hs = slice(h * DA_V, (h + 1) * DA_V)
            _flash_update(qqs[h], kn_ref[:, hs], vn_ref[:, hs], None, m_ref.at[h], l_ref.at[h], acc_ref.at[h])

    for h in range(DA_HEADS):
        hs = slice(h * DA_V, (h + 1) * DA_V)
        _flash_update(qqs[h], ck_ref[0, :, hs].astype(BF16), cv_ref[0, :, hs].astype(BF16), None,
                      m_ref.at[h], l_ref.at[h], acc_ref.at[h])

    @pl.when(j == nj - 1)
    def _():
        lam = _lambda(lamv_ref[...], lam_init)
        od = jnp.concatenate([_diff_finish(acc_ref[h], l_ref[h], lam, dng_ref[...], lam_init)
                              for h in range(DA_HEADS)], axis=1)
        _sample_tail(b, b == nb - 1, od, g_ref, mq_ref, gm_ref, cmk_ref, cmv_ref, x_ref, wout_ref, lng_ref,
                     lnb_ref, y_ref, mix_ref, T)


def _diff_sample(lam_init, lamv, q, kn, vn, ck, cv, g, mq, gm, cmk, cmv, x, w_out, ln_g, ln_b, dng, *, NB, T, tk):
    P = ck.shape[1]
    row = lambda w: pl.BlockSpec((T, w), lambda b, j: (b, 0))
    full = lambda shape: pl.BlockSpec(shape, lambda b, j: (0,) * len(shape))
    cache = pl.BlockSpec((1, tk, DA_W), lambda b, j: (b, j, 0))
    memc = pl.BlockSpec((1, N_MEM, MEM_W), lambda b, j: (b, 0, 0))
    return pl.pallas_call(
        functools.partial(_diff_sample_kernel, lam_init, T),
        grid=(NB, P // tk),
        in_specs=[full((4, DA_QK)), row(DA_W), row(DA_W), row(DA_W), cache, cache, row(DA_W), row(MEM_W),
                  row(MEM_W), memc, memc, full((NB * T, D_MODEL)), full((MIX_W, D_MODEL)), full((1, D_MODEL)),
                  full((1, D_MODEL)), full((1, DA_V))],
        out_specs=full((NB * T, D_MODEL)),
        out_shape=jax.ShapeDtypeStruct((NB * T, D_MODEL), F32),
        scratch_shapes=[pltpu.VMEM((DA_HEADS, 2 * T, 1), F32), pltpu.VMEM((DA_HEADS, 2 * T, 1), F32),
                        pltpu.VMEM((DA_HEADS, 2 * T, DA_V), F32), pltpu.VMEM((NB * T, MIX_W), BF16)],
        compiler_params=_params("arbitrary", "arbitrary"),
        name="diff_sample",
    )(lamv, q, kn, vn, ck, cv, g, mq, gm, cmk, cmv, x, w_out, ln_g, ln_b, dng)


def _swa_sample_kernel(T, sinks_ref, q_ref, kn_ref, knr_ref, vn_ref, vnr_ref, ck_ref, cv_ref, g_ref, mq_ref, gm_ref,
                       cmk_ref, cmv_ref, x_ref, wout_ref, lng_ref, lnb_ref, y_ref, mix_ref):
    b = pl.program_id(0)
    ck, cv = ck_ref[0], cv_ref[0]
    cat = lambda c, n: jnp.concatenate([c.astype(BF16), n], axis=0)
    k = cat(ck, kn_ref[...])
    kr = cat(pltpu.roll(ck, SW_HD, 1), knr_ref[...])
    v = cat(cv, vn_ref[...])
    vr = cat(pltpu.roll(cv, SW_HD, 1), vnr_ref[...])
    o = _swa_heads(q_ref[...], k, kr, v, vr, lambda h: sinks_ref[h], None)
    _sample_tail(b, b == pl.num_programs(0) - 1, o, g_ref, mq_ref, gm_ref, cmk_ref, cmv_ref, x_ref, wout_ref,
                 lng_ref, lnb_ref, y_ref, mix_ref, T)


def _swa_sample(sinks, q, kn, knr, vn, vnr, ck, cv, g, mq, gm, cmk, cmv, x, w_out, ln_g, ln_b, *, NB, T):
    W = ck.shape[1]
    row = lambda w: pl.BlockSpec((T, w), lambda b: (b, 0))
    full = lambda shape: pl.BlockSpec(shape, lambda b: (0,) * len(shape))
    cache = pl.BlockSpec((1, W, LANES), lambda b: (b, 0, 0))
    memc = pl.BlockSpec((1, N_MEM, MEM_W), lambda b: (b, 0, 0))
    return pl.pallas_call(
        functools.partial(_swa_sample_kernel, T),
        grid=(NB,),
        in_specs=[pl.BlockSpec(memory_space=pltpu.SMEM), row(SW_W), row(LANES), row(LANES), row(LANES), row(LANES),
                  cache, cache, row(SW_W), row(MEM_W), row(MEM_W), memc, memc, full((NB * T, D_MODEL)),
                  full((MIX_W, D_MODEL)), full((1, D_MODEL)), full((1, D_MODEL))],
        out_specs=full((NB * T, D_MODEL)),
        out_shape=jax.ShapeDtypeStruct((NB * T, D_MODEL), F32),
        scratch_shapes=[pltpu.VMEM((NB * T, MIX_W), BF16)],
        compiler_params=_params("arbitrary"),
        name="swa_sample",
    )(sinks, q, kn, knr, vn, vnr, ck, cv, g, mq, gm, cmk, cmv, x, w_out, ln_g, ln_b)


def _rope_tables(pos):
    T = pos.shape[0]
    inv = ROPE_THETA ** (-jnp.arange(ROPE_HALF, dtype=F32) / ROPE_HALF)
    ang = pos.astype(F32)[:, None] * inv[None, :]
    cos, sin = jnp.cos(ang), jnp.sin(ang)
    rest = SW_HD - 2 * ROPE_HALF
    c = jnp.concatenate([cos, cos, jnp.ones((T, rest), F32)], axis=1)
    a = jnp.concatenate([-sin, jnp.zeros((T, SW_HD - ROPE_HALF), F32)], axis=1)
    b = jnp.concatenate([jnp.zeros((T, ROPE_HALF), F32), sin, jnp.zeros((T, rest), F32)], axis=1)
    return tuple(jnp.tile(t, (1, LANES // SW_HD)) for t in (c, a, b))


def kernel(x_prompt, x_sample, mem_prompt, cache_diff_k, cache_diff_v, cache_swa_k, cache_swa_v, cache_mem_k,
           cache_mem_v, w_in_a, lam_q1, lam_k1, lam_q2, lam_k2, diff_norm_g, w_in_b, sinks, w_kv_shared, w_mem_kv,
           w_out, ln_g, ln_b):
    B, S, _ = x_prompt.shape
    NB, T, _ = x_sample.shape
    P = cache_diff_k.shape[2]
    WR = cache_swa_k.shape[1]
    assert w_in_a.shape[0] == 1 and w_in_b.shape[0] == 1 and w_out.shape[0] == DEPTH

    tabs_p = _rope_tables(jnp.tile(jnp.arange(S), B))
    tabs_s = _rope_tables(jnp.tile(P + jnp.arange(T), NB))
    xp = x_prompt.reshape(B * S, D_MODEL)
    xs = x_sample.reshape(NB * T, D_MODEL)
    wa = w_in_a[0].astype(BF16)
    wb = jnp.concatenate([w_in_b[0], w_kv_shared], axis=1).astype(BF16)
    wo = w_out.astype(BF16)
    lng = ln_g.reshape(DEPTH, 1, D_MODEL)
    lnb = ln_b.reshape(DEPTH, 1, D_MODEL)
    lamv = jnp.concatenate([lam_q1, lam_k1, lam_q2, lam_k2], axis=0)
    dng = diff_norm_g.reshape(1, DA_V)
    lam_init = 0.8 - 0.6 * math.exp(-0.3 * 0)

    mkf, mvf, mkb, mvb = _mem_kv(mem_prompt.reshape(B * N_MEM, D_MODEL).astype(BF16), w_mem_kv.astype(BF16))
    mem_b = lambda l: (mkb[l].reshape(B, N_MEM, MEM_W), mvb[l].reshape(B, N_MEM, MEM_W))
    cmk = cache_mem_k.reshape(DEPTH, NB, N_MEM, MEM_W)
    cmv = cache_mem_v.reshape(DEPTH, NB, N_MEM, MEM_W)

    q, kf, vf, kb, vb, g, om = _project("a", xp, wa, tabs_p, mem_b(0), tm=512, rows_per_batch=S)
    xp1 = _diff_prompt(lam_init, lamv, q, kb, vb, g, om, xp, wo[0], lng[0], lnb[0], dng, B=B, S=S)
    qs, kfs, vfs, kbs, vbs, gs_, mqs, gms = _project("a", xs, wa, tabs_s, tm=NB * T)
    xs1 = _diff_sample(lam_init, lamv, qs, kbs, vbs, cache_diff_k[0].reshape(NB, P, DA_W),
                       cache_diff_v[0].reshape(NB, P, DA_W), gs_, mqs, gms, cmk[0], cmv[0], xs, wo[0], lng[0],
                       lnb[0], dng, NB=NB, T=T, tk=1024)

    q, g, om, skf, svf, kb1, kr1, vb1, vr1 = _project("b", xp1, wb, tabs_p, mem_b(1), tm=512, rows_per_batch=S)
    yp = _swa_prompt(sinks[0], q, kb1, kr1, vb1, vr1, g, om, xp1, wo[1], lng[1], lnb[1], B=B, S=S, tq=256)
    qs, gs_, mqs, gms, skfs, svfs, kbs1, krs1, vbs1, vrs1 = _project("b", xs1, wb, tabs_s, tm=NB * T)
    ys = _swa_sample(sinks[0], qs, kbs1, krs1, vbs1, vrs1, cache_swa_k.reshape(NB, WR, LANES),
                     cache_swa_v.reshape(NB, WR, LANES), gs_, mqs, gms, cmk[1], cmv[1], xs1, wo[1], lng[1], lnb[1],
                     NB=NB, T=T)

    wr_p = min(WINDOW, S)
    swa_kp = skf.reshape(B, S, SW_KV, SW_HD)[:, S - wr_p:]
    swa_vp = svf.reshape(B, S, SW_KV, SW_HD)[:, S - wr_p:]
    roll = lambda c, n: jnp.concatenate([c, n.reshape(NB, T, SW_KV, SW_HD)], axis=1)[:, T:]
    return (yp.reshape(B, S, D_MODEL), ys.reshape(NB, T, D_MODEL),
            kf.reshape(1, B, S, DA_HEADS, 2, DA_QK), vf.reshape(1, B, S, DA_HEADS, DA_V),
            kfs.reshape(1, NB, T, DA_HEADS, 2, DA_QK), vfs.reshape(1, NB, T, DA_HEADS, DA_V),
            swa_kp, swa_vp, roll(cache_swa_k, skfs), roll(cache_swa_v, svfs),
            mkf.reshape(DEPTH, B, N_MEM, MEM_HEADS, MEM_HD), mvf.reshape(DEPTH, B, N_MEM, MEM_HEADS, MEM_HD))
```

```python
import functools
import math

import jax
import jax.numpy as jnp
from jax import lax
from jax.experimental import pallas as pl
from jax.experimental.pallas import tpu as pltpu

D_MODEL = 1024
CHUNK = 64
N_MEM = 256
DA_HEADS = 4
DA_QK = 64
DA_V = 128
DA_W = 512
SW_HEADS = 8
SW_KV = 2
SW_GROUP = 4
SW_HD = 64
SW_W = 512
WINDOW = 128
MEM_HEADS = 4
MEM_HD = 128
MEM_W = 512
MIX_W = 1024
ROPE_THETA = 500000.0
ROPE_HALF = 8
DEPTH = 2
DN_ALPHA = (2 * DEPTH) ** 0.25
LN_EPS = 1e-5
NEG = -1e30
LANES = 128
DIFF_TILE = 256
PIPE_LEAD = 3
SUM_ROWS = 16
LOG2E = math.log2(math.e)

F32 = jnp.float32
BF16 = jnp.bfloat16
VMEM_LIMIT = 48 * 1024 * 1024

_NT = (((1,), (1,)), ((), ()))


def _params(*sem):
    return pltpu.CompilerParams(dimension_semantics=sem, vmem_limit_bytes=VMEM_LIMIT)


def _rope(x, c, a, b):
    outs = []
    for i in range(x.shape[1] // LANES):
        blk = x[:, i * LANES:(i + 1) * LANES]
        outs.append(blk * c + pltpu.roll(blk, LANES - ROPE_HALF, 1) * a + pltpu.roll(blk, ROPE_HALF, 1) * b)
    return outs[0] if len(outs) == 1 else jnp.concatenate(outs, axis=1)


def _silu(g):
    return g * (1.0 / (1.0 + jnp.exp(-g)))


def _layer_norm(z, g, b):
    mu = jnp.mean(z, axis=-1, keepdims=True)
    d = z - mu
    var = jnp.mean(d * d, axis=-1, keepdims=True)
    return d * lax.rsqrt(var + LN_EPS) * g + b


def _mem_attend(mq, mk_of, mv_of):
    outs = []
    for h in range(MEM_HEADS):
        sl = slice(h * MEM_HD, (h + 1) * MEM_HD)
        s = lax.dot_general(mq[:, sl], mk_of(h), _NT, preferred_element_type=F32) * (MEM_HD ** -0.5)
        p = jnp.exp(s - jnp.max(s, axis=1, keepdims=True))
        l = jnp.sum(p, axis=1, keepdims=True)
        outs.append(jnp.dot(p.astype(BF16), mv_of(h), preferred_element_type=F32) / l)
    return jnp.concatenate(outs, axis=1)


def _head_cols(ref):
    return lambda h: ref[0, :, h * MEM_HD:(h + 1) * MEM_HD]


def _head_planes(ref):
    return lambda h: ref[0, 0, :, h, :].astype(BF16)


def _merge(x, mix, w_out, ln_g, ln_b):
    return _layer_norm(DN_ALPHA * x + jnp.dot(mix, w_out, preferred_element_type=F32), ln_g, ln_b)


def _lambda(lamv, lam_init):
    e1 = jnp.exp(jnp.sum(lamv[0:1, :] * lamv[1:2, :], axis=1, keepdims=True))
    e2 = jnp.exp(jnp.sum(lamv[2:3, :] * lamv[3:4, :], axis=1, keepdims=True))
    return e1 - e2 + lam_init


def _split_components(qh):
    lane = lax.broadcasted_iota(jnp.int32, qh.shape, 1)
    zero = jnp.zeros_like(qh)
    return jnp.concatenate([jnp.where(lane < DA_QK, qh, zero), jnp.where(lane >= DA_QK, qh, zero)], axis=0)


def _flash_update(qq, ks, vs, keys_on_lanes, m_ref, l_ref, acc_ref):
    if keys_on_lanes:
        s = jnp.dot(qq, ks, preferred_element_type=F32)
    else:
        s = lax.dot_general(qq, ks, _NT, preferred_element_type=F32)
    m_old = m_ref[...]
    m_new = jnp.maximum(m_old, jnp.max(s, axis=1, keepdims=True))
    alpha = jnp.exp2(m_old - m_new)
    p = jnp.exp2(s - m_new)
    l_ref[...] = alpha * l_ref[...] + jnp.sum(p, axis=1, keepdims=True)
    acc_ref[...] = alpha * acc_ref[...] + jnp.dot(p.astype(BF16), vs, preferred_element_type=F32)
    m_ref[...] = m_new


def _stage_scores(qq, ks, mask, s_ref, m_ref, alpha_ref):
    s = lax.dot_general(ks, qq, _NT, preferred_element_type=F32)
    if mask is not None:
        s = jnp.where(mask, s, NEG)
    s_ref[...] = s
    m_old = m_ref[...]
    m_new = jnp.maximum(m_old, jnp.max(s, axis=0, keepdims=True))
    alpha_ref[...] = jnp.exp2(m_old - m_new)
    m_ref[...] = m_new


def _stage_exp(s_ref, p_ref, m_ref):
    p_ref[...] = jnp.exp2(s_ref[...] - m_ref[...]).astype(BF16)


def _stage_pv(vts, p_ref, alpha_ref, acc_ref):
    ones = jnp.ones((SUM_ROWS, vts.shape[1]), BF16)
    pv = jnp.dot(jnp.concatenate([vts, ones], axis=0), p_ref[...], preferred_element_type=F32)
    acc_ref[...] = alpha_ref[...] * acc_ref[...] + pv


def _diff_finish(acc, l, lam, dng, lam_init):
    rows = acc.shape[0] // 2
    o = acc / l
    od = o[:rows] - lam * o[rows:]
    return od * lax.rsqrt(jnp.mean(od * od, axis=-1, keepdims=True) + LN_EPS) * dng * (1.0 - lam_init)


def _sink_attend(qq, ks, vs, sink, mask):
    s = lax.dot_general(qq, ks, _NT, preferred_element_type=F32)
    if mask is not None:
        s = jnp.where(mask, s, NEG)
    m = jnp.maximum(jnp.max(s, axis=1, keepdims=True), sink)
    p = jnp.exp(s - m)
    l = jnp.sum(p, axis=1, keepdims=True) + jnp.exp(sink - m)
    return jnp.dot(p.astype(BF16), vs, preferred_element_type=F32) / l


def _sink_attend_cached(qq, kt, kn, vt, vn, sink):
    s_c = jnp.dot(qq, kt, preferred_element_type=F32)
    s_n = lax.dot_general(qq, kn, _NT, preferred_element_type=F32)
    m = jnp.maximum(jnp.maximum(jnp.max(s_c, axis=1, keepdims=True), jnp.max(s_n, axis=1, keepdims=True)), sink)
    p_c, p_n = jnp.exp(s_c - m), jnp.exp(s_n - m)
    l = jnp.sum(p_c, axis=1, keepdims=True) + jnp.sum(p_n, axis=1, keepdims=True) + jnp.exp(sink - m)
    o = (lax.dot_general(p_c.astype(BF16), vt, _NT, preferred_element_type=F32)
         + jnp.dot(p_n.astype(BF16), vn, preferred_element_type=F32))
    return o / l


def _swa_heads(q, attend, sink_of):
    low = lax.broadcasted_iota(jnp.int32, (q.shape[0], LANES), 1) < SW_HD
    outs = []
    for pair in range(SW_HEADS // 2):
        kv = pair // 2
        slab = q[:, pair * LANES:(pair + 1) * LANES]
        zero = jnp.zeros_like(slab)
        o_even = attend(jnp.where(low, slab, zero), kv == 1, sink_of(2 * pair))
        o_odd = attend(jnp.where(low, zero, slab), kv == 0, sink_of(2 * pair + 1))
        outs.append(jnp.where(low, o_even, o_odd))
    return jnp.concatenate(outs, axis=1)


def _mem_kv_kernel(mem_ref, w_ref, kf_ref, vf_ref, kb_ref, vb_ref):
    h = jnp.dot(mem_ref[...], w_ref[0], preferred_element_type=F32)
    k, v = h[:, :MEM_W], h[:, MEM_W:]
    for hd in range(MEM_HEADS):
        kf_ref[0, :, hd, :] = k[:, hd * MEM_HD:(hd + 1) * MEM_HD]
        vf_ref[0, :, hd, :] = v[:, hd * MEM_HD:(hd + 1) * MEM_HD]
    kb_ref[0] = k.astype(BF16)
    vb_ref[0] = v.astype(BF16)


def _mem_kv(mem_bf, w_bf):
    rows = mem_bf.shape[0]
    fshape, fspec = (DEPTH, rows, MEM_HEADS, MEM_HD), pl.BlockSpec((1, rows, MEM_HEADS, MEM_HD), lambda l: (l, 0, 0, 0))
    bshape, bspec = (DEPTH, rows, MEM_W), pl.BlockSpec((1, rows, MEM_W), lambda l: (l, 0, 0))
    return pl.pallas_call(
        _mem_kv_kernel,
        grid=(DEPTH,),
        in_specs=[pl.BlockSpec((rows, D_MODEL), lambda l: (0, 0)),
                  pl.BlockSpec((1, D_MODEL, 2 * MEM_W), lambda l: (l, 0, 0))],
        out_specs=[fspec, fspec, bspec, bspec],
        out_shape=[jax.ShapeDtypeStruct(fshape, F32), jax.ShapeDtypeStruct(fshape, F32),
                   jax.ShapeDtypeStruct(bshape, BF16), jax.ShapeDtypeStruct(bshape, BF16)],
        compiler_params=_params("arbitrary"),
        name="mem_kv",
    )(mem_bf, w_bf)


def _proj_a_kernel(fuse_mem, x_ref, w_ref, c_ref, a_ref, b_ref, *refs):
    if fuse_mem:
        mk_ref, mv_ref, q_ref, kf_ref, vf_ref, kb_ref, vb_ref, g_ref, om_ref = refs
    else:
        q_ref, kf_ref, vf_ref, kb_ref, vb_ref, g_ref, mq_ref, gm_ref = refs
    x = x_ref[...].astype(BF16)
    c, a, b = c_ref[...], a_ref[...], b_ref[...]

    def cols(i):
        return jnp.dot(x, w_ref[:, i * DA_W:(i + 1) * DA_W], preferred_element_type=F32)

    q_ref[...] = (_rope(cols(0), c, a, b) * (DA_QK ** -0.5 * LOG2E)).astype(BF16)
    k = _rope(cols(1), c, a, b)
    kb_ref[...] = k.astype(BF16)
    v = cols(2)
    for hd in range(DA_HEADS):
        vf_ref[:, hd, :] = v[:, hd * DA_V:(hd + 1) * DA_V]
    if fuse_mem:
        kf_ref[0] = k.T
        for blk in range(vb_ref.shape[0]):
            vb_ref[blk] = v[blk * DIFF_TILE:(blk + 1) * DIFF_TILE, :].T.astype(BF16)
    else:
        kf_ref[...] = k
        vb_ref[...] = v.astype(BF16)
    g_ref[...] = _silu(cols(3)).astype(BF16)
    mq = cols(4).astype(BF16)
    gm = _silu(cols(5))
    if fuse_mem:
        om_ref[...] = (_mem_attend(mq, _head_cols(mk_ref), _head_cols(mv_ref)) * gm).astype(BF16)
    else:
        mq_ref[...] = mq
        gm_ref[...] = gm.astype(BF16)


def _proj_b_kernel(fuse_mem, x_ref, w_ref, c_ref, a_ref, b_ref, *refs):
    if fuse_mem:
        mk_ref, mv_ref, q_ref, g_ref, om_ref, kf_ref, vf_ref, kb_ref, kr_ref, vb_ref, vr_ref = refs
    else:
        q_ref, g_ref, mq_ref, gm_ref, kf_ref, vf_ref, kb_ref, kr_ref, vb_ref, vr_ref = refs
    x = x_ref[...].astype(BF16)
    c, a, b = c_ref[...], a_ref[...], b_ref[...]

    def cols(i):
        return jnp.dot(x, w_ref[:, i * SW_W:(i + 1) * SW_W], preferred_element_type=F32)

    q_ref[...] = (_rope(cols(0), c, a, b) * (SW_HD ** -0.5)).astype(BF16)
    g_ref[...] = _silu(cols(1)).astype(BF16)
    mq = cols(2).astype(BF16)
    gm = _silu(cols(3))
    if fuse_mem:
        om_ref[...] = (_mem_attend(mq, _head_cols(mk_ref), _head_cols(mv_ref)) * gm).astype(BF16)
    else:
        mq_ref[...] = mq
        gm_ref[...] = gm.astype(BF16)
    kv = jnp.dot(x, w_ref[:, 4 * SW_W:], preferred_element_type=F32)
    k = _rope(kv[:, :LANES], c, a, b)
    v = kv[:, LANES:]
    kf_ref[...] = k
    vf_ref[...] = v
    kb_ref[...] = k.astype(BF16)
    kr_ref[...] = pltpu.roll(k, SW_HD, 1).astype(BF16)
    vb_ref[...] = v.astype(BF16)
    vr_ref[...] = pltpu.roll(v, SW_HD, 1).astype(BF16)


def _project(layer, x, w_bf, tabs, mem=None, *, tm, rows_per_batch=None):
    M = x.shape[0]
    fuse = mem is not None
    per = (rows_per_batch or M) // tm
    row = lambda w: pl.BlockSpec((tm, w), lambda i: (i, 0))
    tab = pl.BlockSpec((tm, LANES), lambda i: (i % per, 0))
    in_specs = [row(D_MODEL), pl.BlockSpec(w_bf.shape, lambda i: (0, 0)), tab, tab, tab]
    args = [x, w_bf, *tabs]
    if fuse:
        mspec = pl.BlockSpec((1, N_MEM, MEM_W), lambda i: (i // per, 0, 0))
        in_specs += [mspec, mspec]
        args += list(mem)
    o = lambda w, dt: (row(w), jax.ShapeDtypeStruct((M, w), dt))
    if layer == "a":
        vt = (pl.BlockSpec((tm // DIFF_TILE, DA_W, DIFF_TILE), lambda i: (i, 0, 0)),
              jax.ShapeDtypeStruct((M // DIFF_TILE, DA_W, DIFF_TILE), BF16))
        kt = (pl.BlockSpec((1, DA_W, tm), lambda i: (i // per, 0, i % per)),
              jax.ShapeDtypeStruct((M // (per * tm), DA_W, per * tm), F32))
        vf = (pl.BlockSpec((tm, DA_HEADS, DA_V), lambda i: (i, 0, 0)), jax.ShapeDtypeStruct((M, DA_HEADS, DA_V), F32))
        outs = [o(DA_W, BF16), kt if fuse else o(DA_W, F32), vf, o(DA_W, BF16), vt if fuse else o(DA_W, BF16),
                o(DA_W, BF16)]
        body = _proj_a_kernel
    else:
        outs = [o(SW_W, BF16), o(SW_W, BF16)]
        body = _proj_b_kernel
    mem_outs = [o(MEM_W, BF16)] if fuse else [o(MEM_W, BF16), o(MEM_W, BF16)]
    if layer == "a":
        outs = outs + mem_outs
    else:
        outs = outs + mem_outs + [o(LANES, F32), o(LANES, F32)] + [o(LANES, BF16)] * 4
    return pl.pallas_call(
        functools.partial(body, fuse),
        grid=(M // tm,),
        in_specs=in_specs,
        out_specs=[s for s, _ in outs],
        out_shape=[t for _, t in outs],
        compiler_params=_params("arbitrary"),
        name="proj_" + layer,
    )(*args)


def _diff_prompt_kernel(lam_init, lamv_ref, q_ref, k_ref, vt_ref, g_ref, om_ref, x_ref, wout_ref,
                        lng_ref, lnb_ref, dng_ref, y_ref, m_ref, alpha_ref, acc_ref, s_ref, p_ref, mix_ref):
    t = DIFF_TILE
    SLOTS = s_ref.shape[0]
    LEAD = PIPE_LEAD
    i = pl.program_id(1)
    heads = [slice(h * DA_V, (h + 1) * DA_V) for h in range(DA_HEADS)]
    lane = lax.broadcasted_iota(jnp.int32, (t, DA_V), 1)
    units = []
    for hs in heads:
        qh = q_ref[:, hs]
        zero = jnp.zeros_like(qh)
        units += [(hs, jnp.where(lane < DA_QK, qh, zero)), (hs, jnp.where(lane >= DA_QK, qh, zero))]
    m_ref[...] = jnp.full(m_ref.shape, -jnp.inf, F32)
    acc_ref[...] = jnp.zeros(acc_ref.shape, F32)

    n = len(units)
    assert n == SLOTS and LEAD + 1 < n

    def exp_stage(u):
        _stage_exp(s_ref.at[u], p_ref.at[u], m_ref.at[u])

    def pv_stage(j, u):
        _stage_pv(vt_ref[0, j, units[u][0], :], p_ref.at[u], alpha_ref.at[u], acc_ref.at[u])

    def block(j, mask, first):
        rows = pl.ds(pl.multiple_of(j * t, t), t)
        for u, (hs, qq) in enumerate(units):
            _stage_scores(qq, k_ref[0, rows, hs], mask, s_ref.at[u], m_ref.at[u], alpha_ref.at[u])
            if u >= LEAD:
                exp_stage(u - LEAD)
            elif not first:
                exp_stage(u - LEAD + n)
            if u >= LEAD + 1:
                pv_stage(j, u - LEAD - 1)
            elif not first:
                pv_stage(j - 1, u - LEAD - 1 + n)

    def body(j, carry):
        block(j, None, False)
        return carry

    kc = lax.broadcasted_iota(jnp.int32, (t, t), 0) // CHUNK
    qc = lax.broadcasted_iota(jnp.int32, (t, t), 1) // CHUNK
    diag_mask = kc <= qc
    block(0, diag_mask | (i > 0), True)
    lax.fori_loop(1, i, body, 0)

    @pl.when(i >= 1)
    def _():
        block(i, diag_mask, False)

    for u in range(n - LEAD, n):
        exp_stage(u)
        pv_stage(i, u - 1)
    pv_stage(i, n - 1)

    lam = _lambda(lamv_ref[...], lam_init)
    for h, hs in enumerate(heads):
        o1 = acc_ref[2 * h, :DA_V, :] / acc_ref[2 * h, DA_V:DA_V + 1, :]
        o2 = acc_ref[2 * h + 1, :DA_V, :] / acc_ref[2 * h + 1, DA_V:DA_V + 1, :]
        od = (o1 - lam * o2).T
        od = od * lax.rsqrt(jnp.mean(od * od, axis=-1, keepdims=True) + LN_EPS) * dng_ref[...] * (1.0 - lam_init)
        mix_ref[:, hs] = (od * g_ref[:, hs].astype(F32)).astype(BF16)
    mix_ref[:, DA_W:] = om_ref[...]
    y_ref[...] = _merge(x_ref[...], mix_ref[...], wout_ref[...], lng_ref[...], lnb_ref[...])


def _diff_prompt(lam_init, lamv, q, k, vt, g, om, x, w_out, ln_g, ln_b, dng, *, B, S):
    t = DIFF_TILE
    nq = S // t
    row = lambda w: pl.BlockSpec((t, w), lambda b, i: (b * nq + i, 0))
    full = lambda shape: pl.BlockSpec(shape, lambda b, i: (0,) * len(shape))
    return pl.pallas_call(
        functools.partial(_diff_prompt_kernel, lam_init),
        grid=(B, nq),
        in_specs=[full((4, DA_QK)), row(DA_W), pl.BlockSpec((1, S, DA_W), lambda b, i: (b, 0, 0)),
                  pl.BlockSpec((1, nq, DA_W, t), lambda b, i: (b, 0, 0, 0)), row(DA_W), row(MEM_W), row(D_MODEL),
                  full((MIX_W, D_MODEL)), full((1, D_MODEL)), full((1, D_MODEL)), full((1, DA_V))],
        out_specs=row(D_MODEL),
        out_shape=jax.ShapeDtypeStruct((B * S, D_MODEL), F32),
        scratch_shapes=[pltpu.VMEM((2 * DA_HEADS, 1, t), F32), pltpu.VMEM((2 * DA_HEADS, 1, t), F32),
                        pltpu.VMEM((2 * DA_HEADS, DA_V + SUM_ROWS, t), F32),
                        pltpu.VMEM((2 * DA_HEADS, t, t), F32), pltpu.VMEM((2 * DA_HEADS, t, t), BF16),
                        pltpu.VMEM((t, MIX_W), BF16)],
        compiler_params=_params("arbitrary", "arbitrary"),
        name="diff_prompt",
    )(lamv, q, k.reshape(B, S, DA_W), vt.reshape(B, nq, DA_W, t), g, om, x, w_out, ln_g, ln_b, dng)


def _swa_prompt_kernel(tq, sinks_ref, q_ref, k_ref, kr_ref, v_ref, vr_ref, g_ref, om_ref, x_ref, wout_ref,
                       lng_ref, lnb_ref, y_ref):
    i = pl.program_id(1)
    nk = tq + WINDOW
    start = pl.multiple_of(jnp.maximum(i * tq - WINDOW, 0), CHUNK)
    rows = pl.ds(start, nk)
    shift = jnp.where(i == 0, 0, WINDOW // CHUNK)
    kc = lax.broadcasted_iota(jnp.int32, (tq, nk), 1) // CHUNK - shift
    qc = lax.broadcasted_iota(jnp.int32, (tq, nk), 0) // CHUNK
    mask = (kc <= qc) & (kc >= qc - WINDOW // CHUNK)
    k, kr, v, vr = k_ref[0, rows, :], kr_ref[0, rows, :], v_ref[0, rows, :], vr_ref[0, rows, :]
    attend = lambda qq, swapped, sink: _sink_attend(qq, kr if swapped else k, vr if swapped else v, sink, mask)
    o = _swa_heads(q_ref[...], attend, lambda h: sinks_ref[h])
    mix = jnp.concatenate([(o * g_ref[...].astype(F32)).astype(BF16), om_ref[...]], axis=1)
    y_ref[...] = _merge(x_ref[...], mix, wout_ref[...], lng_ref[...], lnb_ref[...])


def _swa_prompt(sinks, q, k, kr, v, vr, g, om, x, w_out, ln_g, ln_b, *, B, S, tq):
    nq = S // tq
    row = lambda w: pl.BlockSpec((tq, w), lambda b, i: (b * nq + i, 0))
    full = lambda shape: pl.BlockSpec(shape, lambda b, i: (0,) * len(shape))
    seq = pl.BlockSpec((1, S, LANES), lambda b, i: (b, 0, 0))
    r3 = lambda t: t.reshape(B, S, LANES)
    return pl.pallas_call(
        functools.partial(_swa_prompt_kernel, tq),
        grid=(B, nq),
        in_specs=[pl.BlockSpec(memory_space=pltpu.SMEM), row(SW_W), seq, seq, seq, seq, row(SW_W), row(MEM_W),
                  row(D_MODEL), full((MIX_W, D_MODEL)), full((1, D_MODEL)), full((1, D_MODEL))],
        out_specs=row(D_MODEL),
        out_shape=jax.ShapeDtypeStruct((B * S, D_MODEL), F32),
        compiler_params=_params("arbitrary", "arbitrary"),
        name="swa_prompt",
    )(sinks, q, r3(k), r3(kr), r3(v), r3(vr), g, om, x, w_out, ln_g, ln_b)


def _sample_tail(b, last, o_first, g_ref, mq_ref, gm_ref, cmk_ref, cmv_ref, x_ref, wout_ref, lng_ref, lnb_ref,
                 y_ref, mix_ref, T):
    om = _mem_attend(mq_ref[...], _head_planes(cmk_ref), _head_planes(cmv_ref)) * gm_ref[...].astype(F32)
    mix = jnp.concatenate([(o_first * g_ref[...].astype(F32)).astype(BF16), om.astype(BF16)], axis=1)
    mix_ref[pl.ds(pl.multiple_of(b * T, T), T), :] = mix

    @pl.when(last)
    def _():
        y_ref[...] = _merge(x_ref[...], mix_ref[...], wout_ref[...], lng_ref[...], lnb_ref[...])


def _diff_sample_kernel(lam_init, T, lamv_ref, q_ref, kn_ref, vn_ref, ckt_ref, cv_ref, g_ref, mq_ref, gm_ref,
                        cmk_ref, cmv_ref, x_ref, wout_ref, lng_ref, lnb_ref, dng_ref, y_ref,
                        m_ref, l_ref, acc_ref, mix_ref):
    b, j = pl.program_id(0), pl.program_id(1)
    nb, nj = pl.num_programs(0), pl.num_programs(1)
    heads = [slice(h * DA_V, (h + 1) * DA_V) for h in range(DA_HEADS)]
    qqs = [_split_components(q_ref[:, hs]) for hs in heads]

    @pl.when(j == 0)
    def _():
        m_ref[...] = jnp.full(m_ref.shape, -jnp.inf, F32)
        l_ref[...] = jnp.zeros(l_ref.shape, F32)
        acc_ref[...] = jnp.zeros(acc_ref.shape, F32)
        for h, hs in enumerate(heads):
            _flash_update(qqs[h], kn_ref[:, hs], vn_ref[:, hs], False, m_ref.at[h], l_ref.at[h], acc_ref.at[h])

    for h, hs in enumerate(heads):
        _flash_update(qqs[h], ckt_ref[0, hs, :].astype(BF16), cv_ref[0, :, h, :].astype(BF16), True,
                      m_ref.at[h], l_ref.at[h], acc_ref.at[h])

    @pl.when(j == nj - 1)
    def _():
        lam = _lambda(lamv_ref[...], lam_init)
        od = jnp.concatenate([_diff_finish(acc_ref[h], l_ref[h], lam, dng_ref[...], lam_init)
                              for h in range(DA_HEADS)], axis=1)
        _sample_tail(b, b == nb - 1, od, g_ref, mq_ref, gm_ref, cmk_ref, cmv_ref, x_ref, wout_ref, lng_ref,
                     lnb_ref, y_ref, mix_ref, T)


def _diff_sample(lam_init, lamv, q, kn, vn, ckt, cv, g, mq, gm, cmk, cmv, layer, x, w_out, ln_g, ln_b, dng, *,
                 NB, T, tk):
    P = ckt.shape[2]
    row = lambda w: pl.BlockSpec((T, w), lambda b, j: (b, 0))
    full = lambda shape: pl.BlockSpec(shape, lambda b, j: (0,) * len(shape))
    memc = pl.BlockSpec((1, 1, N_MEM, MEM_HEADS, MEM_HD), lambda b, j: (layer, b, 0, 0, 0))
    return pl.pallas_call(
        functools.partial(_diff_sample_kernel, lam_init, T),
        grid=(NB, P // tk),
        in_specs=[full((4, DA_QK)), row(DA_W), row(DA_W), row(DA_W),
                  pl.BlockSpec((1, DA_W, tk), lambda b, j: (b, 0, j)),
                  pl.BlockSpec((1, tk, DA_HEADS, DA_V), lambda b, j: (b, j, 0, 0)),
                  row(DA_W), row(MEM_W), row(MEM_W), memc, memc, full((NB * T, D_MODEL)), full((MIX_W, D_MODEL)),
                  full((1, D_MODEL)), full((1, D_MODEL)), full((1, DA_V))],
        out_specs=full((NB * T, D_MODEL)),
        out_shape=jax.ShapeDtypeStruct((NB * T, D_MODEL), F32),
        scratch_shapes=[pltpu.VMEM((DA_HEADS, 2 * T, 1), F32), pltpu.VMEM((DA_HEADS, 2 * T, 1), F32),
                        pltpu.VMEM((DA_HEADS, 2 * T, DA_V), F32), pltpu.VMEM((NB * T, MIX_W), BF16)],
        compiler_params=_params("arbitrary", "arbitrary"),
        name="diff_sample",
    )(lamv, q, kn, vn, ckt, cv, g, mq, gm, cmk, cmv, x, w_out, ln_g, ln_b, dng)


def _swa_sample_kernel(T, sinks_ref, q_ref, kn_ref, knr_ref, vn_ref, vnr_ref, ckt_ref, cvt_ref, g_ref, mq_ref, gm_ref,
                       cmk_ref, cmv_ref, x_ref, wout_ref, lng_ref, lnb_ref, y_ref, mix_ref):
    b = pl.program_id(0)
    swap = lambda t: jnp.concatenate([t[SW_HD:], t[:SW_HD]], axis=0).astype(BF16)
    ckt, cvt = ckt_ref[0], cvt_ref[0]
    kt, ktr, vt, vtr = ckt.astype(BF16), swap(ckt), cvt.astype(BF16), swap(cvt)
    kn, knr, vn, vnr = kn_ref[...], knr_ref[...], vn_ref[...], vnr_ref[...]
    attend = lambda qq, swapped, sink: _sink_attend_cached(
        qq, ktr if swapped else kt, knr if swapped else kn, vtr if swapped else vt, vnr if swapped else vn, sink)
    o = _swa_heads(q_ref[...], attend, lambda h: sinks_ref[h])
    _sample_tail(b, b == pl.num_programs(0) - 1, o, g_ref, mq_ref, gm_ref, cmk_ref, cmv_ref, x_ref, wout_ref,
                 lng_ref, lnb_ref, y_ref, mix_ref, T)


def _swa_sample(sinks, q, kn, knr, vn, vnr, ckt, cvt, g, mq, gm, cmk, cmv, layer, x, w_out, ln_g, ln_b, *, NB, T):
    W = ckt.shape[2]
    row = lambda w: pl.BlockSpec((T, w), lambda b: (b, 0))
    full = lambda shape: pl.BlockSpec(shape, lambda b: (0,) * len(shape))
    cache = pl.BlockSpec((1, LANES, W), lambda b: (b, 0, 0))
    memc = pl.BlockSpec((1, 1, N_MEM, MEM_HEADS, MEM_HD), lambda b: (layer, b, 0, 0, 0))
    return pl.pallas_call(
        functools.partial(_swa_sample_kernel, T),
        grid=(NB,),
        in_specs=[pl.BlockSpec(memory_space=pltpu.SMEM), row(SW_W), row(LANES), row(LANES), row(LANES), row(LANES),
                  cache, cache, row(SW_W), row(MEM_W), row(MEM_W), memc, memc, full((NB * T, D_MODEL)),
                  full((MIX_W, D_MODEL)), full((1, D_MODEL)), full((1, D_MODEL))],
        out_specs=full((NB * T, D_MODEL)),
        out_shape=jax.ShapeDtypeStruct((NB * T, D_MODEL), F32),
        scratch_shapes=[pltpu.VMEM((NB * T, MIX_W), BF16)],
        compiler_params=_params("arbitrary"),
        name="swa_sample",
    )(sinks, q, kn, knr, vn, vnr, ckt, cvt, g, mq, gm, cmk, cmv, x, w_out, ln_g, ln_b)


def _rope_tables(pos):
    T = pos.shape[0]
    inv = ROPE_THETA ** (-jnp.arange(ROPE_HALF, dtype=F32) / ROPE_HALF)
    ang = pos.astype(F32)[:, None] * inv[None, :]
    cos, sin = jnp.cos(ang), jnp.sin(ang)
    rest = SW_HD - 2 * ROPE_HALF
    c = jnp.concatenate([cos, cos, jnp.ones((T, rest), F32)], axis=1)
    a = jnp.concatenate([-sin, jnp.zeros((T, SW_HD - ROPE_HALF), F32)], axis=1)
    b = jnp.concatenate([jnp.zeros((T, ROPE_HALF), F32), sin, jnp.zeros((T, rest), F32)], axis=1)
    return tuple(jnp.tile(t, (1, LANES // SW_HD)) for t in (c, a, b))


def _feature_major(t):
    n = t.ndim
    return jnp.transpose(t, (0,) + tuple(range(2, n)) + (1,)).reshape(t.shape[0], -1, t.shape[1])


def _rows_major(t, tail):
    n = len(tail)
    return jnp.transpose(t.reshape((t.shape[0],) + tail + (t.shape[2],)), (0, n + 1) + tuple(range(1, n + 1)))


def kernel(x_prompt, x_sample, mem_prompt, cache_diff_k, cache_diff_v, cache_swa_k, cache_swa_v, cache_mem_k,
           cache_mem_v, w_in_a, lam_q1, lam_k1, lam_q2, lam_k2, diff_norm_g, w_in_b, sinks, w_kv_shared, w_mem_kv,
           w_out, ln_g, ln_b):
    B, S, _ = x_prompt.shape
    NB, T, _ = x_sample.shape
    P = cache_diff_k.shape[2]
    assert w_in_a.shape[0] == 1 and w_in_b.shape[0] == 1 and w_out.shape[0] == DEPTH

    tabs_p = _rope_tables(jnp.arange(S))
    tabs_s = _rope_tables(jnp.tile(P + jnp.arange(T), NB))
    xp = x_prompt.reshape(B * S, D_MODEL)
    xs = x_sample.reshape(NB * T, D_MODEL)
    wa = w_in_a[0].astype(BF16)
    wb = jnp.concatenate([w_in_b[0], w_kv_shared], axis=1).astype(BF16)
    wo = w_out.astype(BF16)
    lng = ln_g.reshape(DEPTH, 1, D_MODEL)
    lnb = ln_b.reshape(DEPTH, 1, D_MODEL)
    lamv = jnp.concatenate([lam_q1, lam_k1, lam_q2, lam_k2], axis=0)
    dng = diff_norm_g.reshape(1, DA_V)
    lam_init = 0.8 - 0.6 * math.exp(-0.3 * 0)

    mkf, mvf, mkb, mvb = _mem_kv(mem_prompt.reshape(B * N_MEM, D_MODEL).astype(BF16), w_mem_kv.astype(BF16))
    mem_b = lambda l: (mkb[l].reshape(B, N_MEM, MEM_W), mvb[l].reshape(B, N_MEM, MEM_W))

    q, kt, vf, kb, vb, g, om = _project("a", xp, wa, tabs_p, mem_b(0), tm=512, rows_per_batch=S)
    xp1 = _diff_prompt(lam_init, lamv, q, kb, vb, g, om, xp, wo[0], lng[0], lnb[0], dng, B=B, S=S)
    qs, kfs, vfs, kbs, vbs, gs_, mqs, gms = _project("a", xs, wa, tabs_s, tm=NB * T)
    xs1 = _diff_sample(lam_init, lamv, qs, kbs, vbs, _feature_major(cache_diff_k[0]), cache_diff_v[0], gs_, mqs, gms,
                       cache_mem_k, cache_mem_v, 0, xs, wo[0], lng[0], lnb[0], dng, NB=NB, T=T, tk=1024)

    q, g, om, skf, svf, kb1, kr1, vb1, vr1 = _project("b", xp1, wb, tabs_p, mem_b(1), tm=512, rows_per_batch=S)
    yp = _swa_prompt(sinks[0], q, kb1, kr1, vb1, vr1, g, om, xp1, wo[1], lng[1], lnb[1], B=B, S=S, tq=256)
    qs, gs_, mqs, gms, skfs, svfs, kbs1, krs1, vbs1, vrs1 = _project("b", xs1, wb, tabs_s, tm=NB * T)
    ys = _swa_sample(sinks[0], qs, kbs1, krs1, vbs1, vrs1, _feature_major(cache_swa_k), _feature_major(cache_swa_v),
                     gs_, mqs, gms, cache_mem_k, cache_mem_v, 1, xs1, wo[1], lng[1], lnb[1], NB=NB, T=T)

    wr_p = min(WINDOW, S)
    window = lambda t: t.reshape(B, S, LANES)[:, S - wr_p:, :].reshape(B, wr_p, SW_KV, SW_HD)
    swa_kp, swa_vp = window(skf), window(svf)
    roll = lambda c, n: jnp.concatenate([c, n.reshape(NB, T, SW_KV, SW_HD)], axis=1)[:, T:]
    return (yp.reshape(B, S, D_MODEL), ys.reshape(NB, T, D_MODEL),
            _rows_major(kt, (DA_HEADS, 2, DA_QK))[None], vf.reshape(1, B, S, DA_HEADS, DA_V),
            kfs.reshape(1, NB, T, DA_HEADS, 2, DA_QK), vfs.reshape(1, NB, T, DA_HEADS, DA_V),
            swa_kp, swa_vp, roll(cache_swa_k, skfs), roll(cache_swa_v, svfs),
            mkf.reshape(DEPTH, B, N_MEM, MEM_HEADS, MEM_HD), mvf.reshape(DEPTH, B, N_MEM, MEM_HEADS, MEM_HD))
```

```python
import functools
import math

import jax
import jax.numpy as jnp
from jax import lax
from jax.experimental import pallas as pl
from jax.experimental.pallas import tpu as pltpu

D_MODEL = 1024
CHUNK = 64
N_MEM = 256
DA_HEADS = 4
DA_QK = 64
DA_V = 128
DA_W = 512
SW_HEADS = 8
SW_KV = 2
SW_GROUP = 4
SW_HD = 64
SW_W = 512
WINDOW = 128
MEM_HEADS = 4
MEM_HD = 128
MEM_W = 512
MIX_W = 1024
ROPE_THETA = 500000.0
ROPE_HALF = 8
DEPTH = 2
DN_ALPHA = (2 * DEPTH) ** 0.25
LN_EPS = 1e-5
NEG = -1e30
LANES = 128
DIFF_TILE = 256
PIPE_LEAD = 3
SUM_ROWS = 16
LOG2E = math.log2(math.e)

F32 = jnp.float32
BF16 = jnp.bfloat16
VMEM_LIMIT = 48 * 1024 * 1024

_NT = (((1,), (1,)), ((), ()))


def _params(*sem):
    return pltpu.CompilerParams(dimension_semantics=sem, vmem_limit_bytes=VMEM_LIMIT)


def _rope(x, c, a, b):
    outs = []
    for i in range(x.shape[1] // LANES):
        blk = x[:, i * LANES:(i + 1) * LANES]
        outs.append(blk * c + pltpu.roll(blk, LANES - ROPE_HALF, 1) * a + pltpu.roll(blk, ROPE_HALF, 1) * b)
    return outs[0] if len(outs) == 1 else jnp.concatenate(outs, axis=1)


def _silu(g):
    return g * (1.0 / (1.0 + jnp.exp(-g)))


def _layer_norm(z, g, b):
    mu = jnp.mean(z, axis=-1, keepdims=True)
    d = z - mu
    var = jnp.mean(d * d, axis=-1, keepdims=True)
    return d * lax.rsqrt(var + LN_EPS) * g + b


def _mem_attend(mq, mk_of, mv_of):
    outs = []
    for h in range(MEM_HEADS):
        sl = slice(h * MEM_HD, (h + 1) * MEM_HD)
        s = lax.dot_general(mq[:, sl], mk_of(h), _NT, preferred_element_type=F32) * (MEM_HD ** -0.5)
        p = jnp.exp(s - jnp.max(s, axis=1, keepdims=True))
        l = jnp.sum(p, axis=1, keepdims=True)
        outs.append(jnp.dot(p.astype(BF16), mv_of(h), preferred_element_type=F32) / l)
    return jnp.concatenate(outs, axis=1)


def _head_cols(ref):
    return lambda h: ref[0, :, h * MEM_HD:(h + 1) * MEM_HD]


def _head_idx(ref, heads, h, *lead):
    return lead + (pl.ds(h, ref.shape[-2] // heads, stride=heads), slice(None))


def _head_rows(ref, heads, *lead):
    return lambda h: ref[_head_idx(ref, heads, h, *lead)].astype(BF16)


def _merge(x, mix, w_out, ln_g, ln_b):
    return _layer_norm(DN_ALPHA * x + jnp.dot(mix, w_out, preferred_element_type=F32), ln_g, ln_b)


def _lambda(lamv, lam_init):
    e1 = jnp.exp(jnp.sum(lamv[0:1, :] * lamv[1:2, :], axis=1, keepdims=True))
    e2 = jnp.exp(jnp.sum(lamv[2:3, :] * lamv[3:4, :], axis=1, keepdims=True))
    return e1 - e2 + lam_init


def _split_components(qh):
    lane = lax.broadcasted_iota(jnp.int32, qh.shape, 1)
    zero = jnp.zeros_like(qh)
    return jnp.concatenate([jnp.where(lane < DA_QK, qh, zero), jnp.where(lane >= DA_QK, qh, zero)], axis=0)


def _flash_update(qqs, k_of, v_of, keys_on_lanes, m_ref, l_ref, acc_ref):
    if keys_on_lanes:
        s = [jnp.dot(qq, k_of(h), preferred_element_type=F32) for h, qq in enumerate(qqs)]
    else:
        s = [lax.dot_general(qq, k_of(h), _NT, preferred_element_type=F32) for h, qq in enumerate(qqs)]
    s = jnp.concatenate(s, axis=0)
    m_old = m_ref[...]
    m_new = jnp.maximum(m_old, jnp.max(s, axis=1, keepdims=True))
    alpha = jnp.exp2(m_old - m_new)
    p = jnp.exp2(s - m_new)
    l_ref[...] = alpha * l_ref[...] + jnp.sum(p, axis=1, keepdims=True)
    pb = p.astype(BF16)
    r = qqs[0].shape[0]
    pv = [jnp.dot(pb[h * r:(h + 1) * r], v_of(h), preferred_element_type=F32) for h in range(len(qqs))]
    acc_ref[...] = alpha * acc_ref[...] + jnp.concatenate(pv, axis=0)
    m_ref[...] = m_new


def _stage_scores(qq, ks, mask, s_ref, m_ref, alpha_ref):
    s = lax.dot_general(ks, qq, _NT, preferred_element_type=F32)
    if mask is not None:
        s = jnp.where(mask, s, NEG)
    s_ref[...] = s
    m_old = m_ref[...]
    m_new = jnp.maximum(m_old, jnp.max(s, axis=0, keepdims=True))
    alpha_ref[...] = jnp.exp2(m_old - m_new)
    m_ref[...] = m_new


def _stage_exp(s_ref, p_ref, m_ref):
    p_ref[...] = jnp.exp2(s_ref[...] - m_ref[...]).astype(BF16)


def _stage_pv(vts, p_ref, alpha_ref, acc_ref):
    ones = jnp.ones((SUM_ROWS, vts.shape[1]), BF16)
    pv = jnp.dot(jnp.concatenate([vts, ones], axis=0), p_ref[...], preferred_element_type=F32)
    acc_ref[...] = alpha_ref[...] * acc_ref[...] + pv


def _diff_finish(acc, l, lam, dng, lam_init):
    rows = acc.shape[0] // 2
    o = acc / l
    od = o[:rows] - lam * o[rows:]
    return od * lax.rsqrt(jnp.mean(od * od, axis=-1, keepdims=True) + LN_EPS) * dng * (1.0 - lam_init)


def _sink_attend(qq, ks, vs, sink, mask):
    s = lax.dot_general(qq, ks, _NT, preferred_element_type=F32)
    if mask is not None:
        s = jnp.where(mask, s, NEG)
    m = jnp.maximum(jnp.max(s, axis=1, keepdims=True), sink)
    p = jnp.exp(s - m)
    l = jnp.sum(p, axis=1, keepdims=True) + jnp.exp(sink - m)
    return jnp.dot(p.astype(BF16), vs, preferred_element_type=F32) / l


def _sink_attend_cached(qq, kt, kn, vt, vn, sink):
    s_c = jnp.dot(qq, kt, preferred_element_type=F32)
    s_n = lax.dot_general(qq, kn, _NT, preferred_element_type=F32)
    m = jnp.maximum(jnp.maximum(jnp.max(s_c, axis=1, keepdims=True), jnp.max(s_n, axis=1, keepdims=True)), sink)
    p_c, p_n = jnp.exp(s_c - m), jnp.exp(s_n - m)
    l = jnp.sum(p_c, axis=1, keepdims=True) + jnp.sum(p_n, axis=1, keepdims=True) + jnp.exp(sink - m)
    o = (lax.dot_general(p_c.astype(BF16), vt, _NT, preferred_element_type=F32)
         + jnp.dot(p_n.astype(BF16), vn, preferred_element_type=F32))
    return o / l


def _swa_heads(q, attend, sink_of):
    low = lax.broadcasted_iota(jnp.int32, (q.shape[0], LANES), 1) < SW_HD
    outs = []
    for pair in range(SW_HEADS // 2):
        kv = pair // 2
        slab = q[:, pair * LANES:(pair + 1) * LANES]
        zero = jnp.zeros_like(slab)
        o_even = attend(jnp.where(low, slab, zero), kv == 1, sink_of(2 * pair))
        o_odd = attend(jnp.where(low, zero, slab), kv == 0, sink_of(2 * pair + 1))
        outs.append(jnp.where(low, o_even, o_odd))
    return jnp.concatenate(outs, axis=1)


def _mem_kv_kernel(mem_ref, w_ref, kf_ref, vf_ref, kb_ref, vb_ref):
    h = jnp.dot(mem_ref[...], w_ref[0], preferred_element_type=F32)
    k, v = h[:, :MEM_W], h[:, MEM_W:]
    for hd in range(MEM_HEADS):
        kf_ref[_head_idx(kf_ref, MEM_HEADS, hd, 0)] = k[:, hd * MEM_HD:(hd + 1) * MEM_HD]
        vf_ref[_head_idx(vf_ref, MEM_HEADS, hd, 0)] = v[:, hd * MEM_HD:(hd + 1) * MEM_HD]
    kb_ref[0] = k.astype(BF16)
    vb_ref[0] = v.astype(BF16)


def _mem_kv(mem_bf, w_bf):
    rows = mem_bf.shape[0]
    fshape, fspec = (DEPTH, rows * MEM_HEADS, MEM_HD), pl.BlockSpec((1, rows * MEM_HEADS, MEM_HD), lambda l: (l, 0, 0))
    bshape, bspec = (DEPTH, rows, MEM_W), pl.BlockSpec((1, rows, MEM_W), lambda l: (l, 0, 0))
    return pl.pallas_call(
        _mem_kv_kernel,
        grid=(DEPTH,),
        in_specs=[pl.BlockSpec((rows, D_MODEL), lambda l: (0, 0)),
                  pl.BlockSpec((1, D_MODEL, 2 * MEM_W), lambda l: (l, 0, 0))],
        out_specs=[fspec, fspec, bspec, bspec],
        out_shape=[jax.ShapeDtypeStruct(fshape, F32), jax.ShapeDtypeStruct(fshape, F32),
                   jax.ShapeDtypeStruct(bshape, BF16), jax.ShapeDtypeStruct(bshape, BF16)],
        compiler_params=_params("arbitrary"),
        name="mem_kv",
    )(mem_bf, w_bf)


def _proj_a_kernel(fuse_mem, x_ref, w_ref, c_ref, a_ref, b_ref, *refs):
    if fuse_mem:
        mk_ref, mv_ref, q_ref, kf_ref, vf_ref, kb_ref, vb_ref, g_ref, om_ref = refs
    else:
        q_ref, kf_ref, vf_ref, kb_ref, vb_ref, g_ref, mq_ref, gm_ref = refs
    x = x_ref[...].astype(BF16)
    c, a, b = c_ref[...], a_ref[...], b_ref[...]

    def cols(i):
        return jnp.dot(x, w_ref[:, i * DA_W:(i + 1) * DA_W], preferred_element_type=F32)

    q_ref[...] = (_rope(cols(0), c, a, b) * (DA_QK ** -0.5 * LOG2E)).astype(BF16)
    k = _rope(cols(1), c, a, b)
    kb_ref[...] = k.astype(BF16)
    v = cols(2)
    for hd in range(DA_HEADS):
        vf_ref[_head_idx(vf_ref, DA_HEADS, hd)] = v[:, hd * DA_V:(hd + 1) * DA_V]
    if fuse_mem:
        kf_ref[0] = k.T
        for blk in range(vb_ref.shape[0]):
            vb_ref[blk] = v[blk * DIFF_TILE:(blk + 1) * DIFF_TILE, :].T.astype(BF16)
    else:
        kf_ref[...] = k
        vb_ref[...] = v.astype(BF16)
    g_ref[...] = _silu(cols(3)).astype(BF16)
    mq = cols(4).astype(BF16)
    gm = _silu(cols(5))
    if fuse_mem:
        om_ref[...] = (_mem_attend(mq, _head_cols(mk_ref), _head_cols(mv_ref)) * gm).astype(BF16)
    else:
        mq_ref[...] = mq
        gm_ref[...] = gm.astype(BF16)


def _proj_b_kernel(fuse_mem, x_ref, w_ref, c_ref, a_ref, b_ref, *refs):
    if fuse_mem:
        mk_ref, mv_ref, q_ref, g_ref, om_ref, kf_ref, vf_ref, kb_ref, kr_ref, vb_ref, vr_ref = refs
    else:
        q_ref, g_ref, mq_ref, gm_ref, kf_ref, vf_ref, kb_ref, kr_ref, vb_ref, vr_ref = refs
    x = x_ref[...].astype(BF16)
    c, a, b = c_ref[...], a_ref[...], b_ref[...]

    def cols(i):
        return jnp.dot(x, w_ref[:, i * SW_W:(i + 1) * SW_W], preferred_element_type=F32)

    q_ref[...] = (_rope(cols(0), c, a, b) * (SW_HD ** -0.5)).astype(BF16)
    g_ref[...] = _silu(cols(1)).astype(BF16)
    mq = cols(2).astype(BF16)
    gm = _silu(cols(3))
    if fuse_mem:
        om_ref[...] = (_mem_attend(mq, _head_cols(mk_ref), _head_cols(mv_ref)) * gm).astype(BF16)
    else:
        mq_ref[...] = mq
        gm_ref[...] = gm.astype(BF16)
    kv = jnp.dot(x, w_ref[:, 4 * SW_W:], preferred_element_type=F32)
    k = _rope(kv[:, :LANES], c, a, b)
    v = kv[:, LANES:]
    kf_ref[...] = k
    vf_ref[...] = v
    kb_ref[...] = k.astype(BF16)
    kr_ref[...] = pltpu.roll(k, SW_HD, 1).astype(BF16)
    vb_ref[...] = v.astype(BF16)
    vr_ref[...] = pltpu.roll(v, SW_HD, 1).astype(BF16)


def _project(layer, x, w_bf, tabs, mem=None, *, tm, rows_per_batch=None):
    M = x.shape[0]
    fuse = mem is not None
    per = (rows_per_batch or M) // tm
    row = lambda w: pl.BlockSpec((tm, w), lambda i: (i, 0))
    tab = pl.BlockSpec((tm, LANES), lambda i: (i % per, 0))
    in_specs = [row(D_MODEL), pl.BlockSpec(w_bf.shape, lambda i: (0, 0)), tab, tab, tab]
    args = [x, w_bf, *tabs]
    if fuse:
        mspec = pl.BlockSpec((1, N_MEM, MEM_W), lambda i: (i // per, 0, 0))
        in_specs += [mspec, mspec]
        args += list(mem)
    o = lambda w, dt: (row(w), jax.ShapeDtypeStruct((M, w), dt))
    if layer == "a":
        vt = (pl.BlockSpec((tm // DIFF_TILE, DA_W, DIFF_TILE), lambda i: (i, 0, 0)),
              jax.ShapeDtypeStruct((M // DIFF_TILE, DA_W, DIFF_TILE), BF16))
        kt = (pl.BlockSpec((1, DA_W, tm), lambda i: (i // per, 0, i % per)),
              jax.ShapeDtypeStruct((M // (per * tm), DA_W, per * tm), F32))
        vf = (pl.BlockSpec((tm * DA_HEADS, DA_V), lambda i: (i, 0)), jax.ShapeDtypeStruct((M * DA_HEADS, DA_V), F32))
        outs = [o(DA_W, BF16), kt if fuse else o(DA_W, F32), vf, o(DA_W, BF16), vt if fuse else o(DA_W, BF16),
                o(DA_W, BF16)]
        body = _proj_a_kernel
    else:
        outs = [o(SW_W, BF16), o(SW_W, BF16)]
        body = _proj_b_kernel
    mem_outs = [o(MEM_W, BF16)] if fuse else [o(MEM_W, BF16), o(MEM_W, BF16)]
    if layer == "a":
        outs = outs + mem_outs
    else:
        outs = outs + mem_outs + [o(LANES, F32), o(LANES, F32)] + [o(LANES, BF16)] * 4
    return pl.pallas_call(
        functools.partial(body, fuse),
        grid=(M // tm,),
        in_specs=in_specs,
        out_specs=[s for s, _ in outs],
        out_shape=[t for _, t in outs],
        compiler_params=_params("arbitrary"),
        name="proj_" + layer,
    )(*args)


def _diff_prompt_kernel(lam_init, lamv_ref, q_ref, k_ref, vt_ref, g_ref, om_ref, x_ref, wout_ref,
                        lng_ref, lnb_ref, dng_ref, y_ref, m_ref, alpha_ref, acc_ref, s_ref, p_ref, mix_ref):
    t = DIFF_TILE
    SLOTS = s_ref.shape[0]
    LEAD = PIPE_LEAD
    i = pl.program_id(1)
    heads = [slice(h * DA_V, (h + 1) * DA_V) for h in range(DA_HEADS)]
    lane = lax.broadcasted_iota(jnp.int32, (t, DA_V), 1)
    units = []
    for hs in heads:
        qh = q_ref[:, hs]
        zero = jnp.zeros_like(qh)
        units += [(hs, jnp.where(lane < DA_QK, qh, zero)), (hs, jnp.where(lane >= DA_QK, qh, zero))]
    m_ref[...] = jnp.full(m_ref.shape, -jnp.inf, F32)
    acc_ref[...] = jnp.zeros(acc_ref.shape, F32)

    n = len(units)
    assert n == SLOTS and LEAD + 1 < n

    def exp_stage(u):
        _stage_exp(s_ref.at[u], p_ref.at[u], m_ref.at[u])

    def pv_stage(j, u):
        _stage_pv(vt_ref[0, j, units[u][0], :], p_ref.at[u], alpha_ref.at[u], acc_ref.at[u])

    def block(j, mask, first):
        rows = pl.ds(pl.multiple_of(j * t, t), t)
        for u, (hs, qq) in enumerate(units):
            _stage_scores(qq, k_ref[0, rows, hs], mask, s_ref.at[u], m_ref.at[u], alpha_ref.at[u])
            if u >= LEAD:
                exp_stage(u - LEAD)
            elif not first:
                exp_stage(u - LEAD + n)
            if u >= LEAD + 1:
                pv_stage(j, u - LEAD - 1)
            elif not first:
                pv_stage(j - 1, u - LEAD - 1 + n)

    def body(j, carry):
        block(j, None, False)
        return carry

    kc = lax.broadcasted_iota(jnp.int32, (t, t), 0) // CHUNK
    qc = lax.broadcasted_iota(jnp.int32, (t, t), 1) // CHUNK
    diag_mask = kc <= qc
    block(0, diag_mask | (i > 0), True)
    lax.fori_loop(1, i, body, 0)

    @pl.when(i >= 1)
    def _():
        block(i, diag_mask, False)

    for u in range(n - LEAD, n):
        exp_stage(u)
        pv_stage(i, u - 1)
    pv_stage(i, n - 1)

    lam = _lambda(lamv_ref[...], lam_init)
    for h, hs in enumerate(heads):
        o1 = acc_ref[2 * h, :DA_V, :] / acc_ref[2 * h, DA_V:DA_V + 1, :]
        o2 = acc_ref[2 * h + 1, :DA_V, :] / acc_ref[2 * h + 1, DA_V:DA_V + 1, :]
        od = (o1 - lam * o2).T
        od = od * lax.rsqrt(jnp.mean(od * od, axis=-1, keepdims=True) + LN_EPS) * dng_ref[...] * (1.0 - lam_init)
        mix_ref[:, hs] = (od * g_ref[:, hs].astype(F32)).astype(BF16)
    mix_ref[:, DA_W:] = om_ref[...]
    y_ref[...] = _merge(x_ref[...], mix_ref[...], wout_ref[...], lng_ref[...], lnb_ref[...])


def _diff_prompt(lam_init, lamv, q, k, vt, g, om, x, w_out, ln_g, ln_b, dng, *, B, S):
    t = DIFF_TILE
    nq = S // t
    row = lambda w: pl.BlockSpec((t, w), lambda b, i: (b * nq + i, 0))
    full = lambda shape: pl.BlockSpec(shape, lambda b, i: (0,) * len(shape))
    return pl.pallas_call(
        functools.partial(_diff_prompt_kernel, lam_init),
        grid=(B, nq),
        in_specs=[full((4, DA_QK)), row(DA_W), pl.BlockSpec((1, S, DA_W), lambda b, i: (b, 0, 0)),
                  pl.BlockSpec((1, nq, DA_W, t), lambda b, i: (b, 0, 0, 0)), row(DA_W), row(MEM_W), row(D_MODEL),
                  full((MIX_W, D_MODEL)), full((1, D_MODEL)), full((1, D_MODEL)), full((1, DA_V))],
        out_specs=row(D_MODEL),
        out_shape=jax.ShapeDtypeStruct((B * S, D_MODEL), F32),
        scratch_shapes=[pltpu.VMEM((2 * DA_HEADS, 1, t), F32), pltpu.VMEM((2 * DA_HEADS, 1, t), F32),
                        pltpu.VMEM((2 * DA_HEADS, DA_V + SUM_ROWS, t), F32),
                        pltpu.VMEM((2 * DA_HEADS, t, t), F32), pltpu.VMEM((2 * DA_HEADS, t, t), BF16),
                        pltpu.VMEM((t, MIX_W), BF16)],
        compiler_params=_params("arbitrary", "arbitrary"),
        name="diff_prompt",
    )(lamv, q, k.reshape(B, S, DA_W), vt.reshape(B, nq, DA_W, t), g, om, x, w_out, ln_g, ln_b, dng)


def _swa_prompt_kernel(tq, sinks_ref, q_ref, k_ref, kr_ref, v_ref, vr_ref, g_ref, om_ref, x_ref, wout_ref,
                       lng_ref, lnb_ref, y_ref):
    i = pl.program_id(1)
    nk = tq + WINDOW
    start = pl.multiple_of(jnp.maximum(i * tq - WINDOW, 0), CHUNK)
    rows = pl.ds(start, nk)
    shift = jnp.where(i == 0, 0, WINDOW // CHUNK)
    kc = lax.broadcasted_iota(jnp.int32, (tq, nk), 1) // CHUNK - shift
    qc = lax.broadcasted_iota(jnp.int32, (tq, nk), 0) // CHUNK
    mask = (kc <= qc) & (kc >= qc - WINDOW // CHUNK)
    k, kr, v, vr = k_ref[0, rows, :], kr_ref[0, rows, :], v_ref[0, rows, :], vr_ref[0, rows, :]
    attend = lambda qq, swapped, sink: _sink_attend(qq, kr if swapped else k, vr if swapped else v, sink, mask)
    o = _swa_heads(q_ref[...], attend, lambda h: sinks_ref[h])
    mix = jnp.concatenate([(o * g_ref[...].astype(F32)).astype(BF16), om_ref[...]], axis=1)
    y_ref[...] = _merge(x_ref[...], mix, wout_ref[...], lng_ref[...], lnb_ref[...])


def _swa_prompt(sinks, q, k, kr, v, vr, g, om, x, w_out, ln_g, ln_b, *, B, S, tq):
    nq = S // tq
    row = lambda w: pl.BlockSpec((tq, w), lambda b, i: (b * nq + i, 0))
    full = lambda shape: pl.BlockSpec(shape, lambda b, i: (0,) * len(shape))
    seq = pl.BlockSpec((1, S, LANES), lambda b, i: (b, 0, 0))
    r3 = lambda t: t.reshape(B, S, LANES)
    return pl.pallas_call(
        functools.partial(_swa_prompt_kernel, tq),
        grid=(B, nq),
        in_specs=[pl.BlockSpec(memory_space=pltpu.SMEM), row(SW_W), seq, seq, seq, seq, row(SW_W), row(MEM_W),
                  row(D_MODEL), full((MIX_W, D_MODEL)), full((1, D_MODEL)), full((1, D_MODEL))],
        out_specs=row(D_MODEL),
        out_shape=jax.ShapeDtypeStruct((B * S, D_MODEL), F32),
        compiler_params=_params("arbitrary", "arbitrary"),
        name="swa_prompt",
    )(sinks, q, r3(k), r3(kr), r3(v), r3(vr), g, om, x, w_out, ln_g, ln_b)


def _sample_tail(b, last, o_first, g_ref, mq_ref, gm_ref, cmk_ref, cmv_ref, x_ref, wout_ref, lng_ref, lnb_ref,
                 y_ref, mix_ref, T):
    om = _mem_attend(mq_ref[...], _head_rows(cmk_ref, MEM_HEADS, 0, 0), _head_rows(cmv_ref, MEM_HEADS, 0, 0))
    om = om * gm_ref[...].astype(F32)
    mix = jnp.concatenate([(o_first * g_ref[...].astype(F32)).astype(BF16), om.astype(BF16)], axis=1)
    mix_ref[pl.ds(pl.multiple_of(b * T, T), T), :] = mix

    @pl.when(last)
    def _():
        y_ref[...] = _merge(x_ref[...], mix_ref[...], wout_ref[...], lng_ref[...], lnb_ref[...])


def _diff_sample_kernel(lam_init, T, lamv_ref, q_ref, kn_ref, vn_ref, ckt_ref, cv_ref, g_ref, mq_ref, gm_ref,
                        cmk_ref, cmv_ref, x_ref, wout_ref, lng_ref, lnb_ref, dng_ref, y_ref,
                        m_ref, l_ref, acc_ref, mix_ref):
    b, j = pl.program_id(0), pl.program_id(1)
    nb, nj = pl.num_programs(0), pl.num_programs(1)
    heads = [slice(h * DA_V, (h + 1) * DA_V) for h in range(DA_HEADS)]
    qqs = [_split_components(q_ref[:, hs]) for hs in heads]

    @pl.when(j == 0)
    def _():
        m_ref[...] = jnp.full(m_ref.shape, -jnp.inf, F32)
        l_ref[...] = jnp.zeros(l_ref.shape, F32)
        acc_ref[...] = jnp.zeros(acc_ref.shape, F32)
        _flash_update(qqs, lambda h: kn_ref[:, heads[h]], lambda h: vn_ref[:, heads[h]], False, m_ref, l_ref, acc_ref)

    _flash_update(qqs, lambda h: ckt_ref[0, heads[h], :].astype(BF16), _head_rows(cv_ref, DA_HEADS, 0), True,
                  m_ref, l_ref, acc_ref)

    @pl.when(j == nj - 1)
    def _():
        lam = _lambda(lamv_ref[...], lam_init)
        rows = [slice(h * 2 * T, (h + 1) * 2 * T) for h in range(DA_HEADS)]
        od = jnp.concatenate([_diff_finish(acc_ref[r, :], l_ref[r, :], lam, dng_ref[...], lam_init) for r in rows],
                             axis=1)
        _sample_tail(b, b == nb - 1, od, g_ref, mq_ref, gm_ref, cmk_ref, cmv_ref, x_ref, wout_ref, lng_ref,
                     lnb_ref, y_ref, mix_ref, T)


def _diff_sample(lam_init, lamv, q, kn, vn, ckt, cv, g, mq, gm, cmk, cmv, layer, x, w_out, ln_g, ln_b, dng, *,
                 NB, T, tk):
    P = ckt.shape[2]
    row = lambda w: pl.BlockSpec((T, w), lambda b, j: (b, 0))
    full = lambda shape: pl.BlockSpec(shape, lambda b, j: (0,) * len(shape))
    memc = pl.BlockSpec((1, 1, N_MEM * MEM_HEADS, MEM_HD), lambda b, j: (layer, b, 0, 0))
    return pl.pallas_call(
        functools.partial(_diff_sample_kernel, lam_init, T),
        grid=(NB, P // tk),
        in_specs=[full((4, DA_QK)), row(DA_W), row(DA_W), row(DA_W),
                  pl.BlockSpec((1, DA_W, tk), lambda b, j: (b, 0, j)),
                  pl.BlockSpec((1, tk * DA_HEADS, DA_V), lambda b, j: (b, j, 0)),
                  row(DA_W), row(MEM_W), row(MEM_W), memc, memc, full((NB * T, D_MODEL)), full((MIX_W, D_MODEL)),
                  full((1, D_MODEL)), full((1, D_MODEL)), full((1, DA_V))],
        out_specs=full((NB * T, D_MODEL)),
        out_shape=jax.ShapeDtypeStruct((NB * T, D_MODEL), F32),
        scratch_shapes=[pltpu.VMEM((DA_HEADS * 2 * T, 1), F32), pltpu.VMEM((DA_HEADS * 2 * T, 1), F32),
                        pltpu.VMEM((DA_HEADS * 2 * T, DA_V), F32), pltpu.VMEM((NB * T, MIX_W), BF16)],
        compiler_params=_params("arbitrary", "arbitrary"),
        name="diff_sample",
    )(lamv, q, kn, vn, ckt, cv, g, mq, gm, cmk, cmv, x, w_out, ln_g, ln_b, dng)


def _swa_sample_kernel(T, sinks_ref, q_ref, kn_ref, knr_ref, vn_ref, vnr_ref, ckt_ref, cvt_ref, g_ref, mq_ref, gm_ref,
                       cmk_ref, cmv_ref, x_ref, wout_ref, lng_ref, lnb_ref, y_ref, mix_ref):
    b = pl.program_id(0)
    swap = lambda t: jnp.concatenate([t[SW_HD:], t[:SW_HD]], axis=0).astype(BF16)
    ckt, cvt = ckt_ref[0], cvt_ref[0]
    kt, ktr, vt, vtr = ckt.astype(BF16), swap(ckt), cvt.astype(BF16), swap(cvt)
    kn, knr, vn, vnr = kn_ref[...], knr_ref[...], vn_ref[...], vnr_ref[...]
    attend = lambda qq, swapped, sink: _sink_attend_cached(
        qq, ktr if swapped else kt, knr if swapped else kn, vtr if swapped else vt, vnr if swapped else vn, sink)
    o = _swa_heads(q_ref[...], attend, lambda h: sinks_ref[h])
    _sample_tail(b, b == pl.num_programs(0) - 1, o, g_ref, mq_ref, gm_ref, cmk_ref, cmv_ref, x_ref, wout_ref,
                 lng_ref, lnb_ref, y_ref, mix_ref, T)


def _swa_sample(sinks, q, kn, knr, vn, vnr, ckt, cvt, g, mq, gm, cmk, cmv, layer, x, w_out, ln_g, ln_b, *, NB, T):
    W = ckt.shape[2]
    row = lambda w: pl.BlockSpec((T, w), lambda b: (b, 0))
    full = lambda shape: pl.BlockSpec(shape, lambda b: (0,) * len(shape))
    cache = pl.BlockSpec((1, LANES, W), lambda b: (b, 0, 0))
    memc = pl.BlockSpec((1, 1, N_MEM * MEM_HEADS, MEM_HD), lambda b: (layer, b, 0, 0))
    return pl.pallas_call(
        functools.partial(_swa_sample_kernel, T),
        grid=(NB,),
        in_specs=[pl.BlockSpec(memory_space=pltpu.SMEM), row(SW_W), row(LANES), row(LANES), row(LANES), row(LANES),
                  cache, cache, row(SW_W), row(MEM_W), row(MEM_W), memc, memc, full((NB * T, D_MODEL)),
                  full((MIX_W, D_MODEL)), full((1, D_MODEL)), full((1, D_MODEL))],
        out_specs=full((NB * T, D_MODEL)),
        out_shape=jax.ShapeDtypeStruct((NB * T, D_MODEL), F32),
        scratch_shapes=[pltpu.VMEM((NB * T, MIX_W), BF16)],
        compiler_params=_params("arbitrary"),
        name="swa_sample",
    )(sinks, q, kn, knr, vn, vnr, ckt, cvt, g, mq, gm, cmk, cmv, x, w_out, ln_g, ln_b)


def _rope_tables(pos):
    T = pos.shape[0]
    inv = ROPE_THETA ** (-jnp.arange(ROPE_HALF, dtype=F32) / ROPE_HALF)
    ang = pos.astype(F32)[:, None] * inv[None, :]
    cos, sin = jnp.cos(ang), jnp.sin(ang)
    rest = SW_HD - 2 * ROPE_HALF
    c = jnp.concatenate([cos, cos, jnp.ones((T, rest), F32)], axis=1)
    a = jnp.concatenate([-sin, jnp.zeros((T, SW_HD - ROPE_HALF), F32)], axis=1)
    b = jnp.concatenate([jnp.zeros((T, ROPE_HALF), F32), sin, jnp.zeros((T, rest), F32)], axis=1)
    return tuple(jnp.tile(t, (1, LANES // SW_HD)) for t in (c, a, b))


def _feature_major(t):
    n = t.ndim
    return jnp.transpose(t, (0,) + tuple(range(2, n)) + (1,)).reshape(t.shape[0], -1, t.shape[1])


def _rows_major(t, tail):
    n = len(tail)
    return jnp.transpose(t.reshape((t.shape[0],) + tail + (t.shape[2],)), (0, n + 1) + tuple(range(1, n + 1)))


def kernel(x_prompt, x_sample, mem_prompt, cache_diff_k, cache_diff_v, cache_swa_k, cache_swa_v, cache_mem_k,
           cache_mem_v, w_in_a, lam_q1, lam_k1, lam_q2, lam_k2, diff_norm_g, w_in_b, sinks, w_kv_shared, w_mem_kv,
           w_out, ln_g, ln_b):
    B, S, _ = x_prompt.shape
    NB, T, _ = x_sample.shape
    P = cache_diff_k.shape[2]
    assert w_in_a.shape[0] == 1 and w_in_b.shape[0] == 1 and w_out.shape[0] == DEPTH

    tabs_p = _rope_tables(jnp.arange(S))
    tabs_s = _rope_tables(jnp.tile(P + jnp.arange(T), NB))
    xp = x_prompt.reshape(B * S, D_MODEL)
    xs = x_sample.reshape(NB * T, D_MODEL)
    wa = w_in_a[0].astype(BF16)
    wb = jnp.concatenate([w_in_b[0], w_kv_shared], axis=1).astype(BF16)
    wo = w_out.astype(BF16)
    lng = ln_g.reshape(DEPTH, 1, D_MODEL)
    lnb = ln_b.reshape(DEPTH, 1, D_MODEL)
    lamv = jnp.concatenate([lam_q1, lam_k1, lam_q2, lam_k2], axis=0)
    dng = diff_norm_g.reshape(1, DA_V)
    lam_init = 0.8 - 0.6 * math.exp(-0.3 * 0)

    mkf, mvf, mkb, mvb = _mem_kv(mem_prompt.reshape(B * N_MEM, D_MODEL).astype(BF16), w_mem_kv.astype(BF16))
    mem_b = lambda l: (mkb[l].reshape(B, N_MEM, MEM_W), mvb[l].reshape(B, N_MEM, MEM_W))
    head_rows = lambda t: t.reshape(t.shape[:-3] + (t.shape[-3] * t.shape[-2], t.shape[-1]))
    cmk, cmv = head_rows(cache_mem_k), head_rows(cache_mem_v)

    q, kt, vf, kb, vb, g, om = _project("a", xp, wa, tabs_p, mem_b(0), tm=512, rows_per_batch=S)
    xp1 = _diff_prompt(lam_init, lamv, q, kb, vb, g, om, xp, wo[0], lng[0], lnb[0], dng, B=B, S=S)
    qs, kfs, vfs, kbs, vbs, gs_, mqs, gms = _project("a", xs, wa, tabs_s, tm=NB * T)
    xs1 = _diff_sample(lam_init, lamv, qs, kbs, vbs, _feature_major(cache_diff_k[0]), head_rows(cache_diff_v[0]), gs_,
                       mqs, gms, cmk, cmv, 0, xs, wo[0], lng[0], lnb[0], dng, NB=NB, T=T, tk=1024)

    q, g, om, skf, svf, kb1, kr1, vb1, vr1 = _project("b", xp1, wb, tabs_p, mem_b(1), tm=512, rows_per_batch=S)
    yp = _swa_prompt(sinks[0], q, kb1, kr1, vb1, vr1, g, om, xp1, wo[1], lng[1], lnb[1], B=B, S=S, tq=256)
    qs, gs_, mqs, gms, skfs, svfs, kbs1, krs1, vbs1, vrs1 = _project("b", xs1, wb, tabs_s, tm=NB * T)
    ys = _swa_sample(sinks[0], qs, kbs1, krs1, vbs1, vrs1, _feature_major(cache_swa_k), _feature_major(cache_swa_v),
                     gs_, mqs, gms, cmk, cmv, 1, xs1, wo[1], lng[1], lnb[1], NB=NB, T=T)

    wr_p = min(WINDOW, S)
    window = lambda t: t.reshape(B, S, LANES)[:, S - wr_p:, :].reshape(B, wr_p, SW_KV, SW_HD)
    swa_kp, swa_vp = window(skf), window(svf)
    roll = lambda c, n: jnp.concatenate([c, n.reshape(NB, T, SW_KV, SW_HD)], axis=1)[:, T:]
    return (yp.reshape(B, S, D_MODEL), ys.reshape(NB, T, D_MODEL),
            _rows_major(kt, (DA_HEADS, 2, DA_QK))[None], vf.reshape(1, B, S, DA_HEADS, DA_V),
            kfs.reshape(1, NB, T, DA_HEADS, 2, DA_QK), vfs.reshape(1, NB, T, DA_HEADS, DA_V),
            swa_kp, swa_vp, roll(cache_swa_k, skfs), roll(cache_swa_v, svfs),
            mkf.reshape(DEPTH, B, N_MEM, MEM_HEADS, MEM_HD), mvf.reshape(DEPTH, B, N_MEM, MEM_HEADS, MEM_HD))
```

```python
import functools
import math

import jax
import jax.numpy as jnp
from jax import lax
from jax.experimental import pallas as pl
from jax.experimental.pallas import tpu as pltpu

D_MODEL = 1024
CHUNK = 64
N_MEM = 256
DA_HEADS = 4
DA_QK = 64
DA_V = 128
DA_W = 512
SW_HEADS = 8
SW_KV = 2
SW_GROUP = 4
SW_HD = 64
SW_W = 512
WINDOW = 128
MEM_HEADS = 4
MEM_HD = 128
MEM_W = 512
MIX_W = 1024
ROPE_THETA = 500000.0
ROPE_HALF = 8
DEPTH = 2
DN_ALPHA = (2 * DEPTH) ** 0.25
LN_EPS = 1e-5
NEG = -1e30
LANES = 128
DIFF_TILE = 256
PIPE_LEAD = 4
SWA_TILE = 128
SUM_ROWS = 16
LOG2E = math.log2(math.e)

F32 = jnp.float32
BF16 = jnp.bfloat16
VMEM_LIMIT = 48 * 1024 * 1024

_NT = (((1,), (1,)), ((), ()))


def _params(*sem):
    return pltpu.CompilerParams(dimension_semantics=sem, vmem_limit_bytes=VMEM_LIMIT)


def _rope(x, c, a, b):
    outs = []
    for i in range(x.shape[1] // LANES):
        blk = x[:, i * LANES:(i + 1) * LANES]
        outs.append(blk * c + pltpu.roll(blk, LANES - ROPE_HALF, 1) * a + pltpu.roll(blk, ROPE_HALF, 1) * b)
    return outs[0] if len(outs) == 1 else jnp.concatenate(outs, axis=1)


def _silu(g):
    return g * (1.0 / (1.0 + jnp.exp(-g)))


def _layer_norm(z, g, b):
    mu = jnp.mean(z, axis=-1, keepdims=True)
    d = z - mu
    var = jnp.mean(d * d, axis=-1, keepdims=True)
    return d * lax.rsqrt(var + LN_EPS) * g + b


def _mem_attend(mq, mk_of, mv_of):
    outs = []
    for h in range(MEM_HEADS):
        sl = slice(h * MEM_HD, (h + 1) * MEM_HD)
        s = lax.dot_general(mq[:, sl], mk_of(h), _NT, preferred_element_type=F32) * (MEM_HD ** -0.5)
        p = jnp.exp(s - jnp.max(s, axis=1, keepdims=True))
        l = jnp.sum(p, axis=1, keepdims=True)
        outs.append(jnp.dot(p.astype(BF16), mv_of(h), preferred_element_type=F32) / l)
    return jnp.concatenate(outs, axis=1)


def _head_cols(ref):
    return lambda h: ref[0, :, h * MEM_HD:(h + 1) * MEM_HD]


def _head_idx(ref, heads, h, *lead):
    return lead + (pl.ds(h, ref.shape[-2] // heads, stride=heads), slice(None))


def _head_rows(ref, heads, *lead):
    return lambda h: ref[_head_idx(ref, heads, h, *lead)].astype(BF16)


def _merge(x, mix, w_out, ln_g, ln_b):
    return _layer_norm(DN_ALPHA * x + jnp.dot(mix, w_out, preferred_element_type=F32), ln_g, ln_b)


def _lambda(lamv, lam_init):
    e1 = jnp.exp(jnp.sum(lamv[0:1, :] * lamv[1:2, :], axis=1, keepdims=True))
    e2 = jnp.exp(jnp.sum(lamv[2:3, :] * lamv[3:4, :], axis=1, keepdims=True))
    return e1 - e2 + lam_init


def _split_components(qh):
    lane = lax.broadcasted_iota(jnp.int32, qh.shape, 1)
    zero = jnp.zeros_like(qh)
    return jnp.concatenate([jnp.where(lane < DA_QK, qh, zero), jnp.where(lane >= DA_QK, qh, zero)], axis=0)


def _flash_update(qqs, k_of, v_of, keys_on_lanes, m_ref, l_ref, acc_ref):
    if keys_on_lanes:
        s = [jnp.dot(qq, k_of(h), preferred_element_type=F32) for h, qq in enumerate(qqs)]
    else:
        s = [lax.dot_general(qq, k_of(h), _NT, preferred_element_type=F32) for h, qq in enumerate(qqs)]
    s = jnp.concatenate(s, axis=0)
    m_old = m_ref[...]
    m_new = jnp.maximum(m_old, jnp.max(s, axis=1, keepdims=True))
    alpha = jnp.exp2(m_old - m_new)
    p = jnp.exp2(s - m_new)
    l_ref[...] = alpha * l_ref[...] + jnp.sum(p, axis=1, keepdims=True)
    pb = p.astype(BF16)
    r = qqs[0].shape[0]
    pv = [jnp.dot(pb[h * r:(h + 1) * r], v_of(h), preferred_element_type=F32) for h in range(len(qqs))]
    acc_ref[...] = alpha * acc_ref[...] + jnp.concatenate(pv, axis=0)
    m_ref[...] = m_new


def _stage_scores(qq, ks, mask, s_ref, m_ref, alpha_ref):
    s = lax.dot_general(ks, qq, _NT, preferred_element_type=F32)
    if mask is not None:
        s = jnp.where(mask, s, NEG)
    s_ref[...] = s
    m_old = m_ref[...]
    m_new = jnp.maximum(m_old, jnp.max(s, axis=0, keepdims=True))
    alpha_ref[...] = jnp.exp2(m_old - m_new)
    m_ref[...] = m_new


def _stage_exp(s_ref, p_ref, m_ref):
    p_ref[...] = jnp.exp2(s_ref[...] - m_ref[...]).astype(BF16)


def _stage_pv(vts, p_ref, alpha_ref, acc_ref):
    ones = jnp.ones((SUM_ROWS, vts.shape[1]), BF16)
    pv = jnp.dot(jnp.concatenate([vts, ones], axis=0), p_ref[...], preferred_element_type=F32)
    acc_ref[...] = alpha_ref[...] * acc_ref[...] + pv


def _diff_finish(acc, l, lam, dng, lam_init):
    rows = acc.shape[0] // 2
    o = acc / l
    od = o[:rows] - lam * o[rows:]
    return od * lax.rsqrt(jnp.mean(od * od, axis=-1, keepdims=True) + LN_EPS) * dng * (1.0 - lam_init)


def _sink_attend_cached(qq, kt, kn, vt, vn, sink):
    s_c = jnp.dot(qq, kt, preferred_element_type=F32)
    s_n = lax.dot_general(qq, kn, _NT, preferred_element_type=F32)
    m = jnp.maximum(jnp.maximum(jnp.max(s_c, axis=1, keepdims=True), jnp.max(s_n, axis=1, keepdims=True)), sink)
    p_c, p_n = jnp.exp2(s_c - m), jnp.exp2(s_n - m)
    l = jnp.sum(p_c, axis=1, keepdims=True) + jnp.sum(p_n, axis=1, keepdims=True) + jnp.exp2(sink - m)
    o = (lax.dot_general(p_c.astype(BF16), vt, _NT, preferred_element_type=F32)
         + jnp.dot(p_n.astype(BF16), vn, preferred_element_type=F32))
    return o / l


def _window_scores(qq4, ks, mask):
    return jnp.where(mask, lax.dot_general(ks, qq4, _NT, preferred_element_type=F32), NEG)


def _window_attend(s, vts, sinks4):
    m = jnp.maximum(jnp.max(s, axis=0, keepdims=True), sinks4)
    p = jnp.exp2(s - m).astype(BF16)
    ones = jnp.ones((SUM_ROWS, vts.shape[1]), BF16)
    o = jnp.dot(jnp.concatenate([vts, ones], axis=0), p, preferred_element_type=F32)
    return o[:LANES] / (o[LANES:LANES + 1] + jnp.exp2(sinks4 - m))


def _swa_heads(q, attend, sink_of):
    low = lax.broadcasted_iota(jnp.int32, (q.shape[0], LANES), 1) < SW_HD
    outs = []
    for pair in range(SW_HEADS // 2):
        kv = pair // 2
        slab = q[:, pair * LANES:(pair + 1) * LANES]
        zero = jnp.zeros_like(slab)
        o_even = attend(jnp.where(low, slab, zero), kv == 1, sink_of(2 * pair))
        o_odd = attend(jnp.where(low, zero, slab), kv == 0, sink_of(2 * pair + 1))
        outs.append(jnp.where(low, o_even, o_odd))
    return jnp.concatenate(outs, axis=1)


def _mem_kv_kernel(mem_ref, w_ref, kf_ref, vf_ref, kb_ref, vb_ref):
    h = jnp.dot(mem_ref[...], w_ref[0], preferred_element_type=F32)
    k, v = h[:, :MEM_W], h[:, MEM_W:]
    for hd in range(MEM_HEADS):
        kf_ref[_head_idx(kf_ref, MEM_HEADS, hd, 0)] = k[:, hd * MEM_HD:(hd + 1) * MEM_HD]
        vf_ref[_head_idx(vf_ref, MEM_HEADS, hd, 0)] = v[:, hd * MEM_HD:(hd + 1) * MEM_HD]
    kb_ref[0] = k.astype(BF16)
    vb_ref[0] = v.astype(BF16)


def _mem_kv(mem_bf, w_bf):
    rows = mem_bf.shape[0]
    fshape, fspec = (DEPTH, rows * MEM_HEADS, MEM_HD), pl.BlockSpec((1, rows * MEM_HEADS, MEM_HD), lambda l: (l, 0, 0))
    bshape, bspec = (DEPTH, rows, MEM_W), pl.BlockSpec((1, rows, MEM_W), lambda l: (l, 0, 0))
    return pl.pallas_call(
        _mem_kv_kernel,
        grid=(DEPTH,),
        in_specs=[pl.BlockSpec((rows, D_MODEL), lambda l: (0, 0)),
                  pl.BlockSpec((1, D_MODEL, 2 * MEM_W), lambda l: (l, 0, 0))],
        out_specs=[fspec, fspec, bspec, bspec],
        out_shape=[jax.ShapeDtypeStruct(fshape, F32), jax.ShapeDtypeStruct(fshape, F32),
                   jax.ShapeDtypeStruct(bshape, BF16), jax.ShapeDtypeStruct(bshape, BF16)],
        compiler_params=_params("arbitrary"),
        name="mem_kv",
    )(mem_bf, w_bf)


def _proj_a_kernel(fuse_mem, x_ref, w_ref, c_ref, a_ref, b_ref, *refs):
    if fuse_mem:
        mk_ref, mv_ref, q_ref, kf_ref, vf_ref, kb_ref, vb_ref, g_ref, om_ref = refs
    else:
        q_ref, kf_ref, vf_ref, kb_ref, vb_ref, g_ref, mq_ref, gm_ref = refs
    x = x_ref[...].astype(BF16)
    c, a, b = c_ref[...], a_ref[...], b_ref[...]

    def cols(i):
        return jnp.dot(x, w_ref[:, i * DA_W:(i + 1) * DA_W], preferred_element_type=F32)

    q_ref[...] = (_rope(cols(0), c, a, b) * (DA_QK ** -0.5 * LOG2E)).astype(BF16)
    k = _rope(cols(1), c, a, b)
    kb_ref[...] = k.astype(BF16)
    v = cols(2)
    for hd in range(DA_HEADS):
        vf_ref[_head_idx(vf_ref, DA_HEADS, hd)] = v[:, hd * DA_V:(hd + 1) * DA_V]
    if fuse_mem:
        kf_ref[0] = k.T
        for blk in range(vb_ref.shape[0]):
            vb_ref[blk] = v[blk * DIFF_TILE:(blk + 1) * DIFF_TILE, :].T.astype(BF16)
    else:
        kf_ref[...] = k
        vb_ref[...] = v.astype(BF16)
    g_ref[...] = _silu(cols(3)).astype(BF16)
    mq = cols(4).astype(BF16)
    gm = _silu(cols(5))
    if fuse_mem:
        om_ref[...] = (_mem_attend(mq, _head_cols(mk_ref), _head_cols(mv_ref)) * gm).astype(BF16)
    else:
        mq_ref[...] = mq
        gm_ref[...] = gm.astype(BF16)


def _proj_b_kernel(fuse_mem, x_ref, w_ref, c_ref, a_ref, b_ref, *refs):
    if fuse_mem:
        mk_ref, mv_ref, q_ref, g_ref, om_ref, kf_ref, vf_ref, kb_ref, kr_ref, vb_ref, vr_ref = refs
    else:
        q_ref, g_ref, mq_ref, gm_ref, kf_ref, vf_ref, kb_ref, kr_ref, vb_ref, vr_ref = refs
    x = x_ref[...].astype(BF16)
    c, a, b = c_ref[...], a_ref[...], b_ref[...]

    def cols(i):
        return jnp.dot(x, w_ref[:, i * SW_W:(i + 1) * SW_W], preferred_element_type=F32)

    q_ref[...] = (_rope(cols(0), c, a, b) * (SW_HD ** -0.5 * LOG2E)).astype(BF16)
    g_ref[...] = _silu(cols(1)).astype(BF16)
    mq = cols(2).astype(BF16)
    gm = _silu(cols(3))
    if fuse_mem:
        om_ref[...] = (_mem_attend(mq, _head_cols(mk_ref), _head_cols(mv_ref)) * gm).astype(BF16)
    else:
        mq_ref[...] = mq
        gm_ref[...] = gm.astype(BF16)
    kv = jnp.dot(x, w_ref[:, 4 * SW_W:], preferred_element_type=F32)
    k = _rope(kv[:, :LANES], c, a, b)
    v = kv[:, LANES:]
    kf_ref[...] = k
    vf_ref[...] = v
    kb_ref[...] = k.astype(BF16)
    kr_ref[...] = pltpu.roll(k, SW_HD, 1).astype(BF16)
    vr = pltpu.roll(v, SW_HD, 1)
    if fuse_mem:
        for blk in range(vb_ref.shape[0]):
            rows = slice(blk * SWA_TILE, (blk + 1) * SWA_TILE)
            vb_ref[blk] = v[rows, :].T.astype(BF16)
            vr_ref[blk] = vr[rows, :].T.astype(BF16)
    else:
        vb_ref[...] = v.astype(BF16)
        vr_ref[...] = vr.astype(BF16)


def _project(layer, x, w_bf, tabs, mem=None, *, tm, rows_per_batch=None):
    M = x.shape[0]
    fuse = mem is not None
    per = (rows_per_batch or M) // tm
    row = lambda w: pl.BlockSpec((tm, w), lambda i: (i, 0))
    tab = pl.BlockSpec((tm, LANES), lambda i: (i % per, 0))
    in_specs = [row(D_MODEL), pl.BlockSpec(w_bf.shape, lambda i: (0, 0)), tab, tab, tab]
    args = [x, w_bf, *tabs]
    if fuse:
        mspec = pl.BlockSpec((1, N_MEM, MEM_W), lambda i: (i // per, 0, 0))
        in_specs += [mspec, mspec]
        args += list(mem)
    o = lambda w, dt: (row(w), jax.ShapeDtypeStruct((M, w), dt))
    if layer == "a":
        vt = (pl.BlockSpec((tm // DIFF_TILE, DA_W, DIFF_TILE), lambda i: (i, 0, 0)),
              jax.ShapeDtypeStruct((M // DIFF_TILE, DA_W, DIFF_TILE), BF16))
        kt = (pl.BlockSpec((1, DA_W, tm), lambda i: (i // per, 0, i % per)),
              jax.ShapeDtypeStruct((M // (per * tm), DA_W, per * tm), F32))
        vf = (pl.BlockSpec((tm * DA_HEADS, DA_V), lambda i: (i, 0)), jax.ShapeDtypeStruct((M * DA_HEADS, DA_V), F32))
        outs = [o(DA_W, BF16), kt if fuse else o(DA_W, F32), vf, o(DA_W, BF16), vt if fuse else o(DA_W, BF16),
                o(DA_W, BF16)]
        body = _proj_a_kernel
    else:
        outs = [o(SW_W, BF16), o(SW_W, BF16)]
        body = _proj_b_kernel
    mem_outs = [o(MEM_W, BF16)] if fuse else [o(MEM_W, BF16), o(MEM_W, BF16)]
    if layer == "a":
        outs = outs + mem_outs
    else:
        vt = (pl.BlockSpec((tm // SWA_TILE, LANES, SWA_TILE), lambda i: (i, 0, 0)),
              jax.ShapeDtypeStruct((M // SWA_TILE, LANES, SWA_TILE), BF16))
        values = [vt, vt] if fuse else [o(LANES, BF16)] * 2
        outs = outs + mem_outs + [o(LANES, F32), o(LANES, F32)] + [o(LANES, BF16)] * 2 + values
    return pl.pallas_call(
        functools.partial(body, fuse),
        grid=(M // tm,),
        in_specs=in_specs,
        out_specs=[s for s, _ in outs],
        out_shape=[t for _, t in outs],
        compiler_params=_params("arbitrary"),
        name="proj_" + layer,
    )(*args)


def _diff_prompt_kernel(lam_init, lamv_ref, q_ref, k_ref, vt_ref, g_ref, om_ref, x_ref, wout_ref,
                        lng_ref, lnb_ref, dng_ref, y_ref, m_ref, alpha_ref, acc_ref, s_ref, p_ref, mix_ref):
    t = DIFF_TILE
    SLOTS = s_ref.shape[0]
    LEAD = PIPE_LEAD
    i = pl.program_id(1)
    heads = [slice(h * DA_V, (h + 1) * DA_V) for h in range(DA_HEADS)]
    lane = lax.broadcasted_iota(jnp.int32, (t, DA_V), 1)
    units = []
    for hs in heads:
        qh = q_ref[:, hs]
        zero = jnp.zeros_like(qh)
        units += [(hs, jnp.where(lane < DA_QK, qh, zero)), (hs, jnp.where(lane >= DA_QK, qh, zero))]
    m_ref[...] = jnp.full(m_ref.shape, -jnp.inf, F32)
    acc_ref[...] = jnp.zeros(acc_ref.shape, F32)

    n = len(units)
    assert n == SLOTS and LEAD + 1 < n

    def exp_stage(u):
        _stage_exp(s_ref.at[u], p_ref.at[u], m_ref.at[u])

    def pv_stage(j, u):
        _stage_pv(vt_ref[0, j, units[u][0], :], p_ref.at[u], alpha_ref.at[u], acc_ref.at[u])

    def block(j, mask, first):
        rows = pl.ds(pl.multiple_of(j * t, t), t)
        for u, (hs, qq) in enumerate(units):
            _stage_scores(qq, k_ref[0, rows, hs], mask, s_ref.at[u], m_ref.at[u], alpha_ref.at[u])
            if u >= LEAD:
                exp_stage(u - LEAD)
            elif not first:
                exp_stage(u - LEAD + n)
            if u >= LEAD + 1:
                pv_stage(j, u - LEAD - 1)
            elif not first:
                pv_stage(j - 1, u - LEAD - 1 + n)

    def pair(jj, carry):
        block(1 + 2 * jj, None, False)
        block(2 + 2 * jj, None, False)
        return carry

    kc = lax.broadcasted_iota(jnp.int32, (t, t), 0) // CHUNK
    qc = lax.broadcasted_iota(jnp.int32, (t, t), 1) // CHUNK
    diag_mask = kc <= qc
    block(0, diag_mask | (i > 0), True)
    lax.fori_loop(0, jnp.maximum(i - 1, 0) // 2, pair, 0)

    @pl.when((i >= 2) & (i % 2 == 0))
    def _():
        block(i - 1, None, False)

    @pl.when(i >= 1)
    def _():
        block(i, diag_mask, False)

    for u in range(n - LEAD, n):
        exp_stage(u)
        pv_stage(i, u - 1)
    pv_stage(i, n - 1)

    lam = _lambda(lamv_ref[...], lam_init)
    for h, hs in enumerate(heads):
        o1 = acc_ref[2 * h, :DA_V, :] / acc_ref[2 * h, DA_V:DA_V + 1, :]
        o2 = acc_ref[2 * h + 1, :DA_V, :] / acc_ref[2 * h + 1, DA_V:DA_V + 1, :]
        od = (o1 - lam * o2).T
        od = od * lax.rsqrt(jnp.mean(od * od, axis=-1, keepdims=True) + LN_EPS) * dng_ref[...] * (1.0 - lam_init)
        mix_ref[:, hs] = (od * g_ref[:, hs].astype(F32)).astype(BF16)
    mix_ref[:, DA_W:] = om_ref[...]
    y_ref[...] = _merge(x_ref[...], mix_ref[...], wout_ref[...], lng_ref[...], lnb_ref[...])


def _diff_prompt(lam_init, lamv, q, k, vt, g, om, x, w_out, ln_g, ln_b, dng, *, B, S):
    t = DIFF_TILE
    nq = S // t
    row = lambda w: pl.BlockSpec((t, w), lambda b, i: (b * nq + i, 0))
    full = lambda shape: pl.BlockSpec(shape, lambda b, i: (0,) * len(shape))
    return pl.pallas_call(
        functools.partial(_diff_prompt_kernel, lam_init),
        grid=(B, nq),
        in_specs=[full((4, DA_QK)), row(DA_W), pl.BlockSpec((1, S, DA_W), lambda b, i: (b, 0, 0)),
                  pl.BlockSpec((1, nq, DA_W, t), lambda b, i: (b, 0, 0, 0)), row(DA_W), row(MEM_W), row(D_MODEL),
                  full((MIX_W, D_MODEL)), full((1, D_MODEL)), full((1, D_MODEL)), full((1, DA_V))],
        out_specs=row(D_MODEL),
        out_shape=jax.ShapeDtypeStruct((B * S, D_MODEL), F32),
        scratch_shapes=[pltpu.VMEM((2 * DA_HEADS, 1, t), F32), pltpu.VMEM((2 * DA_HEADS, 1, t), F32),
                        pltpu.VMEM((2 * DA_HEADS, DA_V + SUM_ROWS, t), F32),
                        pltpu.VMEM((2 * DA_HEADS, t, t), F32), pltpu.VMEM((2 * DA_HEADS, t, t), BF16),
                        pltpu.VMEM((t, MIX_W), BF16)],
        compiler_params=_params("arbitrary", "arbitrary"),
        name="diff_prompt",
    )(lamv, q, k.reshape(B, S, DA_W), vt.reshape(B, nq, DA_W, t), g, om, x, w_out, ln_g, ln_b, dng)


def _swa_prompt_kernel(tq, sinks_ref, q_ref, k_ref, kr_ref, vt_ref, vtr_ref, g_ref, om_ref, x_ref, wout_ref,
                       lng_ref, lnb_ref, y_ref, mix_ref):
    i = pl.program_id(1)
    w, nk, nsub = SWA_TILE, 2 * SWA_TILE, tq // SWA_TILE
    low = lax.broadcasted_iota(jnp.int32, (w, LANES), 1) < SW_HD
    row_low = lax.broadcasted_iota(jnp.int32, (LANES, w), 0) < SW_HD
    sink_row = lambda hs: jnp.concatenate([jnp.full((1, w), sinks_ref[h] * LOG2E, F32) for h in hs], axis=1)
    sinks_plain, sinks_swapped = sink_row((0, 2, 5, 7)), sink_row((1, 3, 4, 6))
    kc = lax.broadcasted_iota(jnp.int32, (nk, 4 * w), 0) // CHUNK
    qc = lax.broadcasted_iota(jnp.int32, (nk, 4 * w), 1) % w // CHUNK
    firsts, scores = [], []
    for st in range(nsub):
        sub = i * nsub + st
        first = jnp.maximum(sub - 1, 0)
        back = (sub - first) * (w // CHUNK)
        mask = (kc - back <= qc) & (kc - back >= qc - WINDOW // CHUNK)
        rows = pl.ds(pl.multiple_of(first * w, w), nk)
        q = q_ref[st * w:(st + 1) * w, :]
        slabs = [q[:, p * LANES:(p + 1) * LANES] for p in range(SW_HEADS // 2)]
        even = [jnp.where(low, sl, jnp.zeros_like(sl)) for sl in slabs]
        odd = [jnp.where(low, jnp.zeros_like(sl), sl) for sl in slabs]
        firsts.append(first)
        scores.append((_window_scores(jnp.concatenate([even[0], even[1], odd[2], odd[3]], axis=0), k_ref[0, rows, :],
                                      mask),
                       _window_scores(jnp.concatenate([odd[0], odd[1], even[2], even[3]], axis=0), kr_ref[0, rows, :],
                                      mask)))
    for st, (first, (s_plain, s_swapped)) in enumerate(zip(firsts, scores)):
        vts = jnp.concatenate([vt_ref[0, first], vt_ref[0, first + 1]], axis=1)
        vtrs = jnp.concatenate([vtr_ref[0, first], vtr_ref[0, first + 1]], axis=1)
        o_plain = _window_attend(s_plain, vts, sinks_plain)
        o_swapped = _window_attend(s_swapped, vtrs, sinks_swapped)
        for p in range(SW_HEADS // 2):
            cols = slice(p * w, (p + 1) * w)
            lo, hi = (o_plain, o_swapped) if p < 2 else (o_swapped, o_plain)
            slab = jnp.where(row_low, lo[:, cols], hi[:, cols]).T
            gate = g_ref[st * w:(st + 1) * w, p * LANES:(p + 1) * LANES].astype(F32)
            mix_ref[st * w:(st + 1) * w, p * LANES:(p + 1) * LANES] = (slab * gate).astype(BF16)
    mix_ref[:, SW_W:] = om_ref[...]
    y_ref[...] = _merge(x_ref[...], mix_ref[...], wout_ref[...], lng_ref[...], lnb_ref[...])


def _swa_prompt(sinks, q, k, kr, vt, vtr, g, om, x, w_out, ln_g, ln_b, *, B, S, tq):
    nq = S // tq
    row = lambda w: pl.BlockSpec((tq, w), lambda b, i: (b * nq + i, 0))
    full = lambda shape: pl.BlockSpec(shape, lambda b, i: (0,) * len(shape))
    seq = pl.BlockSpec((1, S, LANES), lambda b, i: (b, 0, 0))
    seqt = pl.BlockSpec((1, S // SWA_TILE, LANES, SWA_TILE), lambda b, i: (b, 0, 0, 0))
    r3 = lambda t: t.reshape(B, S, LANES)
    r4 = lambda t: t.reshape(B, S // SWA_TILE, LANES, SWA_TILE)
    return pl.pallas_call(
        functools.partial(_swa_prompt_kernel, tq),
        grid=(B, nq),
        in_specs=[pl.BlockSpec(memory_space=pltpu.SMEM), row(SW_W), seq, seq, seqt, seqt, row(SW_W), row(MEM_W),
                  row(D_MODEL), full((MIX_W, D_MODEL)), full((1, D_MODEL)), full((1, D_MODEL))],
        out_specs=row(D_MODEL),
        out_shape=jax.ShapeDtypeStruct((B * S, D_MODEL), F32),
        scratch_shapes=[pltpu.VMEM((tq, MIX_W), BF16)],
        compiler_params=_params("arbitrary", "arbitrary"),
        name="swa_prompt",
    )(sinks, q, r3(k), r3(kr), r4(vt), r4(vtr), g, om, x, w_out, ln_g, ln_b)


def _sample_tail(b, last, o_first, g_ref, mq_ref, gm_ref, cmk_ref, cmv_ref, x_ref, wout_ref, lng_ref, lnb_ref,
                 y_ref, mix_ref, T):
    om = _mem_attend(mq_ref[...], _head_rows(cmk_ref, MEM_HEADS, 0, 0), _head_rows(cmv_ref, MEM_HEADS, 0, 0))
    om = om * gm_ref[...].astype(F32)
    mix = jnp.concatenate([(o_first * g_ref[...].astype(F32)).astype(BF16), om.astype(BF16)], axis=1)
    mix_ref[pl.ds(pl.multiple_of(b * T, T), T), :] = mix

    @pl.when(last)
    def _():
        y_ref[...] = _merge(x_ref[...], mix_ref[...], wout_ref[...], lng_ref[...], lnb_ref[...])


def _diff_sample_kernel(lam_init, T, lamv_ref, q_ref, kn_ref, vn_ref, ckt_ref, cv_ref, g_ref, mq_ref, gm_ref,
                        cmk_ref, cmv_ref, x_ref, wout_ref, lng_ref, lnb_ref, dng_ref, y_ref,
                        m_ref, l_ref, acc_ref, mix_ref):
    b, j = pl.program_id(0), pl.program_id(1)
    nb, nj = pl.num_programs(0), pl.num_programs(1)
    heads = [slice(h * DA_V, (h + 1) * DA_V) for h in range(DA_HEADS)]
    qqs = [_split_components(q_ref[:, hs]) for hs in heads]

    @pl.when(j == 0)
    def _():
        m_ref[...] = jnp.full(m_ref.shape, -jnp.inf, F32)
        l_ref[...] = jnp.zeros(l_ref.shape, F32)
        acc_ref[...] = jnp.zeros(acc_ref.shape, F32)
        _flash_update(qqs, lambda h: kn_ref[:, heads[h]], lambda h: vn_ref[:, heads[h]], False, m_ref, l_ref, acc_ref)

    _flash_update(qqs, lambda h: ckt_ref[0, heads[h], :].astype(BF16), _head_rows(cv_ref, DA_HEADS, 0), True,
                  m_ref, l_ref, acc_ref)

    @pl.when(j == nj - 1)
    def _():
        lam = _lambda(lamv_ref[...], lam_init)
        rows = [slice(h * 2 * T, (h + 1) * 2 * T) for h in range(DA_HEADS)]
        od = jnp.concatenate([_diff_finish(acc_ref[r, :], l_ref[r, :], lam, dng_ref[...], lam_init) for r in rows],
                             axis=1)
        _sample_tail(b, b == nb - 1, od, g_ref, mq_ref, gm_ref, cmk_ref, cmv_ref, x_ref, wout_ref, lng_ref,
                     lnb_ref, y_ref, mix_ref, T)


def _diff_sample(lam_init, lamv, q, kn, vn, ckt, cv, g, mq, gm, cmk, cmv, layer, x, w_out, ln_g, ln_b, dng, *,
                 NB, T, tk):
    P = ckt.shape[2]
    row = lambda w: pl.BlockSpec((T, w), lambda b, j: (b, 0))
    full = lambda shape: pl.BlockSpec(shape, lambda b, j: (0,) * len(shape))
    memc = pl.BlockSpec((1, 1, N_MEM * MEM_HEADS, MEM_HD), lambda b, j: (layer, b, 0, 0))
    return pl.pallas_call(
        functools.partial(_diff_sample_kernel, lam_init, T),
        grid=(NB, P // tk),
        in_specs=[full((4, DA_QK)), row(DA_W), row(DA_W), row(DA_W),
                  pl.BlockSpec((1, DA_W, tk), lambda b, j: (b, 0, j)),
                  pl.BlockSpec((1, tk * DA_HEADS, DA_V), lambda b, j: (b, j, 0)),
                  row(DA_W), row(MEM_W), row(MEM_W), memc, memc, full((NB * T, D_MODEL)), full((MIX_W, D_MODEL)),
                  full((1, D_MODEL)), full((1, D_MODEL)), full((1, DA_V))],
        out_specs=full((NB * T, D_MODEL)),
        out_shape=jax.ShapeDtypeStruct((NB * T, D_MODEL), F32),
        scratch_shapes=[pltpu.VMEM((DA_HEADS * 2 * T, 1), F32), pltpu.VMEM((DA_HEADS * 2 * T, 1), F32),
                        pltpu.VMEM((DA_HEADS * 2 * T, DA_V), F32), pltpu.VMEM((NB * T, MIX_W), BF16)],
        compiler_params=_params("arbitrary", "arbitrary"),
        name="diff_sample",
    )(lamv, q, kn, vn, ckt, cv, g, mq, gm, cmk, cmv, x, w_out, ln_g, ln_b, dng)


def _swa_sample_kernel(T, sinks_ref, q_ref, kn_ref, knr_ref, vn_ref, vnr_ref, ckt_ref, cvt_ref, g_ref, mq_ref, gm_ref,
                       cmk_ref, cmv_ref, x_ref, wout_ref, lng_ref, lnb_ref, y_ref, mix_ref):
    b = pl.program_id(0)
    swap = lambda t: jnp.concatenate([t[SW_HD:], t[:SW_HD]], axis=0).astype(BF16)
    ckt, cvt = ckt_ref[0], cvt_ref[0]
    kt, ktr, vt, vtr = ckt.astype(BF16), swap(ckt), cvt.astype(BF16), swap(cvt)
    kn, knr, vn, vnr = kn_ref[...], knr_ref[...], vn_ref[...], vnr_ref[...]
    attend = lambda qq, swapped, sink: _sink_attend_cached(
        qq, ktr if swapped else kt, knr if swapped else kn, vtr if swapped else vt, vnr if swapped else vn, sink)
    o = _swa_heads(q_ref[...], attend, lambda h: sinks_ref[h] * LOG2E)
    _sample_tail(b, b == pl.num_programs(0) - 1, o, g_ref, mq_ref, gm_ref, cmk_ref, cmv_ref, x_ref, wout_ref,
                 lng_ref, lnb_ref, y_ref, mix_ref, T)


def _swa_sample(sinks, q, kn, knr, vn, vnr, ckt, cvt, g, mq, gm, cmk, cmv, layer, x, w_out, ln_g, ln_b, *, NB, T):
    W = ckt.shape[2]
    row = lambda w: pl.BlockSpec((T, w), lambda b: (b, 0))
    full = lambda shape: pl.BlockSpec(shape, lambda b: (0,) * len(shape))
    cache = pl.BlockSpec((1, LANES, W), lambda b: (b, 0, 0))
    memc = pl.BlockSpec((1, 1, N_MEM * MEM_HEADS, MEM_HD), lambda b: (layer, b, 0, 0))
    return pl.pallas_call(
        functools.partial(_swa_sample_kernel, T),
        grid=(NB,),
        in_specs=[pl.BlockSpec(memory_space=pltpu.SMEM), row(SW_W), row(LANES), row(LANES), row(LANES), row(LANES),
                  cache, cache, row(SW_W), row(MEM_W), row(MEM_W), memc, memc, full((NB * T, D_MODEL)),
                  full((MIX_W, D_MODEL)), full((1, D_MODEL)), full((1, D_MODEL))],
        out_specs=full((NB * T, D_MODEL)),
        out_shape=jax.ShapeDtypeStruct((NB * T, D_MODEL), F32),
        scratch_shapes=[pltpu.VMEM((NB * T, MIX_W), BF16)],
        compiler_params=_params("arbitrary"),
        name="swa_sample",
    )(sinks, q, kn, knr, vn, vnr, ckt, cvt, g, mq, gm, cmk, cmv, x, w_out, ln_g, ln_b)


def _rope_tables(pos):
    T = pos.shape[0]
    inv = ROPE_THETA ** (-jnp.arange(ROPE_HALF, dtype=F32) / ROPE_HALF)
    ang = pos.astype(F32)[:, None] * inv[None, :]
    cos, sin = jnp.cos(ang), jnp.sin(ang)
    rest = SW_HD - 2 * ROPE_HALF
    c = jnp.concatenate([cos, cos, jnp.ones((T, rest), F32)], axis=1)
    a = jnp.concatenate([-sin, jnp.zeros((T, SW_HD - ROPE_HALF), F32)], axis=1)
    b = jnp.concatenate([jnp.zeros((T, ROPE_HALF), F32), sin, jnp.zeros((T, rest), F32)], axis=1)
    return tuple(jnp.tile(t, (1, LANES // SW_HD)) for t in (c, a, b))


def _feature_major(t):
    n = t.ndim
    return jnp.transpose(t, (0,) + tuple(range(2, n)) + (1,)).reshape(t.shape[0], -1, t.shape[1])


def _rows_major(t, tail):
    n = len(tail)
    return jnp.transpose(t.reshape((t.shape[0],) + tail + (t.shape[2],)), (0, n + 1) + tuple(range(1, n + 1)))


def kernel(x_prompt, x_sample, mem_prompt, cache_diff_k, cache_diff_v, cache_swa_k, cache_swa_v, cache_mem_k,
           cache_mem_v, w_in_a, lam_q1, lam_k1, lam_q2, lam_k2, diff_norm_g, w_in_b, sinks, w_kv_shared, w_mem_kv,
           w_out, ln_g, ln_b):
    B, S, _ = x_prompt.shape
    NB, T, _ = x_sample.shape
    P = cache_diff_k.shape[2]
    assert w_in_a.shape[0] == 1 and w_in_b.shape[0] == 1 and w_out.shape[0] == DEPTH

    tabs_p = _rope_tables(jnp.arange(S))
    tabs_s = _rope_tables(jnp.tile(P + jnp.arange(T), NB))
    xp = x_prompt.reshape(B * S, D_MODEL)
    xs = x_sample.reshape(NB * T, D_MODEL)
    wa = w_in_a[0].astype(BF16)
    wb = jnp.concatenate([w_in_b[0], w_kv_shared], axis=1).astype(BF16)
    wo = w_out.astype(BF16)
    lng = ln_g.reshape(DEPTH, 1, D_MODEL)
    lnb = ln_b.reshape(DEPTH, 1, D_MODEL)
    lamv = jnp.concatenate([lam_q1, lam_k1, lam_q2, lam_k2], axis=0)
    dng = diff_norm_g.reshape(1, DA_V)
    lam_init = 0.8 - 0.6 * math.exp(-0.3 * 0)

    mkf, mvf, mkb, mvb = _mem_kv(mem_prompt.reshape(B * N_MEM, D_MODEL).astype(BF16), w_mem_kv.astype(BF16))
    mem_b = lambda l: (mkb[l].reshape(B, N_MEM, MEM_W), mvb[l].reshape(B, N_MEM, MEM_W))
    head_rows = lambda t: t.reshape(t.shape[:-3] + (t.shape[-3] * t.shape[-2], t.shape[-1]))
    cmk, cmv = head_rows(cache_mem_k), head_rows(cache_mem_v)

    q, kt, vf, kb, vb, g, om = _project("a", xp, wa, tabs_p, mem_b(0), tm=512, rows_per_batch=S)
    xp1 = _diff_prompt(lam_init, lamv, q, kb, vb, g, om, xp, wo[0], lng[0], lnb[0], dng, B=B, S=S)
    qs, kfs, vfs, kbs, vbs, gs_, mqs, gms = _project("a", xs, wa, tabs_s, tm=NB * T)
    xs1 = _diff_sample(lam_init, lamv, qs, kbs, vbs, _feature_major(cache_diff_k[0]), head_rows(cache_diff_v[0]), gs_,
                       mqs, gms, cmk, cmv, 0, xs, wo[0], lng[0], lnb[0], dng, NB=NB, T=T, tk=1024)

    q, g, om, skf, svf, kb1, kr1, vb1, vr1 = _project("b", xp1, wb, tabs_p, mem_b(1), tm=512, rows_per_batch=S)
    yp = _swa_prompt(sinks[0], q, kb1, kr1, vb1, vr1, g, om, xp1, wo[1], lng[1], lnb[1], B=B, S=S, tq=512)
    qs, gs_, mqs, gms, skfs, svfs, kbs1, krs1, vbs1, vrs1 = _project("b", xs1, wb, tabs_s, tm=NB * T)
    ys = _swa_sample(sinks[0], qs, kbs1, krs1, vbs1, vrs1, _feature_major(cache_swa_k), _feature_major(cache_swa_v),
                     gs_, mqs, gms, cmk, cmv, 1, xs1, wo[1], lng[1], lnb[1], NB=NB, T=T)

    wr_p = min(WINDOW, S)
    window = lambda t: t.reshape(B, S, LANES)[:, S - wr_p:, :].reshape(B, wr_p, SW_KV, SW_HD)
    swa_kp, swa_vp = window(skf), window(svf)
    roll = lambda c, n: jnp.concatenate([c, n.reshape(NB, T, SW_KV, SW_HD)], axis=1)[:, T:]
    return (yp.reshape(B, S, D_MODEL), ys.reshape(NB, T, D_MODEL),
            _rows_major(kt, (DA_HEADS, 2, DA_QK))[None], vf.reshape(1, B, S, DA_HEADS, DA_V),
            kfs.reshape(1, NB, T, DA_HEADS, 2, DA_QK), vfs.reshape(1, NB, T, DA_HEADS, DA_V),
            swa_kp, swa_vp, roll(cache_swa_k, skfs), roll(cache_swa_v, svfs),
            mkf.reshape(DEPTH, B, N_MEM, MEM_HEADS, MEM_HD), mvf.reshape(DEPTH, B, N_MEM, MEM_HEADS, MEM_HD))
```

```python
import functools
import math

import jax
import jax.numpy as jnp
import numpy as np
from jax import lax
from jax.experimental import pallas as pl
from jax.experimental.pallas import tpu as pltpu

D_MODEL = 1024
CHUNK = 64
N_MEM = 256
DA_HEADS = 4
DA_QK = 64
DA_V = 128
DA_W = 512
SW_HEADS = 8
SW_KV = 2
SW_GROUP = 4
SW_HD = 64
SW_W = 512
WINDOW = 128
MEM_HEADS = 4
MEM_HD = 128
MEM_W = 512
MIX_W = 1024
ROPE_THETA = 500000.0
ROPE_HALF = 8
DEPTH = 2
DN_ALPHA = (2 * DEPTH) ** 0.25
LN_EPS = 1e-5
NEG = -1e30
LANES = 128
DIFF_TILE = 256
PIPE_LEAD = 4
UNROLL = 4
SWA_TILE = 128
SUM_ROWS = 16
LOG2E = math.log2(math.e)

F32 = jnp.float32
BF16 = jnp.bfloat16
VMEM_LIMIT = 48 * 1024 * 1024

_NT = (((1,), (1,)), ((), ()))


def _params(*sem):
    return pltpu.CompilerParams(dimension_semantics=sem, vmem_limit_bytes=VMEM_LIMIT)


def _rope(x, c, a, b):
    outs = []
    for i in range(x.shape[1] // LANES):
        blk = x[:, i * LANES:(i + 1) * LANES]
        outs.append(blk * c + pltpu.roll(blk, LANES - ROPE_HALF, 1) * a + pltpu.roll(blk, ROPE_HALF, 1) * b)
    return outs[0] if len(outs) == 1 else jnp.concatenate(outs, axis=1)


def _silu(g):
    return g * (1.0 / (1.0 + jnp.exp(-g)))


def _layer_norm(z, g, b):
    mu = jnp.mean(z, axis=-1, keepdims=True)
    d = z - mu
    var = jnp.mean(d * d, axis=-1, keepdims=True)
    return d * lax.rsqrt(var + LN_EPS) * g + b


def _mem_attend(mq, mk_of, mv_of):
    outs = []
    for h in range(MEM_HEADS):
        sl = slice(h * MEM_HD, (h + 1) * MEM_HD)
        s = lax.dot_general(mq[:, sl], mk_of(h), _NT, preferred_element_type=F32) * (MEM_HD ** -0.5)
        p = jnp.exp(s - jnp.max(s, axis=1, keepdims=True))
        l = jnp.sum(p, axis=1, keepdims=True)
        outs.append(jnp.dot(p.astype(BF16), mv_of(h), preferred_element_type=F32) / l)
    return jnp.concatenate(outs, axis=1)


def _head_cols(ref):
    return lambda h: ref[0, :, h * MEM_HD:(h + 1) * MEM_HD]


def _head_idx(ref, heads, h, *lead):
    return lead + (pl.ds(h, ref.shape[-2] // heads, stride=heads), slice(None))


def _head_rows(ref, heads, *lead):
    return lambda h: ref[_head_idx(ref, heads, h, *lead)].astype(BF16)


def _merge(x, mix, w_out, ln_g, ln_b):
    return _layer_norm(DN_ALPHA * x + jnp.dot(mix, w_out, preferred_element_type=F32), ln_g, ln_b)


def _lambda(lamv, lam_init):
    e1 = jnp.exp(jnp.sum(lamv[0:1, :] * lamv[1:2, :], axis=1, keepdims=True))
    e2 = jnp.exp(jnp.sum(lamv[2:3, :] * lamv[3:4, :], axis=1, keepdims=True))
    return e1 - e2 + lam_init


def _split_components(qh):
    lane = lax.broadcasted_iota(jnp.int32, qh.shape, 1)
    zero = jnp.zeros_like(qh)
    return jnp.concatenate([jnp.where(lane < DA_QK, qh, zero), jnp.where(lane >= DA_QK, qh, zero)], axis=0)


def _flash_update(qqs, k_of, v_of, keys_on_lanes, m_ref, l_ref, acc_ref):
    if keys_on_lanes:
        s = [jnp.dot(qq, k_of(h), preferred_element_type=F32) for h, qq in enumerate(qqs)]
    else:
        s = [lax.dot_general(qq, k_of(h), _NT, preferred_element_type=F32) for h, qq in enumerate(qqs)]
    s = jnp.concatenate(s, axis=0)
    m_old = m_ref[...]
    m_new = jnp.maximum(m_old, jnp.max(s, axis=1, keepdims=True))
    alpha = jnp.exp2(m_old - m_new)
    p = jnp.exp2(s - m_new)
    l_ref[...] = alpha * l_ref[...] + jnp.sum(p, axis=1, keepdims=True)
    pb = p.astype(BF16)
    r = qqs[0].shape[0]
    pv = [jnp.dot(pb[h * r:(h + 1) * r], v_of(h), preferred_element_type=F32) for h in range(len(qqs))]
    acc_ref[...] = alpha * acc_ref[...] + jnp.concatenate(pv, axis=0)
    m_ref[...] = m_new


def _stage_scores(qq, ks, mask, s_ref, m_ref, alpha_ref):
    s = lax.dot_general(ks, qq, _NT, preferred_element_type=F32)
    if mask is not None:
        s = jnp.where(mask, s, NEG)
    s_ref[...] = s
    m_old = m_ref[...]
    m_new = jnp.maximum(m_old, jnp.max(s, axis=0, keepdims=True))
    alpha_ref[...] = jnp.exp2(m_old - m_new)
    m_ref[...] = m_new


def _stage_exp(s_ref, p_ref, m_ref):
    p_ref[...] = jnp.exp2(s_ref[...] - m_ref[...]).astype(BF16)


def _stage_pv(vts, p_ref, alpha_ref, acc_ref):
    ones = jnp.ones((SUM_ROWS, vts.shape[1]), BF16)
    pv = jnp.dot(jnp.concatenate([vts, ones], axis=0), p_ref[...], preferred_element_type=F32)
    acc_ref[...] = alpha_ref[...] * acc_ref[...] + pv


def _diff_finish(acc, l, lam, dng, lam_init):
    rows = acc.shape[0] // 2
    o = acc / l
    od = o[:rows] - lam * o[rows:]
    return od * lax.rsqrt(jnp.mean(od * od, axis=-1, keepdims=True) + LN_EPS) * dng * (1.0 - lam_init)


def _sink_attend_cached(qq, kt, kn, vt, vn, sink):
    s_c = jnp.dot(qq, kt, preferred_element_type=F32)
    s_n = lax.dot_general(qq, kn, _NT, preferred_element_type=F32)
    m = jnp.maximum(jnp.maximum(jnp.max(s_c, axis=1, keepdims=True), jnp.max(s_n, axis=1, keepdims=True)), sink)
    p_c, p_n = jnp.exp2(s_c - m), jnp.exp2(s_n - m)
    l = jnp.sum(p_c, axis=1, keepdims=True) + jnp.sum(p_n, axis=1, keepdims=True) + jnp.exp2(sink - m)
    o = (lax.dot_general(p_c.astype(BF16), vt, _NT, preferred_element_type=F32)
         + jnp.dot(p_n.astype(BF16), vn, preferred_element_type=F32))
    return o / l


def _window_scores(qq4, ks, mask):
    return jnp.where(mask, lax.dot_general(ks, qq4, _NT, preferred_element_type=F32), NEG)


def _window_attend(s, vts, sinks4):
    m = jnp.maximum(jnp.max(s, axis=0, keepdims=True), sinks4)
    p = jnp.exp2(s - m).astype(BF16)
    ones = jnp.ones((SUM_ROWS, vts.shape[1]), BF16)
    o = jnp.dot(jnp.concatenate([vts, ones], axis=0), p, preferred_element_type=F32)
    return o[:LANES] / (o[LANES:LANES + 1] + jnp.exp2(sinks4 - m))


def _swa_heads(q, attend, sink_of):
    low = lax.broadcasted_iota(jnp.int32, (q.shape[0], LANES), 1) < SW_HD
    outs = []
    for pair in range(SW_HEADS // 2):
        kv = pair // 2
        slab = q[:, pair * LANES:(pair + 1) * LANES]
        zero = jnp.zeros_like(slab)
        o_even = attend(jnp.where(low, slab, zero), kv == 1, sink_of(2 * pair))
        o_odd = attend(jnp.where(low, zero, slab), kv == 0, sink_of(2 * pair + 1))
        outs.append(jnp.where(low, o_even, o_odd))
    return jnp.concatenate(outs, axis=1)


def _mem_kv_kernel(mem_ref, w_ref, kf_ref, vf_ref, kb_ref, vb_ref):
    h = jnp.dot(mem_ref[...], w_ref[0], preferred_element_type=F32)
    k, v = h[:, :MEM_W], h[:, MEM_W:]
    for hd in range(MEM_HEADS):
        kf_ref[_head_idx(kf_ref, MEM_HEADS, hd, 0)] = k[:, hd * MEM_HD:(hd + 1) * MEM_HD]
        vf_ref[_head_idx(vf_ref, MEM_HEADS, hd, 0)] = v[:, hd * MEM_HD:(hd + 1) * MEM_HD]
    kb_ref[0] = k.astype(BF16)
    vb_ref[0] = v.astype(BF16)


def _mem_kv(mem_bf, w_bf):
    rows = mem_bf.shape[0]
    fshape, fspec = (DEPTH, rows * MEM_HEADS, MEM_HD), pl.BlockSpec((1, rows * MEM_HEADS, MEM_HD), lambda l: (l, 0, 0))
    bshape, bspec = (DEPTH, rows, MEM_W), pl.BlockSpec((1, rows, MEM_W), lambda l: (l, 0, 0))
    return pl.pallas_call(
        _mem_kv_kernel,
        grid=(DEPTH,),
        in_specs=[pl.BlockSpec((rows, D_MODEL), lambda l: (0, 0)),
                  pl.BlockSpec((1, D_MODEL, 2 * MEM_W), lambda l: (l, 0, 0))],
        out_specs=[fspec, fspec, bspec, bspec],
        out_shape=[jax.ShapeDtypeStruct(fshape, F32), jax.ShapeDtypeStruct(fshape, F32),
                   jax.ShapeDtypeStruct(bshape, BF16), jax.ShapeDtypeStruct(bshape, BF16)],
        compiler_params=_params("arbitrary"),
        name="mem_kv",
    )(mem_bf, w_bf)


def _proj_a_kernel(fuse_mem, x_ref, w_ref, c_ref, a_ref, b_ref, *refs):
    if fuse_mem:
        mk_ref, mv_ref, q_ref, kf_ref, vf_ref, kb_ref, vb_ref, g_ref, om_ref = refs
    else:
        q_ref, kf_ref, vf_ref, kb_ref, vb_ref, g_ref, mq_ref, gm_ref = refs
    x = x_ref[...].astype(BF16)
    c, a, b = c_ref[...], a_ref[...], b_ref[...]

    def cols(i):
        return jnp.dot(x, w_ref[:, i * DA_W:(i + 1) * DA_W], preferred_element_type=F32)

    q_ref[...] = (_rope(cols(0), c, a, b) * (DA_QK ** -0.5 * LOG2E)).astype(BF16)
    k = _rope(cols(1), c, a, b)
    kb_ref[...] = k.astype(BF16)
    v = cols(2)
    for hd in range(DA_HEADS):
        vf_ref[_head_idx(vf_ref, DA_HEADS, hd)] = v[:, hd * DA_V:(hd + 1) * DA_V]
    if fuse_mem:
        kf_ref[0] = k.T
        for blk in range(vb_ref.shape[0]):
            vb_ref[blk] = v[blk * DIFF_TILE:(blk + 1) * DIFF_TILE, :].T.astype(BF16)
    else:
        kf_ref[...] = k
        vb_ref[...] = v.astype(BF16)
    g_ref[...] = _silu(cols(3)).astype(BF16)
    mq = cols(4).astype(BF16)
    gm = _silu(cols(5))
    if fuse_mem:
        om_ref[...] = (_mem_attend(mq, _head_cols(mk_ref), _head_cols(mv_ref)) * gm).astype(BF16)
    else:
        mq_ref[...] = mq
        gm_ref[...] = gm.astype(BF16)


def _proj_b_kernel(fuse_mem, x_ref, w_ref, c_ref, a_ref, b_ref, *refs):
    if fuse_mem:
        mk_ref, mv_ref, q_ref, g_ref, om_ref, kf_ref, vf_ref, kb_ref, kr_ref, vb_ref, vr_ref = refs
    else:
        q_ref, g_ref, mq_ref, gm_ref, kf_ref, vf_ref, kb_ref, kr_ref, vb_ref, vr_ref = refs
    x = x_ref[...].astype(BF16)
    c, a, b = c_ref[...], a_ref[...], b_ref[...]

    def cols(i):
        return jnp.dot(x, w_ref[:, i * SW_W:(i + 1) * SW_W], preferred_element_type=F32)

    q_ref[...] = (_rope(cols(0), c, a, b) * (SW_HD ** -0.5 * LOG2E)).astype(BF16)
    g_ref[...] = _silu(cols(1)).astype(BF16)
    mq = cols(2).astype(BF16)
    gm = _silu(cols(3))
    if fuse_mem:
        om_ref[...] = (_mem_attend(mq, _head_cols(mk_ref), _head_cols(mv_ref)) * gm).astype(BF16)
    else:
        mq_ref[...] = mq
        gm_ref[...] = gm.astype(BF16)
    kv = jnp.dot(x, w_ref[:, 4 * SW_W:], preferred_element_type=F32)
    k = _rope(kv[:, :LANES], c, a, b)
    v = kv[:, LANES:]
    kf_ref[...] = k
    vf_ref[...] = v
    kb_ref[...] = k.astype(BF16)
    kr_ref[...] = pltpu.roll(k, SW_HD, 1).astype(BF16)
    vr = pltpu.roll(v, SW_HD, 1)
    if fuse_mem:
        for blk in range(vb_ref.shape[0]):
            rows = slice(blk * SWA_TILE, (blk + 1) * SWA_TILE)
            vb_ref[blk] = v[rows, :].T.astype(BF16)
            vr_ref[blk] = vr[rows, :].T.astype(BF16)
    else:
        vb_ref[...] = v.astype(BF16)
        vr_ref[...] = vr.astype(BF16)


def _project(layer, x, w_bf, tabs, mem=None, *, tm, rows_per_batch=None):
    M = x.shape[0]
    fuse = mem is not None
    per = (rows_per_batch or M) // tm
    row = lambda w: pl.BlockSpec((tm, w), lambda i: (i, 0))
    tab = pl.BlockSpec((tm, LANES), lambda i: (i % per, 0))
    in_specs = [row(D_MODEL), pl.BlockSpec(w_bf.shape, lambda i: (0, 0)), tab, tab, tab]
    args = [x, w_bf, *tabs]
    if fuse:
        mspec = pl.BlockSpec((1, N_MEM, MEM_W), lambda i: (i // per, 0, 0))
        in_specs += [mspec, mspec]
        args += list(mem)
    o = lambda w, dt: (row(w), jax.ShapeDtypeStruct((M, w), dt))
    if layer == "a":
        vt = (pl.BlockSpec((tm // DIFF_TILE, DA_W, DIFF_TILE), lambda i: (i, 0, 0)),
              jax.ShapeDtypeStruct((M // DIFF_TILE, DA_W, DIFF_TILE), BF16))
        kt = (pl.BlockSpec((1, DA_W, tm), lambda i: (i // per, 0, i % per)),
              jax.ShapeDtypeStruct((M // (per * tm), DA_W, per * tm), F32))
        vf = (pl.BlockSpec((tm * DA_HEADS, DA_V), lambda i: (i, 0)), jax.ShapeDtypeStruct((M * DA_HEADS, DA_V), F32))
        outs = [o(DA_W, BF16), kt if fuse else o(DA_W, F32), vf, o(DA_W, BF16), vt if fuse else o(DA_W, BF16),
                o(DA_W, BF16)]
        body = _proj_a_kernel
    else:
        outs = [o(SW_W, BF16), o(SW_W, BF16)]
        body = _proj_b_kernel
    mem_outs = [o(MEM_W, BF16)] if fuse else [o(MEM_W, BF16), o(MEM_W, BF16)]
    if layer == "a":
        outs = outs + mem_outs
    else:
        vt = (pl.BlockSpec((tm // SWA_TILE, LANES, SWA_TILE), lambda i: (i, 0, 0)),
              jax.ShapeDtypeStruct((M // SWA_TILE, LANES, SWA_TILE), BF16))
        values = [vt, vt] if fuse else [o(LANES, BF16)] * 2
        outs = outs + mem_outs + [o(LANES, F32), o(LANES, F32)] + [o(LANES, BF16)] * 2 + values
    return pl.pallas_call(
        functools.partial(body, fuse),
        grid=(M // tm,),
        in_specs=in_specs,
        out_specs=[s for s, _ in outs],
        out_shape=[t for _, t in outs],
        compiler_params=_params("arbitrary"),
        name="proj_" + layer,
    )(*args)


def _diff_prompt_kernel(lam_init, lamv_ref, q_ref, k_ref, vt_ref, g_ref, om_ref, x_ref, wout_ref,
                        lng_ref, lnb_ref, dng_ref, y_ref, m_ref, alpha_ref, acc_ref, s_ref, p_ref, mix_ref):
    t = DIFF_TILE
    SLOTS = s_ref.shape[0]
    LEAD = PIPE_LEAD
    i = pl.program_id(1)
    heads = [slice(h * DA_V, (h + 1) * DA_V) for h in range(DA_HEADS)]
    lane = lax.broadcasted_iota(jnp.int32, (t, DA_V), 1)
    units = []
    for hs in heads:
        qh = q_ref[:, hs]
        zero = jnp.zeros_like(qh)
        units += [(hs, jnp.where(lane < DA_QK, qh, zero)), (hs, jnp.where(lane >= DA_QK, qh, zero))]
    m_ref[...] = jnp.full(m_ref.shape, -jnp.inf, F32)
    acc_ref[...] = jnp.zeros(acc_ref.shape, F32)

    n = len(units)
    assert n == SLOTS and LEAD + 1 < n

    def exp_stage(u):
        _stage_exp(s_ref.at[u], p_ref.at[u], m_ref.at[u])

    def pv_stage(j, u):
        _stage_pv(vt_ref[0, j, units[u][0], :], p_ref.at[u], alpha_ref.at[u], acc_ref.at[u])

    def block(j, mask, first):
        rows = pl.ds(pl.multiple_of(j * t, t), t)
        for u, (hs, qq) in enumerate(units):
            _stage_scores(qq, k_ref[0, rows, hs], mask, s_ref.at[u], m_ref.at[u], alpha_ref.at[u])
            if u >= LEAD:
                exp_stage(u - LEAD)
            elif not first:
                exp_stage(u - LEAD + n)
            if u >= LEAD + 1:
                pv_stage(j, u - LEAD - 1)
            elif not first:
                pv_stage(j - 1, u - LEAD - 1 + n)

    def run(j0, count):
        for d in range(count):
            block(j0 + d, None, False)

    def quad(jj, carry):
        run(1 + UNROLL * jj, UNROLL)
        return carry

    kc = lax.broadcasted_iota(jnp.int32, (t, t), 0) // CHUNK
    qc = lax.broadcasted_iota(jnp.int32, (t, t), 1) // CHUNK
    diag_mask = kc <= qc
    block(0, diag_mask | (i > 0), True)
    plain = jnp.maximum(i - 1, 0)
    lax.fori_loop(0, plain // UNROLL, quad, 0)
    done = 1 + plain // UNROLL * UNROLL
    rest = plain % UNROLL
    for part in (2, 1):
        @pl.when(rest & part != 0)
        def _(part=part):
            run(done + (rest & ~(2 * part - 1)), part)

    @pl.when(i >= 1)
    def _():
        block(i, diag_mask, False)

    for u in range(n - LEAD, n):
        exp_stage(u)
        pv_stage(i, u - 1)
    pv_stage(i, n - 1)

    lam = _lambda(lamv_ref[...], lam_init)
    for h, hs in enumerate(heads):
        o1 = acc_ref[2 * h, :DA_V, :] / acc_ref[2 * h, DA_V:DA_V + 1, :]
        o2 = acc_ref[2 * h + 1, :DA_V, :] / acc_ref[2 * h + 1, DA_V:DA_V + 1, :]
        od = (o1 - lam * o2).T
        od = od * lax.rsqrt(jnp.mean(od * od, axis=-1, keepdims=True) + LN_EPS) * dng_ref[...] * (1.0 - lam_init)
        mix_ref[:, hs] = (od * g_ref[:, hs].astype(F32)).astype(BF16)
    mix_ref[:, DA_W:] = om_ref[...]
    y_ref[...] = _merge(x_ref[...], mix_ref[...], wout_ref[...], lng_ref[...], lnb_ref[...])


def _diff_prompt(lam_init, lamv, q, k, vt, g, om, x, w_out, ln_g, ln_b, dng, *, B, S):
    t = DIFF_TILE
    nq = S // t
    row = lambda w: pl.BlockSpec((t, w), lambda b, i: (b * nq + i, 0))
    full = lambda shape: pl.BlockSpec(shape, lambda b, i: (0,) * len(shape))
    return pl.pallas_call(
        functools.partial(_diff_prompt_kernel, lam_init),
        grid=(B, nq),
        in_specs=[full((4, DA_QK)), row(DA_W), pl.BlockSpec((1, S, DA_W), lambda b, i: (b, 0, 0)),
                  pl.BlockSpec((1, nq, DA_W, t), lambda b, i: (b, 0, 0, 0)), row(DA_W), row(MEM_W), row(D_MODEL),
                  full((MIX_W, D_MODEL)), full((1, D_MODEL)), full((1, D_MODEL)), full((1, DA_V))],
        out_specs=row(D_MODEL),
        out_shape=jax.ShapeDtypeStruct((B * S, D_MODEL), F32),
        scratch_shapes=[pltpu.VMEM((2 * DA_HEADS, 1, t), F32), pltpu.VMEM((2 * DA_HEADS, 1, t), F32),
                        pltpu.VMEM((2 * DA_HEADS, DA_V + SUM_ROWS, t), F32),
                        pltpu.VMEM((2 * DA_HEADS, t, t), F32), pltpu.VMEM((2 * DA_HEADS, t, t), BF16),
                        pltpu.VMEM((t, MIX_W), BF16)],
        compiler_params=_params("arbitrary", "arbitrary"),
        name="diff_prompt",
    )(lamv, q, k.reshape(B, S, DA_W), vt.reshape(B, nq, DA_W, t), g, om, x, w_out, ln_g, ln_b, dng)


def _swa_prompt_kernel(tq, sinks_ref, q_ref, k_ref, kr_ref, vt_ref, vtr_ref, g_ref, om_ref, x_ref, wout_ref,
                       lng_ref, lnb_ref, y_ref, mix_ref):
    i = pl.program_id(1)
    w, nk, nsub = SWA_TILE, 2 * SWA_TILE, tq // SWA_TILE
    low = lax.broadcasted_iota(jnp.int32, (w, LANES), 1) < SW_HD
    row_low = lax.broadcasted_iota(jnp.int32, (LANES, w), 0) < SW_HD
    sink_row = lambda hs: jnp.concatenate([jnp.full((1, w), sinks_ref[h] * LOG2E, F32) for h in hs], axis=1)
    sinks_plain, sinks_swapped = sink_row((0, 2, 5, 7)), sink_row((1, 3, 4, 6))
    kc = lax.broadcasted_iota(jnp.int32, (nk, 4 * w), 0) // CHUNK
    qc = lax.broadcasted_iota(jnp.int32, (nk, 4 * w), 1) % w // CHUNK
    firsts, scores = [], []
    for st in range(nsub):
        sub = i * nsub + st
        first = jnp.maximum(sub - 1, 0)
        back = (sub - first) * (w // CHUNK)
        mask = (kc - back <= qc) & (kc - back >= qc - WINDOW // CHUNK)
        rows = pl.ds(pl.multiple_of(first * w, w), nk)
        q = q_ref[st * w:(st + 1) * w, :]
        slabs = [q[:, p * LANES:(p + 1) * LANES] for p in range(SW_HEADS // 2)]
        even = [jnp.where(low, sl, jnp.zeros_like(sl)) for sl in slabs]
        odd = [jnp.where(low, jnp.zeros_like(sl), sl) for sl in slabs]
        firsts.append(first)
        scores.append((_window_scores(jnp.concatenate([even[0], even[1], odd[2], odd[3]], axis=0), k_ref[0, rows, :],
                                      mask),
                       _window_scores(jnp.concatenate([odd[0], odd[1], even[2], even[3]], axis=0), kr_ref[0, rows, :],
                                      mask)))
    for st, (first, (s_plain, s_swapped)) in enumerate(zip(firsts, scores)):
        vts = jnp.concatenate([vt_ref[0, first], vt_ref[0, first + 1]], axis=1)
        vtrs = jnp.concatenate([vtr_ref[0, first], vtr_ref[0, first + 1]], axis=1)
        o_plain = _window_attend(s_plain, vts, sinks_plain)
        o_swapped = _window_attend(s_swapped, vtrs, sinks_swapped)
        for p in range(SW_HEADS // 2):
            cols = slice(p * w, (p + 1) * w)
            lo, hi = (o_plain, o_swapped) if p < 2 else (o_swapped, o_plain)
            slab = jnp.where(row_low, lo[:, cols], hi[:, cols]).T
            gate = g_ref[st * w:(st + 1) * w, p * LANES:(p + 1) * LANES].astype(F32)
            mix_ref[st * w:(st + 1) * w, p * LANES:(p + 1) * LANES] = (slab * gate).astype(BF16)
    mix_ref[:, SW_W:] = om_ref[...]
    y_ref[...] = _merge(x_ref[...], mix_ref[...], wout_ref[...], lng_ref[...], lnb_ref[...])


def _swa_prompt(sinks, q, k, kr, vt, vtr, g, om, x, w_out, ln_g, ln_b, *, B, S, tq):
    nq = S // tq
    row = lambda w: pl.BlockSpec((tq, w), lambda b, i: (b * nq + i, 0))
    full = lambda shape: pl.BlockSpec(shape, lambda b, i: (0,) * len(shape))
    seq = pl.BlockSpec((1, S, LANES), lambda b, i: (b, 0, 0))
    seqt = pl.BlockSpec((1, S // SWA_TILE, LANES, SWA_TILE), lambda b, i: (b, 0, 0, 0))
    r3 = lambda t: t.reshape(B, S, LANES)
    r4 = lambda t: t.reshape(B, S // SWA_TILE, LANES, SWA_TILE)
    return pl.pallas_call(
        functools.partial(_swa_prompt_kernel, tq),
        grid=(B, nq),
        in_specs=[pl.BlockSpec(memory_space=pltpu.SMEM), row(SW_W), seq, seq, seqt, seqt, row(SW_W), row(MEM_W),
                  row(D_MODEL), full((MIX_W, D_MODEL)), full((1, D_MODEL)), full((1, D_MODEL))],
        out_specs=row(D_MODEL),
        out_shape=jax.ShapeDtypeStruct((B * S, D_MODEL), F32),
        scratch_shapes=[pltpu.VMEM((tq, MIX_W), BF16)],
        compiler_params=_params("arbitrary", "arbitrary"),
        name="swa_prompt",
    )(sinks, q, r3(k), r3(kr), r4(vt), r4(vtr), g, om, x, w_out, ln_g, ln_b)


def _sample_tail(b, last, o_first, g_ref, mq_ref, gm_ref, cmk_ref, cmv_ref, x_ref, wout_ref, lng_ref, lnb_ref,
                 y_ref, mix_ref, T):
    om = _mem_attend(mq_ref[...], _head_rows(cmk_ref, MEM_HEADS, 0, 0), _head_rows(cmv_ref, MEM_HEADS, 0, 0))
    om = om * gm_ref[...].astype(F32)
    mix = jnp.concatenate([(o_first * g_ref[...].astype(F32)).astype(BF16), om.astype(BF16)], axis=1)
    mix_ref[pl.ds(pl.multiple_of(b * T, T), T), :] = mix

    @pl.when(last)
    def _():
        y_ref[...] = _merge(x_ref[...], mix_ref[...], wout_ref[...], lng_ref[...], lnb_ref[...])


def _diff_sample_kernel(lam_init, T, lamv_ref, q_ref, kn_ref, vn_ref, ckt_ref, cv_ref, g_ref, mq_ref, gm_ref,
                        cmk_ref, cmv_ref, x_ref, wout_ref, lng_ref, lnb_ref, dng_ref, y_ref,
                        m_ref, l_ref, acc_ref, mix_ref):
    b, j = pl.program_id(0), pl.program_id(1)
    nb, nj = pl.num_programs(0), pl.num_programs(1)
    heads = [slice(h * DA_V, (h + 1) * DA_V) for h in range(DA_HEADS)]
    qqs = [_split_components(q_ref[:, hs]) for hs in heads]

    @pl.when(j == 0)
    def _():
        m_ref[...] = jnp.full(m_ref.shape, -jnp.inf, F32)
        l_ref[...] = jnp.zeros(l_ref.shape, F32)
        acc_ref[...] = jnp.zeros(acc_ref.shape, F32)
        _flash_update(qqs, lambda h: kn_ref[:, heads[h]], lambda h: vn_ref[:, heads[h]], False, m_ref, l_ref, acc_ref)

    _flash_update(qqs, lambda h: ckt_ref[0, heads[h], :].astype(BF16), _head_rows(cv_ref, DA_HEADS, 0), True,
                  m_ref, l_ref, acc_ref)

    @pl.when(j == nj - 1)
    def _():
        lam = _lambda(lamv_ref[...], lam_init)
        rows = [slice(h * 2 * T, (h + 1) * 2 * T) for h in range(DA_HEADS)]
        od = jnp.concatenate([_diff_finish(acc_ref[r, :], l_ref[r, :], lam, dng_ref[...], lam_init) for r in rows],
                             axis=1)
        _sample_tail(b, b == nb - 1, od, g_ref, mq_ref, gm_ref, cmk_ref, cmv_ref, x_ref, wout_ref, lng_ref,
                     lnb_ref, y_ref, mix_ref, T)


def _diff_sample(lam_init, lamv, q, kn, vn, ckt, cv, g, mq, gm, cmk, cmv, layer, x, w_out, ln_g, ln_b, dng, *,
                 NB, T, tk):
    P = ckt.shape[2]
    row = lambda w: pl.BlockSpec((T, w), lambda b, j: (b, 0))
    full = lambda shape: pl.BlockSpec(shape, lambda b, j: (0,) * len(shape))
    memc = pl.BlockSpec((1, 1, N_MEM * MEM_HEADS, MEM_HD), lambda b, j: (layer, b, 0, 0))
    return pl.pallas_call(
        functools.partial(_diff_sample_kernel, lam_init, T),
        grid=(NB, P // tk),
        in_specs=[full((4, DA_QK)), row(DA_W), row(DA_W), row(DA_W),
                  pl.BlockSpec((1, DA_W, tk), lambda b, j: (b, 0, j)),
                  pl.BlockSpec((1, tk * DA_HEADS, DA_V), lambda b, j: (b, j, 0)),
                  row(DA_W), row(MEM_W), row(MEM_W), memc, memc, full((NB * T, D_MODEL)), full((MIX_W, D_MODEL)),
                  full((1, D_MODEL)), full((1, D_MODEL)), full((1, DA_V))],
        out_specs=full((NB * T, D_MODEL)),
        out_shape=jax.ShapeDtypeStruct((NB * T, D_MODEL), F32),
        scratch_shapes=[pltpu.VMEM((DA_HEADS * 2 * T, 1), F32), pltpu.VMEM((DA_HEADS * 2 * T, 1), F32),
                        pltpu.VMEM((DA_HEADS * 2 * T, DA_V), F32), pltpu.VMEM((NB * T, MIX_W), BF16)],
        compiler_params=_params("arbitrary", "arbitrary"),
        name="diff_sample",
    )(lamv, q, kn, vn, ckt, cv, g, mq, gm, cmk, cmv, x, w_out, ln_g, ln_b, dng)


def _swa_sample_kernel(T, sinks_ref, q_ref, kn_ref, knr_ref, vn_ref, vnr_ref, ckt_ref, cvt_ref, g_ref, mq_ref, gm_ref,
                       cmk_ref, cmv_ref, x_ref, wout_ref, lng_ref, lnb_ref, y_ref, mix_ref):
    b = pl.program_id(0)
    swap = lambda t: jnp.concatenate([t[SW_HD:], t[:SW_HD]], axis=0).astype(BF16)
    ckt, cvt = ckt_ref[0], cvt_ref[0]
    kt, ktr, vt, vtr = ckt.astype(BF16), swap(ckt), cvt.astype(BF16), swap(cvt)
    kn, knr, vn, vnr = kn_ref[...], knr_ref[...], vn_ref[...], vnr_ref[...]
    attend = lambda qq, swapped, sink: _sink_attend_cached(
        qq, ktr if swapped else kt, knr if swapped else kn, vtr if swapped else vt, vnr if swapped else vn, sink)
    o = _swa_heads(q_ref[...], attend, lambda h: sinks_ref[h] * LOG2E)
    _sample_tail(b, b == pl.num_programs(0) - 1, o, g_ref, mq_ref, gm_ref, cmk_ref, cmv_ref, x_ref, wout_ref,
                 lng_ref, lnb_ref, y_ref, mix_ref, T)


def _swa_sample(sinks, q, kn, knr, vn, vnr, ckt, cvt, g, mq, gm, cmk, cmv, layer, x, w_out, ln_g, ln_b, *, NB, T):
    W = ckt.shape[2]
    row = lambda w: pl.BlockSpec((T, w), lambda b: (b, 0))
    full = lambda shape: pl.BlockSpec(shape, lambda b: (0,) * len(shape))
    cache = pl.BlockSpec((1, LANES, W), lambda b: (b, 0, 0))
    memc = pl.BlockSpec((1, 1, N_MEM * MEM_HEADS, MEM_HD), lambda b: (layer, b, 0, 0))
    return pl.pallas_call(
        functools.partial(_swa_sample_kernel, T),
        grid=(NB,),
        in_specs=[pl.BlockSpec(memory_space=pltpu.SMEM), row(SW_W), row(LANES), row(LANES), row(LANES), row(LANES),
                  cache, cache, row(SW_W), row(MEM_W), row(MEM_W), memc, memc, full((NB * T, D_MODEL)),
                  full((MIX_W, D_MODEL)), full((1, D_MODEL)), full((1, D_MODEL))],
        out_specs=full((NB * T, D_MODEL)),
        out_shape=jax.ShapeDtypeStruct((NB * T, D_MODEL), F32),
        scratch_shapes=[pltpu.VMEM((NB * T, MIX_W), BF16)],
        compiler_params=_params("arbitrary"),
        name="swa_sample",
    )(sinks, q, kn, knr, vn, vnr, ckt, cvt, g, mq, gm, cmk, cmv, x, w_out, ln_g, ln_b)


def _rope_tables(pos):
    T = pos.shape[0]
    inv = ROPE_THETA ** (-np.arange(ROPE_HALF, dtype=np.float64) / ROPE_HALF)
    ang = pos.astype(np.float64)[:, None] * inv[None, :]
    cos, sin = np.cos(ang), np.sin(ang)
    rest = SW_HD - 2 * ROPE_HALF
    c = np.concatenate([cos, cos, np.ones((T, rest))], axis=1)
    a = np.concatenate([-sin, np.zeros((T, SW_HD - ROPE_HALF))], axis=1)
    b = np.concatenate([np.zeros((T, ROPE_HALF)), sin, np.zeros((T, rest))], axis=1)
    return tuple(jnp.asarray(np.tile(t, (1, LANES // SW_HD)), F32) for t in (c, a, b))


def _feature_major(t):
    n = t.ndim
    return jnp.transpose(t, (0,) + tuple(range(2, n)) + (1,)).reshape(t.shape[0], -1, t.shape[1])


def _rows_major(t, tail):
    n = len(tail)
    return jnp.transpose(t.reshape((t.shape[0],) + tail + (t.shape[2],)), (0, n + 1) + tuple(range(1, n + 1)))


def kernel(x_prompt, x_sample, mem_prompt, cache_diff_k, cache_diff_v, cache_swa_k, cache_swa_v, cache_mem_k,
           cache_mem_v, w_in_a, lam_q1, lam_k1, lam_q2, lam_k2, diff_norm_g, w_in_b, sinks, w_kv_shared, w_mem_kv,
           w_out, ln_g, ln_b):
    B, S, _ = x_prompt.shape
    NB, T, _ = x_sample.shape
    P = cache_diff_k.shape[2]
    assert w_in_a.shape[0] == 1 and w_in_b.shape[0] == 1 and w_out.shape[0] == DEPTH

    tabs_p = _rope_tables(np.arange(S))
    tabs_s = _rope_tables(np.tile(P + np.arange(T), NB))
    xp = x_prompt.reshape(B * S, D_MODEL)
    xs = x_sample.reshape(NB * T, D_MODEL)
    wa = w_in_a[0].astype(BF16)
    wb = jnp.concatenate([w_in_b[0], w_kv_shared], axis=1).astype(BF16)
    wo = w_out.astype(BF16)
    lng = ln_g.reshape(DEPTH, 1, D_MODEL)
    lnb = ln_b.reshape(DEPTH, 1, D_MODEL)
    lamv = jnp.concatenate([lam_q1, lam_k1, lam_q2, lam_k2], axis=0)
    dng = diff_norm_g.reshape(1, DA_V)
    lam_init = 0.8 - 0.6 * math.exp(-0.3 * 0)

    mkf, mvf, mkb, mvb = _mem_kv(mem_prompt.reshape(B * N_MEM, D_MODEL).astype(BF16), w_mem_kv.astype(BF16))
    mem_b = lambda l: (mkb[l].reshape(B, N_MEM, MEM_W), mvb[l].reshape(B, N_MEM, MEM_W))
    head_rows = lambda t: t.reshape(t.shape[:-3] + (t.shape[-3] * t.shape[-2], t.shape[-1]))
    cmk, cmv = head_rows(cache_mem_k), head_rows(cache_mem_v)

    q, kt, vf, kb, vb, g, om = _project("a", xp, wa, tabs_p, mem_b(0), tm=512, rows_per_batch=S)
    xp1 = _diff_prompt(lam_init, lamv, q, kb, vb, g, om, xp, wo[0], lng[0], lnb[0], dng, B=B, S=S)
    qs, kfs, vfs, kbs, vbs, gs_, mqs, gms = _project("a", xs, wa, tabs_s, tm=NB * T)
    xs1 = _diff_sample(lam_init, lamv, qs, kbs, vbs, _feature_major(cache_diff_k[0]), head_rows(cache_diff_v[0]), gs_,
                       mqs, gms, cmk, cmv, 0, xs, wo[0], lng[0], lnb[0], dng, NB=NB, T=T, tk=2048)

    q, g, om, skf, svf, kb1, kr1, vb1, vr1 = _project("b", xp1, wb, tabs_p, mem_b(1), tm=512, rows_per_batch=S)
    yp = _swa_prompt(sinks[0], q, kb1, kr1, vb1, vr1, g, om, xp1, wo[1], lng[1], lnb[1], B=B, S=S, tq=512)
    qs, gs_, mqs, gms, skfs, svfs, kbs1, krs1, vbs1, vrs1 = _project("b", xs1, wb, tabs_s, tm=NB * T)
    ys = _swa_sample(sinks[0], qs, kbs1, krs1, vbs1, vrs1, _feature_major(cache_swa_k), _feature_major(cache_swa_v),
                     gs_, mqs, gms, cmk, cmv, 1, xs1, wo[1], lng[1], lnb[1], NB=NB, T=T)

    wr_p = min(WINDOW, S)
    window = lambda t: t.reshape(B, S, LANES)[:, S - wr_p:, :].reshape(B, wr_p, SW_KV, SW_HD)
    swa_kp, swa_vp = window(skf), window(svf)
    roll = lambda c, n: jnp.concatenate([c, n.reshape(NB, T, SW_KV, SW_HD)], axis=1)[:, T:]
    return (yp.reshape(B, S, D_MODEL), ys.reshape(NB, T, D_MODEL),
            _rows_major(kt, (DA_HEADS, 2, DA_QK))[None], vf.reshape(1, B, S, DA_HEADS, DA_V),
            kfs.reshape(1, NB, T, DA_HEADS, 2, DA_QK), vfs.reshape(1, NB, T, DA_HEADS, DA_V),
            swa_kp, swa_vp, roll(cache_swa_k, skfs), roll(cache_swa_v, svfs),
            mkf.reshape(DEPTH, B, N_MEM, MEM_HEADS, MEM_HD), mvf.reshape(DEPTH, B, N_MEM, MEM_HEADS, MEM_HD))
```

```python
import functools
import math

import jax
import jax.numpy as jnp
import numpy as np
from jax import lax
from jax.experimental import pallas as pl
from jax.experimental.pallas import tpu as pltpu

D_MODEL = 1024
CHUNK = 64
N_MEM = 256
DA_HEADS = 4
DA_QK = 64
DA_V = 128
DA_W = 512
SW_HEADS = 8
SW_KV = 2
SW_GROUP = 4
SW_HD = 64
SW_W = 512
WINDOW = 128
MEM_HEADS = 4
MEM_HD = 128
MEM_W = 512
MIX_W = 1024
ROPE_THETA = 500000.0
ROPE_HALF = 8
DEPTH = 2
DN_ALPHA = (2 * DEPTH) ** 0.25
LN_EPS = 1e-5
NEG = -1e30
LANES = 128
DIFF_TILE = 256
PIPE_LEAD = 4
UNROLL = 4
SWA_TILE = 128
SUM_ROWS = 16
LOG2E = math.log2(math.e)

F32 = jnp.float32
BF16 = jnp.bfloat16
VMEM_LIMIT = 48 * 1024 * 1024

_NT = (((1,), (1,)), ((), ()))


def _resident(shape, *first):
    idx = first or (0,) * len(shape)
    return pl.BlockSpec(shape, lambda *_: idx, pipeline_mode=pl.Buffered(1))


def _params(*sem):
    return pltpu.CompilerParams(dimension_semantics=sem, vmem_limit_bytes=VMEM_LIMIT)


def _rope(x, c, a, b):
    outs = []
    for i in range(x.shape[1] // LANES):
        blk = x[:, i * LANES:(i + 1) * LANES]
        outs.append(blk * c + pltpu.roll(blk, LANES - ROPE_HALF, 1) * a + pltpu.roll(blk, ROPE_HALF, 1) * b)
    return outs[0] if len(outs) == 1 else jnp.concatenate(outs, axis=1)


def _silu(g):
    return g * (1.0 / (1.0 + jnp.exp(-g)))


def _layer_norm(z, g, b):
    mu = jnp.mean(z, axis=-1, keepdims=True)
    d = z - mu
    var = jnp.mean(d * d, axis=-1, keepdims=True)
    return d * lax.rsqrt(var + LN_EPS) * g + b


def _mem_attend(mq, mk_of, mv_of):
    outs = []
    for h in range(MEM_HEADS):
        sl = slice(h * MEM_HD, (h + 1) * MEM_HD)
        s = lax.dot_general(mq[:, sl], mk_of(h), _NT, preferred_element_type=F32) * (MEM_HD ** -0.5)
        p = jnp.exp(s - jnp.max(s, axis=1, keepdims=True))
        l = jnp.sum(p, axis=1, keepdims=True)
        outs.append(jnp.dot(p.astype(BF16), mv_of(h), preferred_element_type=F32) / l)
    return jnp.concatenate(outs, axis=1)


def _head_cols(ref):
    return lambda h: ref[0, :, h * MEM_HD:(h + 1) * MEM_HD]


def _head_idx(ref, heads, h, *lead):
    return lead + (pl.ds(h, ref.shape[-2] // heads, stride=heads), slice(None))


def _head_rows(ref, heads, *lead):
    return lambda h: ref[_head_idx(ref, heads, h, *lead)].astype(BF16)


def _merge(x, mix, w_out, ln_g, ln_b):
    return _layer_norm(DN_ALPHA * x + jnp.dot(mix, w_out.astype(BF16), preferred_element_type=F32), ln_g, ln_b)


def _lambda(lamv, lam_init):
    e1 = jnp.exp(jnp.sum(lamv[0:1, :] * lamv[1:2, :], axis=1, keepdims=True))
    e2 = jnp.exp(jnp.sum(lamv[2:3, :] * lamv[3:4, :], axis=1, keepdims=True))
    return e1 - e2 + lam_init


def _split_components(qh):
    lane = lax.broadcasted_iota(jnp.int32, qh.shape, 1)
    zero = jnp.zeros_like(qh)
    return jnp.concatenate([jnp.where(lane < DA_QK, qh, zero), jnp.where(lane >= DA_QK, qh, zero)], axis=0)


def _flash_update(qqs, k_of, v_of, keys_on_lanes, m_ref, l_ref, acc_ref):
    if keys_on_lanes:
        s = [jnp.dot(qq, k_of(h), preferred_element_type=F32) for h, qq in enumerate(qqs)]
    else:
        s = [lax.dot_general(qq, k_of(h), _NT, preferred_element_type=F32) for h, qq in enumerate(qqs)]
    s = jnp.concatenate(s, axis=0)
    m_old = m_ref[...]
    m_new = jnp.maximum(m_old, jnp.max(s, axis=1, keepdims=True))
    alpha = jnp.exp2(m_old - m_new)
    p = jnp.exp2(s - m_new)
    l_ref[...] = alpha * l_ref[...] + jnp.sum(p, axis=1, keepdims=True)
    pb = p.astype(BF16)
    r = qqs[0].shape[0]
    pv = [jnp.dot(pb[h * r:(h + 1) * r], v_of(h), preferred_element_type=F32) for h in range(len(qqs))]
    acc_ref[...] = alpha * acc_ref[...] + jnp.concatenate(pv, axis=0)
    m_ref[...] = m_new


def _stage_scores(qq, ks, mask, s_ref, m_ref, alpha_ref):
    s = lax.dot_general(ks, qq, _NT, preferred_element_type=F32)
    if mask is not None:
        s = jnp.where(mask, s, NEG)
    s_ref[...] = s
    m_old = m_ref[...]
    m_new = jnp.maximum(m_old, jnp.max(s, axis=0, keepdims=True))
    alpha_ref[...] = jnp.exp2(m_old - m_new)
    m_ref[...] = m_new


def _stage_exp(s_ref, p_ref, m_ref):
    p_ref[...] = jnp.exp2(s_ref[...] - m_ref[...]).astype(BF16)


def _stage_pv(vts, p_ref, alpha_ref, acc_ref):
    ones = jnp.ones((SUM_ROWS, vts.shape[1]), BF16)
    pv = jnp.dot(jnp.concatenate([vts, ones], axis=0), p_ref[...], preferred_element_type=F32)
    acc_ref[...] = alpha_ref[...] * acc_ref[...] + pv


def _diff_finish(acc, l, lam, dng, lam_init):
    rows = acc.shape[0] // 2
    o = acc / l
    od = o[:rows] - lam * o[rows:]
    return od * lax.rsqrt(jnp.mean(od * od, axis=-1, keepdims=True) + LN_EPS) * dng * (1.0 - lam_init)


def _sink_attend_cached(qq, kt, kn, vt, vn, sink):
    s_c = jnp.dot(qq, kt, preferred_element_type=F32)
    s_n = lax.dot_general(qq, kn, _NT, preferred_element_type=F32)
    m = jnp.maximum(jnp.maximum(jnp.max(s_c, axis=1, keepdims=True), jnp.max(s_n, axis=1, keepdims=True)), sink)
    p_c, p_n = jnp.exp2(s_c - m), jnp.exp2(s_n - m)
    l = jnp.sum(p_c, axis=1, keepdims=True) + jnp.sum(p_n, axis=1, keepdims=True) + jnp.exp2(sink - m)
    o = (lax.dot_general(p_c.astype(BF16), vt, _NT, preferred_element_type=F32)
         + jnp.dot(p_n.astype(BF16), vn, preferred_element_type=F32))
    return o / l


def _window_scores(qq4, ks, mask):
    return jnp.where(mask, lax.dot_general(ks, qq4, _NT, preferred_element_type=F32), NEG)


def _window_attend(s, vts, sinks4):
    m = jnp.maximum(jnp.max(s, axis=0, keepdims=True), sinks4)
    p = jnp.exp2(s - m).astype(BF16)
    ones = jnp.ones((SUM_ROWS, vts.shape[1]), BF16)
    o = jnp.dot(jnp.concatenate([vts, ones], axis=0), p, preferred_element_type=F32)
    return o[:LANES] / (o[LANES:LANES + 1] + jnp.exp2(sinks4 - m))


def _swa_heads(q, attend, sink_of):
    low = lax.broadcasted_iota(jnp.int32, (q.shape[0], LANES), 1) < SW_HD
    outs = []
    for pair in range(SW_HEADS // 2):
        kv = pair // 2
        slab = q[:, pair * LANES:(pair + 1) * LANES]
        zero = jnp.zeros_like(slab)
        o_even = attend(jnp.where(low, slab, zero), kv == 1, sink_of(2 * pair))
        o_odd = attend(jnp.where(low, zero, slab), kv == 0, sink_of(2 * pair + 1))
        outs.append(jnp.where(low, o_even, o_odd))
    return jnp.concatenate(outs, axis=1)


def _mem_kv_kernel(mem_ref, w_ref, kf_ref, vf_ref, kb_ref, vb_ref):
    h = jnp.dot(mem_ref[...].astype(BF16), w_ref[0].astype(BF16), preferred_element_type=F32)
    k, v = h[:, :MEM_W], h[:, MEM_W:]
    for hd in range(MEM_HEADS):
        kf_ref[_head_idx(kf_ref, MEM_HEADS, hd, 0)] = k[:, hd * MEM_HD:(hd + 1) * MEM_HD]
        vf_ref[_head_idx(vf_ref, MEM_HEADS, hd, 0)] = v[:, hd * MEM_HD:(hd + 1) * MEM_HD]
    kb_ref[0] = k.astype(BF16)
    vb_ref[0] = v.astype(BF16)


def _mem_kv(mem_bf, w_bf):
    rows = mem_bf.shape[0]
    fshape, fspec = (DEPTH, rows * MEM_HEADS, MEM_HD), pl.BlockSpec((1, rows * MEM_HEADS, MEM_HD), lambda l: (l, 0, 0))
    bshape, bspec = (DEPTH, rows, MEM_W), pl.BlockSpec((1, rows, MEM_W), lambda l: (l, 0, 0))
    return pl.pallas_call(
        _mem_kv_kernel,
        grid=(DEPTH,),
        in_specs=[pl.BlockSpec((rows, D_MODEL), lambda l: (0, 0)),
                  pl.BlockSpec((1, D_MODEL, 2 * MEM_W), lambda l: (l, 0, 0))],
        out_specs=[fspec, fspec, bspec, bspec],
        out_shape=[jax.ShapeDtypeStruct(fshape, F32), jax.ShapeDtypeStruct(fshape, F32),
                   jax.ShapeDtypeStruct(bshape, BF16), jax.ShapeDtypeStruct(bshape, BF16)],
        compiler_params=_params("arbitrary"),
        name="mem_kv",
    )(mem_bf, w_bf)


def _proj_a_kernel(fuse_mem, x_ref, w_ref, c_ref, a_ref, b_ref, *refs):
    if fuse_mem:
        mk_ref, mv_ref, q_ref, kf_ref, vf_ref, kb_ref, vb_ref, g_ref, om_ref = refs
    else:
        q_ref, kf_ref, vf_ref, kb_ref, vb_ref, g_ref, mq_ref, gm_ref = refs
    x = x_ref[...].astype(BF16)
    c, a, b = c_ref[...], a_ref[...], b_ref[...]

    def cols(i):
        return jnp.dot(x, w_ref[:, i * DA_W:(i + 1) * DA_W].astype(BF16), preferred_element_type=F32)

    q_ref[...] = (_rope(cols(0), c, a, b) * (DA_QK ** -0.5 * LOG2E)).astype(BF16)
    k = _rope(cols(1), c, a, b)
    kb_ref[...] = k.astype(BF16)
    v = cols(2)
    for hd in range(DA_HEADS):
        vf_ref[_head_idx(vf_ref, DA_HEADS, hd)] = v[:, hd * DA_V:(hd + 1) * DA_V]
    if fuse_mem:
        kf_ref[0] = k.T
        for blk in range(vb_ref.shape[0]):
            vb_ref[blk] = v[blk * DIFF_TILE:(blk + 1) * DIFF_TILE, :].T.astype(BF16)
    else:
        kf_ref[...] = k
        vb_ref[...] = v.astype(BF16)
    g_ref[...] = _silu(cols(3)).astype(BF16)
    mq = cols(4).astype(BF16)
    gm = _silu(cols(5))
    if fuse_mem:
        om_ref[...] = (_mem_attend(mq, _head_cols(mk_ref), _head_cols(mv_ref)) * gm).astype(BF16)
    else:
        mq_ref[...] = mq
        gm_ref[...] = gm.astype(BF16)


def _proj_b_kernel(fuse_mem, x_ref, w_ref, wkv_ref, c_ref, a_ref, b_ref, *refs):
    if fuse_mem:
        mk_ref, mv_ref, q_ref, g_ref, om_ref, kf_ref, vf_ref, kb_ref, kr_ref, vb_ref, vr_ref = refs
    else:
        q_ref, g_ref, mq_ref, gm_ref, kf_ref, vf_ref, kb_ref, kr_ref, vb_ref, vr_ref = refs
    x = x_ref[...].astype(BF16)
    c, a, b = c_ref[...], a_ref[...], b_ref[...]

    def cols(i):
        return jnp.dot(x, w_ref[:, i * SW_W:(i + 1) * SW_W].astype(BF16), preferred_element_type=F32)

    q_ref[...] = (_rope(cols(0), c, a, b) * (SW_HD ** -0.5 * LOG2E)).astype(BF16)
    g_ref[...] = _silu(cols(1)).astype(BF16)
    mq = cols(2).astype(BF16)
    gm = _silu(cols(3))
    if fuse_mem:
        om_ref[...] = (_mem_attend(mq, _head_cols(mk_ref), _head_cols(mv_ref)) * gm).astype(BF16)
    else:
        mq_ref[...] = mq
        gm_ref[...] = gm.astype(BF16)
    kv = jnp.dot(x, wkv_ref[...].astype(BF16), preferred_element_type=F32)
    k = _rope(kv[:, :LANES], c, a, b)
    v = kv[:, LANES:]
    kf_ref[...] = k
    vf_ref[...] = v
    kb_ref[...] = k.astype(BF16)
    kr_ref[...] = pltpu.roll(k, SW_HD, 1).astype(BF16)
    vr = pltpu.roll(v, SW_HD, 1)
    if fuse_mem:
        for blk in range(vb_ref.shape[0]):
            rows = slice(blk * SWA_TILE, (blk + 1) * SWA_TILE)
            vb_ref[blk] = v[rows, :].T.astype(BF16)
            vr_ref[blk] = vr[rows, :].T.astype(BF16)
    else:
        vb_ref[...] = v.astype(BF16)
        vr_ref[...] = vr.astype(BF16)


def _project(layer, x, ws, tabs, mem=None, *, tm, rows_per_batch=None):
    M = x.shape[0]
    fuse = mem is not None
    per = (rows_per_batch or M) // tm
    row = lambda w: pl.BlockSpec((tm, w), lambda i: (i, 0))
    tab = pl.BlockSpec((tm, LANES), lambda i: (i % per, 0))
    in_specs = [row(D_MODEL)] + [_resident(w.shape) for w in ws] + [tab, tab, tab]
    args = [x, *ws, *tabs]
    if fuse:
        mspec = pl.BlockSpec((1, N_MEM, MEM_W), lambda i: (i // per, 0, 0))
        in_specs += [mspec, mspec]
        args += list(mem)
    o = lambda w, dt: (row(w), jax.ShapeDtypeStruct((M, w), dt))
    if layer == "a":
        vt = (pl.BlockSpec((tm // DIFF_TILE, DA_W, DIFF_TILE), lambda i: (i, 0, 0)),
              jax.ShapeDtypeStruct((M // DIFF_TILE, DA_W, DIFF_TILE), BF16))
        kt = (pl.BlockSpec((1, DA_W, tm), lambda i: (i // per, 0, i % per)),
              jax.ShapeDtypeStruct((M // (per * tm), DA_W, per * tm), F32))
        vf = (pl.BlockSpec((tm * DA_HEADS, DA_V), lambda i: (i, 0)), jax.ShapeDtypeStruct((M * DA_HEADS, DA_V), F32))
        outs = [o(DA_W, BF16), kt if fuse else o(DA_W, F32), vf, o(DA_W, BF16), vt if fuse else o(DA_W, BF16),
                o(DA_W, BF16)]
        body = _proj_a_kernel
    else:
        outs = [o(SW_W, BF16), o(SW_W, BF16)]
        body = _proj_b_kernel
    mem_outs = [o(MEM_W, BF16)] if fuse else [o(MEM_W, BF16), o(MEM_W, BF16)]
    if layer == "a":
        outs = outs + mem_outs
    else:
        vt = (pl.BlockSpec((tm // SWA_TILE, LANES, SWA_TILE), lambda i: (i, 0, 0)),
              jax.ShapeDtypeStruct((M // SWA_TILE, LANES, SWA_TILE), BF16))
        values = [vt, vt] if fuse else [o(LANES, BF16)] * 2
        outs = outs + mem_outs + [o(LANES, F32), o(LANES, F32)] + [o(LANES, BF16)] * 2 + values
    return pl.pallas_call(
        functools.partial(body, fuse),
        grid=(M // tm,),
        in_specs=in_specs,
        out_specs=[s for s, _ in outs],
        out_shape=[t for _, t in outs],
        compiler_params=_params("arbitrary"),
        name="proj_" + layer,
    )(*args)


def _diff_prompt_kernel(lam_init, lamv_ref, q_ref, k_ref, vt_ref, g_ref, om_ref, x_ref, wout_ref,
                        lng_ref, lnb_ref, dng_ref, y_ref, m_ref, alpha_ref, acc_ref, s_ref, p_ref, mix_ref):
    t = DIFF_TILE
    SLOTS = s_ref.shape[0]
    LEAD = PIPE_LEAD
    i = pl.program_id(1)
    heads = [slice(h * DA_V, (h + 1) * DA_V) for h in range(DA_HEADS)]
    lane = lax.broadcasted_iota(jnp.int32, (t, DA_V), 1)
    units = []
    for hs in heads:
        qh = q_ref[:, hs]
        zero = jnp.zeros_like(qh)
        units += [(hs, jnp.where(lane < DA_QK, qh, zero)), (hs, jnp.where(lane >= DA_QK, qh, zero))]
    m_ref[...] = jnp.full(m_ref.shape, -jnp.inf, F32)
    acc_ref[...] = jnp.zeros(acc_ref.shape, F32)

    n = len(units)
    assert n == SLOTS and LEAD + 1 < n

    def exp_stage(u):
        _stage_exp(s_ref.at[u], p_ref.at[u], m_ref.at[u])

    def pv_stage(j, u):
        _stage_pv(vt_ref[0, j, units[u][0], :], p_ref.at[u], alpha_ref.at[u], acc_ref.at[u])

    def block(j, mask, first):
        rows = pl.ds(pl.multiple_of(j * t, t), t)
        for u, (hs, qq) in enumerate(units):
            _stage_scores(qq, k_ref[0, rows, hs], mask, s_ref.at[u], m_ref.at[u], alpha_ref.at[u])
            if u >= LEAD:
                exp_stage(u - LEAD)
            elif not first:
                exp_stage(u - LEAD + n)
            if u >= LEAD + 1:
                pv_stage(j, u - LEAD - 1)
            elif not first:
                pv_stage(j - 1, u - LEAD - 1 + n)

    def run(j0, count):
        for d in range(count):
            block(j0 + d, None, False)

    def quad(jj, carry):
        run(1 + UNROLL * jj, UNROLL)
        return carry

    kc = lax.broadcasted_iota(jnp.int32, (t, t), 0) // CHUNK
    qc = lax.broadcasted_iota(jnp.int32, (t, t), 1) // CHUNK
    diag_mask = kc <= qc
    block(0, diag_mask | (i > 0), True)
    plain = jnp.maximum(i - 1, 0)
    lax.fori_loop(0, plain // UNROLL, quad, 0)
    done = 1 + plain // UNROLL * UNROLL
    rest = plain % UNROLL
    for part in (2, 1):
        @pl.when(rest & part != 0)
        def _(part=part):
            run(done + (rest & ~(2 * part - 1)), part)

    @pl.when(i >= 1)
    def _():
        block(i, diag_mask, False)

    for u in range(n - LEAD, n):
        exp_stage(u)
        pv_stage(i, u - 1)
    pv_stage(i, n - 1)

    lam = _lambda(lamv_ref[...], lam_init)
    for h, hs in enumerate(heads):
        o1 = acc_ref[2 * h, :DA_V, :] / acc_ref[2 * h, DA_V:DA_V + 1, :]
        o2 = acc_ref[2 * h + 1, :DA_V, :] / acc_ref[2 * h + 1, DA_V:DA_V + 1, :]
        od = (o1 - lam * o2).T
        od = od * lax.rsqrt(jnp.mean(od * od, axis=-1, keepdims=True) + LN_EPS) * dng_ref[...] * (1.0 - lam_init)
        mix_ref[:, hs] = (od * g_ref[:, hs].astype(F32)).astype(BF16)
    mix_ref[:, DA_W:] = om_ref[...]
    y_ref[...] = _merge(x_ref[...], mix_ref[...], wout_ref[0], lng_ref[...], lnb_ref[...])


def _diff_prompt(lam_init, lamv, q, k, vt, g, om, x, w_out, layer, ln_g, ln_b, dng, *, B, S):
    t = DIFF_TILE
    nq = S // t
    row = lambda w: pl.BlockSpec((t, w), lambda b, i: (b * nq + i, 0))
    full = lambda shape: pl.BlockSpec(shape, lambda b, i: (0,) * len(shape))
    return pl.pallas_call(
        functools.partial(_diff_prompt_kernel, lam_init),
        grid=(B, nq),
        in_specs=[full((4, DA_QK)), row(DA_W), pl.BlockSpec((1, S, DA_W), lambda b, i: (b, 0, 0)),
                  pl.BlockSpec((1, nq, DA_W, t), lambda b, i: (b, 0, 0, 0)), row(DA_W), row(MEM_W), row(D_MODEL),
                  _resident((1, MIX_W, D_MODEL), layer, 0, 0), full((1, D_MODEL)), full((1, D_MODEL)), full((1, DA_V))],
        out_specs=row(D_MODEL),
        out_shape=jax.ShapeDtypeStruct((B * S, D_MODEL), F32),
        scratch_shapes=[pltpu.VMEM((2 * DA_HEADS, 1, t), F32), pltpu.VMEM((2 * DA_HEADS, 1, t), F32),
                        pltpu.VMEM((2 * DA_HEADS, DA_V + SUM_ROWS, t), F32),
                        pltpu.VMEM((2 * DA_HEADS, t, t), F32), pltpu.VMEM((2 * DA_HEADS, t, t), BF16),
                        pltpu.VMEM((t, MIX_W), BF16)],
        compiler_params=_params("arbitrary", "arbitrary"),
        name="diff_prompt",
    )(lamv, q, k.reshape(B, S, DA_W), vt.reshape(B, nq, DA_W, t), g, om, x, w_out, ln_g, ln_b, dng)


def _swa_prompt_kernel(tq, sinks_ref, q_ref, k_ref, kr_ref, vt_ref, vtr_ref, g_ref, om_ref, x_ref, wout_ref,
                       lng_ref, lnb_ref, y_ref, mix_ref):
    i = pl.program_id(1)
    w, nk, nsub = SWA_TILE, 2 * SWA_TILE, tq // SWA_TILE
    low = lax.broadcasted_iota(jnp.int32, (w, LANES), 1) < SW_HD
    row_low = lax.broadcasted_iota(jnp.int32, (LANES, w), 0) < SW_HD
    sink_row = lambda hs: jnp.concatenate([jnp.full((1, w), sinks_ref[h] * LOG2E, F32) for h in hs], axis=1)
    sinks_plain, sinks_swapped = sink_row((0, 2, 5, 7)), sink_row((1, 3, 4, 6))
    kc = lax.broadcasted_iota(jnp.int32, (nk, 4 * w), 0) // CHUNK
    qc = lax.broadcasted_iota(jnp.int32, (nk, 4 * w), 1) % w // CHUNK
    firsts, scores = [], []
    for st in range(nsub):
        sub = i * nsub + st
        first = jnp.maximum(sub - 1, 0)
        back = (sub - first) * (w // CHUNK)
        mask = (kc - back <= qc) & (kc - back >= qc - WINDOW // CHUNK)
        rows = pl.ds(pl.multiple_of(first * w, w), nk)
        q = q_ref[st * w:(st + 1) * w, :]
        slabs = [q[:, p * LANES:(p + 1) * LANES] for p in range(SW_HEADS // 2)]
        even = [jnp.where(low, sl, jnp.zeros_like(sl)) for sl in slabs]
        odd = [jnp.where(low, jnp.zeros_like(sl), sl) for sl in slabs]
        firsts.append(first)
        scores.append((_window_scores(jnp.concatenate([even[0], even[1], odd[2], odd[3]], axis=0), k_ref[0, rows, :],
                                      mask),
                       _window_scores(jnp.concatenate([odd[0], odd[1], even[2], even[3]], axis=0), kr_ref[0, rows, :],
                                      mask)))
    for st, (first, (s_plain, s_swapped)) in enumerate(zip(firsts, scores)):
        vts = jnp.concatenate([vt_ref[0, first], vt_ref[0, first + 1]], axis=1)
        vtrs = jnp.concatenate([vtr_ref[0, first], vtr_ref[0, first + 1]], axis=1)
        o_plain = _window_attend(s_plain, vts, sinks_plain)
        o_swapped = _window_attend(s_swapped, vtrs, sinks_swapped)
        for p in range(SW_HEADS // 2):
            cols = slice(p * w, (p + 1) * w)
            lo, hi = (o_plain, o_swapped) if p < 2 else (o_swapped, o_plain)
            slab = jnp.where(row_low, lo[:, cols], hi[:, cols]).T
            gate = g_ref[st * w:(st + 1) * w, p * LANES:(p + 1) * LANES].astype(F32)
            mix_ref[st * w:(st + 1) * w, p * LANES:(p + 1) * LANES] = (slab * gate).astype(BF16)
    mix_ref[:, SW_W:] = om_ref[...]
    y_ref[...] = _merge(x_ref[...], mix_ref[...], wout_ref[0], lng_ref[...], lnb_ref[...])


def _swa_prompt(sinks, q, k, kr, vt, vtr, g, om, x, w_out, layer, ln_g, ln_b, *, B, S, tq):
    nq = S // tq
    row = lambda w: pl.BlockSpec((tq, w), lambda b, i: (b * nq + i, 0))
    full = lambda shape: pl.BlockSpec(shape, lambda b, i: (0,) * len(shape))
    seq = pl.BlockSpec((1, S, LANES), lambda b, i: (b, 0, 0))
    seqt = pl.BlockSpec((1, S // SWA_TILE, LANES, SWA_TILE), lambda b, i: (b, 0, 0, 0))
    r3 = lambda t: t.reshape(B, S, LANES)
    r4 = lambda t: t.reshape(B, S // SWA_TILE, LANES, SWA_TILE)
    return pl.pallas_call(
        functools.partial(_swa_prompt_kernel, tq),
        grid=(B, nq),
        in_specs=[pl.BlockSpec(memory_space=pltpu.SMEM), row(SW_W), seq, seq, seqt, seqt, row(SW_W), row(MEM_W),
                  row(D_MODEL), _resident((1, MIX_W, D_MODEL), layer, 0, 0), full((1, D_MODEL)), full((1, D_MODEL))],
        out_specs=row(D_MODEL),
        out_shape=jax.ShapeDtypeStruct((B * S, D_MODEL), F32),
        scratch_shapes=[pltpu.VMEM((tq, MIX_W), BF16)],
        compiler_params=_params("arbitrary", "arbitrary"),
        name="swa_prompt",
    )(sinks, q, r3(k), r3(kr), r4(vt), r4(vtr), g, om, x, w_out, ln_g, ln_b)


def _sample_tail(b, last, o_first, g_ref, mq_ref, gm_ref, cmk_ref, cmv_ref, x_ref, wout_ref, lng_ref, lnb_ref,
                 y_ref, mix_ref, T):
    om = _mem_attend(mq_ref[...], _head_rows(cmk_ref, MEM_HEADS, 0, 0), _head_rows(cmv_ref, MEM_HEADS, 0, 0))
    om = om * gm_ref[...].astype(F32)
    mix = jnp.concatenate([(o_first * g_ref[...].astype(F32)).astype(BF16), om.astype(BF16)], axis=1)
    mix_ref[pl.ds(pl.multiple_of(b * T, T), T), :] = mix

    @pl.when(last)
    def _():
        y_ref[...] = _merge(x_ref[...], mix_ref[...], wout_ref[0], lng_ref[...], lnb_ref[...])


def _diff_sample_kernel(lam_init, T, lamv_ref, q_ref, kn_ref, vn_ref, ckt_ref, cv_ref, g_ref, mq_ref, gm_ref,
                        cmk_ref, cmv_ref, x_ref, wout_ref, lng_ref, lnb_ref, dng_ref, y_ref,
                        m_ref, l_ref, acc_ref, mix_ref):
    b, j = pl.program_id(0), pl.program_id(1)
    nb, nj = pl.num_programs(0), pl.num_programs(1)
    heads = [slice(h * DA_V, (h + 1) * DA_V) for h in range(DA_HEADS)]
    qqs = [_split_components(q_ref[:, hs]) for hs in heads]

    @pl.when(j == 0)
    def _():
        m_ref[...] = jnp.full(m_ref.shape, -jnp.inf, F32)
        l_ref[...] = jnp.zeros(l_ref.shape, F32)
        acc_ref[...] = jnp.zeros(acc_ref.shape, F32)
        _flash_update(qqs, lambda h: kn_ref[:, heads[h]], lambda h: vn_ref[:, heads[h]], False, m_ref, l_ref, acc_ref)

    _flash_update(qqs, lambda h: ckt_ref[0, heads[h], :].astype(BF16), _head_rows(cv_ref, DA_HEADS, 0), True,
                  m_ref, l_ref, acc_ref)

    @pl.when(j == nj - 1)
    def _():
        lam = _lambda(lamv_ref[...], lam_init)
        rows = [slice(h * 2 * T, (h + 1) * 2 * T) for h in range(DA_HEADS)]
        od = jnp.concatenate([_diff_finish(acc_ref[r, :], l_ref[r, :], lam, dng_ref[...], lam_init) for r in rows],
                             axis=1)
        _sample_tail(b, b == nb - 1, od, g_ref, mq_ref, gm_ref, cmk_ref, cmv_ref, x_ref, wout_ref, lng_ref,
                     lnb_ref, y_ref, mix_ref, T)


def _diff_sample(lam_init, lamv, q, kn, vn, ckt, cv, g, mq, gm, cmk, cmv, layer, x, w_out, ln_g, ln_b, dng, *,
                 NB, T, tk):
    P = ckt.shape[2]
    row = lambda w: pl.BlockSpec((T, w), lambda b, j: (b, 0))
    full = lambda shape: pl.BlockSpec(shape, lambda b, j: (0,) * len(shape))
    memc = pl.BlockSpec((1, 1, N_MEM * MEM_HEADS, MEM_HD), lambda b, j: (layer, b, 0, 0))
    return pl.pallas_call(
        functools.partial(_diff_sample_kernel, lam_init, T),
        grid=(NB, P // tk),
        in_specs=[full((4, DA_QK)), row(DA_W), row(DA_W), row(DA_W),
                  pl.BlockSpec((1, DA_W, tk), lambda b, j: (b, 0, j)),
                  pl.BlockSpec((1, tk * DA_HEADS, DA_V), lambda b, j: (b, j, 0)),
                  row(DA_W), row(MEM_W), row(MEM_W), memc, memc, full((NB * T, D_MODEL)), _resident((1, MIX_W, D_MODEL), layer, 0, 0),
                  full((1, D_MODEL)), full((1, D_MODEL)), full((1, DA_V))],
        out_specs=full((NB * T, D_MODEL)),
        out_shape=jax.ShapeDtypeStruct((NB * T, D_MODEL), F32),
        scratch_shapes=[pltpu.VMEM((DA_HEADS * 2 * T, 1), F32), pltpu.VMEM((DA_HEADS * 2 * T, 1), F32),
                        pltpu.VMEM((DA_HEADS * 2 * T, DA_V), F32), pltpu.VMEM((NB * T, MIX_W), BF16)],
        compiler_params=_params("arbitrary", "arbitrary"),
        name="diff_sample",
    )(lamv, q, kn, vn, ckt, cv, g, mq, gm, cmk, cmv, x, w_out, ln_g, ln_b, dng)


def _swa_sample_kernel(T, sinks_ref, q_ref, kn_ref, knr_ref, vn_ref, vnr_ref, ckt_ref, cvt_ref, g_ref, mq_ref, gm_ref,
                       cmk_ref, cmv_ref, x_ref, wout_ref, lng_ref, lnb_ref, y_ref, mix_ref):
    b = pl.program_id(0)
    swap = lambda t: jnp.concatenate([t[SW_HD:], t[:SW_HD]], axis=0).astype(BF16)
    ckt, cvt = ckt_ref[0], cvt_ref[0]
    kt, ktr, vt, vtr = ckt.astype(BF16), swap(ckt), cvt.astype(BF16), swap(cvt)
    kn, knr, vn, vnr = kn_ref[...], knr_ref[...], vn_ref[...], vnr_ref[...]
    attend = lambda qq, swapped, sink: _sink_attend_cached(
        qq, ktr if swapped else kt, knr if swapped else kn, vtr if swapped else vt, vnr if swapped else vn, sink)
    o = _swa_heads(q_ref[...], attend, lambda h: sinks_ref[h] * LOG2E)
    _sample_tail(b, b == pl.num_programs(0) - 1, o, g_ref, mq_ref, gm_ref, cmk_ref, cmv_ref, x_ref, wout_ref,
                 lng_ref, lnb_ref, y_ref, mix_ref, T)


def _swa_sample(sinks, q, kn, knr, vn, vnr, ckt, cvt, g, mq, gm, cmk, cmv, layer, x, w_out, ln_g, ln_b, *, NB, T):
    W = ckt.shape[2]
    row = lambda w: pl.BlockSpec((T, w), lambda b: (b, 0))
    full = lambda shape: pl.BlockSpec(shape, lambda b: (0,) * len(shape))
    cache = pl.BlockSpec((1, LANES, W), lambda b: (b, 0, 0))
    memc = pl.BlockSpec((1, 1, N_MEM * MEM_HEADS, MEM_HD), lambda b: (layer, b, 0, 0))
    return pl.pallas_call(
        functools.partial(_swa_sample_kernel, T),
        grid=(NB,),
        in_specs=[pl.BlockSpec(memory_space=pltpu.SMEM), row(SW_W), row(LANES), row(LANES), row(LANES), row(LANES),
                  cache, cache, row(SW_W), row(MEM_W), row(MEM_W), memc, memc, full((NB * T, D_MODEL)),
                  _resident((1, MIX_W, D_MODEL), layer, 0, 0), full((1, D_MODEL)), full((1, D_MODEL))],
        out_specs=full((NB * T, D_MODEL)),
        out_shape=jax.ShapeDtypeStruct((NB * T, D_MODEL), F32),
        scratch_shapes=[pltpu.VMEM((NB * T, MIX_W), BF16)],
        compiler_params=_params("arbitrary"),
        name="swa_sample",
    )(sinks, q, kn, knr, vn, vnr, ckt, cvt, g, mq, gm, cmk, cmv, x, w_out, ln_g, ln_b)


def _rope_tables(pos):
    T = pos.shape[0]
    inv = ROPE_THETA ** (-np.arange(ROPE_HALF, dtype=np.float64) / ROPE_HALF)
    ang = pos.astype(np.float64)[:, None] * inv[None, :]
    cos, sin = np.cos(ang), np.sin(ang)
    rest = SW_HD - 2 * ROPE_HALF
    c = np.concatenate([cos, cos, np.ones((T, rest))], axis=1)
    a = np.concatenate([-sin, np.zeros((T, SW_HD - ROPE_HALF))], axis=1)
    b = np.concatenate([np.zeros((T, ROPE_HALF)), sin, np.zeros((T, rest))], axis=1)
    return tuple(jnp.asarray(np.tile(t, (1, LANES // SW_HD)), F32) for t in (c, a, b))


def _feature_major(t):
    n = t.ndim
    return jnp.transpose(t, (0,) + tuple(range(2, n)) + (1,)).reshape(t.shape[0], -1, t.shape[1])


def _rows_major(t, tail):
    n = len(tail)
    return jnp.transpose(t.reshape((t.shape[0],) + tail + (t.shape[2],)), (0, n + 1) + tuple(range(1, n + 1)))


def kernel(x_prompt, x_sample, mem_prompt, cache_diff_k, cache_diff_v, cache_swa_k, cache_swa_v, cache_mem_k,
           cache_mem_v, w_in_a, lam_q1, lam_k1, lam_q2, lam_k2, diff_norm_g, w_in_b, sinks, w_kv_shared, w_mem_kv,
           w_out, ln_g, ln_b):
    B, S, _ = x_prompt.shape
    NB, T, _ = x_sample.shape
    P = cache_diff_k.shape[2]
    assert w_in_a.shape[0] == 1 and w_in_b.shape[0] == 1 and w_out.shape[0] == DEPTH

    tabs_p = _rope_tables(np.arange(S))
    tabs_s = _rope_tables(np.tile(P + np.arange(T), NB))
    xp = x_prompt.reshape(B * S, D_MODEL)
    xs = x_sample.reshape(NB * T, D_MODEL)
    wa, wb, wo = (w_in_a[0],), (w_in_b[0], w_kv_shared), w_out
    lng = ln_g.reshape(DEPTH, 1, D_MODEL)
    lnb = ln_b.reshape(DEPTH, 1, D_MODEL)
    lamv = jnp.concatenate([lam_q1, lam_k1, lam_q2, lam_k2], axis=0)
    dng = diff_norm_g.reshape(1, DA_V)
    lam_init = 0.8 - 0.6 * math.exp(-0.3 * 0)

    mkf, mvf, mkb, mvb = _mem_kv(mem_prompt.reshape(B * N_MEM, D_MODEL), w_mem_kv)
    mem_b = lambda l: (mkb[l].reshape(B, N_MEM, MEM_W), mvb[l].reshape(B, N_MEM, MEM_W))
    head_rows = lambda t: t.reshape(t.shape[:-3] + (t.shape[-3] * t.shape[-2], t.shape[-1]))
    cmk, cmv = head_rows(cache_mem_k), head_rows(cache_mem_v)

    q, kt, vf, kb, vb, g, om = _project("a", xp, wa, tabs_p, mem_b(0), tm=512, rows_per_batch=S)
    xp1 = _diff_prompt(lam_init, lamv, q, kb, vb, g, om, xp, wo, 0, lng[0], lnb[0], dng, B=B, S=S)
    qs, kfs, vfs, kbs, vbs, gs_, mqs, gms = _project("a", xs, wa, tabs_s, tm=NB * T)
    xs1 = _diff_sample(lam_init, lamv, qs, kbs, vbs, _feature_major(cache_diff_k[0]), head_rows(cache_diff_v[0]), gs_,
                       mqs, gms, cmk, cmv, 0, xs, wo, lng[0], lnb[0], dng, NB=NB, T=T, tk=2048)

    q, g, om, skf, svf, kb1, kr1, vb1, vr1 = _project("b", xp1, wb, tabs_p, mem_b(1), tm=512, rows_per_batch=S)
    yp = _swa_prompt(sinks[0], q, kb1, kr1, vb1, vr1, g, om, xp1, wo, 1, lng[1], lnb[1], B=B, S=S, tq=512)
    qs, gs_, mqs, gms, skfs, svfs, kbs1, krs1, vbs1, vrs1 = _project("b", xs1, wb, tabs_s, tm=NB * T)
    ys = _swa_sample(sinks[0], qs, kbs1, krs1, vbs1, vrs1, _feature_major(cache_swa_k), _feature_major(cache_swa_v),
                     gs_, mqs, gms, cmk, cmv, 1, xs1, wo, lng[1], lnb[1], NB=NB, T=T)

    wr_p = min(WINDOW, S)
    window = lambda t: t.reshape(B, S, LANES)[:, S - wr_p:, :].reshape(B, wr_p, SW_KV, SW_HD)
    swa_kp, swa_vp = window(skf), window(svf)
    roll = lambda c, n: jnp.concatenate([c, n.reshape(NB, T, SW_KV, SW_HD)], axis=1)[:, T:]
    return (yp.reshape(B, S, D_MODEL), ys.reshape(NB, T, D_MODEL),
            _rows_major(kt, (DA_HEADS, 2, DA_QK))[None], vf.reshape(1, B, S, DA_HEADS, DA_V),
            kfs.reshape(1, NB, T, DA_HEADS, 2, DA_QK), vfs.reshape(1, NB, T, DA_HEADS, DA_V),
            swa_kp, swa_vp, roll(cache_swa_k, skfs), roll(cache_swa_v, svfs),
            mkf.reshape(DEPTH, B, N_MEM, MEM_HEADS, MEM_HD), mvf.reshape(DEPTH, B, N_MEM, MEM_HEADS, MEM_HD))
```

```python
import functools
import math

import jax
import jax.numpy as jnp
import numpy as np
from jax import lax
from jax.experimental import pallas as pl
from jax.experimental.pallas import tpu as pltpu

D_MODEL = 1024
CHUNK = 64
N_MEM = 256
DA_HEADS = 4
DA_QK = 64
DA_V = 128
DA_W = 512
SW_HEADS = 8
SW_KV = 2
SW_GROUP = 4
SW_HD = 64
SW_W = 512
WINDOW = 128
MEM_HEADS = 4
MEM_HD = 128
MEM_W = 512
MIX_W = 1024
ROPE_THETA = 500000.0
ROPE_HALF = 8
DEPTH = 2
DN_ALPHA = (2 * DEPTH) ** 0.25
LN_EPS = 1e-5
NEG = -1e30
LANES = 128
DIFF_TILE = 256
PIPE_LEAD = 4
UNROLL = 4
SWA_TILE = 128
SUM_ROWS = 16
LOG2E = math.log2(math.e)

F32 = jnp.float32
BF16 = jnp.bfloat16
VMEM_LIMIT = 48 * 1024 * 1024

_NT = (((1,), (1,)), ((), ()))


def _resident(shape, *first):
    idx = first or (0,) * len(shape)
    return pl.BlockSpec(shape, lambda *_: idx, pipeline_mode=pl.Buffered(1))


def _params(*sem):
    return pltpu.CompilerParams(dimension_semantics=sem, vmem_limit_bytes=VMEM_LIMIT)


def _rope(x, c, a, b):
    outs = []
    for i in range(x.shape[1] // LANES):
        blk = x[:, i * LANES:(i + 1) * LANES]
        outs.append(blk * c + pltpu.roll(blk, LANES - ROPE_HALF, 1) * a + pltpu.roll(blk, ROPE_HALF, 1) * b)
    return outs[0] if len(outs) == 1 else jnp.concatenate(outs, axis=1)


def _silu(g):
    return g * (1.0 / (1.0 + jnp.exp(-g)))


def _layer_norm(z, g, b):
    mu = jnp.mean(z, axis=-1, keepdims=True)
    d = z - mu
    var = jnp.mean(d * d, axis=-1, keepdims=True)
    return d * lax.rsqrt(var + LN_EPS) * g + b


def _mem_attend(mq, mk_of, mv_of):
    outs = []
    for h in range(MEM_HEADS):
        sl = slice(h * MEM_HD, (h + 1) * MEM_HD)
        s = lax.dot_general(mq[:, sl], mk_of(h), _NT, preferred_element_type=F32) * (MEM_HD ** -0.5)
        p = jnp.exp(s - jnp.max(s, axis=1, keepdims=True))
        l = jnp.sum(p, axis=1, keepdims=True)
        outs.append(jnp.dot(p.astype(BF16), mv_of(h), preferred_element_type=F32) / l)
    return jnp.concatenate(outs, axis=1)


def _head_cols(ref):
    return lambda h: ref[0, :, h * MEM_HD:(h + 1) * MEM_HD]


def _head_idx(ref, heads, h, *lead):
    return lead + (pl.ds(h, ref.shape[-2] // heads, stride=heads), slice(None))


def _head_rows(ref, heads, *lead):
    return lambda h: ref[_head_idx(ref, heads, h, *lead)].astype(BF16)


def _merge(x, mix, w_out, ln_g, ln_b):
    return _layer_norm(DN_ALPHA * x + jnp.dot(mix, w_out.astype(BF16), preferred_element_type=F32), ln_g, ln_b)


def _lambda(lamv, lam_init):
    e1 = jnp.exp(jnp.sum(lamv[0:1, :] * lamv[1:2, :], axis=1, keepdims=True))
    e2 = jnp.exp(jnp.sum(lamv[2:3, :] * lamv[3:4, :], axis=1, keepdims=True))
    return e1 - e2 + lam_init


def _split_components(qh):
    lane = lax.broadcasted_iota(jnp.int32, qh.shape, 1)
    zero = jnp.zeros_like(qh)
    return jnp.concatenate([jnp.where(lane < DA_QK, qh, zero), jnp.where(lane >= DA_QK, qh, zero)], axis=0)


def _flash_update(qqs, k_of, v_of, keys_on_lanes, m_ref, l_ref, acc_ref):
    if keys_on_lanes:
        s = [jnp.dot(qq, k_of(h), preferred_element_type=F32) for h, qq in enumerate(qqs)]
    else:
        s = [lax.dot_general(qq, k_of(h), _NT, preferred_element_type=F32) for h, qq in enumerate(qqs)]
    s = jnp.concatenate(s, axis=0)
    m_old = m_ref[...]
    m_new = jnp.maximum(m_old, jnp.max(s, axis=1, keepdims=True))
    alpha = jnp.exp2(m_old - m_new)
    p = jnp.exp2(s - m_new)
    l_ref[...] = alpha * l_ref[...] + jnp.sum(p, axis=1, keepdims=True)
    pb = p.astype(BF16)
    r = qqs[0].shape[0]
    pv = [jnp.dot(pb[h * r:(h + 1) * r], v_of(h), preferred_element_type=F32) for h in range(len(qqs))]
    acc_ref[...] = alpha * acc_ref[...] + jnp.concatenate(pv, axis=0)
    m_ref[...] = m_new


def _stage_scores(qq, ks, mask, s_ref, m_ref, alpha_ref):
    s = lax.dot_general(ks, qq, _NT, preferred_element_type=F32)
    if mask is not None:
        s = jnp.where(mask, s, NEG)
    s_ref[...] = s
    m_old = m_ref[...]
    m_new = jnp.maximum(m_old, jnp.max(s, axis=0, keepdims=True))
    alpha_ref[...] = jnp.exp2(m_old - m_new)
    m_ref[...] = m_new


def _stage_exp(s_ref, p_ref, m_ref):
    p_ref[...] = jnp.exp2(s_ref[...] - m_ref[...]).astype(BF16)


def _stage_pv(vts, p_ref, alpha_ref, acc_ref):
    ones = jnp.ones((SUM_ROWS, vts.shape[1]), BF16)
    pv = jnp.dot(jnp.concatenate([vts, ones], axis=0), p_ref[...], preferred_element_type=F32)
    acc_ref[...] = alpha_ref[...] * acc_ref[...] + pv


def _diff_finish(acc, l, lam, dng, lam_init):
    rows = acc.shape[0] // 2
    o = acc / l
    od = o[:rows] - lam * o[rows:]
    return od * lax.rsqrt(jnp.mean(od * od, axis=-1, keepdims=True) + LN_EPS) * dng * (1.0 - lam_init)


def _sink_attend_cached(qq, kt, kn, vt, vn, sink):
    s_c = jnp.dot(qq, kt, preferred_element_type=F32)
    s_n = lax.dot_general(qq, kn, _NT, preferred_element_type=F32)
    m = jnp.maximum(jnp.maximum(jnp.max(s_c, axis=1, keepdims=True), jnp.max(s_n, axis=1, keepdims=True)), sink)
    p_c, p_n = jnp.exp2(s_c - m), jnp.exp2(s_n - m)
    l = jnp.sum(p_c, axis=1, keepdims=True) + jnp.sum(p_n, axis=1, keepdims=True) + jnp.exp2(sink - m)
    o = (lax.dot_general(p_c.astype(BF16), vt, _NT, preferred_element_type=F32)
         + jnp.dot(p_n.astype(BF16), vn, preferred_element_type=F32))
    return o / l


def _window_scores(qq4, ks, mask):
    return jnp.where(mask, lax.dot_general(ks, qq4, _NT, preferred_element_type=F32), NEG)


def _window_attend(s, vts, sinks4):
    m = jnp.maximum(jnp.max(s, axis=0, keepdims=True), sinks4)
    p = jnp.exp2(s - m).astype(BF16)
    ones = jnp.ones((SUM_ROWS, vts.shape[1]), BF16)
    o = jnp.dot(jnp.concatenate([vts, ones], axis=0), p, preferred_element_type=F32)
    return o[:LANES] / (o[LANES:LANES + 1] + jnp.exp2(sinks4 - m))


def _swa_heads(q, attend, sink_of):
    low = lax.broadcasted_iota(jnp.int32, (q.shape[0], LANES), 1) < SW_HD
    outs = []
    for pair in range(SW_HEADS // 2):
        kv = pair // 2
        slab = q[:, pair * LANES:(pair + 1) * LANES]
        zero = jnp.zeros_like(slab)
        o_even = attend(jnp.where(low, slab, zero), kv == 1, sink_of(2 * pair))
        o_odd = attend(jnp.where(low, zero, slab), kv == 0, sink_of(2 * pair + 1))
        outs.append(jnp.where(low, o_even, o_odd))
    return jnp.concatenate(outs, axis=1)


def _mem_kv_kernel(mem_ref, w_ref, kf_ref, vf_ref, kb_ref, vb_ref):
    h = jnp.dot(mem_ref[...].astype(BF16), w_ref[0].astype(BF16), preferred_element_type=F32)
    k, v = h[:, :MEM_W], h[:, MEM_W:]
    for hd in range(MEM_HEADS):
        kf_ref[_head_idx(kf_ref, MEM_HEADS, hd, 0)] = k[:, hd * MEM_HD:(hd + 1) * MEM_HD]
        vf_ref[_head_idx(vf_ref, MEM_HEADS, hd, 0)] = v[:, hd * MEM_HD:(hd + 1) * MEM_HD]
    kb_ref[0] = k.astype(BF16)
    vb_ref[0] = v.astype(BF16)


def _mem_kv(mem_bf, w_bf):
    rows = mem_bf.shape[0]
    fshape, fspec = (DEPTH, rows * MEM_HEADS, MEM_HD), pl.BlockSpec((1, rows * MEM_HEADS, MEM_HD), lambda l: (l, 0, 0))
    bshape, bspec = (DEPTH, rows, MEM_W), pl.BlockSpec((1, rows, MEM_W), lambda l: (l, 0, 0))
    return pl.pallas_call(
        _mem_kv_kernel,
        grid=(DEPTH,),
        in_specs=[pl.BlockSpec((rows, D_MODEL), lambda l: (0, 0)),
                  pl.BlockSpec((1, D_MODEL, 2 * MEM_W), lambda l: (l, 0, 0))],
        out_specs=[fspec, fspec, bspec, bspec],
        out_shape=[jax.ShapeDtypeStruct(fshape, F32), jax.ShapeDtypeStruct(fshape, F32),
                   jax.ShapeDtypeStruct(bshape, BF16), jax.ShapeDtypeStruct(bshape, BF16)],
        compiler_params=_params("arbitrary"),
        name="mem_kv",
    )(mem_bf, w_bf)


def _proj_a_kernel(fuse_mem, x_ref, w_ref, c_ref, a_ref, b_ref, *refs):
    if fuse_mem:
        mk_ref, mv_ref, q_ref, kf_ref, vf_ref, kb_ref, vb_ref, g_ref, om_ref = refs
    else:
        q_ref, kf_ref, vf_ref, kb_ref, vb_ref, g_ref, mq_ref, gm_ref = refs
    x = x_ref[...].astype(BF16)
    c, a, b = c_ref[...], a_ref[...], b_ref[...]

    def cols(i):
        return jnp.dot(x, w_ref[:, i * DA_W:(i + 1) * DA_W].astype(BF16), preferred_element_type=F32)

    q_ref[...] = (_rope(cols(0), c, a, b) * (DA_QK ** -0.5 * LOG2E)).astype(BF16)
    k = _rope(cols(1), c, a, b)
    kb_ref[...] = k.astype(BF16)
    v = cols(2)
    for hd in range(DA_HEADS):
        vf_ref[_head_idx(vf_ref, DA_HEADS, hd)] = v[:, hd * DA_V:(hd + 1) * DA_V]
    if fuse_mem:
        kf_ref[0] = k.T
        for blk in range(vb_ref.shape[0]):
            vb_ref[blk] = v[blk * DIFF_TILE:(blk + 1) * DIFF_TILE, :].T.astype(BF16)
    else:
        kf_ref[...] = k
        vb_ref[...] = v.astype(BF16)
    g_ref[...] = _silu(cols(3)).astype(BF16)
    mq = cols(4).astype(BF16)
    gm = _silu(cols(5))
    if fuse_mem:
        om_ref[...] = (_mem_attend(mq, _head_cols(mk_ref), _head_cols(mv_ref)) * gm).astype(BF16)
    else:
        mq_ref[...] = mq
        gm_ref[...] = gm.astype(BF16)


def _proj_b_kernel(fuse_mem, x_ref, w_ref, wkv_ref, c_ref, a_ref, b_ref, *refs):
    if fuse_mem:
        mk_ref, mv_ref, q_ref, g_ref, om_ref, kf_ref, vf_ref, kb_ref, kr_ref, vb_ref, vr_ref = refs
    else:
        q_ref, g_ref, mq_ref, gm_ref, kf_ref, vf_ref, kb_ref, kr_ref, vb_ref, vr_ref = refs
    x = x_ref[...].astype(BF16)
    c, a, b = c_ref[...], a_ref[...], b_ref[...]

    def cols(i):
        return jnp.dot(x, w_ref[:, i * SW_W:(i + 1) * SW_W].astype(BF16), preferred_element_type=F32)

    q_ref[...] = (_rope(cols(0), c, a, b) * (SW_HD ** -0.5 * LOG2E)).astype(BF16)
    g_ref[...] = _silu(cols(1)).astype(BF16)
    mq = cols(2).astype(BF16)
    gm = _silu(cols(3))
    if fuse_mem:
        om_ref[...] = (_mem_attend(mq, _head_cols(mk_ref), _head_cols(mv_ref)) * gm).astype(BF16)
    else:
        mq_ref[...] = mq
        gm_ref[...] = gm.astype(BF16)
    kv = jnp.dot(x, wkv_ref[...].astype(BF16), preferred_element_type=F32)
    k = _rope(kv[:, :LANES], c, a, b)
    v = kv[:, LANES:]
    kf_ref[...] = k
    vf_ref[...] = v
    kb_ref[...] = k.astype(BF16)
    kr_ref[...] = pltpu.roll(k, SW_HD, 1).astype(BF16)
    vr = pltpu.roll(v, SW_HD, 1)
    if fuse_mem:
        for blk in range(vb_ref.shape[0]):
            rows = slice(blk * SWA_TILE, (blk + 1) * SWA_TILE)
            vb_ref[blk] = v[rows, :].T.astype(BF16)
            vr_ref[blk] = vr[rows, :].T.astype(BF16)
    else:
        vb_ref[...] = v.astype(BF16)
        vr_ref[...] = vr.astype(BF16)


def _project(layer, x, ws, tabs, mem=None, *, tm, rows_per_batch=None):
    M = x.shape[0]
    fuse = mem is not None
    per = (rows_per_batch or M) // tm
    row = lambda w: pl.BlockSpec((tm, w), lambda i: (i, 0))
    tab = pl.BlockSpec((tm, LANES), lambda i: (i % per, 0))
    in_specs = [row(D_MODEL)] + [_resident(w.shape) for w in ws] + [tab, tab, tab]
    args = [x, *ws, *tabs]
    if fuse:
        mspec = pl.BlockSpec((1, N_MEM, MEM_W), lambda i: (i // per, 0, 0))
        in_specs += [mspec, mspec]
        args += list(mem)
    o = lambda w, dt: (row(w), jax.ShapeDtypeStruct((M, w), dt))
    if layer == "a":
        vt = (pl.BlockSpec((tm // DIFF_TILE, DA_W, DIFF_TILE), lambda i: (i, 0, 0)),
              jax.ShapeDtypeStruct((M // DIFF_TILE, DA_W, DIFF_TILE), BF16))
        kt = (pl.BlockSpec((1, DA_W, tm), lambda i: (i // per, 0, i % per)),
              jax.ShapeDtypeStruct((M // (per * tm), DA_W, per * tm), F32))
        vf = (pl.BlockSpec((tm * DA_HEADS, DA_V), lambda i: (i, 0)), jax.ShapeDtypeStruct((M * DA_HEADS, DA_V), F32))
        outs = [o(DA_W, BF16), kt if fuse else o(DA_W, F32), vf, o(DA_W, BF16), vt if fuse else o(DA_W, BF16),
                o(DA_W, BF16)]
        body = _proj_a_kernel
    else:
        outs = [o(SW_W, BF16), o(SW_W, BF16)]
        body = _proj_b_kernel
    mem_outs = [o(MEM_W, BF16)] if fuse else [o(MEM_W, BF16), o(MEM_W, BF16)]
    if layer == "a":
        outs = outs + mem_outs
    else:
        vt = (pl.BlockSpec((tm // SWA_TILE, LANES, SWA_TILE), lambda i: (i, 0, 0)),
              jax.ShapeDtypeStruct((M // SWA_TILE, LANES, SWA_TILE), BF16))
        values = [vt, vt] if fuse else [o(LANES, BF16)] * 2
        outs = outs + mem_outs + [o(LANES, F32), o(LANES, F32)] + [o(LANES, BF16)] * 2 + values
    return pl.pallas_call(
        functools.partial(body, fuse),
        grid=(M // tm,),
        in_specs=in_specs,
        out_specs=[s for s, _ in outs],
        out_shape=[t for _, t in outs],
        compiler_params=_params("arbitrary"),
        name="proj_" + layer,
    )(*args)


def _diff_prompt_kernel(lam_init, lamv_ref, q_ref, k_ref, vt_ref, g_ref, om_ref, x_ref, wout_ref,
                        lng_ref, lnb_ref, dng_ref, y_ref, m_ref, alpha_ref, acc_ref, s_ref, p_ref, mix_ref):
    t = DIFF_TILE
    SLOTS = s_ref.shape[0]
    LEAD = PIPE_LEAD
    i = pl.program_id(1)
    heads = [slice(h * DA_V, (h + 1) * DA_V) for h in range(DA_HEADS)]
    lane = lax.broadcasted_iota(jnp.int32, (t, DA_V), 1)
    units = []
    for hs in heads:
        qh = q_ref[:, hs]
        zero = jnp.zeros_like(qh)
        units += [(hs, jnp.where(lane < DA_QK, qh, zero)), (hs, jnp.where(lane >= DA_QK, qh, zero))]
    m_ref[...] = jnp.full(m_ref.shape, -jnp.inf, F32)
    acc_ref[...] = jnp.zeros(acc_ref.shape, F32)

    n = len(units)
    assert n == SLOTS and LEAD + 1 < n

    def exp_stage(u):
        _stage_exp(s_ref.at[u], p_ref.at[u], m_ref.at[u])

    def pv_stage(j, u):
        _stage_pv(vt_ref[0, j, units[u][0], :], p_ref.at[u], alpha_ref.at[u], acc_ref.at[u])

    def block(j, mask, j_before):
        rows = pl.ds(pl.multiple_of(j * t, t), t)
        for u, (hs, qq) in enumerate(units):
            _stage_scores(qq, k_ref[0, rows, hs], mask, s_ref.at[u], m_ref.at[u], alpha_ref.at[u])
            if u >= LEAD:
                exp_stage(u - LEAD)
            elif j_before is not None:
                exp_stage(u - LEAD + n)
            if u >= LEAD + 1:
                pv_stage(j, u - LEAD - 1)
            elif j_before is not None:
                pv_stage(j_before, u - LEAD - 1 + n)

    def run(j0, count):
        block(j0, None, jnp.where(j0 == 0, i, j0 - 1))
        for d in range(1, count):
            block(j0 + d, None, j0 + d - 1)

    def quad(jj, carry):
        run(UNROLL * jj, UNROLL)
        return carry

    kc = lax.broadcasted_iota(jnp.int32, (t, t), 0) // CHUNK
    qc = lax.broadcasted_iota(jnp.int32, (t, t), 1) // CHUNK
    block(i, kc <= qc, None)
    lax.fori_loop(0, i // UNROLL, quad, 0)
    done = i // UNROLL * UNROLL
    rest = i % UNROLL
    for part in (2, 1):
        @pl.when(rest & part != 0)
        def _(part=part):
            run(done + (rest & ~(2 * part - 1)), part)

    last = jnp.where(i == 0, i, i - 1)
    for u in range(n - LEAD, n):
        exp_stage(u)
        pv_stage(last, u - 1)
    pv_stage(last, n - 1)

    lam = _lambda(lamv_ref[...], lam_init)
    for h, hs in enumerate(heads):
        o1 = acc_ref[2 * h, :DA_V, :] / acc_ref[2 * h, DA_V:DA_V + 1, :]
        o2 = acc_ref[2 * h + 1, :DA_V, :] / acc_ref[2 * h + 1, DA_V:DA_V + 1, :]
        od = (o1 - lam * o2).T
        od = od * lax.rsqrt(jnp.mean(od * od, axis=-1, keepdims=True) + LN_EPS) * dng_ref[...] * (1.0 - lam_init)
        mix_ref[:, hs] = (od * g_ref[:, hs].astype(F32)).astype(BF16)
    mix_ref[:, DA_W:] = om_ref[...]
    y_ref[...] = _merge(x_ref[...], mix_ref[...], wout_ref[0], lng_ref[...], lnb_ref[...])


def _diff_prompt(lam_init, lamv, q, k, vt, g, om, x, w_out, layer, ln_g, ln_b, dng, *, B, S):
    t = DIFF_TILE
    nq = S // t
    row = lambda w: pl.BlockSpec((t, w), lambda b, i: (b * nq + i, 0))
    full = lambda shape: pl.BlockSpec(shape, lambda b, i: (0,) * len(shape))
    return pl.pallas_call(
        functools.partial(_diff_prompt_kernel, lam_init),
        grid=(B, nq),
        in_specs=[full((4, DA_QK)), row(DA_W), pl.BlockSpec((1, S, DA_W), lambda b, i: (b, 0, 0)),
                  pl.BlockSpec((1, nq, DA_W, t), lambda b, i: (b, 0, 0, 0)), row(DA_W), row(MEM_W), row(D_MODEL),
                  _resident((1, MIX_W, D_MODEL), layer, 0, 0), full((1, D_MODEL)), full((1, D_MODEL)), full((1, DA_V))],
        out_specs=row(D_MODEL),
        out_shape=jax.ShapeDtypeStruct((B * S, D_MODEL), F32),
        scratch_shapes=[pltpu.VMEM((2 * DA_HEADS, 1, t), F32), pltpu.VMEM((2 * DA_HEADS, 1, t), F32),
                        pltpu.VMEM((2 * DA_HEADS, DA_V + SUM_ROWS, t), F32),
                        pltpu.VMEM((2 * DA_HEADS, t, t), F32), pltpu.VMEM((2 * DA_HEADS, t, t), BF16),
                        pltpu.VMEM((t, MIX_W), BF16)],
        compiler_params=_params("arbitrary", "arbitrary"),
        name="diff_prompt",
    )(lamv, q, k.reshape(B, S, DA_W), vt.reshape(B, nq, DA_W, t), g, om, x, w_out, ln_g, ln_b, dng)


def _swa_prompt_kernel(tq, sinks_ref, q_ref, k_ref, kr_ref, vt_ref, vtr_ref, g_ref, om_ref, x_ref, wout_ref,
                       lng_ref, lnb_ref, y_ref, mix_ref):
    i = pl.program_id(1)
    w, nk, nsub = SWA_TILE, 2 * SWA_TILE, tq // SWA_TILE
    low = lax.broadcasted_iota(jnp.int32, (w, LANES), 1) < SW_HD
    row_low = lax.broadcasted_iota(jnp.int32, (LANES, w), 0) < SW_HD
    sink_row = lambda hs: jnp.concatenate([jnp.full((1, w), sinks_ref[h] * LOG2E, F32) for h in hs], axis=1)
    sinks_plain, sinks_swapped = sink_row((0, 2, 5, 7)), sink_row((1, 3, 4, 6))
    kc = lax.broadcasted_iota(jnp.int32, (nk, 4 * w), 0) // CHUNK
    qc = lax.broadcasted_iota(jnp.int32, (nk, 4 * w), 1) % w // CHUNK
    firsts, scores = [], []
    for st in range(nsub):
        sub = i * nsub + st
        first = jnp.maximum(sub - 1, 0)
        back = (sub - first) * (w // CHUNK)
        mask = (kc - back <= qc) & (kc - back >= qc - WINDOW // CHUNK)
        rows = pl.ds(pl.multiple_of(first * w, w), nk)
        q = q_ref[st * w:(st + 1) * w, :]
        slabs = [q[:, p * LANES:(p + 1) * LANES] for p in range(SW_HEADS // 2)]
        even = [jnp.where(low, sl, jnp.zeros_like(sl)) for sl in slabs]
        odd = [jnp.where(low, jnp.zeros_like(sl), sl) for sl in slabs]
        firsts.append(first)
        scores.append((_window_scores(jnp.concatenate([even[0], even[1], odd[2], odd[3]], axis=0), k_ref[0, rows, :],
                                      mask),
                       _window_scores(jnp.concatenate([odd[0], odd[1], even[2], even[3]], axis=0), kr_ref[0, rows, :],
                                      mask)))
    for st, (first, (s_plain, s_swapped)) in enumerate(zip(firsts, scores)):
        vts = jnp.concatenate([vt_ref[0, first], vt_ref[0, first + 1]], axis=1)
        vtrs = jnp.concatenate([vtr_ref[0, first], vtr_ref[0, first + 1]], axis=1)
        o_plain = _window_attend(s_plain, vts, sinks_plain)
        o_swapped = _window_attend(s_swapped, vtrs, sinks_swapped)
        for p in range(SW_HEADS // 2):
            cols = slice(p * w, (p + 1) * w)
            lo, hi = (o_plain, o_swapped) if p < 2 else (o_swapped, o_plain)
            slab = jnp.where(row_low, lo[:, cols], hi[:, cols]).T
            gate = g_ref[st * w:(st + 1) * w, p * LANES:(p + 1) * LANES].astype(F32)
            mix_ref[st * w:(st + 1) * w, p * LANES:(p + 1) * LANES] = (slab * gate).astype(BF16)
    mix_ref[:, SW_W:] = om_ref[...]
    y_ref[...] = _merge(x_ref[...], mix_ref[...], wout_ref[0], lng_ref[...], lnb_ref[...])


def _swa_prompt(sinks, q, k, kr, vt, vtr, g, om, x, w_out, layer, ln_g, ln_b, *, B, S, tq):
    nq = S // tq
    row = lambda w: pl.BlockSpec((tq, w), lambda b, i: (b * nq + i, 0))
    full = lambda shape: pl.BlockSpec(shape, lambda b, i: (0,) * len(shape))
    seq = pl.BlockSpec((1, S, LANES), lambda b, i: (b, 0, 0))
    seqt = pl.BlockSpec((1, S // SWA_TILE, LANES, SWA_TILE), lambda b, i: (b, 0, 0, 0))
    r3 = lambda t: t.reshape(B, S, LANES)
    r4 = lambda t: t.reshape(B, S // SWA_TILE, LANES, SWA_TILE)
    return pl.pallas_call(
        functools.partial(_swa_prompt_kernel, tq),
        grid=(B, nq),
        in_specs=[pl.BlockSpec(memory_space=pltpu.SMEM), row(SW_W), seq, seq, seqt, seqt, row(SW_W), row(MEM_W),
                  row(D_MODEL), _resident((1, MIX_W, D_MODEL), layer, 0, 0), full((1, D_MODEL)), full((1, D_MODEL))],
        out_specs=row(D_MODEL),
        out_shape=jax.ShapeDtypeStruct((B * S, D_MODEL), F32),
        scratch_shapes=[pltpu.VMEM((tq, MIX_W), BF16)],
        compiler_params=_params("arbitrary", "arbitrary"),
        name="swa_prompt",
    )(sinks, q, r3(k), r3(kr), r4(vt), r4(vtr), g, om, x, w_out, ln_g, ln_b)


def _sample_tail(b, last, o_first, g_ref, mq_ref, gm_ref, cmk_ref, cmv_ref, x_ref, wout_ref, lng_ref, lnb_ref,
                 y_ref, mix_ref, T):
    om = _mem_attend(mq_ref[...], _head_rows(cmk_ref, MEM_HEADS, 0, 0), _head_rows(cmv_ref, MEM_HEADS, 0, 0))
    om = om * gm_ref[...].astype(F32)
    mix = jnp.concatenate([(o_first * g_ref[...].astype(F32)).astype(BF16), om.astype(BF16)], axis=1)
    mix_ref[pl.ds(pl.multiple_of(b * T, T), T), :] = mix

    @pl.when(last)
    def _():
        y_ref[...] = _merge(x_ref[...], mix_ref[...], wout_ref[0], lng_ref[...], lnb_ref[...])


def _diff_sample_kernel(lam_init, T, lamv_ref, q_ref, kn_ref, vn_ref, ckt_ref, cv_ref, g_ref, mq_ref, gm_ref,
                        cmk_ref, cmv_ref, x_ref, wout_ref, lng_ref, lnb_ref, dng_ref, y_ref,
                        m_ref, l_ref, acc_ref, mix_ref):
    b, j = pl.program_id(0), pl.program_id(1)
    nb, nj = pl.num_programs(0), pl.num_programs(1)
    heads = [slice(h * DA_V, (h + 1) * DA_V) for h in range(DA_HEADS)]
    qqs = [_split_components(q_ref[:, hs]) for hs in heads]

    @pl.when(j == 0)
    def _():
        m_ref[...] = jnp.full(m_ref.shape, -jnp.inf, F32)
        l_ref[...] = jnp.zeros(l_ref.shape, F32)
        acc_ref[...] = jnp.zeros(acc_ref.shape, F32)
        _flash_update(qqs, lambda h: kn_ref[:, heads[h]], lambda h: vn_ref[:, heads[h]], False, m_ref, l_ref, acc_ref)

    _flash_update(qqs, lambda h: ckt_ref[0, heads[h], :].astype(BF16), _head_rows(cv_ref, DA_HEADS, 0), True,
                  m_ref, l_ref, acc_ref)

    @pl.when(j == nj - 1)
    def _():
        lam = _lambda(lamv_ref[...], lam_init)
        rows = [slice(h * 2 * T, (h + 1) * 2 * T) for h in range(DA_HEADS)]
        od = jnp.concatenate([_diff_finish(acc_ref[r, :], l_ref[r, :], lam, dng_ref[...], lam_init) for r in rows],
                             axis=1)
        _sample_tail(b, b == nb - 1, od, g_ref, mq_ref, gm_ref, cmk_ref, cmv_ref, x_ref, wout_ref, lng_ref,
                     lnb_ref, y_ref, mix_ref, T)


def _diff_sample(lam_init, lamv, q, kn, vn, ckt, cv, g, mq, gm, cmk, cmv, layer, x, w_out, ln_g, ln_b, dng, *,
                 NB, T, tk):
    P = ckt.shape[2]
    row = lambda w: pl.BlockSpec((T, w), lambda b, j: (b, 0))
    full = lambda shape: pl.BlockSpec(shape, lambda b, j: (0,) * len(shape))
    memc = pl.BlockSpec((1, 1, N_MEM * MEM_HEADS, MEM_HD), lambda b, j: (layer, b, 0, 0))
    return pl.pallas_call(
        functools.partial(_diff_sample_kernel, lam_init, T),
        grid=(NB, P // tk),
        in_specs=[full((4, DA_QK)), row(DA_W), row(DA_W), row(DA_W),
                  pl.BlockSpec((1, DA_W, tk), lambda b, j: (b, 0, j)),
                  pl.BlockSpec((1, tk * DA_HEADS, DA_V), lambda b, j: (b, j, 0)),
                  row(DA_W), row(MEM_W), row(MEM_W), memc, memc, full((NB * T, D_MODEL)), _resident((1, MIX_W, D_MODEL), layer, 0, 0),
                  full((1, D_MODEL)), full((1, D_MODEL)), full((1, DA_V))],
        out_specs=full((NB * T, D_MODEL)),
        out_shape=jax.ShapeDtypeStruct((NB * T, D_MODEL), F32),
        scratch_shapes=[pltpu.VMEM((DA_HEADS * 2 * T, 1), F32), pltpu.VMEM((DA_HEADS * 2 * T, 1), F32),
                        pltpu.VMEM((DA_HEADS * 2 * T, DA_V), F32), pltpu.VMEM((NB * T, MIX_W), BF16)],
        compiler_params=_params("arbitrary", "arbitrary"),
        name="diff_sample",
    )(lamv, q, kn, vn, ckt, cv, g, mq, gm, cmk, cmv, x, w_out, ln_g, ln_b, dng)


def _swa_sample_kernel(T, sinks_ref, q_ref, kn_ref, knr_ref, vn_ref, vnr_ref, ckt_ref, cvt_ref, g_ref, mq_ref, gm_ref,
                       cmk_ref, cmv_ref, x_ref, wout_ref, lng_ref, lnb_ref, y_ref, mix_ref):
    b = pl.program_id(0)
    swap = lambda t: jnp.concatenate([t[SW_HD:], t[:SW_HD]], axis=0).astype(BF16)
    ckt, cvt = ckt_ref[0], cvt_ref[0]
    kt, ktr, vt, vtr = ckt.astype(BF16), swap(ckt), cvt.astype(BF16), swap(cvt)
    kn, knr, vn, vnr = kn_ref[...], knr_ref[...], vn_ref[...], vnr_ref[...]
    attend = lambda qq, swapped, sink: _sink_attend_cached(
        qq, ktr if swapped else kt, knr if swapped else kn, vtr if swapped else vt, vnr if swapped else vn, sink)
    o = _swa_heads(q_ref[...], attend, lambda h: sinks_ref[h] * LOG2E)
    _sample_tail(b, b == pl.num_programs(0) - 1, o, g_ref, mq_ref, gm_ref, cmk_ref, cmv_ref, x_ref, wout_ref,
                 lng_ref, lnb_ref, y_ref, mix_ref, T)


def _swa_sample(sinks, q, kn, knr, vn, vnr, ckt, cvt, g, mq, gm, cmk, cmv, layer, x, w_out, ln_g, ln_b, *, NB, T):
    W = ckt.shape[2]
    row = lambda w: pl.BlockSpec((T, w), lambda b: (b, 0))
    full = lambda shape: pl.BlockSpec(shape, lambda b: (0,) * len(shape))
    cache = pl.BlockSpec((1, LANES, W), lambda b: (b, 0, 0))
    memc = pl.BlockSpec((1, 1, N_MEM * MEM_HEADS, MEM_HD), lambda b: (layer, b, 0, 0))
    return pl.pallas_call(
        functools.partial(_swa_sample_kernel, T),
        grid=(NB,),
        in_specs=[pl.BlockSpec(memory_space=pltpu.SMEM), row(SW_W), row(LANES), row(LANES), row(LANES), row(LANES),
                  cache, cache, row(SW_W), row(MEM_W), row(MEM_W), memc, memc, full((NB * T, D_MODEL)),
                  _resident((1, MIX_W, D_MODEL), layer, 0, 0), full((1, D_MODEL)), full((1, D_MODEL))],
        out_specs=full((NB * T, D_MODEL)),
        out_shape=jax.ShapeDtypeStruct((NB * T, D_MODEL), F32),
        scratch_shapes=[pltpu.VMEM((NB * T, MIX_W), BF16)],
        compiler_params=_params("arbitrary"),
        name="swa_sample",
    )(sinks, q, kn, knr, vn, vnr, ckt, cvt, g, mq, gm, cmk, cmv, x, w_out, ln_g, ln_b)


def _rope_tables(pos):
    T = pos.shape[0]
    inv = ROPE_THETA ** (-np.arange(ROPE_HALF, dtype=np.float64) / ROPE_HALF)
    ang = pos.astype(np.float64)[:, None] * inv[None, :]
    cos, sin = np.cos(ang), np.sin(ang)
    rest = SW_HD - 2 * ROPE_HALF
    c = np.concatenate([cos, cos, np.ones((T, rest))], axis=1)
    a = np.concatenate([-sin, np.zeros((T, SW_HD - ROPE_HALF))], axis=1)
    b = np.concatenate([np.zeros((T, ROPE_HALF)), sin, np.zeros((T, rest))], axis=1)
    return tuple(jnp.asarray(np.tile(t, (1, LANES // SW_HD)), F32) for t in (c, a, b))


def _feature_major(t):
    n = t.ndim
    return jnp.transpose(t, (0,) + tuple(range(2, n)) + (1,)).reshape(t.shape[0], -1, t.shape[1])


def _rows_major(t, tail):
    n = len(tail)
    return jnp.transpose(t.reshape((t.shape[0],) + tail + (t.shape[2],)), (0, n + 1) + tuple(range(1, n + 1)))


def kernel(x_prompt, x_sample, mem_prompt, cache_diff_k, cache_diff_v, cache_swa_k, cache_swa_v, cache_mem_k,
           cache_mem_v, w_in_a, lam_q1, lam_k1, lam_q2, lam_k2, diff_norm_g, w_in_b, sinks, w_kv_shared, w_mem_kv,
           w_out, ln_g, ln_b):
    B, S, _ = x_prompt.shape
    NB, T, _ = x_sample.shape
    P = cache_diff_k.shape[2]
    assert w_in_a.shape[0] == 1 and w_in_b.shape[0] == 1 and w_out.shape[0] == DEPTH

    tabs_p = _rope_tables(np.arange(S))
    tabs_s = _rope_tables(np.tile(P + np.arange(T), NB))
    xp = x_prompt.reshape(B * S, D_MODEL)
    xs = x_sample.reshape(NB * T, D_MODEL)
    wa, wb, wo = (w_in_a[0],), (w_in_b[0], w_kv_shared), w_out
    lng = ln_g.reshape(DEPTH, 1, D_MODEL)
    lnb = ln_b.reshape(DEPTH, 1, D_MODEL)
    lamv = jnp.concatenate([lam_q1, lam_k1, lam_q2, lam_k2], axis=0)
    dng = diff_norm_g.reshape(1, DA_V)
    lam_init = 0.8 - 0.6 * math.exp(-0.3 * 0)

    mkf, mvf, mkb, mvb = _mem_kv(mem_prompt.reshape(B * N_MEM, D_MODEL), w_mem_kv)
    mem_b = lambda l: (mkb[l].reshape(B, N_MEM, MEM_W), mvb[l].reshape(B, N_MEM, MEM_W))
    head_rows = lambda t: t.reshape(t.shape[:-3] + (t.shape[-3] * t.shape[-2], t.shape[-1]))
    cmk, cmv = head_rows(cache_mem_k), head_rows(cache_mem_v)

    q, kt, vf, kb, vb, g, om = _project("a", xp, wa, tabs_p, mem_b(0), tm=512, rows_per_batch=S)
    xp1 = _diff_prompt(lam_init, lamv, q, kb, vb, g, om, xp, wo, 0, lng[0], lnb[0], dng, B=B, S=S)
    qs, kfs, vfs, kbs, vbs, gs_, mqs, gms = _project("a", xs, wa, tabs_s, tm=NB * T)
    xs1 = _diff_sample(lam_init, lamv, qs, kbs, vbs, _feature_major(cache_diff_k[0]), head_rows(cache_diff_v[0]), gs_,
                       mqs, gms, cmk, cmv, 0, xs, wo, lng[0], lnb[0], dng, NB=NB, T=T, tk=2048)

    q, g, om, skf, svf, kb1, kr1, vb1, vr1 = _project("b", xp1, wb, tabs_p, mem_b(1), tm=512, rows_per_batch=S)
    yp = _swa_prompt(sinks[0], q, kb1, kr1, vb1, vr1, g, om, xp1, wo, 1, lng[1], lnb[1], B=B, S=S, tq=512)
    qs, gs_, mqs, gms, skfs, svfs, kbs1, krs1, vbs1, vrs1 = _project("b", xs1, wb, tabs_s, tm=NB * T)
    ys = _swa_sample(sinks[0], qs, kbs1, krs1, vbs1, vrs1, _feature_major(cache_swa_k), _feature_major(cache_swa_v),
                     gs_, mqs, gms, cmk, cmv, 1, xs1, wo, lng[1], lnb[1], NB=NB, T=T)

    wr_p = min(WINDOW, S)
    window = lambda t: t.reshape(B, S, LANES)[:, S - wr_p:, :].reshape(B, wr_p, SW_KV, SW_HD)
    swa_kp, swa_vp = window(skf), window(svf)
    roll = lambda c, n: jnp.concatenate([c, n.reshape(NB, T, SW_KV, SW_HD)], axis=1)[:, T:]
    return (yp.reshape(B, S, D_MODEL), ys.reshape(NB, T, D_MODEL),
            _rows_major(kt, (DA_HEADS, 2, DA_QK))[None], vf.reshape(1, B, S, DA_HEADS, DA_V),
            kfs.reshape(1, NB, T, DA_HEADS, 2, DA_QK), vfs.reshape(1, NB, T, DA_HEADS, DA_V),
            swa_kp, swa_vp, roll(cache_swa_k, skfs), roll(cache_swa_v, svfs),
            mkf.reshape(DEPTH, B, N_MEM, MEM_HEADS, MEM_HD), mvf.reshape(DEPTH, B, N_MEM, MEM_HEADS, MEM_HD))
```

```python
import functools
import math

import jax
import jax.numpy as jnp
import numpy as np
from jax import lax
from jax.experimental import pallas as pl
from jax.experimental.pallas import tpu as pltpu

D_MODEL = 1024
CHUNK = 64
N_MEM = 256
DA_HEADS = 4
DA_QK = 64
DA_V = 128
DA_W = 512
SW_HEADS = 8
SW_KV = 2
SW_GROUP = 4
SW_HD = 64
SW_W = 512
WINDOW = 128
MEM_HEADS = 4
MEM_HD = 128
MEM_W = 512
MIX_W = 1024
ROPE_THETA = 500000.0
ROPE_HALF = 8
DEPTH = 2
DN_ALPHA = (2 * DEPTH) ** 0.25
LN_EPS = 1e-5
NEG = -1e30
LANES = 128
DIFF_TILE = 256
PIPE_LEAD = 4
UNROLL = 4
SWA_TILE = 128
SUM_ROWS = 16
PROJ_ROWS = 512
SWA_ROWS = 512
SAMPLE_KEYS = 2048
LOG2E = math.log2(math.e)

F32 = jnp.float32
BF16 = jnp.bfloat16
VMEM_LIMIT = 48 * 1024 * 1024

_NT = (((1,), (1,)), ((), ()))


def _resident(shape, *first):
    idx = first or (0,) * len(shape)
    return pl.BlockSpec(shape, lambda *_: idx, pipeline_mode=pl.Buffered(1))


def _params(*sem):
    return pltpu.CompilerParams(dimension_semantics=sem, vmem_limit_bytes=VMEM_LIMIT)


def _rope(x, c, a, b):
    outs = []
    for i in range(x.shape[1] // LANES):
        blk = x[:, i * LANES:(i + 1) * LANES]
        outs.append(blk * c + pltpu.roll(blk, LANES - ROPE_HALF, 1) * a + pltpu.roll(blk, ROPE_HALF, 1) * b)
    return outs[0] if len(outs) == 1 else jnp.concatenate(outs, axis=1)


def _silu(g):
    return g * (1.0 / (1.0 + jnp.exp(-g)))


def _layer_norm(z, g, b):
    mu = jnp.mean(z, axis=-1, keepdims=True)
    d = z - mu
    var = jnp.mean(d * d, axis=-1, keepdims=True)
    return d * lax.rsqrt(var + LN_EPS) * g + b


def _mem_attend(mq, mk_of, mv_of):
    outs = []
    for h in range(MEM_HEADS):
        sl = slice(h * MEM_HD, (h + 1) * MEM_HD)
        s = lax.dot_general(mq[:, sl], mk_of(h), _NT, preferred_element_type=F32) * (MEM_HD ** -0.5)
        p = jnp.exp(s - jnp.max(s, axis=1, keepdims=True))
        l = jnp.sum(p, axis=1, keepdims=True)
        outs.append(jnp.dot(p.astype(BF16), mv_of(h), preferred_element_type=F32) / l)
    return jnp.concatenate(outs, axis=1)


def _head_cols(ref):
    return lambda h: ref[0, :, h * MEM_HD:(h + 1) * MEM_HD]


def _head_idx(ref, heads, h, *lead):
    return lead + (pl.ds(h, ref.shape[-2] // heads, stride=heads), slice(None))


def _head_rows(ref, heads, *lead):
    return lambda h: ref[_head_idx(ref, heads, h, *lead)].astype(BF16)


def _merge(x, mix, w_out, ln_g, ln_b):
    return _layer_norm(DN_ALPHA * x + jnp.dot(mix, w_out.astype(BF16), preferred_element_type=F32), ln_g, ln_b)


def _lambda(lamv, lam_init):
    e1 = jnp.exp(jnp.sum(lamv[0:1, :] * lamv[1:2, :], axis=1, keepdims=True))
    e2 = jnp.exp(jnp.sum(lamv[2:3, :] * lamv[3:4, :], axis=1, keepdims=True))
    return e1 - e2 + lam_init


def _split_components(qh):
    lane = lax.broadcasted_iota(jnp.int32, qh.shape, 1)
    zero = jnp.zeros_like(qh)
    return jnp.concatenate([jnp.where(lane < DA_QK, qh, zero), jnp.where(lane >= DA_QK, qh, zero)], axis=0)


def _flash_update(qqs, k_of, v_of, keys_on_lanes, m_ref, l_ref, acc_ref):
    if keys_on_lanes:
        s = [jnp.dot(qq, k_of(h), preferred_element_type=F32) for h, qq in enumerate(qqs)]
    else:
        s = [lax.dot_general(qq, k_of(h), _NT, preferred_element_type=F32) for h, qq in enumerate(qqs)]
    s = jnp.concatenate(s, axis=0)
    m_old = m_ref[...]
    m_new = jnp.maximum(m_old, jnp.max(s, axis=1, keepdims=True))
    alpha = jnp.exp2(m_old - m_new)
    p = jnp.exp2(s - m_new)
    l_ref[...] = alpha * l_ref[...] + jnp.sum(p, axis=1, keepdims=True)
    pb = p.astype(BF16)
    r = qqs[0].shape[0]
    pv = [jnp.dot(pb[h * r:(h + 1) * r], v_of(h), preferred_element_type=F32) for h in range(len(qqs))]
    acc_ref[...] = alpha * acc_ref[...] + jnp.concatenate(pv, axis=0)
    m_ref[...] = m_new


def _stage_scores(qq, ks, mask, s_ref, m_ref, alpha_ref):
    s = lax.dot_general(ks, qq, _NT, preferred_element_type=F32)
    if mask is not None:
        s = jnp.where(mask, s, NEG)
    s_ref[...] = s
    m_old = m_ref[...]
    m_new = jnp.maximum(m_old, jnp.max(s, axis=0, keepdims=True))
    alpha_ref[...] = jnp.exp2(m_old - m_new)
    m_ref[...] = m_new


def _stage_exp(s_ref, p_ref, m_ref):
    p_ref[...] = jnp.exp2(s_ref[...] - m_ref[...]).astype(BF16)


def _stage_pv(vts, p_ref, alpha_ref, acc_ref):
    ones = jnp.ones((SUM_ROWS, vts.shape[1]), BF16)
    pv = jnp.dot(jnp.concatenate([vts, ones], axis=0), p_ref[...], preferred_element_type=F32)
    acc_ref[...] = alpha_ref[...] * acc_ref[...] + pv


def _diff_finish(acc, l, lam, dng, lam_init):
    rows = acc.shape[0] // 2
    o = acc / l
    od = o[:rows] - lam * o[rows:]
    return od * lax.rsqrt(jnp.mean(od * od, axis=-1, keepdims=True) + LN_EPS) * dng * (1.0 - lam_init)


def _sink_attend_cached(qq, kt, kn, vt, vn, sink):
    s_c = jnp.dot(qq, kt, preferred_element_type=F32)
    s_n = lax.dot_general(qq, kn, _NT, preferred_element_type=F32)
    m = jnp.maximum(jnp.maximum(jnp.max(s_c, axis=1, keepdims=True), jnp.max(s_n, axis=1, keepdims=True)), sink)
    p_c, p_n = jnp.exp2(s_c - m), jnp.exp2(s_n - m)
    l = jnp.sum(p_c, axis=1, keepdims=True) + jnp.sum(p_n, axis=1, keepdims=True) + jnp.exp2(sink - m)
    o = (lax.dot_general(p_c.astype(BF16), vt, _NT, preferred_element_type=F32)
         + jnp.dot(p_n.astype(BF16), vn, preferred_element_type=F32))
    return o / l


def _window_scores(qq4, ks, mask):
    return jnp.where(mask, lax.dot_general(ks, qq4, _NT, preferred_element_type=F32), NEG)


def _window_attend(s, vts, sinks4):
    m = jnp.maximum(jnp.max(s, axis=0, keepdims=True), sinks4)
    p = jnp.exp2(s - m).astype(BF16)
    ones = jnp.ones((SUM_ROWS, vts.shape[1]), BF16)
    o = jnp.dot(jnp.concatenate([vts, ones], axis=0), p, preferred_element_type=F32)
    return o[:LANES] / (o[LANES:LANES + 1] + jnp.exp2(sinks4 - m))


def _swa_heads(q, attend, sink_of):
    low = lax.broadcasted_iota(jnp.int32, (q.shape[0], LANES), 1) < SW_HD
    outs = []
    for pair in range(SW_HEADS // 2):
        kv = pair // 2
        slab = q[:, pair * LANES:(pair + 1) * LANES]
        zero = jnp.zeros_like(slab)
        o_even = attend(jnp.where(low, slab, zero), kv == 1, sink_of(2 * pair))
        o_odd = attend(jnp.where(low, zero, slab), kv == 0, sink_of(2 * pair + 1))
        outs.append(jnp.where(low, o_even, o_odd))
    return jnp.concatenate(outs, axis=1)


def _mem_kv_kernel(mem_ref, w_ref, kf_ref, vf_ref, kb_ref, vb_ref):
    h = jnp.dot(mem_ref[...].astype(BF16), w_ref[0].astype(BF16), preferred_element_type=F32)
    k, v = h[:, :MEM_W], h[:, MEM_W:]
    for hd in range(MEM_HEADS):
        kf_ref[_head_idx(kf_ref, MEM_HEADS, hd, 0)] = k[:, hd * MEM_HD:(hd + 1) * MEM_HD]
        vf_ref[_head_idx(vf_ref, MEM_HEADS, hd, 0)] = v[:, hd * MEM_HD:(hd + 1) * MEM_HD]
    kb_ref[0] = k.astype(BF16)
    vb_ref[0] = v.astype(BF16)


def _mem_kv(mem, w):
    rows = mem.shape[0]
    fshape, fspec = (DEPTH, rows * MEM_HEADS, MEM_HD), pl.BlockSpec((1, rows * MEM_HEADS, MEM_HD), lambda l: (l, 0, 0))
    bshape, bspec = (DEPTH, rows, MEM_W), pl.BlockSpec((1, rows, MEM_W), lambda l: (l, 0, 0))
    return pl.pallas_call(
        _mem_kv_kernel,
        grid=(DEPTH,),
        in_specs=[pl.BlockSpec((rows, D_MODEL), lambda l: (0, 0)),
                  pl.BlockSpec((1, D_MODEL, 2 * MEM_W), lambda l: (l, 0, 0))],
        out_specs=[fspec, fspec, bspec, bspec],
        out_shape=[jax.ShapeDtypeStruct(fshape, F32), jax.ShapeDtypeStruct(fshape, F32),
                   jax.ShapeDtypeStruct(bshape, BF16), jax.ShapeDtypeStruct(bshape, BF16)],
        compiler_params=_params("arbitrary"),
        name="mem_kv",
    )(mem, w)


def _proj_a_kernel(fuse_mem, x_ref, w_ref, c_ref, a_ref, b_ref, *refs):
    if fuse_mem:
        mk_ref, mv_ref, q_ref, kf_ref, vf_ref, kb_ref, vb_ref, g_ref, om_ref = refs
    else:
        q_ref, kf_ref, vf_ref, kb_ref, vb_ref, g_ref, mq_ref, gm_ref = refs
    x = x_ref[...].astype(BF16)
    c, a, b = c_ref[...], a_ref[...], b_ref[...]

    def cols(i):
        return jnp.dot(x, w_ref[:, i * DA_W:(i + 1) * DA_W].astype(BF16), preferred_element_type=F32)

    q_ref[...] = (_rope(cols(0), c, a, b) * (DA_QK ** -0.5 * LOG2E)).astype(BF16)
    k = _rope(cols(1), c, a, b)
    kb_ref[...] = k.astype(BF16)
    v = cols(2)
    for hd in range(DA_HEADS):
        vf_ref[_head_idx(vf_ref, DA_HEADS, hd)] = v[:, hd * DA_V:(hd + 1) * DA_V]
    if fuse_mem:
        kf_ref[0] = k.T
        for blk in range(vb_ref.shape[0]):
            vb_ref[blk] = v[blk * DIFF_TILE:(blk + 1) * DIFF_TILE, :].T.astype(BF16)
    else:
        kf_ref[...] = k
        vb_ref[...] = v.astype(BF16)
    g_ref[...] = _silu(cols(3)).astype(BF16)
    mq = cols(4).astype(BF16)
    gm = _silu(cols(5))
    if fuse_mem:
        om_ref[...] = (_mem_attend(mq, _head_cols(mk_ref), _head_cols(mv_ref)) * gm).astype(BF16)
    else:
        mq_ref[...] = mq
        gm_ref[...] = gm.astype(BF16)


def _proj_b_kernel(fuse_mem, x_ref, w_ref, wkv_ref, c_ref, a_ref, b_ref, *refs):
    if fuse_mem:
        mk_ref, mv_ref, q_ref, g_ref, om_ref, kf_ref, vf_ref, kb_ref, kr_ref, vb_ref, vr_ref = refs
    else:
        q_ref, g_ref, mq_ref, gm_ref, kf_ref, vf_ref, kb_ref, kr_ref, vb_ref, vr_ref = refs
    x = x_ref[...].astype(BF16)
    c, a, b = c_ref[...], a_ref[...], b_ref[...]

    def cols(i):
        return jnp.dot(x, w_ref[:, i * SW_W:(i + 1) * SW_W].astype(BF16), preferred_element_type=F32)

    q_ref[...] = (_rope(cols(0), c, a, b) * (SW_HD ** -0.5 * LOG2E)).astype(BF16)
    g_ref[...] = _silu(cols(1)).astype(BF16)
    mq = cols(2).astype(BF16)
    gm = _silu(cols(3))
    if fuse_mem:
        om_ref[...] = (_mem_attend(mq, _head_cols(mk_ref), _head_cols(mv_ref)) * gm).astype(BF16)
    else:
        mq_ref[...] = mq
        gm_ref[...] = gm.astype(BF16)
    kv = jnp.dot(x, wkv_ref[...].astype(BF16), preferred_element_type=F32)
    k = _rope(kv[:, :LANES], c, a, b)
    v = kv[:, LANES:]
    kf_ref[...] = k
    vf_ref[...] = v
    kb_ref[...] = k.astype(BF16)
    kr_ref[...] = pltpu.roll(k, SW_HD, 1).astype(BF16)
    vr = pltpu.roll(v, SW_HD, 1)
    if fuse_mem:
        for blk in range(vb_ref.shape[0]):
            rows = slice(blk * SWA_TILE, (blk + 1) * SWA_TILE)
            vb_ref[blk] = v[rows, :].T.astype(BF16)
            vr_ref[blk] = vr[rows, :].T.astype(BF16)
    else:
        vb_ref[...] = v.astype(BF16)
        vr_ref[...] = vr.astype(BF16)


def _project(layer, x, ws, tabs, mem=None, *, tm, rows_per_batch=None):
    M = x.shape[0]
    fuse = mem is not None
    per = (rows_per_batch or M) // tm
    row = lambda w: pl.BlockSpec((tm, w), lambda i: (i, 0))
    tab = pl.BlockSpec((tm, LANES), lambda i: (i % per, 0))
    in_specs = [row(D_MODEL)] + [_resident(w.shape) for w in ws] + [tab, tab, tab]
    args = [x, *ws, *tabs]
    if fuse:
        mspec = pl.BlockSpec((1, N_MEM, MEM_W), lambda i: (i // per, 0, 0))
        in_specs += [mspec, mspec]
        args += list(mem)
    o = lambda w, dt: (row(w), jax.ShapeDtypeStruct((M, w), dt))
    if layer == "a":
        vt = (pl.BlockSpec((tm // DIFF_TILE, DA_W, DIFF_TILE), lambda i: (i, 0, 0)),
              jax.ShapeDtypeStruct((M // DIFF_TILE, DA_W, DIFF_TILE), BF16))
        kt = (pl.BlockSpec((1, DA_W, tm), lambda i: (i // per, 0, i % per)),
              jax.ShapeDtypeStruct((M // (per * tm), DA_W, per * tm), F32))
        vf = (pl.BlockSpec((tm * DA_HEADS, DA_V), lambda i: (i, 0)), jax.ShapeDtypeStruct((M * DA_HEADS, DA_V), F32))
        outs = [o(DA_W, BF16), kt if fuse else o(DA_W, F32), vf, o(DA_W, BF16), vt if fuse else o(DA_W, BF16),
                o(DA_W, BF16)]
        body = _proj_a_kernel
    else:
        outs = [o(SW_W, BF16), o(SW_W, BF16)]
        body = _proj_b_kernel
    mem_outs = [o(MEM_W, BF16)] if fuse else [o(MEM_W, BF16), o(MEM_W, BF16)]
    if layer == "a":
        outs = outs + mem_outs
    else:
        vt = (pl.BlockSpec((tm // SWA_TILE, LANES, SWA_TILE), lambda i: (i, 0, 0)),
              jax.ShapeDtypeStruct((M // SWA_TILE, LANES, SWA_TILE), BF16))
        values = [vt, vt] if fuse else [o(LANES, BF16)] * 2
        outs = outs + mem_outs + [o(LANES, F32), o(LANES, F32)] + [o(LANES, BF16)] * 2 + values
    return pl.pallas_call(
        functools.partial(body, fuse),
        grid=(M // tm,),
        in_specs=in_specs,
        out_specs=[s for s, _ in outs],
        out_shape=[t for _, t in outs],
        compiler_params=_params("arbitrary"),
        name="proj_" + layer,
    )(*args)


def _diff_prompt_kernel(lam_init, lamv_ref, q_ref, k_ref, vt_ref, g_ref, om_ref, x_ref, wout_ref,
                        lng_ref, lnb_ref, dng_ref, y_ref, m_ref, alpha_ref, acc_ref, s_ref, p_ref, mix_ref):
    t = DIFF_TILE
    SLOTS = s_ref.shape[0]
    LEAD = PIPE_LEAD
    i = pl.program_id(1)
    heads = [slice(h * DA_V, (h + 1) * DA_V) for h in range(DA_HEADS)]
    lane = lax.broadcasted_iota(jnp.int32, (t, DA_V), 1)
    units = []
    for hs in heads:
        qh = q_ref[:, hs]
        zero = jnp.zeros_like(qh)
        units += [(hs, jnp.where(lane < DA_QK, qh, zero)), (hs, jnp.where(lane >= DA_QK, qh, zero))]
    m_ref[...] = jnp.full(m_ref.shape, -jnp.inf, F32)
    acc_ref[...] = jnp.zeros(acc_ref.shape, F32)

    n = len(units)
    assert n == SLOTS and LEAD + 1 < n

    def exp_stage(u):
        _stage_exp(s_ref.at[u], p_ref.at[u], m_ref.at[u])

    def pv_stage(j, u):
        _stage_pv(vt_ref[0, j, units[u][0], :], p_ref.at[u], alpha_ref.at[u], acc_ref.at[u])

    def block(j, mask, j_before):
        rows = pl.ds(pl.multiple_of(j * t, t), t)
        for u, (hs, qq) in enumerate(units):
            _stage_scores(qq, k_ref[0, rows, hs], mask, s_ref.at[u], m_ref.at[u], alpha_ref.at[u])
            if u >= LEAD:
                exp_stage(u - LEAD)
            elif j_before is not None:
                exp_stage(u - LEAD + n)
            if u >= LEAD + 1:
                pv_stage(j, u - LEAD - 1)
            elif j_before is not None:
                pv_stage(j_before, u - LEAD - 1 + n)

    def run(j0, count):
        block(j0, None, jnp.where(j0 == 0, i, j0 - 1))
        for d in range(1, count):
            block(j0 + d, None, j0 + d - 1)

    def quad(jj, carry):
        run(UNROLL * jj, UNROLL)
        return carry

    kc = lax.broadcasted_iota(jnp.int32, (t, t), 0) // CHUNK
    qc = lax.broadcasted_iota(jnp.int32, (t, t), 1) // CHUNK
    block(i, kc <= qc, None)
    lax.fori_loop(0, i // UNROLL, quad, 0)
    done = i // UNROLL * UNROLL
    rest = i % UNROLL
    for part in (2, 1):
        @pl.when(rest & part != 0)
        def _(part=part):
            run(done + (rest & ~(2 * part - 1)), part)

    last = jnp.where(i == 0, i, i - 1)
    for u in range(n - LEAD, n):
        exp_stage(u)
        pv_stage(last, u - 1)
    pv_stage(last, n - 1)

    lam = _lambda(lamv_ref[...], lam_init)
    for h, hs in enumerate(heads):
        o1 = acc_ref[2 * h, :DA_V, :] / acc_ref[2 * h, DA_V:DA_V + 1, :]
        o2 = acc_ref[2 * h + 1, :DA_V, :] / acc_ref[2 * h + 1, DA_V:DA_V + 1, :]
        od = (o1 - lam * o2).T
        od = od * lax.rsqrt(jnp.mean(od * od, axis=-1, keepdims=True) + LN_EPS) * dng_ref[...] * (1.0 - lam_init)
        mix_ref[:, hs] = (od * g_ref[:, hs].astype(F32)).astype(BF16)
    mix_ref[:, DA_W:] = om_ref[...]
    y_ref[...] = _merge(x_ref[...], mix_ref[...], wout_ref[0], lng_ref[...], lnb_ref[...])


def _diff_prompt(lam_init, lamv, q, k, vt, g, om, x, w_out, layer, ln_g, ln_b, dng, *, B, S):
    t = DIFF_TILE
    nq = S // t
    row = lambda w: pl.BlockSpec((t, w), lambda b, i: (b * nq + i, 0))
    full = lambda shape: pl.BlockSpec(shape, lambda b, i: (0,) * len(shape))
    return pl.pallas_call(
        functools.partial(_diff_prompt_kernel, lam_init),
        grid=(B, nq),
        in_specs=[full((4, DA_QK)), row(DA_W), pl.BlockSpec((1, S, DA_W), lambda b, i: (b, 0, 0)),
                  pl.BlockSpec((1, nq, DA_W, t), lambda b, i: (b, 0, 0, 0)), row(DA_W), row(MEM_W), row(D_MODEL),
                  _resident((1, MIX_W, D_MODEL), layer, 0, 0), full((1, D_MODEL)), full((1, D_MODEL)), full((1, DA_V))],
        out_specs=row(D_MODEL),
        out_shape=jax.ShapeDtypeStruct((B * S, D_MODEL), F32),
        scratch_shapes=[pltpu.VMEM((2 * DA_HEADS, 1, t), F32), pltpu.VMEM((2 * DA_HEADS, 1, t), F32),
                        pltpu.VMEM((2 * DA_HEADS, DA_V + SUM_ROWS, t), F32),
                        pltpu.VMEM((2 * DA_HEADS, t, t), F32), pltpu.VMEM((2 * DA_HEADS, t, t), BF16),
                        pltpu.VMEM((t, MIX_W), BF16)],
        compiler_params=_params("arbitrary", "arbitrary"),
        name="diff_prompt",
    )(lamv, q, k.reshape(B, S, DA_W), vt.reshape(B, nq, DA_W, t), g, om, x, w_out, ln_g, ln_b, dng)


def _swa_prompt_kernel(tq, sinks_ref, q_ref, k_ref, kr_ref, vt_ref, vtr_ref, g_ref, om_ref, x_ref, wout_ref,
                       lng_ref, lnb_ref, y_ref, mix_ref):
    i = pl.program_id(1)
    w, nk, nsub = SWA_TILE, 2 * SWA_TILE, tq // SWA_TILE
    low = lax.broadcasted_iota(jnp.int32, (w, LANES), 1) < SW_HD
    row_low = lax.broadcasted_iota(jnp.int32, (LANES, w), 0) < SW_HD
    sink_row = lambda hs: jnp.concatenate([jnp.full((1, w), sinks_ref[h] * LOG2E, F32) for h in hs], axis=1)
    sinks_plain, sinks_swapped = sink_row((0, 2, 5, 7)), sink_row((1, 3, 4, 6))
    kc = lax.broadcasted_iota(jnp.int32, (nk, 4 * w), 0) // CHUNK
    qc = lax.broadcasted_iota(jnp.int32, (nk, 4 * w), 1) % w // CHUNK
    firsts, scores = [], []
    for st in range(nsub):
        sub = i * nsub + st
        first = jnp.maximum(sub - 1, 0)
        back = (sub - first) * (w // CHUNK)
        mask = (kc - back <= qc) & (kc - back >= qc - WINDOW // CHUNK)
        rows = pl.ds(pl.multiple_of(first * w, w), nk)
        q = q_ref[st * w:(st + 1) * w, :]
        slabs = [q[:, p * LANES:(p + 1) * LANES] for p in range(SW_HEADS // 2)]
        even = [jnp.where(low, sl, jnp.zeros_like(sl)) for sl in slabs]
        odd = [jnp.where(low, jnp.zeros_like(sl), sl) for sl in slabs]
        firsts.append(first)
        scores.append((_window_scores(jnp.concatenate([even[0], even[1], odd[2], odd[3]], axis=0), k_ref[0, rows, :],
                                      mask),
                       _window_scores(jnp.concatenate([odd[0], odd[1], even[2], even[3]], axis=0), kr_ref[0, rows, :],
                                      mask)))
    for st, (first, (s_plain, s_swapped)) in enumerate(zip(firsts, scores)):
        vts = jnp.concatenate([vt_ref[0, first], vt_ref[0, first + 1]], axis=1)
        vtrs = jnp.concatenate([vtr_ref[0, first], vtr_ref[0, first + 1]], axis=1)
        o_plain = _window_attend(s_plain, vts, sinks_plain)
        o_swapped = _window_attend(s_swapped, vtrs, sinks_swapped)
        for p in range(SW_HEADS // 2):
            cols = slice(p * w, (p + 1) * w)
            lo, hi = (o_plain, o_swapped) if p < 2 else (o_swapped, o_plain)
            slab = jnp.where(row_low, lo[:, cols], hi[:, cols]).T
            gate = g_ref[st * w:(st + 1) * w, p * LANES:(p + 1) * LANES].astype(F32)
            mix_ref[st * w:(st + 1) * w, p * LANES:(p + 1) * LANES] = (slab * gate).astype(BF16)
    mix_ref[:, SW_W:] = om_ref[...]
    y_ref[...] = _merge(x_ref[...], mix_ref[...], wout_ref[0], lng_ref[...], lnb_ref[...])


def _swa_prompt(sinks, q, k, kr, vt, vtr, g, om, x, w_out, layer, ln_g, ln_b, *, B, S, tq):
    nq = S // tq
    row = lambda w: pl.BlockSpec((tq, w), lambda b, i: (b * nq + i, 0))
    full = lambda shape: pl.BlockSpec(shape, lambda b, i: (0,) * len(shape))
    seq = pl.BlockSpec((1, S, LANES), lambda b, i: (b, 0, 0))
    seqt = pl.BlockSpec((1, S // SWA_TILE, LANES, SWA_TILE), lambda b, i: (b, 0, 0, 0))
    r3 = lambda t: t.reshape(B, S, LANES)
    r4 = lambda t: t.reshape(B, S // SWA_TILE, LANES, SWA_TILE)
    return pl.pallas_call(
        functools.partial(_swa_prompt_kernel, tq),
        grid=(B, nq),
        in_specs=[pl.BlockSpec(memory_space=pltpu.SMEM), row(SW_W), seq, seq, seqt, seqt, row(SW_W), row(MEM_W),
                  row(D_MODEL), _resident((1, MIX_W, D_MODEL), layer, 0, 0), full((1, D_MODEL)), full((1, D_MODEL))],
        out_specs=row(D_MODEL),
        out_shape=jax.ShapeDtypeStruct((B * S, D_MODEL), F32),
        scratch_shapes=[pltpu.VMEM((tq, MIX_W), BF16)],
        compiler_params=_params("arbitrary", "arbitrary"),
        name="swa_prompt",
    )(sinks, q, r3(k), r3(kr), r4(vt), r4(vtr), g, om, x, w_out, ln_g, ln_b)


def _sample_tail(b, last, o_first, g_ref, mq_ref, gm_ref, cmk_ref, cmv_ref, x_ref, wout_ref, lng_ref, lnb_ref,
                 y_ref, mix_ref, T):
    om = _mem_attend(mq_ref[...], _head_rows(cmk_ref, MEM_HEADS, 0, 0), _head_rows(cmv_ref, MEM_HEADS, 0, 0))
    om = om * gm_ref[...].astype(F32)
    mix = jnp.concatenate([(o_first * g_ref[...].astype(F32)).astype(BF16), om.astype(BF16)], axis=1)
    mix_ref[pl.ds(pl.multiple_of(b * T, T), T), :] = mix

    @pl.when(last)
    def _():
        y_ref[...] = _merge(x_ref[...], mix_ref[...], wout_ref[0], lng_ref[...], lnb_ref[...])


def _diff_sample_kernel(lam_init, T, lamv_ref, q_ref, kn_ref, vn_ref, ckt_ref, cv_ref, g_ref, mq_ref, gm_ref,
                        cmk_ref, cmv_ref, x_ref, wout_ref, lng_ref, lnb_ref, dng_ref, y_ref,
                        m_ref, l_ref, acc_ref, mix_ref):
    b, j = pl.program_id(0), pl.program_id(1)
    nb, nj = pl.num_programs(0), pl.num_programs(1)
    heads = [slice(h * DA_V, (h + 1) * DA_V) for h in range(DA_HEADS)]
    qqs = [_split_components(q_ref[:, hs]) for hs in heads]

    @pl.when(j == 0)
    def _():
        m_ref[...] = jnp.full(m_ref.shape, -jnp.inf, F32)
        l_ref[...] = jnp.zeros(l_ref.shape, F32)
        acc_ref[...] = jnp.zeros(acc_ref.shape, F32)
        _flash_update(qqs, lambda h: kn_ref[:, heads[h]], lambda h: vn_ref[:, heads[h]], False, m_ref, l_ref, acc_ref)

    _flash_update(qqs, lambda h: ckt_ref[0, heads[h], :].astype(BF16), _head_rows(cv_ref, DA_HEADS, 0), True,
                  m_ref, l_ref, acc_ref)

    @pl.when(j == nj - 1)
    def _():
        lam = _lambda(lamv_ref[...], lam_init)
        rows = [slice(h * 2 * T, (h + 1) * 2 * T) for h in range(DA_HEADS)]
        od = jnp.concatenate([_diff_finish(acc_ref[r, :], l_ref[r, :], lam, dng_ref[...], lam_init) for r in rows],
                             axis=1)
        _sample_tail(b, b == nb - 1, od, g_ref, mq_ref, gm_ref, cmk_ref, cmv_ref, x_ref, wout_ref, lng_ref,
                     lnb_ref, y_ref, mix_ref, T)


def _diff_sample(lam_init, lamv, q, kn, vn, ckt, cv, g, mq, gm, cmk, cmv, layer, x, w_out, ln_g, ln_b, dng, *,
                 NB, T, tk):
    P = ckt.shape[2]
    row = lambda w: pl.BlockSpec((T, w), lambda b, j: (b, 0))
    full = lambda shape: pl.BlockSpec(shape, lambda b, j: (0,) * len(shape))
    memc = pl.BlockSpec((1, 1, N_MEM * MEM_HEADS, MEM_HD), lambda b, j: (layer, b, 0, 0))
    return pl.pallas_call(
        functools.partial(_diff_sample_kernel, lam_init, T),
        grid=(NB, P // tk),
        in_specs=[full((4, DA_QK)), row(DA_W), row(DA_W), row(DA_W),
                  pl.BlockSpec((1, DA_W, tk), lambda b, j: (b, 0, j)),
                  pl.BlockSpec((1, tk * DA_HEADS, DA_V), lambda b, j: (b, j, 0)),
                  row(DA_W), row(MEM_W), row(MEM_W), memc, memc, full((NB * T, D_MODEL)),
                  _resident((1, MIX_W, D_MODEL), layer, 0, 0), full((1, D_MODEL)), full((1, D_MODEL)), full((1, DA_V))],
        out_specs=full((NB * T, D_MODEL)),
        out_shape=jax.ShapeDtypeStruct((NB * T, D_MODEL), F32),
        scratch_shapes=[pltpu.VMEM((DA_HEADS * 2 * T, 1), F32), pltpu.VMEM((DA_HEADS * 2 * T, 1), F32),
                        pltpu.VMEM((DA_HEADS * 2 * T, DA_V), F32), pltpu.VMEM((NB * T, MIX_W), BF16)],
        compiler_params=_params("arbitrary", "arbitrary"),
        name="diff_sample",
    )(lamv, q, kn, vn, ckt, cv, g, mq, gm, cmk, cmv, x, w_out, ln_g, ln_b, dng)


def _swa_sample_kernel(T, sinks_ref, q_ref, kn_ref, knr_ref, vn_ref, vnr_ref, ckt_ref, cvt_ref, g_ref, mq_ref, gm_ref,
                       cmk_ref, cmv_ref, x_ref, wout_ref, lng_ref, lnb_ref, y_ref, mix_ref):
    b = pl.program_id(0)
    swap = lambda t: jnp.concatenate([t[SW_HD:], t[:SW_HD]], axis=0).astype(BF16)
    ckt, cvt = ckt_ref[0], cvt_ref[0]
    kt, ktr, vt, vtr = ckt.astype(BF16), swap(ckt), cvt.astype(BF16), swap(cvt)
    kn, knr, vn, vnr = kn_ref[...], knr_ref[...], vn_ref[...], vnr_ref[...]
    attend = lambda qq, swapped, sink: _sink_attend_cached(
        qq, ktr if swapped else kt, knr if swapped else kn, vtr if swapped else vt, vnr if swapped else vn, sink)
    o = _swa_heads(q_ref[...], attend, lambda h: sinks_ref[h] * LOG2E)
    _sample_tail(b, b == pl.num_programs(0) - 1, o, g_ref, mq_ref, gm_ref, cmk_ref, cmv_ref, x_ref, wout_ref,
                 lng_ref, lnb_ref, y_ref, mix_ref, T)


def _swa_sample(sinks, q, kn, knr, vn, vnr, ckt, cvt, g, mq, gm, cmk, cmv, layer, x, w_out, ln_g, ln_b, *, NB, T):
    W = ckt.shape[2]
    row = lambda w: pl.BlockSpec((T, w), lambda b: (b, 0))
    full = lambda shape: pl.BlockSpec(shape, lambda b: (0,) * len(shape))
    cache = pl.BlockSpec((1, LANES, W), lambda b: (b, 0, 0))
    memc = pl.BlockSpec((1, 1, N_MEM * MEM_HEADS, MEM_HD), lambda b: (layer, b, 0, 0))
    return pl.pallas_call(
        functools.partial(_swa_sample_kernel, T),
        grid=(NB,),
        in_specs=[pl.BlockSpec(memory_space=pltpu.SMEM), row(SW_W), row(LANES), row(LANES), row(LANES), row(LANES),
                  cache, cache, row(SW_W), row(MEM_W), row(MEM_W), memc, memc, full((NB * T, D_MODEL)),
                  _resident((1, MIX_W, D_MODEL), layer, 0, 0), full((1, D_MODEL)), full((1, D_MODEL))],
        out_specs=full((NB * T, D_MODEL)),
        out_shape=jax.ShapeDtypeStruct((NB * T, D_MODEL), F32),
        scratch_shapes=[pltpu.VMEM((NB * T, MIX_W), BF16)],
        compiler_params=_params("arbitrary"),
        name="swa_sample",
    )(sinks, q, kn, knr, vn, vnr, ckt, cvt, g, mq, gm, cmk, cmv, x, w_out, ln_g, ln_b)


def _rope_tables(pos):
    T = pos.shape[0]
    inv = ROPE_THETA ** (-np.arange(ROPE_HALF, dtype=np.float64) / ROPE_HALF)
    ang = pos.astype(np.float64)[:, None] * inv[None, :]
    cos, sin = np.cos(ang), np.sin(ang)
    rest = SW_HD - 2 * ROPE_HALF
    c = np.concatenate([cos, cos, np.ones((T, rest))], axis=1)
    a = np.concatenate([-sin, np.zeros((T, SW_HD - ROPE_HALF))], axis=1)
    b = np.concatenate([np.zeros((T, ROPE_HALF)), sin, np.zeros((T, rest))], axis=1)
    return tuple(jnp.asarray(np.tile(t, (1, LANES // SW_HD)), F32) for t in (c, a, b))


def _feature_major(t):
    n = t.ndim
    return jnp.transpose(t, (0,) + tuple(range(2, n)) + (1,)).reshape(t.shape[0], -1, t.shape[1])


def _rows_major(t, tail):
    n = len(tail)
    return jnp.transpose(t.reshape((t.shape[0],) + tail + (t.shape[2],)), (0, n + 1) + tuple(range(1, n + 1)))


def kernel(x_prompt, x_sample, mem_prompt, cache_diff_k, cache_diff_v, cache_swa_k, cache_swa_v, cache_mem_k,
           cache_mem_v, w_in_a, lam_q1, lam_k1, lam_q2, lam_k2, diff_norm_g, w_in_b, sinks, w_kv_shared, w_mem_kv,
           w_out, ln_g, ln_b):
    B, S, _ = x_prompt.shape
    NB, T, _ = x_sample.shape
    P = cache_diff_k.shape[2]
    assert w_in_a.shape[0] == 1 and w_in_b.shape[0] == 1 and w_out.shape[0] == DEPTH

    tabs_p = _rope_tables(np.arange(S))
    tabs_s = _rope_tables(np.tile(P + np.arange(T), NB))
    xp = x_prompt.reshape(B * S, D_MODEL)
    xs = x_sample.reshape(NB * T, D_MODEL)
    wa, wb, wo = (w_in_a[0],), (w_in_b[0], w_kv_shared), w_out
    lng = ln_g.reshape(DEPTH, 1, D_MODEL)
    lnb = ln_b.reshape(DEPTH, 1, D_MODEL)
    lamv = jnp.concatenate([lam_q1, lam_k1, lam_q2, lam_k2], axis=0)
    dng = diff_norm_g.reshape(1, DA_V)
    lam_init = 0.8 - 0.6 * math.exp(-0.3 * 0)

    mkf, mvf, mkb, mvb = _mem_kv(mem_prompt.reshape(B * N_MEM, D_MODEL), w_mem_kv)
    mem_b = lambda l: (mkb[l].reshape(B, N_MEM, MEM_W), mvb[l].reshape(B, N_MEM, MEM_W))
    head_rows = lambda t: t.reshape(t.shape[:-3] + (t.shape[-3] * t.shape[-2], t.shape[-1]))
    cmk, cmv = head_rows(cache_mem_k), head_rows(cache_mem_v)

    q, kt, vf, kb, vb, g, om = _project("a", xp, wa, tabs_p, mem_b(0), tm=PROJ_ROWS, rows_per_batch=S)
    xp1 = _diff_prompt(lam_init, lamv, q, kb, vb, g, om, xp, wo, 0, lng[0], lnb[0], dng, B=B, S=S)
    qs, kfs, vfs, kbs, vbs, gs_, mqs, gms = _project("a", xs, wa, tabs_s, tm=NB * T)
    xs1 = _diff_sample(lam_init, lamv, qs, kbs, vbs, _feature_major(cache_diff_k[0]), head_rows(cache_diff_v[0]), gs_,
                       mqs, gms, cmk, cmv, 0, xs, wo, lng[0], lnb[0], dng, NB=NB, T=T, tk=SAMPLE_KEYS)

    q, g, om, skf, svf, kb1, kr1, vb1, vr1 = _project("b", xp1, wb, tabs_p, mem_b(1), tm=PROJ_ROWS, rows_per_batch=S)
    yp = _swa_prompt(sinks[0], q, kb1, kr1, vb1, vr1, g, om, xp1, wo, 1, lng[1], lnb[1], B=B, S=S, tq=SWA_ROWS)
    qs, gs_, mqs, gms, skfs, svfs, kbs1, krs1, vbs1, vrs1 = _project("b", xs1, wb, tabs_s, tm=NB * T)
    ys = _swa_sample(sinks[0], qs, kbs1, krs1, vbs1, vrs1, _feature_major(cache_swa_k), _feature_major(cache_swa_v),
                     gs_, mqs, gms, cmk, cmv, 1, xs1, wo, lng[1], lnb[1], NB=NB, T=T)

    wr_p = min(WINDOW, S)
    window = lambda t: t.reshape(B, S, LANES)[:, S - wr_p:, :].reshape(B, wr_p, SW_KV, SW_HD)
    swa_kp, swa_vp = window(skf), window(svf)
    roll = lambda c, n: jnp.concatenate([c, n.reshape(NB, T, SW_KV, SW_HD)], axis=1)[:, T:]
    return (yp.reshape(B, S, D_MODEL), ys.reshape(NB, T, D_MODEL),
            _rows_major(kt, (DA_HEADS, 2, DA_QK))[None], vf.reshape(1, B, S, DA_HEADS, DA_V),
            kfs.reshape(1, NB, T, DA_HEADS, 2, DA_QK), vfs.reshape(1, NB, T, DA_HEADS, DA_V),
            swa_kp, swa_vp, roll(cache_swa_k, skfs), roll(cache_swa_v, svfs),
            mkf.reshape(DEPTH, B, N_MEM, MEM_HEADS, MEM_HD), mvf.reshape(DEPTH, B, N_MEM, MEM_HEADS, MEM_HD))
```

```python
import functools
import math

import jax
import jax.numpy as jnp
import numpy as np
from jax import lax
from jax.experimental import pallas as pl
from jax.experimental.pallas import tpu as pltpu

D_MODEL = 1024
CHUNK = 64
N_MEM = 256
DA_HEADS = 4
DA_QK = 64
DA_V = 128
DA_W = 512
SW_HEADS = 8
SW_KV = 2
SW_GROUP = 4
SW_HD = 64
SW_W = 512
WINDOW = 128
MEM_HEADS = 4
MEM_HD = 128
MEM_W = 512
MIX_W = 1024
ROPE_THETA = 500000.0
ROPE_HALF = 8
DEPTH = 2
DN_ALPHA = (2 * DEPTH) ** 0.25
LN_EPS = 1e-5
NEG = -1e30
LANES = 128
DIFF_TILE = 256
PIPE_LEAD = 4
UNROLL = 4
SWA_TILE = 128
SUM_ROWS = 16
PROJ_ROWS = 1024
SWA_ROWS = 1024
SAMPLE_KEYS = 2048
LOG2E = math.log2(math.e)

F32 = jnp.float32
BF16 = jnp.bfloat16
VMEM_LIMIT = 48 * 1024 * 1024

_NT = (((1,), (1,)), ((), ()))


def _resident(shape, *first):
    idx = first or (0,) * len(shape)
    return pl.BlockSpec(shape, lambda *_: idx, pipeline_mode=pl.Buffered(1))


def _params(*sem):
    return pltpu.CompilerParams(dimension_semantics=sem, vmem_limit_bytes=VMEM_LIMIT)


def _rope(x, c, a, b):
    outs = []
    for i in range(x.shape[1] // LANES):
        blk = x[:, i * LANES:(i + 1) * LANES]
        outs.append(blk * c + pltpu.roll(blk, LANES - ROPE_HALF, 1) * a + pltpu.roll(blk, ROPE_HALF, 1) * b)
    return outs[0] if len(outs) == 1 else jnp.concatenate(outs, axis=1)


def _silu(g):
    return g * (1.0 / (1.0 + jnp.exp(-g)))


def _layer_norm(z, g, b):
    mu = jnp.mean(z, axis=-1, keepdims=True)
    d = z - mu
    var = jnp.mean(d * d, axis=-1, keepdims=True)
    return d * lax.rsqrt(var + LN_EPS) * g + b


def _mem_attend(mq, mk_of, mv_of):
    outs = []
    for h in range(MEM_HEADS):
        sl = slice(h * MEM_HD, (h + 1) * MEM_HD)
        s = lax.dot_general(mq[:, sl], mk_of(h), _NT, preferred_element_type=F32) * (MEM_HD ** -0.5)
        p = jnp.exp(s - jnp.max(s, axis=1, keepdims=True))
        l = jnp.sum(p, axis=1, keepdims=True)
        outs.append(jnp.dot(p.astype(BF16), mv_of(h), preferred_element_type=F32) / l)
    return jnp.concatenate(outs, axis=1)


def _head_cols(ref):
    return lambda h: ref[0, :, h * MEM_HD:(h + 1) * MEM_HD]


def _head_idx(ref, heads, h, *lead):
    return lead + (pl.ds(h, ref.shape[-2] // heads, stride=heads), slice(None))


def _head_rows(ref, heads, *lead):
    return lambda h: ref[_head_idx(ref, heads, h, *lead)].astype(BF16)


def _merge(x, mix, w_out, ln_g, ln_b):
    return _layer_norm(DN_ALPHA * x + jnp.dot(mix, w_out.astype(BF16), preferred_element_type=F32), ln_g, ln_b)


def _lambda(lamv, lam_init):
    e1 = jnp.exp(jnp.sum(lamv[0:1, :] * lamv[1:2, :], axis=1, keepdims=True))
    e2 = jnp.exp(jnp.sum(lamv[2:3, :] * lamv[3:4, :], axis=1, keepdims=True))
    return e1 - e2 + lam_init


def _split_components(qh):
    lane = lax.broadcasted_iota(jnp.int32, qh.shape, 1)
    zero = jnp.zeros_like(qh)
    return jnp.concatenate([jnp.where(lane < DA_QK, qh, zero), jnp.where(lane >= DA_QK, qh, zero)], axis=0)


def _flash_update(qqs, k_of, v_of, keys_on_lanes, m_ref, l_ref, acc_ref):
    if keys_on_lanes:
        s = [jnp.dot(qq, k_of(h), preferred_element_type=F32) for h, qq in enumerate(qqs)]
    else:
        s = [lax.dot_general(qq, k_of(h), _NT, preferred_element_type=F32) for h, qq in enumerate(qqs)]
    s = jnp.concatenate(s, axis=0)
    m_old = m_ref[...]
    m_new = jnp.maximum(m_old, jnp.max(s, axis=1, keepdims=True))
    alpha = jnp.exp2(m_old - m_new)
    p = jnp.exp2(s - m_new)
    l_ref[...] = alpha * l_ref[...] + jnp.sum(p, axis=1, keepdims=True)
    pb = p.astype(BF16)
    r = qqs[0].shape[0]
    pv = [jnp.dot(pb[h * r:(h + 1) * r], v_of(h), preferred_element_type=F32) for h in range(len(qqs))]
    acc_ref[...] = alpha * acc_ref[...] + jnp.concatenate(pv, axis=0)
    m_ref[...] = m_new


def _stage_scores(qq, ks, mask, s_ref, m_ref, alpha_ref):
    s = lax.dot_general(ks, qq, _NT, preferred_element_type=F32)
    if mask is not None:
        s = jnp.where(mask, s, NEG)
    s_ref[...] = s
    m_old = m_ref[...]
    m_new = jnp.maximum(m_old, jnp.max(s, axis=0, keepdims=True))
    alpha_ref[...] = jnp.exp2(m_old - m_new)
    m_ref[...] = m_new


def _stage_exp(s_ref, p_ref, m_ref):
    p_ref[...] = jnp.exp2(s_ref[...] - m_ref[...]).astype(BF16)


def _stage_pv(vts, p_ref, alpha_ref, acc_ref):
    ones = jnp.ones((SUM_ROWS, vts.shape[1]), BF16)
    pv = jnp.dot(jnp.concatenate([vts, ones], axis=0), p_ref[...], preferred_element_type=F32)
    acc_ref[...] = alpha_ref[...] * acc_ref[...] + pv


def _diff_finish(acc, l, lam, dng, lam_init):
    rows = acc.shape[0] // 2
    o = acc / l
    od = o[:rows] - lam * o[rows:]
    return od * lax.rsqrt(jnp.mean(od * od, axis=-1, keepdims=True) + LN_EPS) * dng * (1.0 - lam_init)


def _sink_attend_cached(qq, kt, kn, vt, vn, sink):
    s_c = jnp.dot(qq, kt, preferred_element_type=F32)
    s_n = lax.dot_general(qq, kn, _NT, preferred_element_type=F32)
    m = jnp.maximum(jnp.maximum(jnp.max(s_c, axis=1, keepdims=True), jnp.max(s_n, axis=1, keepdims=True)), sink)
    p_c, p_n = jnp.exp2(s_c - m), jnp.exp2(s_n - m)
    l = jnp.sum(p_c, axis=1, keepdims=True) + jnp.sum(p_n, axis=1, keepdims=True) + jnp.exp2(sink - m)
    o = (lax.dot_general(p_c.astype(BF16), vt, _NT, preferred_element_type=F32)
         + jnp.dot(p_n.astype(BF16), vn, preferred_element_type=F32))
    return o / l


def _window_scores(qq4, ks, mask):
    return jnp.where(mask, lax.dot_general(ks, qq4, _NT, preferred_element_type=F32), NEG)


def _window_attend(s, vts, sinks4):
    m = jnp.maximum(jnp.max(s, axis=0, keepdims=True), sinks4)
    p = jnp.exp2(s - m).astype(BF16)
    ones = jnp.ones((SUM_ROWS, vts.shape[1]), BF16)
    o = jnp.dot(jnp.concatenate([vts, ones], axis=0), p, preferred_element_type=F32)
    return o[:LANES] / (o[LANES:LANES + 1] + jnp.exp2(sinks4 - m))


def _swa_heads(q, attend, sink_of):
    low = lax.broadcasted_iota(jnp.int32, (q.shape[0], LANES), 1) < SW_HD
    outs = []
    for pair in range(SW_HEADS // 2):
        kv = pair // 2
        slab = q[:, pair * LANES:(pair + 1) * LANES]
        zero = jnp.zeros_like(slab)
        o_even = attend(jnp.where(low, slab, zero), kv == 1, sink_of(2 * pair))
        o_odd = attend(jnp.where(low, zero, slab), kv == 0, sink_of(2 * pair + 1))
        outs.append(jnp.where(low, o_even, o_odd))
    return jnp.concatenate(outs, axis=1)


def _mem_kv_kernel(mem_ref, w_ref, kf_ref, vf_ref, kb_ref, vb_ref):
    h = jnp.dot(mem_ref[...].astype(BF16), w_ref[0].astype(BF16), preferred_element_type=F32)
    k, v = h[:, :MEM_W], h[:, MEM_W:]
    for hd in range(MEM_HEADS):
        kf_ref[_head_idx(kf_ref, MEM_HEADS, hd, 0)] = k[:, hd * MEM_HD:(hd + 1) * MEM_HD]
        vf_ref[_head_idx(vf_ref, MEM_HEADS, hd, 0)] = v[:, hd * MEM_HD:(hd + 1) * MEM_HD]
    kb_ref[0] = k.astype(BF16)
    vb_ref[0] = v.astype(BF16)


def _mem_kv(mem, w):
    rows = mem.shape[0]
    fshape, fspec = (DEPTH, rows * MEM_HEADS, MEM_HD), pl.BlockSpec((1, rows * MEM_HEADS, MEM_HD), lambda l: (l, 0, 0))
    bshape, bspec = (DEPTH, rows, MEM_W), pl.BlockSpec((1, rows, MEM_W), lambda l: (l, 0, 0))
    return pl.pallas_call(
        _mem_kv_kernel,
        grid=(DEPTH,),
        in_specs=[pl.BlockSpec((rows, D_MODEL), lambda l: (0, 0)),
                  pl.BlockSpec((1, D_MODEL, 2 * MEM_W), lambda l: (l, 0, 0))],
        out_specs=[fspec, fspec, bspec, bspec],
        out_shape=[jax.ShapeDtypeStruct(fshape, F32), jax.ShapeDtypeStruct(fshape, F32),
                   jax.ShapeDtypeStruct(bshape, BF16), jax.ShapeDtypeStruct(bshape, BF16)],
        compiler_params=_params("arbitrary"),
        name="mem_kv",
    )(mem, w)


def _proj_a_kernel(fuse_mem, x_ref, w_ref, c_ref, a_ref, b_ref, *refs):
    if fuse_mem:
        mk_ref, mv_ref, q_ref, kf_ref, vf_ref, kb_ref, vb_ref, g_ref, om_ref = refs
    else:
        q_ref, kf_ref, vf_ref, kb_ref, vb_ref, g_ref, mq_ref, gm_ref = refs
    x = x_ref[...].astype(BF16)
    c, a, b = c_ref[...], a_ref[...], b_ref[...]

    def cols(i):
        return jnp.dot(x, w_ref[:, i * DA_W:(i + 1) * DA_W].astype(BF16), preferred_element_type=F32)

    q_ref[...] = (_rope(cols(0), c, a, b) * (DA_QK ** -0.5 * LOG2E)).astype(BF16)
    k = _rope(cols(1), c, a, b)
    kb_ref[...] = k.astype(BF16)
    v = cols(2)
    for hd in range(DA_HEADS):
        vf_ref[_head_idx(vf_ref, DA_HEADS, hd)] = v[:, hd * DA_V:(hd + 1) * DA_V]
    if fuse_mem:
        kf_ref[0] = k.T
        for blk in range(vb_ref.shape[0]):
            vb_ref[blk] = v[blk * DIFF_TILE:(blk + 1) * DIFF_TILE, :].T.astype(BF16)
    else:
        kf_ref[...] = k
        vb_ref[...] = v.astype(BF16)
    g_ref[...] = _silu(cols(3)).astype(BF16)
    mq = cols(4).astype(BF16)
    gm = _silu(cols(5))
    if fuse_mem:
        om_ref[...] = (_mem_attend(mq, _head_cols(mk_ref), _head_cols(mv_ref)) * gm).astype(BF16)
    else:
        mq_ref[...] = mq
        gm_ref[...] = gm.astype(BF16)


def _proj_b_kernel(fuse_mem, x_ref, w_ref, wkv_ref, c_ref, a_ref, b_ref, *refs):
    if fuse_mem:
        mk_ref, mv_ref, q_ref, g_ref, om_ref, kf_ref, vf_ref, kb_ref, kr_ref, vb_ref, vr_ref = refs
    else:
        q_ref, g_ref, mq_ref, gm_ref, kf_ref, vf_ref, kb_ref, kr_ref, vb_ref, vr_ref = refs
    x = x_ref[...].astype(BF16)
    c, a, b = c_ref[...], a_ref[...], b_ref[...]

    def cols(i):
        return jnp.dot(x, w_ref[:, i * SW_W:(i + 1) * SW_W].astype(BF16), preferred_element_type=F32)

    q_ref[...] = (_rope(cols(0), c, a, b) * (SW_HD ** -0.5 * LOG2E)).astype(BF16)
    g_ref[...] = _silu(cols(1)).astype(BF16)
    mq = cols(2).astype(BF16)
    gm = _silu(cols(3))
    if fuse_mem:
        om_ref[...] = (_mem_attend(mq, _head_cols(mk_ref), _head_cols(mv_ref)) * gm).astype(BF16)
    else:
        mq_ref[...] = mq
        gm_ref[...] = gm.astype(BF16)
    kv = jnp.dot(x, wkv_ref[...].astype(BF16), preferred_element_type=F32)
    k = _rope(kv[:, :LANES], c, a, b)
    v = kv[:, LANES:]
    kf_ref[...] = k
    vf_ref[...] = v
    kb_ref[...] = k.astype(BF16)
    kr_ref[...] = pltpu.roll(k, SW_HD, 1).astype(BF16)
    vr = pltpu.roll(v, SW_HD, 1)
    if fuse_mem:
        for blk in range(vb_ref.shape[0]):
            rows = slice(blk * SWA_TILE, (blk + 1) * SWA_TILE)
            vb_ref[blk] = v[rows, :].T.astype(BF16)
            vr_ref[blk] = vr[rows, :].T.astype(BF16)
    else:
        vb_ref[...] = v.astype(BF16)
        vr_ref[...] = vr.astype(BF16)


def _project(layer, x, ws, tabs, mem=None, *, tm, rows_per_batch=None):
    M = x.shape[0]
    fuse = mem is not None
    per = (rows_per_batch or M) // tm
    row = lambda w: pl.BlockSpec((tm, w), lambda i: (i, 0))
    tab = pl.BlockSpec((tm, LANES), lambda i: (i % per, 0))
    in_specs = [row(D_MODEL)] + [_resident(w.shape) for w in ws] + [tab, tab, tab]
    args = [x, *ws, *tabs]
    if fuse:
        mspec = pl.BlockSpec((1, N_MEM, MEM_W), lambda i: (i // per, 0, 0))
        in_specs += [mspec, mspec]
        args += list(mem)
    o = lambda w, dt: (row(w), jax.ShapeDtypeStruct((M, w), dt))
    if layer == "a":
        vt = (pl.BlockSpec((tm // DIFF_TILE, DA_W, DIFF_TILE), lambda i: (i, 0, 0)),
              jax.ShapeDtypeStruct((M // DIFF_TILE, DA_W, DIFF_TILE), BF16))
        kt = (pl.BlockSpec((1, DA_W, tm), lambda i: (i // per, 0, i % per)),
              jax.ShapeDtypeStruct((M // (per * tm), DA_W, per * tm), F32))
        vf = (pl.BlockSpec((tm * DA_HEADS, DA_V), lambda i: (i, 0)), jax.ShapeDtypeStruct((M * DA_HEADS, DA_V), F32))
        outs = [o(DA_W, BF16), kt if fuse else o(DA_W, F32), vf, o(DA_W, BF16), vt if fuse else o(DA_W, BF16),
                o(DA_W, BF16)]
        body = _proj_a_kernel
    else:
        outs = [o(SW_W, BF16), o(SW_W, BF16)]
        body = _proj_b_kernel
    mem_outs = [o(MEM_W, BF16)] if fuse else [o(MEM_W, BF16), o(MEM_W, BF16)]
    if layer == "a":
        outs = outs + mem_outs
    else:
        vt = (pl.BlockSpec((tm // SWA_TILE, LANES, SWA_TILE), lambda i: (i, 0, 0)),
              jax.ShapeDtypeStruct((M // SWA_TILE, LANES, SWA_TILE), BF16))
        values = [vt, vt] if fuse else [o(LANES, BF16)] * 2
        outs = outs + mem_outs + [o(LANES, F32), o(LANES, F32)] + [o(LANES, BF16)] * 2 + values
    return pl.pallas_call(
        functools.partial(body, fuse),
        grid=(M // tm,),
        in_specs=in_specs,
        out_specs=[s for s, _ in outs],
        out_shape=[t for _, t in outs],
        compiler_params=_params("arbitrary"),
        name="proj_" + layer,
    )(*args)


def _diff_prompt_kernel(lam_init, lamv_ref, q_ref, k_ref, vt_ref, g_ref, om_ref, x_ref, wout_ref,
                        lng_ref, lnb_ref, dng_ref, y_ref, m_ref, alpha_ref, acc_ref, s_ref, p_ref, mix_ref):
    t = DIFF_TILE
    SLOTS = s_ref.shape[0]
    LEAD = PIPE_LEAD
    i = pl.program_id(1)
    heads = [slice(h * DA_V, (h + 1) * DA_V) for h in range(DA_HEADS)]
    lane = lax.broadcasted_iota(jnp.int32, (t, DA_V), 1)
    units = []
    for hs in heads:
        qh = q_ref[:, hs]
        zero = jnp.zeros_like(qh)
        units += [(hs, jnp.where(lane < DA_QK, qh, zero)), (hs, jnp.where(lane >= DA_QK, qh, zero))]
    m_ref[...] = jnp.full(m_ref.shape, -jnp.inf, F32)
    acc_ref[...] = jnp.zeros(acc_ref.shape, F32)

    n = len(units)
    assert n == SLOTS and LEAD + 1 < n

    def exp_stage(u):
        _stage_exp(s_ref.at[u], p_ref.at[u], m_ref.at[u])

    def pv_stage(j, u):
        _stage_pv(vt_ref[0, j, units[u][0], :], p_ref.at[u], alpha_ref.at[u], acc_ref.at[u])

    def block(j, mask, j_before):
        rows = pl.ds(pl.multiple_of(j * t, t), t)
        for u, (hs, qq) in enumerate(units):
            _stage_scores(qq, k_ref[0, rows, hs], mask, s_ref.at[u], m_ref.at[u], alpha_ref.at[u])
            if u >= LEAD:
                exp_stage(u - LEAD)
            elif j_before is not None:
                exp_stage(u - LEAD + n)
            if u >= LEAD + 1:
                pv_stage(j, u - LEAD - 1)
            elif j_before is not None:
                pv_stage(j_before, u - LEAD - 1 + n)

    def run(j0, count):
        block(j0, None, jnp.where(j0 == 0, i, j0 - 1))
        for d in range(1, count):
            block(j0 + d, None, j0 + d - 1)

    def quad(jj, carry):
        run(UNROLL * jj, UNROLL)
        return carry

    kc = lax.broadcasted_iota(jnp.int32, (t, t), 0) // CHUNK
    qc = lax.broadcasted_iota(jnp.int32, (t, t), 1) // CHUNK
    block(i, kc <= qc, None)
    lax.fori_loop(0, i // UNROLL, quad, 0)
    done = i // UNROLL * UNROLL
    rest = i % UNROLL
    for part in (2, 1):
        @pl.when(rest & part != 0)
        def _(part=part):
            run(done + (rest & ~(2 * part - 1)), part)

    last = jnp.where(i == 0, i, i - 1)
    for u in range(n - LEAD, n):
        exp_stage(u)
        pv_stage(last, u - 1)
    pv_stage(last, n - 1)

    lam = _lambda(lamv_ref[...], lam_init)
    for h, hs in enumerate(heads):
        o1 = acc_ref[2 * h, :DA_V, :] / acc_ref[2 * h, DA_V:DA_V + 1, :]
        o2 = acc_ref[2 * h + 1, :DA_V, :] / acc_ref[2 * h + 1, DA_V:DA_V + 1, :]
        od = (o1 - lam * o2).T
        od = od * lax.rsqrt(jnp.mean(od * od, axis=-1, keepdims=True) + LN_EPS) * dng_ref[...] * (1.0 - lam_init)
        mix_ref[:, hs] = (od * g_ref[:, hs].astype(F32)).astype(BF16)
    mix_ref[:, DA_W:] = om_ref[...]
    y_ref[...] = _merge(x_ref[...], mix_ref[...], wout_ref[0], lng_ref[...], lnb_ref[...])


def _diff_prompt(lam_init, lamv, q, k, vt, g, om, x, w_out, layer, ln_g, ln_b, dng, *, B, S):
    t = DIFF_TILE
    nq = S // t
    row = lambda w: pl.BlockSpec((t, w), lambda b, i: (b * nq + i, 0))
    full = lambda shape: pl.BlockSpec(shape, lambda b, i: (0,) * len(shape))
    return pl.pallas_call(
        functools.partial(_diff_prompt_kernel, lam_init),
        grid=(B, nq),
        in_specs=[full((4, DA_QK)), row(DA_W), pl.BlockSpec((1, S, DA_W), lambda b, i: (b, 0, 0)),
                  pl.BlockSpec((1, nq, DA_W, t), lambda b, i: (b, 0, 0, 0)), row(DA_W), row(MEM_W), row(D_MODEL),
                  _resident((1, MIX_W, D_MODEL), layer, 0, 0), full((1, D_MODEL)), full((1, D_MODEL)), full((1, DA_V))],
        out_specs=row(D_MODEL),
        out_shape=jax.ShapeDtypeStruct((B * S, D_MODEL), F32),
        scratch_shapes=[pltpu.VMEM((2 * DA_HEADS, 1, t), F32), pltpu.VMEM((2 * DA_HEADS, 1, t), F32),
                        pltpu.VMEM((2 * DA_HEADS, DA_V + SUM_ROWS, t), F32),
                        pltpu.VMEM((2 * DA_HEADS, t, t), F32), pltpu.VMEM((2 * DA_HEADS, t, t), BF16),
                        pltpu.VMEM((t, MIX_W), BF16)],
        compiler_params=_params("arbitrary", "arbitrary"),
        name="diff_prompt",
    )(lamv, q, k.reshape(B, S, DA_W), vt.reshape(B, nq, DA_W, t), g, om, x, w_out, ln_g, ln_b, dng)


def _swa_prompt_kernel(tq, sinks_ref, q_ref, k_ref, kr_ref, vt_ref, vtr_ref, g_ref, om_ref, x_ref, wout_ref,
                       lng_ref, lnb_ref, y_ref, mix_ref):
    i = pl.program_id(1)
    w, nk, nsub = SWA_TILE, 2 * SWA_TILE, tq // SWA_TILE
    low = lax.broadcasted_iota(jnp.int32, (w, LANES), 1) < SW_HD
    row_low = lax.broadcasted_iota(jnp.int32, (LANES, w), 0) < SW_HD
    sink_row = lambda hs: jnp.concatenate([jnp.full((1, w), sinks_ref[h] * LOG2E, F32) for h in hs], axis=1)
    sinks_plain, sinks_swapped = sink_row((0, 2, 5, 7)), sink_row((1, 3, 4, 6))
    kc = lax.broadcasted_iota(jnp.int32, (nk, 4 * w), 0) // CHUNK
    qc = lax.broadcasted_iota(jnp.int32, (nk, 4 * w), 1) % w // CHUNK
    firsts, scores = [], []
    for st in range(nsub):
        sub = i * nsub + st
        first = jnp.maximum(sub - 1, 0)
        back = (sub - first) * (w // CHUNK)
        mask = (kc - back <= qc) & (kc - back >= qc - WINDOW // CHUNK)
        rows = pl.ds(pl.multiple_of(first * w, w), nk)
        q = q_ref[st * w:(st + 1) * w, :]
        slabs = [q[:, p * LANES:(p + 1) * LANES] for p in range(SW_HEADS // 2)]
        even = [jnp.where(low, sl, jnp.zeros_like(sl)) for sl in slabs]
        odd = [jnp.where(low, jnp.zeros_like(sl), sl) for sl in slabs]
        firsts.append(first)
        scores.append((_window_scores(jnp.concatenate([even[0], even[1], odd[2], odd[3]], axis=0), k_ref[0, rows, :],
                                      mask),
                       _window_scores(jnp.concatenate([odd[0], odd[1], even[2], even[3]], axis=0), kr_ref[0, rows, :],
                                      mask)))
    for st, (first, (s_plain, s_swapped)) in enumerate(zip(firsts, scores)):
        vts = jnp.concatenate([vt_ref[0, first], vt_ref[0, first + 1]], axis=1)
        vtrs = jnp.concatenate([vtr_ref[0, first], vtr_ref[0, first + 1]], axis=1)
        o_plain = _window_attend(s_plain, vts, sinks_plain)
        o_swapped = _window_attend(s_swapped, vtrs, sinks_swapped)
        for p in range(SW_HEADS // 2):
            cols = slice(p * w, (p + 1) * w)
            lo, hi = (o_plain, o_swapped) if p < 2 else (o_swapped, o_plain)
            slab = jnp.where(row_low, lo[:, cols], hi[:, cols]).T
            gate = g_ref[st * w:(st + 1) * w, p * LANES:(p + 1) * LANES].astype(F32)
            mix_ref[st * w:(st + 1) * w, p * LANES:(p + 1) * LANES] = (slab * gate).astype(BF16)
    mix_ref[:, SW_W:] = om_ref[...]
    y_ref[...] = _merge(x_ref[...], mix_ref[...], wout_ref[0], lng_ref[...], lnb_ref[...])


def _swa_prompt(sinks, q, k, kr, vt, vtr, g, om, x, w_out, layer, ln_g, ln_b, *, B, S, tq):
    nq = S // tq
    row = lambda w: pl.BlockSpec((tq, w), lambda b, i: (b * nq + i, 0))
    full = lambda shape: pl.BlockSpec(shape, lambda b, i: (0,) * len(shape))
    seq = pl.BlockSpec((1, S, LANES), lambda b, i: (b, 0, 0))
    seqt = pl.BlockSpec((1, S // SWA_TILE, LANES, SWA_TILE), lambda b, i: (b, 0, 0, 0))
    r3 = lambda t: t.reshape(B, S, LANES)
    r4 = lambda t: t.reshape(B, S // SWA_TILE, LANES, SWA_TILE)
    return pl.pallas_call(
        functools.partial(_swa_prompt_kernel, tq),
        grid=(B, nq),
        in_specs=[pl.BlockSpec(memory_space=pltpu.SMEM), row(SW_W), seq, seq, seqt, seqt, row(SW_W), row(MEM_W),
                  row(D_MODEL), _resident((1, MIX_W, D_MODEL), layer, 0, 0), full((1, D_MODEL)), full((1, D_MODEL))],
        out_specs=row(D_MODEL),
        out_shape=jax.ShapeDtypeStruct((B * S, D_MODEL), F32),
        scratch_shapes=[pltpu.VMEM((tq, MIX_W), BF16)],
        compiler_params=_params("arbitrary", "arbitrary"),
        name="swa_prompt",
    )(sinks, q, r3(k), r3(kr), r4(vt), r4(vtr), g, om, x, w_out, ln_g, ln_b)


def _sample_tail(b, last, o_first, g_ref, mq_ref, gm_ref, cmk_ref, cmv_ref, x_ref, wout_ref, lng_ref, lnb_ref,
                 y_ref, mix_ref, T):
    om = _mem_attend(mq_ref[...], _head_rows(cmk_ref, MEM_HEADS, 0, 0), _head_rows(cmv_ref, MEM_HEADS, 0, 0))
    om = om * gm_ref[...].astype(F32)
    mix = jnp.concatenate([(o_first * g_ref[...].astype(F32)).astype(BF16), om.astype(BF16)], axis=1)
    mix_ref[pl.ds(pl.multiple_of(b * T, T), T), :] = mix

    @pl.when(last)
    def _():
        y_ref[...] = _merge(x_ref[...], mix_ref[...], wout_ref[0], lng_ref[...], lnb_ref[...])


def _diff_sample_kernel(lam_init, T, lamv_ref, q_ref, kn_ref, vn_ref, ckt_ref, cv_ref, g_ref, mq_ref, gm_ref,
                        cmk_ref, cmv_ref, x_ref, wout_ref, lng_ref, lnb_ref, dng_ref, y_ref,
                        m_ref, l_ref, acc_ref, mix_ref):
    b, j = pl.program_id(0), pl.program_id(1)
    nb, nj = pl.num_programs(0), pl.num_programs(1)
    heads = [slice(h * DA_V, (h + 1) * DA_V) for h in range(DA_HEADS)]
    qqs = [_split_components(q_ref[:, hs]) for hs in heads]

    @pl.when(j == 0)
    def _():
        m_ref[...] = jnp.full(m_ref.shape, -jnp.inf, F32)
        l_ref[...] = jnp.zeros(l_ref.shape, F32)
        acc_ref[...] = jnp.zeros(acc_ref.shape, F32)
        _flash_update(qqs, lambda h: kn_ref[:, heads[h]], lambda h: vn_ref[:, heads[h]], False, m_ref, l_ref, acc_ref)

    _flash_update(qqs, lambda h: ckt_ref[0, heads[h], :].astype(BF16), _head_rows(cv_ref, DA_HEADS, 0), True,
                  m_ref, l_ref, acc_ref)

    @pl.when(j == nj - 1)
    def _():
        lam = _lambda(lamv_ref[...], lam_init)
        rows = [slice(h * 2 * T, (h + 1) * 2 * T) for h in range(DA_HEADS)]
        od = jnp.concatenate([_diff_finish(acc_ref[r, :], l_ref[r, :], lam, dng_ref[...], lam_init) for r in rows],
                             axis=1)
        _sample_tail(b, b == nb - 1, od, g_ref, mq_ref, gm_ref, cmk_ref, cmv_ref, x_ref, wout_ref, lng_ref,
                     lnb_ref, y_ref, mix_ref, T)


def _diff_sample(lam_init, lamv, q, kn, vn, ckt, cv, g, mq, gm, cmk, cmv, layer, x, w_out, ln_g, ln_b, dng, *,
                 NB, T, tk):
    P = ckt.shape[2]
    row = lambda w: pl.BlockSpec((T, w), lambda b, j: (b, 0))
    full = lambda shape: pl.BlockSpec(shape, lambda b, j: (0,) * len(shape))
    memc = pl.BlockSpec((1, 1, N_MEM * MEM_HEADS, MEM_HD), lambda b, j: (layer, b, 0, 0))
    return pl.pallas_call(
        functools.partial(_diff_sample_kernel, lam_init, T),
        grid=(NB, P // tk),
        in_specs=[full((4, DA_QK)), row(DA_W), row(DA_W), row(DA_W),
                  pl.BlockSpec((1, DA_W, tk), lambda b, j: (b, 0, j)),
                  pl.BlockSpec((1, tk * DA_HEADS, DA_V), lambda b, j: (b, j, 0)),
                  row(DA_W), row(MEM_W), row(MEM_W), memc, memc, full((NB * T, D_MODEL)),
                  _resident((1, MIX_W, D_MODEL), layer, 0, 0), full((1, D_MODEL)), full((1, D_MODEL)), full((1, DA_V))],
        out_specs=full((NB * T, D_MODEL)),
        out_shape=jax.ShapeDtypeStruct((NB * T, D_MODEL), F32),
        scratch_shapes=[pltpu.VMEM((DA_HEADS * 2 * T, 1), F32), pltpu.VMEM((DA_HEADS * 2 * T, 1), F32),
                        pltpu.VMEM((DA_HEADS * 2 * T, DA_V), F32), pltpu.VMEM((NB * T, MIX_W), BF16)],
        compiler_params=_params("arbitrary", "arbitrary"),
        name="diff_sample",
    )(lamv, q, kn, vn, ckt, cv, g, mq, gm, cmk, cmv, x, w_out, ln_g, ln_b, dng)


def _swa_sample_kernel(T, sinks_ref, q_ref, kn_ref, knr_ref, vn_ref, vnr_ref, ckt_ref, cvt_ref, g_ref, mq_ref, gm_ref,
                       cmk_ref, cmv_ref, x_ref, wout_ref, lng_ref, lnb_ref, y_ref, mix_ref):
    b = pl.program_id(0)
    swap = lambda t: jnp.concatenate([t[SW_HD:], t[:SW_HD]], axis=0).astype(BF16)
    ckt, cvt = ckt_ref[0], cvt_ref[0]
    kt, ktr, vt, vtr = ckt.astype(BF16), swap(ckt), cvt.astype(BF16), swap(cvt)
    kn, knr, vn, vnr = kn_ref[...], knr_ref[...], vn_ref[...], vnr_ref[...]
    attend = lambda qq, swapped, sink: _sink_attend_cached(
        qq, ktr if swapped else kt, knr if swapped else kn, vtr if swapped else vt, vnr if swapped else vn, sink)
    o = _swa_heads(q_ref[...], attend, lambda h: sinks_ref[h] * LOG2E)
    _sample_tail(b, b == pl.num_programs(0) - 1, o, g_ref, mq_ref, gm_ref, cmk_ref, cmv_ref, x_ref, wout_ref,
                 lng_ref, lnb_ref, y_ref, mix_ref, T)


def _swa_sample(sinks, q, kn, knr, vn, vnr, ckt, cvt, g, mq, gm, cmk, cmv, layer, x, w_out, ln_g, ln_b, *, NB, T):
    W = ckt.shape[2]
    row = lambda w: pl.BlockSpec((T, w), lambda b: (b, 0))
    full = lambda shape: pl.BlockSpec(shape, lambda b: (0,) * len(shape))
    cache = pl.BlockSpec((1, LANES, W), lambda b: (b, 0, 0))
    memc = pl.BlockSpec((1, 1, N_MEM * MEM_HEADS, MEM_HD), lambda b: (layer, b, 0, 0))
    return pl.pallas_call(
        functools.partial(_swa_sample_kernel, T),
        grid=(NB,),
        in_specs=[pl.BlockSpec(memory_space=pltpu.SMEM), row(SW_W), row(LANES), row(LANES), row(LANES), row(LANES),
                  cache, cache, row(SW_W), row(MEM_W), row(MEM_W), memc, memc, full((NB * T, D_MODEL)),
                  _resident((1, MIX_W, D_MODEL), layer, 0, 0), full((1, D_MODEL)), full((1, D_MODEL))],
        out_specs=full((NB * T, D_MODEL)),
        out_shape=jax.ShapeDtypeStruct((NB * T, D_MODEL), F32),
        scratch_shapes=[pltpu.VMEM((NB * T, MIX_W), BF16)],
        compiler_params=_params("arbitrary"),
        name="swa_sample",
    )(sinks, q, kn, knr, vn, vnr, ckt, cvt, g, mq, gm, cmk, cmv, x, w_out, ln_g, ln_b)


def _rope_tables(pos):
    T = pos.shape[0]
    inv = ROPE_THETA ** (-np.arange(ROPE_HALF, dtype=np.float64) / ROPE_HALF)
    ang = pos.astype(np.float64)[:, None] * inv[None, :]
    cos, sin = np.cos(ang), np.sin(ang)
    rest = SW_HD - 2 * ROPE_HALF
    c = np.concatenate([cos, cos, np.ones((T, rest))], axis=1)
    a = np.concatenate([-sin, np.zeros((T, SW_HD - ROPE_HALF))], axis=1)
    b = np.concatenate([np.zeros((T, ROPE_HALF)), sin, np.zeros((T, rest))], axis=1)
    return tuple(jnp.asarray(np.tile(t, (1, LANES // SW_HD)), F32) for t in (c, a, b))


def _feature_major(t):
    n = t.ndim
    return jnp.transpose(t, (0,) + tuple(range(2, n)) + (1,)).reshape(t.shape[0], -1, t.shape[1])


def _rows_major(t, tail):
    n = len(tail)
    return jnp.transpose(t.reshape((t.shape[0],) + tail + (t.shape[2],)), (0, n + 1) + tuple(range(1, n + 1)))


def kernel(x_prompt, x_sample, mem_prompt, cache_diff_k, cache_diff_v, cache_swa_k, cache_swa_v, cache_mem_k,
           cache_mem_v, w_in_a, lam_q1, lam_k1, lam_q2, lam_k2, diff_norm_g, w_in_b, sinks, w_kv_shared, w_mem_kv,
           w_out, ln_g, ln_b):
    B, S, _ = x_prompt.shape
    NB, T, _ = x_sample.shape
    P = cache_diff_k.shape[2]
    assert w_in_a.shape[0] == 1 and w_in_b.shape[0] == 1 and w_out.shape[0] == DEPTH

    tabs_p = _rope_tables(np.arange(S))
    tabs_s = _rope_tables(np.tile(P + np.arange(T), NB))
    xp = x_prompt.reshape(B * S, D_MODEL)
    xs = x_sample.reshape(NB * T, D_MODEL)
    wa, wb, wo = (w_in_a[0],), (w_in_b[0], w_kv_shared), w_out
    lng = ln_g.reshape(DEPTH, 1, D_MODEL)
    lnb = ln_b.reshape(DEPTH, 1, D_MODEL)
    lamv = jnp.concatenate([lam_q1, lam_k1, lam_q2, lam_k2], axis=0)
    dng = diff_norm_g.reshape(1, DA_V)
    lam_init = 0.8 - 0.6 * math.exp(-0.3 * 0)

    mkf, mvf, mkb, mvb = _mem_kv(mem_prompt.reshape(B * N_MEM, D_MODEL), w_mem_kv)
    mem_b = lambda l: (mkb[l].reshape(B, N_MEM, MEM_W), mvb[l].reshape(B, N_MEM, MEM_W))
    head_rows = lambda t: t.reshape(t.shape[:-3] + (t.shape[-3] * t.shape[-2], t.shape[-1]))
    cmk, cmv = head_rows(cache_mem_k), head_rows(cache_mem_v)

    q, kt, vf, kb, vb, g, om = _project("a", xp, wa, tabs_p, mem_b(0), tm=PROJ_ROWS, rows_per_batch=S)
    xp1 = _diff_prompt(lam_init, lamv, q, kb, vb, g, om, xp, wo, 0, lng[0], lnb[0], dng, B=B, S=S)
    qs, kfs, vfs, kbs, vbs, gs_, mqs, gms = _project("a", xs, wa, tabs_s, tm=NB * T)
    xs1 = _diff_sample(lam_init, lamv, qs, kbs, vbs, _feature_major(cache_diff_k[0]), head_rows(cache_diff_v[0]), gs_,
                       mqs, gms, cmk, cmv, 0, xs, wo, lng[0], lnb[0], dng, NB=NB, T=T, tk=SAMPLE_KEYS)

    q, g, om, skf, svf, kb1, kr1, vb1, vr1 = _project("b", xp1, wb, tabs_p, mem_b(1), tm=PROJ_ROWS, rows_per_batch=S)
    yp = _swa_prompt(sinks[0], q, kb1, kr1, vb1, vr1, g, om, xp1, wo, 1, lng[1], lnb[1], B=B, S=S, tq=SWA_ROWS)
    qs, gs_, mqs, gms, skfs, svfs, kbs1, krs1, vbs1, vrs1 = _project("b", xs1, wb, tabs_s, tm=NB * T)
    ys = _swa_sample(sinks[0], qs, kbs1, krs1, vbs1, vrs1, _feature_major(cache_swa_k), _feature_major(cache_swa_v),
                     gs_, mqs, gms, cmk, cmv, 1, xs1, wo, lng[1], lnb[1], NB=NB, T=T)

    wr_p = min(WINDOW, S)
    window = lambda t: t.reshape(B, S, LANES)[:, S - wr_p:, :].reshape(B, wr_p, SW_KV, SW_HD)
    swa_kp, swa_vp = window(skf), window(svf)
    roll = lambda c, n: jnp.concatenate([c, n.reshape(NB, T, SW_KV, SW_HD)], axis=1)[:, T:]
    return (yp.reshape(B, S, D_MODEL), ys.reshape(NB, T, D_MODEL),
            _rows_major(kt, (DA_HEADS, 2, DA_QK))[None], vf.reshape(1, B, S, DA_HEADS, DA_V),
            kfs.reshape(1, NB, T, DA_HEADS, 2, DA_QK), vfs.reshape(1, NB, T, DA_HEADS, DA_V),
            swa_kp, swa_vp, roll(cache_swa_k, skfs), roll(cache_swa_v, svfs),
            mkf.reshape(DEPTH, B, N_MEM, MEM_HEADS, MEM_HD), mvf.reshape(DEPTH, B, N_MEM, MEM_HEADS, MEM_HD))
```

```python
import functools
import math

import jax
import jax.numpy as jnp
import numpy as np
from jax import lax
from jax.experimental import pallas as pl
from jax.experimental.pallas import tpu as pltpu

D_MODEL = 1024
CHUNK = 64
N_MEM = 256
DA_HEADS = 4
DA_QK = 64
DA_V = 128
DA_W = 512
SW_HEADS = 8
SW_KV = 2
SW_GROUP = 4
SW_HD = 64
SW_W = 512
WINDOW = 128
MEM_HEADS = 4
MEM_HD = 128
MEM_W = 512
MIX_W = 1024
ROPE_THETA = 500000.0
ROPE_HALF = 8
DEPTH = 2
DN_ALPHA = (2 * DEPTH) ** 0.25
LN_EPS = 1e-5
NEG = -1e30
LANES = 128
DIFF_TILE = 256
PIPE_LEAD = 4
DIFF_ROWS = 512
SWA_TILE = 128
SUM_ROWS = 16
PROJ_ROWS = 1024
SWA_ROWS = 1024
SAMPLE_KEYS = 2048
LOG2E = math.log2(math.e)

F32 = jnp.float32
BF16 = jnp.bfloat16
VMEM_LIMIT = 48 * 1024 * 1024

_NT = (((1,), (1,)), ((), ()))


def _resident(shape, *first):
    idx = first or (0,) * len(shape)
    return pl.BlockSpec(shape, lambda *_: idx, pipeline_mode=pl.Buffered(1))


def _params(*sem):
    return pltpu.CompilerParams(dimension_semantics=sem, vmem_limit_bytes=VMEM_LIMIT)


def _rope(x, c, a, b):
    outs = []
    for i in range(x.shape[1] // LANES):
        blk = x[:, i * LANES:(i + 1) * LANES]
        outs.append(blk * c + pltpu.roll(blk, LANES - ROPE_HALF, 1) * a + pltpu.roll(blk, ROPE_HALF, 1) * b)
    return outs[0] if len(outs) == 1 else jnp.concatenate(outs, axis=1)


def _silu(g):
    return g * (1.0 / (1.0 + jnp.exp(-g)))


def _layer_norm(z, g, b):
    mu = jnp.mean(z, axis=-1, keepdims=True)
    d = z - mu
    var = jnp.mean(d * d, axis=-1, keepdims=True)
    return d * lax.rsqrt(var + LN_EPS) * g + b


def _mem_attend(mq, mk_of, mv_of):
    outs = []
    for h in range(MEM_HEADS):
        sl = slice(h * MEM_HD, (h + 1) * MEM_HD)
        s = lax.dot_general(mq[:, sl], mk_of(h), _NT, preferred_element_type=F32) * (MEM_HD ** -0.5)
        p = jnp.exp(s - jnp.max(s, axis=1, keepdims=True))
        l = jnp.sum(p, axis=1, keepdims=True)
        outs.append(jnp.dot(p.astype(BF16), mv_of(h), preferred_element_type=F32) / l)
    return jnp.concatenate(outs, axis=1)


def _head_cols(ref):
    return lambda h: ref[0, :, h * MEM_HD:(h + 1) * MEM_HD]


def _head_idx(ref, heads, h, *lead):
    return lead + (pl.ds(h, ref.shape[-2] // heads, stride=heads), slice(None))


def _head_rows(ref, heads, *lead):
    return lambda h: ref[_head_idx(ref, heads, h, *lead)].astype(BF16)


def _merge(x, mix, w_out, ln_g, ln_b):
    return _layer_norm(DN_ALPHA * x + jnp.dot(mix, w_out.astype(BF16), preferred_element_type=F32), ln_g, ln_b)


def _lambda(lamv, lam_init):
    e1 = jnp.exp(jnp.sum(lamv[0:1, :] * lamv[1:2, :], axis=1, keepdims=True))
    e2 = jnp.exp(jnp.sum(lamv[2:3, :] * lamv[3:4, :], axis=1, keepdims=True))
    return e1 - e2 + lam_init


def _split_components(qh):
    lane = lax.broadcasted_iota(jnp.int32, qh.shape, 1)
    zero = jnp.zeros_like(qh)
    return jnp.concatenate([jnp.where(lane < DA_QK, qh, zero), jnp.where(lane >= DA_QK, qh, zero)], axis=0)


def _flash_update(qqs, k_of, v_of, keys_on_lanes, m_ref, l_ref, acc_ref):
    if keys_on_lanes:
        s = [jnp.dot(qq, k_of(h), preferred_element_type=F32) for h, qq in enumerate(qqs)]
    else:
        s = [lax.dot_general(qq, k_of(h), _NT, preferred_element_type=F32) for h, qq in enumerate(qqs)]
    s = jnp.concatenate(s, axis=0)
    m_old = m_ref[...]
    m_new = jnp.maximum(m_old, jnp.max(s, axis=1, keepdims=True))
    alpha = jnp.exp2(m_old - m_new)
    p = jnp.exp2(s - m_new)
    l_ref[...] = alpha * l_ref[...] + jnp.sum(p, axis=1, keepdims=True)
    pb = p.astype(BF16)
    r = qqs[0].shape[0]
    pv = [jnp.dot(pb[h * r:(h + 1) * r], v_of(h), preferred_element_type=F32) for h in range(len(qqs))]
    acc_ref[...] = alpha * acc_ref[...] + jnp.concatenate(pv, axis=0)
    m_ref[...] = m_new


def _stage_scores(qq, ks, mask, s_ref, m_ref, alpha_ref):
    s = lax.dot_general(ks, qq, _NT, preferred_element_type=F32)
    if mask is not None:
        s = jnp.where(mask, s, NEG)
    s_ref[...] = s
    m_old = m_ref[...]
    m_new = jnp.maximum(m_old, jnp.max(s, axis=0, keepdims=True))
    alpha_ref[...] = jnp.exp2(m_old - m_new)
    m_ref[...] = m_new


def _stage_exp(s_ref, p_ref, m_ref):
    p_ref[...] = jnp.exp2(s_ref[...] - m_ref[...]).astype(BF16)


def _stage_pv(vts, p_ref, alpha_ref, acc_ref):
    ones = jnp.ones((SUM_ROWS, vts.shape[1]), BF16)
    pv = jnp.dot(jnp.concatenate([vts, ones], axis=0), p_ref[...], preferred_element_type=F32)
    acc_ref[...] = alpha_ref[...] * acc_ref[...] + pv


def _diff_finish(acc, l, lam, dng, lam_init):
    rows = acc.shape[0] // 2
    o = acc / l
    od = o[:rows] - lam * o[rows:]
    return od * lax.rsqrt(jnp.mean(od * od, axis=-1, keepdims=True) + LN_EPS) * dng * (1.0 - lam_init)


def _sink_attend_cached(qq, kt, kn, vt, vn, sink):
    s_c = jnp.dot(qq, kt, preferred_element_type=F32)
    s_n = lax.dot_general(qq, kn, _NT, preferred_element_type=F32)
    m = jnp.maximum(jnp.maximum(jnp.max(s_c, axis=1, keepdims=True), jnp.max(s_n, axis=1, keepdims=True)), sink)
    p_c, p_n = jnp.exp2(s_c - m), jnp.exp2(s_n - m)
    l = jnp.sum(p_c, axis=1, keepdims=True) + jnp.sum(p_n, axis=1, keepdims=True) + jnp.exp2(sink - m)
    o = (lax.dot_general(p_c.astype(BF16), vt, _NT, preferred_element_type=F32)
         + jnp.dot(p_n.astype(BF16), vn, preferred_element_type=F32))
    return o / l


def _window_scores(qq4, ks, mask):
    return jnp.where(mask, lax.dot_general(ks, qq4, _NT, preferred_element_type=F32), NEG)


def _window_attend(s, vts, sinks4):
    m = jnp.maximum(jnp.max(s, axis=0, keepdims=True), sinks4)
    p = jnp.exp2(s - m).astype(BF16)
    ones = jnp.ones((SUM_ROWS, vts.shape[1]), BF16)
    o = jnp.dot(jnp.concatenate([vts, ones], axis=0), p, preferred_element_type=F32)
    return o[:LANES] / (o[LANES:LANES + 1] + jnp.exp2(sinks4 - m))


def _swa_heads(q, attend, sink_of):
    low = lax.broadcasted_iota(jnp.int32, (q.shape[0], LANES), 1) < SW_HD
    outs = []
    for pair in range(SW_HEADS // 2):
        kv = pair // 2
        slab = q[:, pair * LANES:(pair + 1) * LANES]
        zero = jnp.zeros_like(slab)
        o_even = attend(jnp.where(low, slab, zero), kv == 1, sink_of(2 * pair))
        o_odd = attend(jnp.where(low, zero, slab), kv == 0, sink_of(2 * pair + 1))
        outs.append(jnp.where(low, o_even, o_odd))
    return jnp.concatenate(outs, axis=1)


def _mem_kv_kernel(mem_ref, w_ref, kf_ref, vf_ref, kb_ref, vb_ref):
    h = jnp.dot(mem_ref[...].astype(BF16), w_ref[0].astype(BF16), preferred_element_type=F32)
    k, v = h[:, :MEM_W], h[:, MEM_W:]
    for hd in range(MEM_HEADS):
        kf_ref[_head_idx(kf_ref, MEM_HEADS, hd, 0)] = k[:, hd * MEM_HD:(hd + 1) * MEM_HD]
        vf_ref[_head_idx(vf_ref, MEM_HEADS, hd, 0)] = v[:, hd * MEM_HD:(hd + 1) * MEM_HD]
    kb_ref[0] = k.astype(BF16)
    vb_ref[0] = v.astype(BF16)


def _mem_kv(mem, w):
    rows = mem.shape[0]
    fshape, fspec = (DEPTH, rows * MEM_HEADS, MEM_HD), pl.BlockSpec((1, rows * MEM_HEADS, MEM_HD), lambda l: (l, 0, 0))
    bshape, bspec = (DEPTH, rows, MEM_W), pl.BlockSpec((1, rows, MEM_W), lambda l: (l, 0, 0))
    return pl.pallas_call(
        _mem_kv_kernel,
        grid=(DEPTH,),
        in_specs=[pl.BlockSpec((rows, D_MODEL), lambda l: (0, 0)),
                  pl.BlockSpec((1, D_MODEL, 2 * MEM_W), lambda l: (l, 0, 0))],
        out_specs=[fspec, fspec, bspec, bspec],
        out_shape=[jax.ShapeDtypeStruct(fshape, F32), jax.ShapeDtypeStruct(fshape, F32),
                   jax.ShapeDtypeStruct(bshape, BF16), jax.ShapeDtypeStruct(bshape, BF16)],
        compiler_params=_params("arbitrary"),
        name="mem_kv",
    )(mem, w)


def _proj_a_kernel(fuse_mem, x_ref, w_ref, c_ref, a_ref, b_ref, *refs):
    if fuse_mem:
        mk_ref, mv_ref, q_ref, kf_ref, vf_ref, kb_ref, vb_ref, g_ref, om_ref = refs
    else:
        q_ref, kf_ref, vf_ref, kb_ref, vb_ref, g_ref, mq_ref, gm_ref = refs
    x = x_ref[...].astype(BF16)
    c, a, b = c_ref[...], a_ref[...], b_ref[...]

    def cols(i):
        return jnp.dot(x, w_ref[:, i * DA_W:(i + 1) * DA_W].astype(BF16), preferred_element_type=F32)

    q_ref[...] = (_rope(cols(0), c, a, b) * (DA_QK ** -0.5 * LOG2E)).astype(BF16)
    k = _rope(cols(1), c, a, b)
    kb_ref[...] = k.astype(BF16)
    v = cols(2)
    for hd in range(DA_HEADS):
        vf_ref[_head_idx(vf_ref, DA_HEADS, hd)] = v[:, hd * DA_V:(hd + 1) * DA_V]
    if fuse_mem:
        kf_ref[0] = k.T
        for blk in range(vb_ref.shape[0]):
            vb_ref[blk] = v[blk * DIFF_TILE:(blk + 1) * DIFF_TILE, :].T.astype(BF16)
    else:
        kf_ref[...] = k
        vb_ref[...] = v.astype(BF16)
    g_ref[...] = _silu(cols(3)).astype(BF16)
    mq = cols(4).astype(BF16)
    gm = _silu(cols(5))
    if fuse_mem:
        om_ref[...] = (_mem_attend(mq, _head_cols(mk_ref), _head_cols(mv_ref)) * gm).astype(BF16)
    else:
        mq_ref[...] = mq
        gm_ref[...] = gm.astype(BF16)


def _proj_b_kernel(fuse_mem, x_ref, w_ref, wkv_ref, c_ref, a_ref, b_ref, *refs):
    if fuse_mem:
        mk_ref, mv_ref, q_ref, g_ref, om_ref, kf_ref, vf_ref, kb_ref, kr_ref, vb_ref, vr_ref = refs
    else:
        q_ref, g_ref, mq_ref, gm_ref, kf_ref, vf_ref, kb_ref, kr_ref, vb_ref, vr_ref = refs
    x = x_ref[...].astype(BF16)
    c, a, b = c_ref[...], a_ref[...], b_ref[...]

    def cols(i):
        return jnp.dot(x, w_ref[:, i * SW_W:(i + 1) * SW_W].astype(BF16), preferred_element_type=F32)

    q_ref[...] = (_rope(cols(0), c, a, b) * (SW_HD ** -0.5 * LOG2E)).astype(BF16)
    g_ref[...] = _silu(cols(1)).astype(BF16)
    mq = cols(2).astype(BF16)
    gm = _silu(cols(3))
    if fuse_mem:
        om_ref[...] = (_mem_attend(mq, _head_cols(mk_ref), _head_cols(mv_ref)) * gm).astype(BF16)
    else:
        mq_ref[...] = mq
        gm_ref[...] = gm.astype(BF16)
    kv = jnp.dot(x, wkv_ref[...].astype(BF16), preferred_element_type=F32)
    k = _rope(kv[:, :LANES], c, a, b)
    v = kv[:, LANES:]
    kf_ref[...] = k
    vf_ref[...] = v
    kb_ref[...] = k.astype(BF16)
    kr_ref[...] = pltpu.roll(k, SW_HD, 1).astype(BF16)
    vr = pltpu.roll(v, SW_HD, 1)
    if fuse_mem:
        for blk in range(vb_ref.shape[0]):
            rows = slice(blk * SWA_TILE, (blk + 1) * SWA_TILE)
            vb_ref[blk] = v[rows, :].T.astype(BF16)
            vr_ref[blk] = vr[rows, :].T.astype(BF16)
    else:
        vb_ref[...] = v.astype(BF16)
        vr_ref[...] = vr.astype(BF16)


def _project(layer, x, ws, tabs, mem=None, *, tm, rows_per_batch=None):
    M = x.shape[0]
    fuse = mem is not None
    per = (rows_per_batch or M) // tm
    row = lambda w: pl.BlockSpec((tm, w), lambda i: (i, 0))
    tab = pl.BlockSpec((tm, LANES), lambda i: (i % per, 0))
    in_specs = [row(D_MODEL)] + [_resident(w.shape) for w in ws] + [tab, tab, tab]
    args = [x, *ws, *tabs]
    if fuse:
        mspec = pl.BlockSpec((1, N_MEM, MEM_W), lambda i: (i // per, 0, 0))
        in_specs += [mspec, mspec]
        args += list(mem)
    o = lambda w, dt: (row(w), jax.ShapeDtypeStruct((M, w), dt))
    if layer == "a":
        vt = (pl.BlockSpec((tm // DIFF_TILE, DA_W, DIFF_TILE), lambda i: (i, 0, 0)),
              jax.ShapeDtypeStruct((M // DIFF_TILE, DA_W, DIFF_TILE), BF16))
        kt = (pl.BlockSpec((1, DA_W, tm), lambda i: (i // per, 0, i % per)),
              jax.ShapeDtypeStruct((M // (per * tm), DA_W, per * tm), F32))
        vf = (pl.BlockSpec((tm * DA_HEADS, DA_V), lambda i: (i, 0)), jax.ShapeDtypeStruct((M * DA_HEADS, DA_V), F32))
        outs = [o(DA_W, BF16), kt if fuse else o(DA_W, F32), vf, o(DA_W, BF16), vt if fuse else o(DA_W, BF16),
                o(DA_W, BF16)]
        body = _proj_a_kernel
    else:
        outs = [o(SW_W, BF16), o(SW_W, BF16)]
        body = _proj_b_kernel
    mem_outs = [o(MEM_W, BF16)] if fuse else [o(MEM_W, BF16), o(MEM_W, BF16)]
    if layer == "a":
        outs = outs + mem_outs
    else:
        vt = (pl.BlockSpec((tm // SWA_TILE, LANES, SWA_TILE), lambda i: (i, 0, 0)),
              jax.ShapeDtypeStruct((M // SWA_TILE, LANES, SWA_TILE), BF16))
        values = [vt, vt] if fuse else [o(LANES, BF16)] * 2
        outs = outs + mem_outs + [o(LANES, F32), o(LANES, F32)] + [o(LANES, BF16)] * 2 + values
    return pl.pallas_call(
        functools.partial(body, fuse),
        grid=(M // tm,),
        in_specs=in_specs,
        out_specs=[s for s, _ in outs],
        out_shape=[t for _, t in outs],
        compiler_params=_params("arbitrary"),
        name="proj_" + layer,
    )(*args)


def _diff_prompt_kernel(lam_init, lamv_ref, q_ref, k_ref, vt_ref, g_ref, om_ref, x_ref, wout_ref,
                        lng_ref, lnb_ref, dng_ref, y_ref, m_ref, alpha_ref, acc_ref, s_ref, p_ref, mix_ref):
    t = DIFF_TILE
    nsub = q_ref.shape[0] // t
    LEAD = PIPE_LEAD
    i = pl.program_id(1)
    heads = [slice(h * DA_V, (h + 1) * DA_V) for h in range(DA_HEADS)]
    lane = lax.broadcasted_iota(jnp.int32, (t, DA_V), 1)
    units = []
    for sub in range(nsub):
        for hs in heads:
            qh = q_ref[sub * t:(sub + 1) * t, hs]
            zero = jnp.zeros_like(qh)
            units += [(sub, hs, jnp.where(lane < DA_QK, qh, zero)), (sub, hs, jnp.where(lane >= DA_QK, qh, zero))]
    n = len(units)
    per = n // nsub
    assert n == s_ref.shape[0] and LEAD + 1 < per
    m_ref[...] = jnp.full(m_ref.shape, -jnp.inf, F32)
    acc_ref[...] = jnp.zeros(acc_ref.shape, F32)

    def exp_stage(u):
        _stage_exp(s_ref.at[u], p_ref.at[u], m_ref.at[u])

    def pv_stage(j, u):
        _stage_pv(vt_ref[0, j, units[u][1], :], p_ref.at[u], alpha_ref.at[u], acc_ref.at[u])

    def block(j, mask_of, j_before, first_unit=0):
        rows = pl.ds(pl.multiple_of(j * t, t), t)
        for idx, u in enumerate(range(first_unit, n)):
            sub, hs, qq = units[u]
            _stage_scores(qq, k_ref[0, rows, hs], mask_of(sub), s_ref.at[u], m_ref.at[u], alpha_ref.at[u])
            if idx >= LEAD:
                exp_stage(u - LEAD)
            elif j_before is not None:
                exp_stage(n - LEAD + idx)
            if idx >= LEAD + 1:
                pv_stage(j, u - LEAD - 1)
            elif j_before is not None:
                pv_stage(j_before, n - LEAD - 1 + idx)

    kc = lax.broadcasted_iota(jnp.int32, (t, t), 0) // CHUNK
    qc = lax.broadcasted_iota(jnp.int32, (t, t), 1) // CHUNK
    diag_mask = kc <= qc
    diag0 = nsub * i
    for sd in range(nsub):
        block(diag0 + sd, lambda sub, sd=sd: diag_mask if sub == sd else None, None if sd == 0 else diag0 + sd - 1,
              first_unit=sd * per)

    def trip(jj, carry):
        j0 = nsub * jj
        block(j0, lambda sub: None, jnp.where(j0 == 0, diag0 + nsub - 1, j0 - 1))
        for d in range(1, nsub):
            block(j0 + d, lambda sub: None, j0 + d - 1)
        return carry

    lax.fori_loop(0, i, trip, 0)

    last = jnp.where(i == 0, diag0 + nsub - 1, diag0 - 1)
    for u in range(n - LEAD, n):
        exp_stage(u)
        pv_stage(last, u - 1)
    pv_stage(last, n - 1)

    lam = _lambda(lamv_ref[...], lam_init)
    for sub in range(nsub):
        rows = slice(sub * t, (sub + 1) * t)
        for h, hs in enumerate(heads):
            u = sub * per + 2 * h
            o1 = acc_ref[u, :DA_V, :] / acc_ref[u, DA_V:DA_V + 1, :]
            o2 = acc_ref[u + 1, :DA_V, :] / acc_ref[u + 1, DA_V:DA_V + 1, :]
            od = (o1 - lam * o2).T
            od = (od * lax.rsqrt(jnp.mean(od * od, axis=-1, keepdims=True) + LN_EPS) * dng_ref[...]
                  * (1.0 - lam_init))
            mix_ref[rows, hs] = (od * g_ref[rows, hs].astype(F32)).astype(BF16)
    mix_ref[:, DA_W:] = om_ref[...]
    y_ref[...] = _merge(x_ref[...], mix_ref[...], wout_ref[0], lng_ref[...], lnb_ref[...])


def _diff_prompt(lam_init, lamv, q, k, vt, g, om, x, w_out, layer, ln_g, ln_b, dng, *, B, S):
    t, rows = DIFF_TILE, DIFF_ROWS
    nq, units = S // rows, 2 * DA_HEADS * (rows // t)
    row = lambda w: pl.BlockSpec((rows, w), lambda b, i: (b * nq + i, 0))
    full = lambda shape: pl.BlockSpec(shape, lambda b, i: (0,) * len(shape))
    return pl.pallas_call(
        functools.partial(_diff_prompt_kernel, lam_init),
        grid=(B, nq),
        in_specs=[full((4, DA_QK)), row(DA_W), pl.BlockSpec((1, S, DA_W), lambda b, i: (b, 0, 0)),
                  pl.BlockSpec((1, S // t, DA_W, t), lambda b, i: (b, 0, 0, 0)), row(DA_W), row(MEM_W), row(D_MODEL),
                  _resident((1, MIX_W, D_MODEL), layer, 0, 0), full((1, D_MODEL)), full((1, D_MODEL)), full((1, DA_V))],
        out_specs=row(D_MODEL),
        out_shape=jax.ShapeDtypeStruct((B * S, D_MODEL), F32),
        scratch_shapes=[pltpu.VMEM((units, 1, t), F32), pltpu.VMEM((units, 1, t), F32),
                        pltpu.VMEM((units, DA_V + SUM_ROWS, t), F32),
                        pltpu.VMEM((units, t, t), F32), pltpu.VMEM((units, t, t), BF16),
                        pltpu.VMEM((rows, MIX_W), BF16)],
        compiler_params=_params("arbitrary", "arbitrary"),
        name="diff_prompt",
    )(lamv, q, k.reshape(B, S, DA_W), vt.reshape(B, S // t, DA_W, t), g, om, x, w_out, ln_g, ln_b, dng)


def _swa_prompt_kernel(tq, sinks_ref, q_ref, k_ref, kr_ref, vt_ref, vtr_ref, g_ref, om_ref, x_ref, wout_ref,
                       lng_ref, lnb_ref, y_ref, mix_ref):
    i = pl.program_id(1)
    w, nk, nsub = SWA_TILE, 2 * SWA_TILE, tq // SWA_TILE
    low = lax.broadcasted_iota(jnp.int32, (w, LANES), 1) < SW_HD
    row_low = lax.broadcasted_iota(jnp.int32, (LANES, w), 0) < SW_HD
    sink_row = lambda hs: jnp.concatenate([jnp.full((1, w), sinks_ref[h] * LOG2E, F32) for h in hs], axis=1)
    sinks_plain, sinks_swapped = sink_row((0, 2, 5, 7)), sink_row((1, 3, 4, 6))
    kc = lax.broadcasted_iota(jnp.int32, (nk, 4 * w), 0) // CHUNK
    qc = lax.broadcasted_iota(jnp.int32, (nk, 4 * w), 1) % w // CHUNK
    firsts, scores = [], []
    for st in range(nsub):
        sub = i * nsub + st
        first = jnp.maximum(sub - 1, 0)
        back = (sub - first) * (w // CHUNK)
        mask = (kc - back <= qc) & (kc - back >= qc - WINDOW // CHUNK)
        rows = pl.ds(pl.multiple_of(first * w, w), nk)
        q = q_ref[st * w:(st + 1) * w, :]
        slabs = [q[:, p * LANES:(p + 1) * LANES] for p in range(SW_HEADS // 2)]
        even = [jnp.where(low, sl, jnp.zeros_like(sl)) for sl in slabs]
        odd = [jnp.where(low, jnp.zeros_like(sl), sl) for sl in slabs]
        firsts.append(first)
        scores.append((_window_scores(jnp.concatenate([even[0], even[1], odd[2], odd[3]], axis=0), k_ref[0, rows, :],
                                      mask),
                       _window_scores(jnp.concatenate([odd[0], odd[1], even[2], even[3]], axis=0), kr_ref[0, rows, :],
                                      mask)))
    for st, (first, (s_plain, s_swapped)) in enumerate(zip(firsts, scores)):
        vts = jnp.concatenate([vt_ref[0, first], vt_ref[0, first + 1]], axis=1)
        vtrs = jnp.concatenate([vtr_ref[0, first], vtr_ref[0, first + 1]], axis=1)
        o_plain = _window_attend(s_plain, vts, sinks_plain)
        o_swapped = _window_attend(s_swapped, vtrs, sinks_swapped)
        for p in range(SW_HEADS // 2):
            cols = slice(p * w, (p + 1) * w)
            lo, hi = (o_plain, o_swapped) if p < 2 else (o_swapped, o_plain)
            slab = jnp.where(row_low, lo[:, cols], hi[:, cols]).T
            gate = g_ref[st * w:(st + 1) * w, p * LANES:(p + 1) * LANES].astype(F32)
            mix_ref[st * w:(st + 1) * w, p * LANES:(p + 1) * LANES] = (slab * gate).astype(BF16)
    mix_ref[:, SW_W:] = om_ref[...]
    y_ref[...] = _merge(x_ref[...], mix_ref[...], wout_ref[0], lng_ref[...], lnb_ref[...])


def _swa_prompt(sinks, q, k, kr, vt, vtr, g, om, x, w_out, layer, ln_g, ln_b, *, B, S, tq):
    nq = S // tq
    row = lambda w: pl.BlockSpec((tq, w), lambda b, i: (b * nq + i, 0))
    full = lambda shape: pl.BlockSpec(shape, lambda b, i: (0,) * len(shape))
    seq = pl.BlockSpec((1, S, LANES), lambda b, i: (b, 0, 0))
    seqt = pl.BlockSpec((1, S // SWA_TILE, LANES, SWA_TILE), lambda b, i: (b, 0, 0, 0))
    r3 = lambda t: t.reshape(B, S, LANES)
    r4 = lambda t: t.reshape(B, S // SWA_TILE, LANES, SWA_TILE)
    return pl.pallas_call(
        functools.partial(_swa_prompt_kernel, tq),
        grid=(B, nq),
        in_specs=[pl.BlockSpec(memory_space=pltpu.SMEM), row(SW_W), seq, seq, seqt, seqt, row(SW_W), row(MEM_W),
                  row(D_MODEL), _resident((1, MIX_W, D_MODEL), layer, 0, 0), full((1, D_MODEL)), full((1, D_MODEL))],
        out_specs=row(D_MODEL),
        out_shape=jax.ShapeDtypeStruct((B * S, D_MODEL), F32),
        scratch_shapes=[pltpu.VMEM((tq, MIX_W), BF16)],
        compiler_params=_params("arbitrary", "arbitrary"),
        name="swa_prompt",
    )(sinks, q, r3(k), r3(kr), r4(vt), r4(vtr), g, om, x, w_out, ln_g, ln_b)


def _sample_tail(b, last, o_first, g_ref, mq_ref, gm_ref, cmk_ref, cmv_ref, x_ref, wout_ref, lng_ref, lnb_ref,
                 y_ref, mix_ref, T):
    om = _mem_attend(mq_ref[...], _head_rows(cmk_ref, MEM_HEADS, 0, 0), _head_rows(cmv_ref, MEM_HEADS, 0, 0))
    om = om * gm_ref[...].astype(F32)
    mix = jnp.concatenate([(o_first * g_ref[...].astype(F32)).astype(BF16), om.astype(BF16)], axis=1)
    mix_ref[pl.ds(pl.multiple_of(b * T, T), T), :] = mix

    @pl.when(last)
    def _():
        y_ref[...] = _merge(x_ref[...], mix_ref[...], wout_ref[0], lng_ref[...], lnb_ref[...])


def _diff_sample_kernel(lam_init, T, lamv_ref, q_ref, kn_ref, vn_ref, ckt_ref, cv_ref, g_ref, mq_ref, gm_ref,
                        cmk_ref, cmv_ref, x_ref, wout_ref, lng_ref, lnb_ref, dng_ref, y_ref,
                        m_ref, l_ref, acc_ref, mix_ref):
    b, j = pl.program_id(0), pl.program_id(1)
    nb, nj = pl.num_programs(0), pl.num_programs(1)
    heads = [slice(h * DA_V, (h + 1) * DA_V) for h in range(DA_HEADS)]
    qqs = [_split_components(q_ref[:, hs]) for hs in heads]

    @pl.when(j == 0)
    def _():
        m_ref[...] = jnp.full(m_ref.shape, -jnp.inf, F32)
        l_ref[...] = jnp.zeros(l_ref.shape, F32)
        acc_ref[...] = jnp.zeros(acc_ref.shape, F32)
        _flash_update(qqs, lambda h: kn_ref[:, heads[h]], lambda h: vn_ref[:, heads[h]], False, m_ref, l_ref, acc_ref)

    _flash_update(qqs, lambda h: ckt_ref[0, heads[h], :].astype(BF16), _head_rows(cv_ref, DA_HEADS, 0), True,
                  m_ref, l_ref, acc_ref)

    @pl.when(j == nj - 1)
    def _():
        lam = _lambda(lamv_ref[...], lam_init)
        rows = [slice(h * 2 * T, (h + 1) * 2 * T) for h in range(DA_HEADS)]
        od = jnp.concatenate([_diff_finish(acc_ref[r, :], l_ref[r, :], lam, dng_ref[...], lam_init) for r in rows],
                             axis=1)
        _sample_tail(b, b == nb - 1, od, g_ref, mq_ref, gm_ref, cmk_ref, cmv_ref, x_ref, wout_ref, lng_ref,
                     lnb_ref, y_ref, mix_ref, T)


def _diff_sample(lam_init, lamv, q, kn, vn, ckt, cv, g, mq, gm, cmk, cmv, layer, x, w_out, ln_g, ln_b, dng, *,
                 NB, T, tk):
    P = ckt.shape[2]
    row = lambda w: pl.BlockSpec((T, w), lambda b, j: (b, 0))
    full = lambda shape: pl.BlockSpec(shape, lambda b, j: (0,) * len(shape))
    memc = pl.BlockSpec((1, 1, N_MEM * MEM_HEADS, MEM_HD), lambda b, j: (layer, b, 0, 0))
    return pl.pallas_call(
        functools.partial(_diff_sample_kernel, lam_init, T),
        grid=(NB, P // tk),
        in_specs=[full((4, DA_QK)), row(DA_W), row(DA_W), row(DA_W),
                  pl.BlockSpec((1, DA_W, tk), lambda b, j: (b, 0, j)),
                  pl.BlockSpec((1, tk * DA_HEADS, DA_V), lambda b, j: (b, j, 0)),
                  row(DA_W), row(MEM_W), row(MEM_W), memc, memc, full((NB * T, D_MODEL)),
                  _resident((1, MIX_W, D_MODEL), layer, 0, 0), full((1, D_MODEL)), full((1, D_MODEL)), full((1, DA_V))],
        out_specs=full((NB * T, D_MODEL)),
        out_shape=jax.ShapeDtypeStruct((NB * T, D_MODEL), F32),
        scratch_shapes=[pltpu.VMEM((DA_HEADS * 2 * T, 1), F32), pltpu.VMEM((DA_HEADS * 2 * T, 1), F32),
                        pltpu.VMEM((DA_HEADS * 2 * T, DA_V), F32), pltpu.VMEM((NB * T, MIX_W), BF16)],
        compiler_params=_params("arbitrary", "arbitrary"),
        name="diff_sample",
    )(lamv, q, kn, vn, ckt, cv, g, mq, gm, cmk, cmv, x, w_out, ln_g, ln_b, dng)


def _swa_sample_kernel(T, sinks_ref, q_ref, kn_ref, knr_ref, vn_ref, vnr_ref, ckt_ref, cvt_ref, g_ref, mq_ref, gm_ref,
                       cmk_ref, cmv_ref, x_ref, wout_ref, lng_ref, lnb_ref, y_ref, mix_ref):
    b = pl.program_id(0)
    swap = lambda t: jnp.concatenate([t[SW_HD:], t[:SW_HD]], axis=0).astype(BF16)
    ckt, cvt = ckt_ref[0], cvt_ref[0]
    kt, ktr, vt, vtr = ckt.astype(BF16), swap(ckt), cvt.astype(BF16), swap(cvt)
    kn, knr, vn, vnr = kn_ref[...], knr_ref[...], vn_ref[...], vnr_ref[...]
    attend = lambda qq, swapped, sink: _sink_attend_cached(
        qq, ktr if swapped else kt, knr if swapped else kn, vtr if swapped else vt, vnr if swapped else vn, sink)
    o = _swa_heads(q_ref[...], attend, lambda h: sinks_ref[h] * LOG2E)
    _sample_tail(b, b == pl.num_programs(0) - 1, o, g_ref, mq_ref, gm_ref, cmk_ref, cmv_ref, x_ref, wout_ref,
                 lng_ref, lnb_ref, y_ref, mix_ref, T)


def _swa_sample(sinks, q, kn, knr, vn, vnr, ckt, cvt, g, mq, gm, cmk, cmv, layer, x, w_out, ln_g, ln_b, *, NB, T):
    W = ckt.shape[2]
    row = lambda w: pl.BlockSpec((T, w), lambda b: (b, 0))
    full = lambda shape: pl.BlockSpec(shape, lambda b: (0,) * len(shape))
    cache = pl.BlockSpec((1, LANES, W), lambda b: (b, 0, 0))
    memc = pl.BlockSpec((1, 1, N_MEM * MEM_HEADS, MEM_HD), lambda b: (layer, b, 0, 0))
    return pl.pallas_call(
        functools.partial(_swa_sample_kernel, T),
        grid=(NB,),
        in_specs=[pl.BlockSpec(memory_space=pltpu.SMEM), row(SW_W), row(LANES), row(LANES), row(LANES), row(LANES),
                  cache, cache, row(SW_W), row(MEM_W), row(MEM_W), memc, memc, full((NB * T, D_MODEL)),
                  _resident((1, MIX_W, D_MODEL), layer, 0, 0), full((1, D_MODEL)), full((1, D_MODEL))],
        out_specs=full((NB * T, D_MODEL)),
        out_shape=jax.ShapeDtypeStruct((NB * T, D_MODEL), F32),
        scratch_shapes=[pltpu.VMEM((NB * T, MIX_W), BF16)],
        compiler_params=_params("arbitrary"),
        name="swa_sample",
    )(sinks, q, kn, knr, vn, vnr, ckt, cvt, g, mq, gm, cmk, cmv, x, w_out, ln_g, ln_b)


def _rope_tables(pos):
    T = pos.shape[0]
    inv = ROPE_THETA ** (-np.arange(ROPE_HALF, dtype=np.float64) / ROPE_HALF)
    ang = pos.astype(np.float64)[:, None] * inv[None, :]
    cos, sin = np.cos(ang), np.sin(ang)
    rest = SW_HD - 2 * ROPE_HALF
    c = np.concatenate([cos, cos, np.ones((T, rest))], axis=1)
    a = np.concatenate([-sin, np.zeros((T, SW_HD - ROPE_HALF))], axis=1)
    b = np.concatenate([np.zeros((T, ROPE_HALF)), sin, np.zeros((T, rest))], axis=1)
    return tuple(jnp.asarray(np.tile(t, (1, LANES // SW_HD)), F32) for t in (c, a, b))


def _feature_major(t):
    n = t.ndim
    return jnp.transpose(t, (0,) + tuple(range(2, n)) + (1,)).reshape(t.shape[0], -1, t.shape[1])


def _rows_major(t, tail):
    n = len(tail)
    return jnp.transpose(t.reshape((t.shape[0],) + tail + (t.shape[2],)), (0, n + 1) + tuple(range(1, n + 1)))


def kernel(x_prompt, x_sample, mem_prompt, cache_diff_k, cache_diff_v, cache_swa_k, cache_swa_v, cache_mem_k,
           cache_mem_v, w_in_a, lam_q1, lam_k1, lam_q2, lam_k2, diff_norm_g, w_in_b, sinks, w_kv_shared, w_mem_kv,
           w_out, ln_g, ln_b):
    B, S, _ = x_prompt.shape
    NB, T, _ = x_sample.shape
    P = cache_diff_k.shape[2]
    assert w_in_a.shape[0] == 1 and w_in_b.shape[0] == 1 and w_out.shape[0] == DEPTH

    tabs_p = _rope_tables(np.arange(S))
    tabs_s = _rope_tables(np.tile(P + np.arange(T), NB))
    xp = x_prompt.reshape(B * S, D_MODEL)
    xs = x_sample.reshape(NB * T, D_MODEL)
    wa, wb, wo = (w_in_a[0],), (w_in_b[0], w_kv_shared), w_out
    lng = ln_g.reshape(DEPTH, 1, D_MODEL)
    lnb = ln_b.reshape(DEPTH, 1, D_MODEL)
    lamv = jnp.concatenate([lam_q1, lam_k1, lam_q2, lam_k2], axis=0)
    dng = diff_norm_g.reshape(1, DA_V)
    lam_init = 0.8 - 0.6 * math.exp(-0.3 * 0)

    mkf, mvf, mkb, mvb = _mem_kv(mem_prompt.reshape(B * N_MEM, D_MODEL), w_mem_kv)
    mem_b = lambda l: (mkb[l].reshape(B, N_MEM, MEM_W), mvb[l].reshape(B, N_MEM, MEM_W))
    head_rows = lambda t: t.reshape(t.shape[:-3] + (t.shape[-3] * t.shape[-2], t.shape[-1]))
    cmk, cmv = head_rows(cache_mem_k), head_rows(cache_mem_v)

    q, kt, vf, kb, vb, g, om = _project("a", xp, wa, tabs_p, mem_b(0), tm=PROJ_ROWS, rows_per_batch=S)
    xp1 = _diff_prompt(lam_init, lamv, q, kb, vb, g, om, xp, wo, 0, lng[0], lnb[0], dng, B=B, S=S)
    qs, kfs, vfs, kbs, vbs, gs_, mqs, gms = _project("a", xs, wa, tabs_s, tm=NB * T)
    xs1 = _diff_sample(lam_init, lamv, qs, kbs, vbs, _feature_major(cache_diff_k[0]), head_rows(cache_diff_v[0]), gs_,
                       mqs, gms, cmk, cmv, 0, xs, wo, lng[0], lnb[0], dng, NB=NB, T=T, tk=SAMPLE_KEYS)

    q, g, om, skf, svf, kb1, kr1, vb1, vr1 = _project("b", xp1, wb, tabs_p, mem_b(1), tm=PROJ_ROWS, rows_per_batch=S)
    yp = _swa_prompt(sinks[0], q, kb1, kr1, vb1, vr1, g, om, xp1, wo, 1, lng[1], lnb[1], B=B, S=S, tq=SWA_ROWS)
    qs, gs_, mqs, gms, skfs, svfs, kbs1, krs1, vbs1, vrs1 = _project("b", xs1, wb, tabs_s, tm=NB * T)
    ys = _swa_sample(sinks[0], qs, kbs1, krs1, vbs1, vrs1, _feature_major(cache_swa_k), _feature_major(cache_swa_v),
                     gs_, mqs, gms, cmk, cmv, 1, xs1, wo, lng[1], lnb[1], NB=NB, T=T)

    wr_p = min(WINDOW, S)
    window = lambda t: t.reshape(B, S, LANES)[:, S - wr_p:, :].reshape(B, wr_p, SW_KV, SW_HD)
    swa_kp, swa_vp = window(skf), window(svf)
    roll = lambda c, n: jnp.concatenate([c, n.reshape(NB, T, SW_KV, SW_HD)], axis=1)[:, T:]
    return (yp.reshape(B, S, D_MODEL), ys.reshape(NB, T, D_MODEL),
            _rows_major(kt, (DA_HEADS, 2, DA_QK))[None], vf.reshape(1, B, S, DA_HEADS, DA_V),
            kfs.reshape(1, NB, T, DA_HEADS, 2, DA_QK), vfs.reshape(1, NB, T, DA_HEADS, DA_V),
            swa_kp, swa_vp, roll(cache_swa_k, skfs), roll(cache_swa_v, svfs),
            mkf.reshape(DEPTH, B, N_MEM, MEM_HEADS, MEM_HD), mvf.reshape(DEPTH, B, N_MEM, MEM_HEADS, MEM_HD))
```

```python
import functools
import math

import jax
import jax.numpy as jnp
import numpy as np
from jax import lax
from jax.experimental import pallas as pl
from jax.experimental.pallas import tpu as pltpu

D_MODEL = 1024
CHUNK = 64
N_MEM = 256
DA_HEADS = 4
DA_QK = 64
DA_V = 128
DA_W = 512
SW_HEADS = 8
SW_KV = 2
SW_HD = 64
SW_W = 512
WINDOW = 128
MEM_HEADS = 4
MEM_HD = 128
MEM_W = 512
MIX_W = 1024
ROPE_THETA = 500000.0
ROPE_HALF = 8
DEPTH = 2
DN_ALPHA = (2 * DEPTH) ** 0.25
LN_EPS = 1e-5
NEG = -1e30
LANES = 128
DIFF_TILE = 256
PIPE_LEAD = 4
DIFF_ROWS = 512
SWA_TILE = 128
SUM_ROWS = 16
PROJ_ROWS = 1024
PROJ_B_ROWS = 512
SWA_ROWS = 1024
SAMPLE_KEYS = 2048
LOG2E = math.log2(math.e)

F32 = jnp.float32
BF16 = jnp.bfloat16
VMEM_LIMIT = 48 * 1024 * 1024

_NT = (((1,), (1,)), ((), ()))


def _resident(shape, *first):
    idx = first or (0,) * len(shape)
    return pl.BlockSpec(shape, lambda *_: idx, pipeline_mode=pl.Buffered(1))


def _params(*sem):
    return pltpu.CompilerParams(dimension_semantics=sem, vmem_limit_bytes=VMEM_LIMIT)


def _rope(x, c, a, b):
    outs = []
    for i in range(x.shape[1] // LANES):
        blk = x[:, i * LANES:(i + 1) * LANES]
        outs.append(blk * c + pltpu.roll(blk, LANES - ROPE_HALF, 1) * a + pltpu.roll(blk, ROPE_HALF, 1) * b)
    return outs[0] if len(outs) == 1 else jnp.concatenate(outs, axis=1)


def _silu(g):
    return g * (1.0 / (1.0 + jnp.exp(-g)))


def _layer_norm(z, g, b):
    mu = jnp.mean(z, axis=-1, keepdims=True)
    d = z - mu
    var = jnp.mean(d * d, axis=-1, keepdims=True)
    return d * lax.rsqrt(var + LN_EPS) * g + b


def _mem_attend(mq, mk_of, mv_of):
    outs = []
    for h in range(MEM_HEADS):
        sl = slice(h * MEM_HD, (h + 1) * MEM_HD)
        s = lax.dot_general(mq[:, sl], mk_of(h), _NT, preferred_element_type=F32) * (MEM_HD ** -0.5)
        p = jnp.exp(s - jnp.max(s, axis=1, keepdims=True))
        l = jnp.sum(p, axis=1, keepdims=True)
        outs.append(jnp.dot(p.astype(BF16), mv_of(h), preferred_element_type=F32) / l)
    return jnp.concatenate(outs, axis=1)


def _head_cols(ref):
    return lambda h: ref[0, :, h * MEM_HD:(h + 1) * MEM_HD]


def _head_idx(ref, heads, h, *lead):
    return lead + (pl.ds(h, ref.shape[-2] // heads, stride=heads), slice(None))


def _head_rows(ref, heads, *lead):
    return lambda h: ref[_head_idx(ref, heads, h, *lead)].astype(BF16)


def _merge(x, mix, w_out, ln_g, ln_b):
    return _layer_norm(DN_ALPHA * x + jnp.dot(mix, w_out.astype(BF16), preferred_element_type=F32), ln_g, ln_b)


def _lambda(lamv, lam_init):
    e1 = jnp.exp(jnp.sum(lamv[0:1, :] * lamv[1:2, :], axis=1, keepdims=True))
    e2 = jnp.exp(jnp.sum(lamv[2:3, :] * lamv[3:4, :], axis=1, keepdims=True))
    return e1 - e2 + lam_init


def _split_components(qh):
    lane = lax.broadcasted_iota(jnp.int32, qh.shape, 1)
    zero = jnp.zeros_like(qh)
    return jnp.concatenate([jnp.where(lane < DA_QK, qh, zero), jnp.where(lane >= DA_QK, qh, zero)], axis=0)


def _flash_update(qqs, k_of, v_of, keys_on_lanes, m_ref, l_ref, acc_ref):
    if keys_on_lanes:
        s = [jnp.dot(qq, k_of(h), preferred_element_type=F32) for h, qq in enumerate(qqs)]
    else:
        s = [lax.dot_general(qq, k_of(h), _NT, preferred_element_type=F32) for h, qq in enumerate(qqs)]
    s = jnp.concatenate(s, axis=0)
    m_old = m_ref[...]
    m_new = jnp.maximum(m_old, jnp.max(s, axis=1, keepdims=True))
    alpha = jnp.exp2(m_old - m_new)
    p = jnp.exp2(s - m_new)
    l_ref[...] = alpha * l_ref[...] + jnp.sum(p, axis=1, keepdims=True)
    pb = p.astype(BF16)
    r = qqs[0].shape[0]
    pv = [jnp.dot(pb[h * r:(h + 1) * r], v_of(h), preferred_element_type=F32) for h in range(len(qqs))]
    acc_ref[...] = alpha * acc_ref[...] + jnp.concatenate(pv, axis=0)
    m_ref[...] = m_new


def _stage_scores(qq, ks, mask, s_ref, m_ref, alpha_ref):
    s = lax.dot_general(ks, qq, _NT, preferred_element_type=F32)
    if mask is not None:
        s = jnp.where(mask, s, NEG)
    s_ref[...] = s
    m_old = m_ref[...]
    m_new = jnp.maximum(m_old, jnp.max(s, axis=0, keepdims=True))
    alpha_ref[...] = jnp.exp2(m_old - m_new)
    m_ref[...] = m_new


def _stage_exp(s_ref, p_ref, m_ref):
    p_ref[...] = jnp.exp2(s_ref[...] - m_ref[...]).astype(BF16)


def _stage_pv(vts, p_ref, alpha_ref, acc_ref):
    ones = jnp.ones((SUM_ROWS, vts.shape[1]), BF16)
    pv = jnp.dot(jnp.concatenate([vts, ones], axis=0), p_ref[...], preferred_element_type=F32)
    acc_ref[...] = alpha_ref[...] * acc_ref[...] + pv


def _diff_finish(acc, l, lam, dng, lam_init):
    rows = acc.shape[0] // 2
    o = acc / l
    od = o[:rows] - lam * o[rows:]
    return od * lax.rsqrt(jnp.mean(od * od, axis=-1, keepdims=True) + LN_EPS) * dng * (1.0 - lam_init)


def _sink_attend_cached(qq, kt, kn, vt, vn, sink):
    s_c = jnp.dot(qq, kt, preferred_element_type=F32)
    s_n = lax.dot_general(qq, kn, _NT, preferred_element_type=F32)
    m = jnp.maximum(jnp.maximum(jnp.max(s_c, axis=1, keepdims=True), jnp.max(s_n, axis=1, keepdims=True)), sink)
    p_c, p_n = jnp.exp2(s_c - m), jnp.exp2(s_n - m)
    l = jnp.sum(p_c, axis=1, keepdims=True) + jnp.sum(p_n, axis=1, keepdims=True) + jnp.exp2(sink - m)
    o = (lax.dot_general(p_c.astype(BF16), vt, _NT, preferred_element_type=F32)
         + jnp.dot(p_n.astype(BF16), vn, preferred_element_type=F32))
    return o / l


def _window_scores(qq4, ks, mask):
    return jnp.where(mask, lax.dot_general(ks, qq4, _NT, preferred_element_type=F32), NEG)


def _window_attend(s, vts, sinks4):
    m = jnp.maximum(jnp.max(s, axis=0, keepdims=True), sinks4)
    p = jnp.exp2(s - m).astype(BF16)
    ones = jnp.ones((SUM_ROWS, vts.shape[1]), BF16)
    o = jnp.dot(jnp.concatenate([vts, ones], axis=0), p, preferred_element_type=F32)
    return o[:LANES] / (o[LANES:LANES + 1] + jnp.exp2(sinks4 - m))


def _swa_heads(q, attend, sink_of):
    low = lax.broadcasted_iota(jnp.int32, (q.shape[0], LANES), 1) < SW_HD
    outs = []
    for pair in range(SW_HEADS // 2):
        kv = pair // 2
        slab = q[:, pair * LANES:(pair + 1) * LANES]
        zero = jnp.zeros_like(slab)
        o_even = attend(jnp.where(low, slab, zero), kv == 1, sink_of(2 * pair))
        o_odd = attend(jnp.where(low, zero, slab), kv == 0, sink_of(2 * pair + 1))
        outs.append(jnp.where(low, o_even, o_odd))
    return jnp.concatenate(outs, axis=1)


def _mem_kv_kernel(mem_ref, w_ref, kf_ref, vf_ref, kb_ref, vb_ref):
    h = jnp.dot(mem_ref[...].astype(BF16), w_ref[0].astype(BF16), preferred_element_type=F32)
    k, v = h[:, :MEM_W], h[:, MEM_W:]
    for hd in range(MEM_HEADS):
        kf_ref[_head_idx(kf_ref, MEM_HEADS, hd, 0)] = k[:, hd * MEM_HD:(hd + 1) * MEM_HD]
        vf_ref[_head_idx(vf_ref, MEM_HEADS, hd, 0)] = v[:, hd * MEM_HD:(hd + 1) * MEM_HD]
    kb_ref[0] = k.astype(BF16)
    vb_ref[0] = v.astype(BF16)


def _mem_kv(mem, w):
    rows = mem.shape[0]
    fshape, fspec = (DEPTH, rows * MEM_HEADS, MEM_HD), pl.BlockSpec((1, rows * MEM_HEADS, MEM_HD), lambda l: (l, 0, 0))
    bshape, bspec = (DEPTH, rows, MEM_W), pl.BlockSpec((1, rows, MEM_W), lambda l: (l, 0, 0))
    return pl.pallas_call(
        _mem_kv_kernel,
        grid=(DEPTH,),
        in_specs=[pl.BlockSpec((rows, D_MODEL), lambda l: (0, 0)),
                  pl.BlockSpec((1, D_MODEL, 2 * MEM_W), lambda l: (l, 0, 0))],
        out_specs=[fspec, fspec, bspec, bspec],
        out_shape=[jax.ShapeDtypeStruct(fshape, F32), jax.ShapeDtypeStruct(fshape, F32),
                   jax.ShapeDtypeStruct(bshape, BF16), jax.ShapeDtypeStruct(bshape, BF16)],
        compiler_params=_params("arbitrary"),
        name="mem_kv",
    )(mem, w)


def _proj_a_kernel(fuse_mem, x_ref, w_ref, c_ref, a_ref, b_ref, *refs):
    if fuse_mem:
        mk_ref, mv_ref, q_ref, kf_ref, vf_ref, kb_ref, vb_ref, g_ref, om_ref = refs
    else:
        q_ref, kf_ref, vf_ref, kb_ref, vb_ref, g_ref, mq_ref, gm_ref = refs
    x = x_ref[...].astype(BF16)
    c, a, b = c_ref[...], a_ref[...], b_ref[...]

    def cols(i):
        return jnp.dot(x, w_ref[:, i * DA_W:(i + 1) * DA_W].astype(BF16), preferred_element_type=F32)

    q_ref[...] = (_rope(cols(0), c, a, b) * (DA_QK ** -0.5 * LOG2E)).astype(BF16)
    k = _rope(cols(1), c, a, b)
    kb_ref[...] = k.astype(BF16)
    v = cols(2)
    for hd in range(DA_HEADS):
        vf_ref[_head_idx(vf_ref, DA_HEADS, hd)] = v[:, hd * DA_V:(hd + 1) * DA_V]
    if fuse_mem:
        kf_ref[0] = k.T
        for blk in range(vb_ref.shape[0]):
            vb_ref[blk] = v[blk * DIFF_TILE:(blk + 1) * DIFF_TILE, :].T.astype(BF16)
    else:
        kf_ref[...] = k
        vb_ref[...] = v.astype(BF16)
    g_ref[...] = _silu(cols(3)).astype(BF16)
    mq = cols(4).astype(BF16)
    gm = _silu(cols(5))
    if fuse_mem:
        om_ref[...] = (_mem_attend(mq, _head_cols(mk_ref), _head_cols(mv_ref)) * gm).astype(BF16)
    else:
        mq_ref[...] = mq
        gm_ref[...] = gm.astype(BF16)


def _proj_b_kernel(fuse_mem, x_ref, w_ref, wkv_ref, c_ref, a_ref, b_ref, *refs):
    if fuse_mem:
        mk_ref, mv_ref, q_ref, g_ref, om_ref, kf_ref, vf_ref, kb_ref, kr_ref, vb_ref, vr_ref = refs
    else:
        q_ref, g_ref, mq_ref, gm_ref, kf_ref, vf_ref, kb_ref, kr_ref, vb_ref, vr_ref = refs
    x = x_ref[...].astype(BF16)
    c, a, b = c_ref[...], a_ref[...], b_ref[...]

    def cols(i):
        return jnp.dot(x, w_ref[:, i * SW_W:(i + 1) * SW_W].astype(BF16), preferred_element_type=F32)

    q_ref[...] = (_rope(cols(0), c, a, b) * (SW_HD ** -0.5 * LOG2E)).astype(BF16)
    g_ref[...] = _silu(cols(1)).astype(BF16)
    mq = cols(2).astype(BF16)
    gm = _silu(cols(3))
    if fuse_mem:
        om_ref[...] = (_mem_attend(mq, _head_cols(mk_ref), _head_cols(mv_ref)) * gm).astype(BF16)
    else:
        mq_ref[...] = mq
        gm_ref[...] = gm.astype(BF16)
    kv = jnp.dot(x, wkv_ref[...].astype(BF16), preferred_element_type=F32)
    k = _rope(kv[:, :LANES], c, a, b)
    v = kv[:, LANES:]
    kf_ref[...] = k
    vf_ref[...] = v
    kb_ref[...] = k.astype(BF16)
    kr_ref[...] = pltpu.roll(k, SW_HD, 1).astype(BF16)
    vr = pltpu.roll(v, SW_HD, 1)
    if fuse_mem:
        for blk in range(vb_ref.shape[0]):
            rows = slice(blk * SWA_TILE, (blk + 1) * SWA_TILE)
            vb_ref[blk] = v[rows, :].T.astype(BF16)
            vr_ref[blk] = vr[rows, :].T.astype(BF16)
    else:
        vb_ref[...] = v.astype(BF16)
        vr_ref[...] = vr.astype(BF16)


def _with_rider(host_body, n_in, n_out, rider, *refs):
    r_in, r_out = len(rider["args"]), len(rider["out_specs"])
    o0 = n_in + r_in
    host_body(*refs[:n_in], *refs[o0:o0 + n_out])
    rider["body"](*refs[n_in:o0], *refs[o0 + n_out:])


def _project(layer, x, ws, tabs, mem=None, *, tm, rows_per_batch=None, rider=None):
    M = x.shape[0]
    fuse = mem is not None
    per = (rows_per_batch or M) // tm
    row = lambda w: pl.BlockSpec((tm, w), lambda i: (i, 0))
    tab = pl.BlockSpec((tm, LANES), lambda i: (i % per, 0))
    in_specs = [row(D_MODEL)] + [_resident(w.shape) for w in ws] + [tab, tab, tab]
    args = [x, *ws, *tabs]
    if fuse:
        mspec = pl.BlockSpec((1, N_MEM, MEM_W), lambda i: (i // per, 0, 0))
        in_specs += [mspec, mspec]
        args += list(mem)
    o = lambda w, dt: (row(w), jax.ShapeDtypeStruct((M, w), dt))
    if layer == "a":
        vt = (pl.BlockSpec((tm // DIFF_TILE, DA_W, DIFF_TILE), lambda i: (i, 0, 0)),
              jax.ShapeDtypeStruct((M // DIFF_TILE, DA_W, DIFF_TILE), BF16))
        kt = (pl.BlockSpec((1, DA_W, tm), lambda i: (i // per, 0, i % per)),
              jax.ShapeDtypeStruct((M // (per * tm), DA_W, per * tm), F32))
        vf = (pl.BlockSpec((tm * DA_HEADS, DA_V), lambda i: (i, 0)), jax.ShapeDtypeStruct((M * DA_HEADS, DA_V), F32))
        outs = [o(DA_W, BF16), kt if fuse else o(DA_W, F32), vf, o(DA_W, BF16), vt if fuse else o(DA_W, BF16),
                o(DA_W, BF16)]
        body = _proj_a_kernel
    else:
        outs = [o(SW_W, BF16), o(SW_W, BF16)]
        body = _proj_b_kernel
    mem_outs = [o(MEM_W, BF16)] if fuse else [o(MEM_W, BF16), o(MEM_W, BF16)]
    if layer == "a":
        outs = outs + mem_outs
    else:
        vt = (pl.BlockSpec((tm // SWA_TILE, LANES, SWA_TILE), lambda i: (i, 0, 0)),
              jax.ShapeDtypeStruct((M // SWA_TILE, LANES, SWA_TILE), BF16))
        values = [vt, vt] if fuse else [o(LANES, BF16)] * 2
        outs = outs + mem_outs + [o(LANES, F32), o(LANES, F32)] + [o(LANES, BF16)] * 2 + values
    kernel_body, out_specs, out_shape, scratch = functools.partial(body, fuse), [s for s, _ in outs], [t for _, t in outs], []
    if rider is not None:
        assert rider["steps"] == M // tm
        kernel_body = functools.partial(_with_rider, kernel_body, len(args), len(outs), rider)
        in_specs, args = in_specs + rider["in_specs"], args + rider["args"]
        out_specs, out_shape, scratch = out_specs + rider["out_specs"], out_shape + rider["out_shape"], rider["scratch"]
    return pl.pallas_call(
        kernel_body,
        grid=(M // tm,),
        in_specs=in_specs,
        out_specs=out_specs,
        out_shape=out_shape,
        scratch_shapes=scratch,
        compiler_params=_params("arbitrary"),
        name="proj_" + layer,
    )(*args)


def _diff_prompt_kernel(lam_init, lamv_ref, q_ref, k_ref, vt_ref, g_ref, om_ref, x_ref, wout_ref,
                        lng_ref, lnb_ref, dng_ref, y_ref, m_ref, alpha_ref, acc_ref, s_ref, p_ref, mix_ref):
    t = DIFF_TILE
    nsub = q_ref.shape[0] // t
    LEAD = PIPE_LEAD
    i = pl.program_id(1)
    heads = [slice(h * DA_V, (h + 1) * DA_V) for h in range(DA_HEADS)]
    lane = lax.broadcasted_iota(jnp.int32, (t, DA_V), 1)
    units = []
    for sub in range(nsub):
        for hs in heads:
            qh = q_ref[sub * t:(sub + 1) * t, hs]
            zero = jnp.zeros_like(qh)
            units += [(sub, hs, jnp.where(lane < DA_QK, qh, zero)), (sub, hs, jnp.where(lane >= DA_QK, qh, zero))]
    n = len(units)
    per = n // nsub
    assert n == s_ref.shape[0] and LEAD + 1 < per
    m_ref[...] = jnp.full(m_ref.shape, -jnp.inf, F32)
    acc_ref[...] = jnp.zeros(acc_ref.shape, F32)

    def exp_stage(u):
        _stage_exp(s_ref.at[u], p_ref.at[u], m_ref.at[u])

    def pv_stage(j, u):
        _stage_pv(vt_ref[0, j, units[u][1], :], p_ref.at[u], alpha_ref.at[u], acc_ref.at[u])

    def block(j, mask_of, j_before, first_unit=0):
        rows = pl.ds(pl.multiple_of(j * t, t), t)
        for idx, u in enumerate(range(first_unit, n)):
            sub, hs, qq = units[u]
            _stage_scores(qq, k_ref[0, rows, hs], mask_of(sub), s_ref.at[u], m_ref.at[u], alpha_ref.at[u])
            if idx >= LEAD:
                exp_stage(u - LEAD)
            elif j_before is not None:
                exp_stage(n - LEAD + idx)
            if idx >= LEAD + 1:
                pv_stage(j, u - LEAD - 1)
            elif j_before is not None:
                pv_stage(j_before, n - LEAD - 1 + idx)

    kc = lax.broadcasted_iota(jnp.int32, (t, t), 0) // CHUNK
    qc = lax.broadcasted_iota(jnp.int32, (t, t), 1) // CHUNK
    diag_mask = kc <= qc
    diag0 = nsub * i
    for sd in range(nsub):
        block(diag0 + sd, lambda sub, sd=sd: diag_mask if sub == sd else None, None if sd == 0 else diag0 + sd - 1,
              first_unit=sd * per)

    def trip(jj, carry):
        j0 = nsub * jj
        block(j0, lambda sub: None, jnp.where(j0 == 0, diag0 + nsub - 1, j0 - 1))
        for d in range(1, nsub):
            block(j0 + d, lambda sub: None, j0 + d - 1)
        return carry

    lax.fori_loop(0, i, trip, 0)

    last = jnp.where(i == 0, diag0 + nsub - 1, diag0 - 1)
    for u in range(n - LEAD, n):
        exp_stage(u)
        pv_stage(last, u - 1)
    pv_stage(last, n - 1)

    lam = _lambda(lamv_ref[...], lam_init)
    for sub in range(nsub):
        rows = slice(sub * t, (sub + 1) * t)
        for h, hs in enumerate(heads):
            u = sub * per + 2 * h
            o1 = acc_ref[u, :DA_V, :] / acc_ref[u, DA_V:DA_V + 1, :]
            o2 = acc_ref[u + 1, :DA_V, :] / acc_ref[u + 1, DA_V:DA_V + 1, :]
            od = (o1 - lam * o2).T
            od = (od * lax.rsqrt(jnp.mean(od * od, axis=-1, keepdims=True) + LN_EPS) * dng_ref[...]
                  * (1.0 - lam_init))
            mix_ref[rows, hs] = (od * g_ref[rows, hs].astype(F32)).astype(BF16)
    mix_ref[:, DA_W:] = om_ref[...]
    y_ref[...] = _merge(x_ref[...], mix_ref[...], wout_ref[0], lng_ref[...], lnb_ref[...])


def _diff_prompt(lam_init, lamv, q, k, vt, g, om, x, w_out, layer, ln_g, ln_b, dng, *, B, S):
    t, rows = DIFF_TILE, DIFF_ROWS
    nq, units = S // rows, 2 * DA_HEADS * (rows // t)
    row = lambda w: pl.BlockSpec((rows, w), lambda b, i: (b * nq + i, 0))
    full = lambda shape: pl.BlockSpec(shape, lambda b, i: (0,) * len(shape))
    return pl.pallas_call(
        functools.partial(_diff_prompt_kernel, lam_init),
        grid=(B, nq),
        in_specs=[full((4, DA_QK)), row(DA_W), pl.BlockSpec((1, S, DA_W), lambda b, i: (b, 0, 0)),
                  pl.BlockSpec((1, S // t, DA_W, t), lambda b, i: (b, 0, 0, 0)), row(DA_W), row(MEM_W), row(D_MODEL),
                  _resident((1, MIX_W, D_MODEL), layer, 0, 0), full((1, D_MODEL)), full((1, D_MODEL)), full((1, DA_V))],
        out_specs=row(D_MODEL),
        out_shape=jax.ShapeDtypeStruct((B * S, D_MODEL), F32),
        scratch_shapes=[pltpu.VMEM((units, 1, t), F32), pltpu.VMEM((units, 1, t), F32),
                        pltpu.VMEM((units, DA_V + SUM_ROWS, t), F32),
                        pltpu.VMEM((units, t, t), F32), pltpu.VMEM((units, t, t), BF16),
                        pltpu.VMEM((rows, MIX_W), BF16)],
        compiler_params=_params("arbitrary", "arbitrary"),
        name="diff_prompt",
    )(lamv, q, k.reshape(B, S, DA_W), vt.reshape(B, S // t, DA_W, t), g, om, x, w_out, ln_g, ln_b, dng)


def _swa_prompt_kernel(tq, sinks_ref, q_ref, k_ref, kr_ref, vt_ref, vtr_ref, g_ref, om_ref, x_ref, wout_ref,
                       lng_ref, lnb_ref, y_ref, mix_ref):
    i = pl.program_id(1)
    w, nk, nsub = SWA_TILE, 2 * SWA_TILE, tq // SWA_TILE
    low = lax.broadcasted_iota(jnp.int32, (w, LANES), 1) < SW_HD
    row_low = lax.broadcasted_iota(jnp.int32, (LANES, w), 0) < SW_HD
    sink_row = lambda hs: jnp.concatenate([jnp.full((1, w), sinks_ref[h] * LOG2E, F32) for h in hs], axis=1)
    sinks_plain, sinks_swapped = sink_row((0, 2, 5, 7)), sink_row((1, 3, 4, 6))
    kc = lax.broadcasted_iota(jnp.int32, (nk, 4 * w), 0) // CHUNK
    qc = lax.broadcasted_iota(jnp.int32, (nk, 4 * w), 1) % w // CHUNK
    firsts, scores = [], []
    for st in range(nsub):
        sub = i * nsub + st
        first = jnp.maximum(sub - 1, 0)
        back = (sub - first) * (w // CHUNK)
        mask = (kc - back <= qc) & (kc - back >= qc - WINDOW // CHUNK)
        rows = pl.ds(pl.multiple_of(first * w, w), nk)
        q = q_ref[st * w:(st + 1) * w, :]
        slabs = [q[:, p * LANES:(p + 1) * LANES] for p in range(SW_HEADS // 2)]
        even = [jnp.where(low, sl, jnp.zeros_like(sl)) for sl in slabs]
        odd = [jnp.where(low, jnp.zeros_like(sl), sl) for sl in slabs]
        firsts.append(first)
        scores.append((_window_scores(jnp.concatenate([even[0], even[1], odd[2], odd[3]], axis=0), k_ref[0, rows, :],
                                      mask),
                       _window_scores(jnp.concatenate([odd[0], odd[1], even[2], even[3]], axis=0), kr_ref[0, rows, :],
                                      mask)))
    for st, (first, (s_plain, s_swapped)) in enumerate(zip(firsts, scores)):
        vts = jnp.concatenate([vt_ref[0, first], vt_ref[0, first + 1]], axis=1)
        vtrs = jnp.concatenate([vtr_ref[0, first], vtr_ref[0, first + 1]], axis=1)
        o_plain = _window_attend(s_plain, vts, sinks_plain)
        o_swapped = _window_attend(s_swapped, vtrs, sinks_swapped)
        for p in range(SW_HEADS // 2):
            cols = slice(p * w, (p + 1) * w)
            lo, hi = (o_plain, o_swapped) if p < 2 else (o_swapped, o_plain)
            slab = jnp.where(row_low, lo[:, cols], hi[:, cols]).T
            gate = g_ref[st * w:(st + 1) * w, p * LANES:(p + 1) * LANES].astype(F32)
            mix_ref[st * w:(st + 1) * w, p * LANES:(p + 1) * LANES] = (slab * gate).astype(BF16)
    mix_ref[:, SW_W:] = om_ref[...]
    y_ref[...] = _merge(x_ref[...], mix_ref[...], wout_ref[0], lng_ref[...], lnb_ref[...])


def _swa_prompt(sinks, q, k, kr, vt, vtr, g, om, x, w_out, layer, ln_g, ln_b, *, B, S, tq):
    nq = S // tq
    row = lambda w: pl.BlockSpec((tq, w), lambda b, i: (b * nq + i, 0))
    full = lambda shape: pl.BlockSpec(shape, lambda b, i: (0,) * len(shape))
    seq = pl.BlockSpec((1, S, LANES), lambda b, i: (b, 0, 0))
    seqt = pl.BlockSpec((1, S // SWA_TILE, LANES, SWA_TILE), lambda b, i: (b, 0, 0, 0))
    r3 = lambda t: t.reshape(B, S, LANES)
    r4 = lambda t: t.reshape(B, S // SWA_TILE, LANES, SWA_TILE)
    return pl.pallas_call(
        functools.partial(_swa_prompt_kernel, tq),
        grid=(B, nq),
        in_specs=[pl.BlockSpec(memory_space=pltpu.SMEM), row(SW_W), seq, seq, seqt, seqt, row(SW_W), row(MEM_W),
                  row(D_MODEL), _resident((1, MIX_W, D_MODEL), layer, 0, 0), full((1, D_MODEL)), full((1, D_MODEL))],
        out_specs=row(D_MODEL),
        out_shape=jax.ShapeDtypeStruct((B * S, D_MODEL), F32),
        scratch_shapes=[pltpu.VMEM((tq, MIX_W), BF16)],
        compiler_params=_params("arbitrary", "arbitrary"),
        name="swa_prompt",
    )(sinks, q, r3(k), r3(kr), r4(vt), r4(vtr), g, om, x, w_out, ln_g, ln_b)


def _sample_tail(b, last, o_first, g_ref, mq_ref, gm_ref, cmk_ref, cmv_ref, x_ref, wout_ref, lng_ref, lnb_ref,
                 y_ref, mix_ref, T):
    om = _mem_attend(mq_ref[...], _head_rows(cmk_ref, MEM_HEADS, 0, 0), _head_rows(cmv_ref, MEM_HEADS, 0, 0))
    om = om * gm_ref[...].astype(F32)
    mix = jnp.concatenate([(o_first * g_ref[...].astype(F32)).astype(BF16), om.astype(BF16)], axis=1)
    mix_ref[pl.ds(pl.multiple_of(b * T, T), T), :] = mix

    @pl.when(last)
    def _():
        y_ref[...] = _merge(x_ref[...], mix_ref[...], wout_ref[0], lng_ref[...], lnb_ref[...])


def _diff_sample_body(lam_init, T, nb, nj, lamv_ref, q_ref, kn_ref, vn_ref, ckt_ref, cv_ref, g_ref, mq_ref, gm_ref,
                      cmk_ref, cmv_ref, x_ref, wout_ref, lng_ref, lnb_ref, dng_ref, y_ref,
                      m_ref, l_ref, acc_ref, mix_ref):
    b, j = pl.program_id(0) // nj, pl.program_id(0) % nj
    heads = [slice(h * DA_V, (h + 1) * DA_V) for h in range(DA_HEADS)]
    qqs = [_split_components(q_ref[:, hs]) for hs in heads]

    @pl.when(j == 0)
    def _():
        m_ref[...] = jnp.full(m_ref.shape, -jnp.inf, F32)
        l_ref[...] = jnp.zeros(l_ref.shape, F32)
        acc_ref[...] = jnp.zeros(acc_ref.shape, F32)
        _flash_update(qqs, lambda h: kn_ref[:, heads[h]], lambda h: vn_ref[:, heads[h]], False, m_ref, l_ref, acc_ref)

    _flash_update(qqs, lambda h: ckt_ref[0, heads[h], :].astype(BF16), _head_rows(cv_ref, DA_HEADS, 0), True,
                  m_ref, l_ref, acc_ref)

    @pl.when(j == nj - 1)
    def _():
        lam = _lambda(lamv_ref[...], lam_init)
        rows = [slice(h * 2 * T, (h + 1) * 2 * T) for h in range(DA_HEADS)]
        od = jnp.concatenate([_diff_finish(acc_ref[r, :], l_ref[r, :], lam, dng_ref[...], lam_init) for r in rows],
                             axis=1)
        _sample_tail(b, b == nb - 1, od, g_ref, mq_ref, gm_ref, cmk_ref, cmv_ref, x_ref, wout_ref, lng_ref,
                     lnb_ref, y_ref, mix_ref, T)


def _diff_sample_rider(lam_init, lamv, q, kn, vn, ckt, cv, g, mq, gm, cmk, cmv, layer, x, w_out, ln_g, ln_b, dng, *,
                       NB, T, tk):
    nj = ckt.shape[2] // tk
    row = lambda w: pl.BlockSpec((T, w), lambda i: (i // nj, 0))
    small = lambda shape: _resident(shape)
    memc = pl.BlockSpec((1, 1, N_MEM * MEM_HEADS, MEM_HD), lambda i: (layer, i // nj, 0, 0))
    return dict(
        steps=NB * nj,
        in_specs=[small((4, DA_QK)), row(DA_W), row(DA_W), row(DA_W),
                  pl.BlockSpec((1, DA_W, tk), lambda i: (i // nj, 0, i % nj)),
                  pl.BlockSpec((1, tk * DA_HEADS, DA_V), lambda i: (i // nj, i % nj, 0)),
                  row(DA_W), row(MEM_W), row(MEM_W), memc, memc, small((NB * T, D_MODEL)),
                  _resident((1, MIX_W, D_MODEL), layer, 0, 0), small((1, D_MODEL)), small((1, D_MODEL)),
                  small((1, DA_V))],
        args=[lamv, q, kn, vn, ckt, cv, g, mq, gm, cmk, cmv, x, w_out, ln_g, ln_b, dng],
        out_specs=[pl.BlockSpec((NB * T, D_MODEL), lambda i: (0, 0))],
        out_shape=[jax.ShapeDtypeStruct((NB * T, D_MODEL), F32)],
        scratch=[pltpu.VMEM((DA_HEADS * 2 * T, 1), F32), pltpu.VMEM((DA_HEADS * 2 * T, 1), F32),
                 pltpu.VMEM((DA_HEADS * 2 * T, DA_V), F32), pltpu.VMEM((NB * T, MIX_W), BF16)],
        body=functools.partial(_diff_sample_body, lam_init, T, NB, nj),
    )


def _swa_sample_kernel(T, sinks_ref, q_ref, kn_ref, knr_ref, vn_ref, vnr_ref, ckt_ref, cvt_ref, g_ref, mq_ref, gm_ref,
                       cmk_ref, cmv_ref, x_ref, wout_ref, lng_ref, lnb_ref, y_ref, mix_ref):
    b = pl.program_id(0)
    swap = lambda t: jnp.concatenate([t[SW_HD:], t[:SW_HD]], axis=0).astype(BF16)
    ckt, cvt = ckt_ref[0], cvt_ref[0]
    kt, ktr, vt, vtr = ckt.astype(BF16), swap(ckt), cvt.astype(BF16), swap(cvt)
    kn, knr, vn, vnr = kn_ref[...], knr_ref[...], vn_ref[...], vnr_ref[...]
    attend = lambda qq, swapped, sink: _sink_attend_cached(
        qq, ktr if swapped else kt, knr if swapped else kn, vtr if swapped else vt, vnr if swapped else vn, sink)
    o = _swa_heads(q_ref[...], attend, lambda h: sinks_ref[h] * LOG2E)
    _sample_tail(b, b == pl.num_programs(0) - 1, o, g_ref, mq_ref, gm_ref, cmk_ref, cmv_ref, x_ref, wout_ref,
                 lng_ref, lnb_ref, y_ref, mix_ref, T)


def _swa_sample(sinks, q, kn, knr, vn, vnr, ckt, cvt, g, mq, gm, cmk, cmv, layer, x, w_out, ln_g, ln_b, *, NB, T):
    W = ckt.shape[2]
    row = lambda w: pl.BlockSpec((T, w), lambda b: (b, 0))
    full = lambda shape: pl.BlockSpec(shape, lambda b: (0,) * len(shape))
    cache = pl.BlockSpec((1, LANES, W), lambda b: (b, 0, 0))
    memc = pl.BlockSpec((1, 1, N_MEM * MEM_HEADS, MEM_HD), lambda b: (layer, b, 0, 0))
    return pl.pallas_call(
        functools.partial(_swa_sample_kernel, T),
        grid=(NB,),
        in_specs=[pl.BlockSpec(memory_space=pltpu.SMEM), row(SW_W), row(LANES), row(LANES), row(LANES), row(LANES),
                  cache, cache, row(SW_W), row(MEM_W), row(MEM_W), memc, memc, full((NB * T, D_MODEL)),
                  _resident((1, MIX_W, D_MODEL), layer, 0, 0), full((1, D_MODEL)), full((1, D_MODEL))],
        out_specs=full((NB * T, D_MODEL)),
        out_shape=jax.ShapeDtypeStruct((NB * T, D_MODEL), F32),
        scratch_shapes=[pltpu.VMEM((NB * T, MIX_W), BF16)],
        compiler_params=_params("arbitrary"),
        name="swa_sample",
    )(sinks, q, kn, knr, vn, vnr, ckt, cvt, g, mq, gm, cmk, cmv, x, w_out, ln_g, ln_b)


def _rope_tables(pos):
    T = pos.shape[0]
    inv = ROPE_THETA ** (-np.arange(ROPE_HALF, dtype=np.float64) / ROPE_HALF)
    ang = pos.astype(np.float64)[:, None] * inv[None, :]
    cos, sin = np.cos(ang), np.sin(ang)
    rest = SW_HD - 2 * ROPE_HALF
    c = np.concatenate([cos, cos, np.ones((T, rest))], axis=1)
    a = np.concatenate([-sin, np.zeros((T, SW_HD - ROPE_HALF))], axis=1)
    b = np.concatenate([np.zeros((T, ROPE_HALF)), sin, np.zeros((T, rest))], axis=1)
    return tuple(jnp.asarray(np.tile(t, (1, LANES // SW_HD)), F32) for t in (c, a, b))


def _feature_major(t):
    n = t.ndim
    return jnp.transpose(t, (0,) + tuple(range(2, n)) + (1,)).reshape(t.shape[0], -1, t.shape[1])


def _rows_major(t, tail):
    n = len(tail)
    return jnp.transpose(t.reshape((t.shape[0],) + tail + (t.shape[2],)), (0, n + 1) + tuple(range(1, n + 1)))


def kernel(x_prompt, x_sample, mem_prompt, cache_diff_k, cache_diff_v, cache_swa_k, cache_swa_v, cache_mem_k,
           cache_mem_v, w_in_a, lam_q1, lam_k1, lam_q2, lam_k2, diff_norm_g, w_in_b, sinks, w_kv_shared, w_mem_kv,
           w_out, ln_g, ln_b):
    B, S, _ = x_prompt.shape
    NB, T, _ = x_sample.shape
    P = cache_diff_k.shape[2]
    assert w_in_a.shape[0] == 1 and w_in_b.shape[0] == 1 and w_out.shape[0] == DEPTH

    tabs_p = _rope_tables(np.arange(S))
    tabs_s = _rope_tables(np.tile(P + np.arange(T), NB))
    xp = x_prompt.reshape(B * S, D_MODEL)
    xs = x_sample.reshape(NB * T, D_MODEL)
    wa, wb, wo = (w_in_a[0],), (w_in_b[0], w_kv_shared), w_out
    lng = ln_g.reshape(DEPTH, 1, D_MODEL)
    lnb = ln_b.reshape(DEPTH, 1, D_MODEL)
    lamv = jnp.concatenate([lam_q1, lam_k1, lam_q2, lam_k2], axis=0)
    dng = diff_norm_g.reshape(1, DA_V)
    lam_init = 0.8 - 0.6 * math.exp(-0.3 * 0)

    mkf, mvf, mkb, mvb = _mem_kv(mem_prompt.reshape(B * N_MEM, D_MODEL), w_mem_kv)
    mem_b = lambda l: (mkb[l].reshape(B, N_MEM, MEM_W), mvb[l].reshape(B, N_MEM, MEM_W))
    head_rows = lambda t: t.reshape(t.shape[:-3] + (t.shape[-3] * t.shape[-2], t.shape[-1]))
    cmk, cmv = head_rows(cache_mem_k), head_rows(cache_mem_v)

    q, kt, vf, kb, vb, g, om = _project("a", xp, wa, tabs_p, mem_b(0), tm=PROJ_ROWS, rows_per_batch=S)
    xp1 = _diff_prompt(lam_init, lamv, q, kb, vb, g, om, xp, wo, 0, lng[0], lnb[0], dng, B=B, S=S)
    qs, kfs, vfs, kbs, vbs, gs_, mqs, gms = _project("a", xs, wa, tabs_s, tm=NB * T)
    sample_a = _diff_sample_rider(lam_init, lamv, qs, kbs, vbs, _feature_major(cache_diff_k[0]),
                                  head_rows(cache_diff_v[0]), gs_, mqs, gms, cmk, cmv, 0, xs, wo, lng[0], lnb[0], dng,
                                  NB=NB, T=T, tk=SAMPLE_KEYS)

    q, g, om, skf, svf, kb1, kr1, vb1, vr1, xs1 = _project("b", xp1, wb, tabs_p, mem_b(1), tm=PROJ_B_ROWS,
                                                          rows_per_batch=S, rider=sample_a)
    yp = _swa_prompt(sinks[0], q, kb1, kr1, vb1, vr1, g, om, xp1, wo, 1, lng[1], lnb[1], B=B, S=S, tq=SWA_ROWS)
    qs, gs_, mqs, gms, skfs, svfs, kbs1, krs1, vbs1, vrs1 = _project("b", xs1, wb, tabs_s, tm=NB * T)
    ys = _swa_sample(sinks[0], qs, kbs1, krs1, vbs1, vrs1, _feature_major(cache_swa_k), _feature_major(cache_swa_v),
                     gs_, mqs, gms, cmk, cmv, 1, xs1, wo, lng[1], lnb[1], NB=NB, T=T)

    wr_p = min(WINDOW, S)
    window = lambda t: t.reshape(B, S, LANES)[:, S - wr_p:, :].reshape(B, wr_p, SW_KV, SW_HD)
    swa_kp, swa_vp = window(skf), window(svf)
    roll = lambda c, n: jnp.concatenate([c, n.reshape(NB, T, SW_KV, SW_HD)], axis=1)[:, T:]
    return (yp.reshape(B, S, D_MODEL), ys.reshape(NB, T, D_MODEL),
            _rows_major(kt, (DA_HEADS, 2, DA_QK))[None], vf.reshape(1, B, S, DA_HEADS, DA_V),
            kfs.reshape(1, NB, T, DA_HEADS, 2, DA_QK), vfs.reshape(1, NB, T, DA_HEADS, DA_V),
            swa_kp, swa_vp, roll(cache_swa_k, skfs), roll(cache_swa_v, svfs),
            mkf.reshape(DEPTH, B, N_MEM, MEM_HEADS, MEM_HD), mvf.reshape(DEPTH, B, N_MEM, MEM_HEADS, MEM_HD))
```

```python
import functools
import math

import jax
import jax.numpy as jnp
import numpy as np
from jax import lax
from jax.experimental import pallas as pl
from jax.experimental.pallas import tpu as pltpu

D_MODEL = 1024
CHUNK = 64
N_MEM = 256
DA_HEADS = 4
DA_QK = 64
DA_V = 128
DA_W = 512
SW_HEADS = 8
SW_KV = 2
SW_HD = 64
SW_W = 512
WINDOW = 128
MEM_HEADS = 4
MEM_HD = 128
MEM_W = 512
MIX_W = 1024
ROPE_THETA = 500000.0
ROPE_HALF = 8
DEPTH = 2
DN_ALPHA = (2 * DEPTH) ** 0.25
LN_EPS = 1e-5
NEG = -1e30
LANES = 128
DIFF_TILE = 256
PIPE_LEAD = 4
DIFF_ROWS = 512
SWA_TILE = 128
SUM_ROWS = 16
PROJ_ROWS = 1024
PROJ_B_ROWS = 512
MERGE_ROWS = 256
SWA_ROWS = 1024
SAMPLE_KEYS = 2048
LOG2E = math.log2(math.e)

F32 = jnp.float32
BF16 = jnp.bfloat16
VMEM_LIMIT = 48 * 1024 * 1024

_NT = (((1,), (1,)), ((), ()))


def _resident(shape, *first):
    idx = first or (0,) * len(shape)
    return pl.BlockSpec(shape, lambda *_: idx, pipeline_mode=pl.Buffered(1))


def _params(*sem):
    return pltpu.CompilerParams(dimension_semantics=sem, vmem_limit_bytes=VMEM_LIMIT)


def _rope(x, c, a, b):
    outs = []
    for i in range(x.shape[1] // LANES):
        blk = x[:, i * LANES:(i + 1) * LANES]
        outs.append(blk * c + pltpu.roll(blk, LANES - ROPE_HALF, 1) * a + pltpu.roll(blk, ROPE_HALF, 1) * b)
    return outs[0] if len(outs) == 1 else jnp.concatenate(outs, axis=1)


def _silu(g):
    return g * (1.0 / (1.0 + jnp.exp(-g)))


def _layer_norm(z, g, b):
    mu = jnp.mean(z, axis=-1, keepdims=True)
    d = z - mu
    var = jnp.mean(d * d, axis=-1, keepdims=True)
    return d * lax.rsqrt(var + LN_EPS) * g + b


def _mem_attend(mq, mk_of, mv_of):
    outs = []
    for h in range(MEM_HEADS):
        sl = slice(h * MEM_HD, (h + 1) * MEM_HD)
        s = lax.dot_general(mq[:, sl], mk_of(h), _NT, preferred_element_type=F32) * (MEM_HD ** -0.5)
        p = jnp.exp(s - jnp.max(s, axis=1, keepdims=True))
        l = jnp.sum(p, axis=1, keepdims=True)
        outs.append(jnp.dot(p.astype(BF16), mv_of(h), preferred_element_type=F32) / l)
    return jnp.concatenate(outs, axis=1)


def _head_cols(ref):
    return lambda h: ref[0, :, h * MEM_HD:(h + 1) * MEM_HD]


def _head_idx(ref, heads, h, *lead):
    return lead + (pl.ds(h, ref.shape[-2] // heads, stride=heads), slice(None))


def _head_rows(ref, heads, *lead):
    return lambda h: ref[_head_idx(ref, heads, h, *lead)].astype(BF16)


def _merge(x, mix, w_out, ln_g, ln_b):
    w = w_out.astype(BF16)
    n = max(x.shape[0] // MERGE_ROWS, 1)
    r = x.shape[0] // n
    z = lambda c: DN_ALPHA * x[c * r:(c + 1) * r] + jnp.dot(mix[c * r:(c + 1) * r], w, preferred_element_type=F32)
    outs, prev = [], z(0)
    for c in range(1, n):
        nxt = z(c)
        outs.append(_layer_norm(prev, ln_g, ln_b))
        prev = nxt
    outs.append(_layer_norm(prev, ln_g, ln_b))
    return outs[0] if n == 1 else jnp.concatenate(outs, axis=0)


def _lambda(lamv, lam_init):
    e1 = jnp.exp(jnp.sum(lamv[0:1, :] * lamv[1:2, :], axis=1, keepdims=True))
    e2 = jnp.exp(jnp.sum(lamv[2:3, :] * lamv[3:4, :], axis=1, keepdims=True))
    return e1 - e2 + lam_init


def _split_components(qh):
    lane = lax.broadcasted_iota(jnp.int32, qh.shape, 1)
    zero = jnp.zeros_like(qh)
    return jnp.concatenate([jnp.where(lane < DA_QK, qh, zero), jnp.where(lane >= DA_QK, qh, zero)], axis=0)


def _flash_update(qqs, k_of, v_of, keys_on_lanes, m_ref, l_ref, acc_ref):
    if keys_on_lanes:
        s = [jnp.dot(qq, k_of(h), preferred_element_type=F32) for h, qq in enumerate(qqs)]
    else:
        s = [lax.dot_general(qq, k_of(h), _NT, preferred_element_type=F32) for h, qq in enumerate(qqs)]
    s = jnp.concatenate(s, axis=0)
    m_old = m_ref[...]
    m_new = jnp.maximum(m_old, jnp.max(s, axis=1, keepdims=True))
    alpha = jnp.exp2(m_old - m_new)
    p = jnp.exp2(s - m_new)
    l_ref[...] = alpha * l_ref[...] + jnp.sum(p, axis=1, keepdims=True)
    pb = p.astype(BF16)
    r = qqs[0].shape[0]
    pv = [jnp.dot(pb[h * r:(h + 1) * r], v_of(h), preferred_element_type=F32) for h in range(len(qqs))]
    acc_ref[...] = alpha * acc_ref[...] + jnp.concatenate(pv, axis=0)
    m_ref[...] = m_new


def _stage_scores(qq, ks, mask, s_ref, m_ref, alpha_ref):
    s = lax.dot_general(ks, qq, _NT, preferred_element_type=F32)
    if mask is not None:
        s = jnp.where(mask, s, NEG)
    s_ref[...] = s
    m_old = m_ref[...]
    m_new = jnp.maximum(m_old, jnp.max(s, axis=0, keepdims=True))
    alpha_ref[...] = jnp.exp2(m_old - m_new)
    m_ref[...] = m_new


def _stage_exp(s_ref, p_ref, m_ref):
    p_ref[...] = jnp.exp2(s_ref[...] - m_ref[...]).astype(BF16)


def _stage_pv(vts, p_ref, alpha_ref, acc_ref):
    ones = jnp.ones((SUM_ROWS, vts.shape[1]), BF16)
    pv = jnp.dot(jnp.concatenate([vts, ones], axis=0), p_ref[...], preferred_element_type=F32)
    acc_ref[...] = alpha_ref[...] * acc_ref[...] + pv


def _diff_finish(acc, l, lam, dng, lam_init):
    rows = acc.shape[0] // 2
    o = acc / l
    od = o[:rows] - lam * o[rows:]
    return od * lax.rsqrt(jnp.mean(od * od, axis=-1, keepdims=True) + LN_EPS) * dng * (1.0 - lam_init)


def _sink_attend_cached(qq, kt, kn, vt, vn, sink):
    s_c = jnp.dot(qq, kt, preferred_element_type=F32)
    s_n = lax.dot_general(qq, kn, _NT, preferred_element_type=F32)
    m = jnp.maximum(jnp.maximum(jnp.max(s_c, axis=1, keepdims=True), jnp.max(s_n, axis=1, keepdims=True)), sink)
    p_c, p_n = jnp.exp2(s_c - m), jnp.exp2(s_n - m)
    l = jnp.sum(p_c, axis=1, keepdims=True) + jnp.sum(p_n, axis=1, keepdims=True) + jnp.exp2(sink - m)
    o = (lax.dot_general(p_c.astype(BF16), vt, _NT, preferred_element_type=F32)
         + jnp.dot(p_n.astype(BF16), vn, preferred_element_type=F32))
    return o / l


def _window_scores(qq4, ks, mask):
    return jnp.where(mask, lax.dot_general(ks, qq4, _NT, preferred_element_type=F32), NEG)


def _window_attend(s, vts, sinks4):
    m = jnp.maximum(jnp.max(s, axis=0, keepdims=True), sinks4)
    p = jnp.exp2(s - m).astype(BF16)
    ones = jnp.ones((SUM_ROWS, vts.shape[1]), BF16)
    o = jnp.dot(jnp.concatenate([vts, ones], axis=0), p, preferred_element_type=F32)
    return o[:LANES] / (o[LANES:LANES + 1] + jnp.exp2(sinks4 - m))


def _swa_heads(q, attend, sink_of):
    low = lax.broadcasted_iota(jnp.int32, (q.shape[0], LANES), 1) < SW_HD
    outs = []
    for pair in range(SW_HEADS // 2):
        kv = pair // 2
        slab = q[:, pair * LANES:(pair + 1) * LANES]
        zero = jnp.zeros_like(slab)
        o_even = attend(jnp.where(low, slab, zero), kv == 1, sink_of(2 * pair))
        o_odd = attend(jnp.where(low, zero, slab), kv == 0, sink_of(2 * pair + 1))
        outs.append(jnp.where(low, o_even, o_odd))
    return jnp.concatenate(outs, axis=1)


def _mem_kv_kernel(mem_ref, w_ref, kf_ref, vf_ref, kb_ref, vb_ref):
    h = jnp.dot(mem_ref[...].astype(BF16), w_ref[0].astype(BF16), preferred_element_type=F32)
    k, v = h[:, :MEM_W], h[:, MEM_W:]
    for hd in range(MEM_HEADS):
        kf_ref[_head_idx(kf_ref, MEM_HEADS, hd, 0)] = k[:, hd * MEM_HD:(hd + 1) * MEM_HD]
        vf_ref[_head_idx(vf_ref, MEM_HEADS, hd, 0)] = v[:, hd * MEM_HD:(hd + 1) * MEM_HD]
    kb_ref[0] = k.astype(BF16)
    vb_ref[0] = v.astype(BF16)


def _mem_kv(mem, w):
    rows = mem.shape[0]
    fshape, fspec = (DEPTH, rows * MEM_HEADS, MEM_HD), pl.BlockSpec((1, rows * MEM_HEADS, MEM_HD), lambda l: (l, 0, 0))
    bshape, bspec = (DEPTH, rows, MEM_W), pl.BlockSpec((1, rows, MEM_W), lambda l: (l, 0, 0))
    return pl.pallas_call(
        _mem_kv_kernel,
        grid=(DEPTH,),
        in_specs=[pl.BlockSpec((rows, D_MODEL), lambda l: (0, 0)),
                  pl.BlockSpec((1, D_MODEL, 2 * MEM_W), lambda l: (l, 0, 0))],
        out_specs=[fspec, fspec, bspec, bspec],
        out_shape=[jax.ShapeDtypeStruct(fshape, F32), jax.ShapeDtypeStruct(fshape, F32),
                   jax.ShapeDtypeStruct(bshape, BF16), jax.ShapeDtypeStruct(bshape, BF16)],
        compiler_params=_params("arbitrary"),
        name="mem_kv",
    )(mem, w)


def _proj_a_kernel(fuse_mem, x_ref, w_ref, c_ref, a_ref, b_ref, *refs):
    if fuse_mem:
        mk_ref, mv_ref, q_ref, kf_ref, vf_ref, kb_ref, vb_ref, g_ref, om_ref = refs
    else:
        q_ref, kf_ref, vf_ref, kb_ref, vb_ref, g_ref, mq_ref, gm_ref = refs
    x = x_ref[...].astype(BF16)
    c, a, b = c_ref[...], a_ref[...], b_ref[...]

    def cols(i):
        return jnp.dot(x, w_ref[:, i * DA_W:(i + 1) * DA_W].astype(BF16), preferred_element_type=F32)

    q_ref[...] = (_rope(cols(0), c, a, b) * (DA_QK ** -0.5 * LOG2E)).astype(BF16)
    k = _rope(cols(1), c, a, b)
    kb_ref[...] = k.astype(BF16)
    v = cols(2)
    for hd in range(DA_HEADS):
        vf_ref[_head_idx(vf_ref, DA_HEADS, hd)] = v[:, hd * DA_V:(hd + 1) * DA_V]
    if fuse_mem:
        kf_ref[0] = k.T
        for blk in range(vb_ref.shape[0]):
            vb_ref[blk] = v[blk * DIFF_TILE:(blk + 1) * DIFF_TILE, :].T.astype(BF16)
    else:
        kf_ref[...] = k
        vb_ref[...] = v.astype(BF16)
    g_ref[...] = _silu(cols(3)).astype(BF16)
    mq = cols(4).astype(BF16)
    gm = _silu(cols(5))
    if fuse_mem:
        om_ref[...] = (_mem_attend(mq, _head_cols(mk_ref), _head_cols(mv_ref)) * gm).astype(BF16)
    else:
        mq_ref[...] = mq
        gm_ref[...] = gm.astype(BF16)


def _proj_b_kernel(fuse_mem, x_ref, w_ref, wkv_ref, c_ref, a_ref, b_ref, *refs):
    if fuse_mem:
        mk_ref, mv_ref, q_ref, g_ref, om_ref, kf_ref, vf_ref, kb_ref, kr_ref, vb_ref, vr_ref = refs
    else:
        q_ref, g_ref, mq_ref, gm_ref, kf_ref, vf_ref, kb_ref, kr_ref, vb_ref, vr_ref = refs
    x = x_ref[...].astype(BF16)
    c, a, b = c_ref[...], a_ref[...], b_ref[...]

    def cols(i):
        return jnp.dot(x, w_ref[:, i * SW_W:(i + 1) * SW_W].astype(BF16), preferred_element_type=F32)

    q_ref[...] = (_rope(cols(0), c, a, b) * (SW_HD ** -0.5 * LOG2E)).astype(BF16)
    g_ref[...] = _silu(cols(1)).astype(BF16)
    mq = cols(2).astype(BF16)
    gm = _silu(cols(3))
    if fuse_mem:
        om_ref[...] = (_mem_attend(mq, _head_cols(mk_ref), _head_cols(mv_ref)) * gm).astype(BF16)
    else:
        mq_ref[...] = mq
        gm_ref[...] = gm.astype(BF16)
    kv = jnp.dot(x, wkv_ref[...].astype(BF16), preferred_element_type=F32)
    k = _rope(kv[:, :LANES], c, a, b)
    v = kv[:, LANES:]
    kf_ref[...] = k
    vf_ref[...] = v
    kb_ref[...] = k.astype(BF16)
    kr_ref[...] = pltpu.roll(k, SW_HD, 1).astype(BF16)
    vr = pltpu.roll(v, SW_HD, 1)
    if fuse_mem:
        for blk in range(vb_ref.shape[0]):
            rows = slice(blk * SWA_TILE, (blk + 1) * SWA_TILE)
            vb_ref[blk] = v[rows, :].T.astype(BF16)
            vr_ref[blk] = vr[rows, :].T.astype(BF16)
    else:
        vb_ref[...] = v.astype(BF16)
        vr_ref[...] = vr.astype(BF16)


def _with_rider(host_body, n_in, n_out, rider, *refs):
    r_in, r_out = len(rider["args"]), len(rider["out_specs"])
    o0 = n_in + r_in
    host_body(*refs[:n_in], *refs[o0:o0 + n_out])
    rider["body"](*refs[n_in:o0], *refs[o0 + n_out:])


def _project(layer, x, ws, tabs, mem=None, *, tm, rows_per_batch=None, rider=None):
    M = x.shape[0]
    fuse = mem is not None
    per = (rows_per_batch or M) // tm
    row = lambda w: pl.BlockSpec((tm, w), lambda i: (i, 0))
    tab = pl.BlockSpec((tm, LANES), lambda i: (i % per, 0))
    in_specs = [row(D_MODEL)] + [_resident(w.shape) for w in ws] + [tab, tab, tab]
    args = [x, *ws, *tabs]
    if fuse:
        mspec = pl.BlockSpec((1, N_MEM, MEM_W), lambda i: (i // per, 0, 0))
        in_specs += [mspec, mspec]
        args += list(mem)
    o = lambda w, dt: (row(w), jax.ShapeDtypeStruct((M, w), dt))
    if layer == "a":
        vt = (pl.BlockSpec((tm // DIFF_TILE, DA_W, DIFF_TILE), lambda i: (i, 0, 0)),
              jax.ShapeDtypeStruct((M // DIFF_TILE, DA_W, DIFF_TILE), BF16))
        kt = (pl.BlockSpec((1, DA_W, tm), lambda i: (i // per, 0, i % per)),
              jax.ShapeDtypeStruct((M // (per * tm), DA_W, per * tm), F32))
        vf = (pl.BlockSpec((tm * DA_HEADS, DA_V), lambda i: (i, 0)), jax.ShapeDtypeStruct((M * DA_HEADS, DA_V), F32))
        outs = [o(DA_W, BF16), kt if fuse else o(DA_W, F32), vf, o(DA_W, BF16), vt if fuse else o(DA_W, BF16),
                o(DA_W, BF16)]
        body = _proj_a_kernel
    else:
        outs = [o(SW_W, BF16), o(SW_W, BF16)]
        body = _proj_b_kernel
    mem_outs = [o(MEM_W, BF16)] if fuse else [o(MEM_W, BF16), o(MEM_W, BF16)]
    if layer == "a":
        outs = outs + mem_outs
    else:
        vt = (pl.BlockSpec((tm // SWA_TILE, LANES, SWA_TILE), lambda i: (i, 0, 0)),
              jax.ShapeDtypeStruct((M // SWA_TILE, LANES, SWA_TILE), BF16))
        values = [vt, vt] if fuse else [o(LANES, BF16)] * 2
        outs = outs + mem_outs + [o(LANES, F32), o(LANES, F32)] + [o(LANES, BF16)] * 2 + values
    kernel_body, out_specs, out_shape, scratch = functools.partial(body, fuse), [s for s, _ in outs], [t for _, t in outs], []
    if rider is not None:
        assert rider["steps"] == M // tm
        kernel_body = functools.partial(_with_rider, kernel_body, len(args), len(outs), rider)
        in_specs, args = in_specs + rider["in_specs"], args + rider["args"]
        out_specs, out_shape, scratch = out_specs + rider["out_specs"], out_shape + rider["out_shape"], rider["scratch"]
    return pl.pallas_call(
        kernel_body,
        grid=(M // tm,),
        in_specs=in_specs,
        out_specs=out_specs,
        out_shape=out_shape,
        scratch_shapes=scratch,
        compiler_params=_params("arbitrary"),
        name="proj_" + layer,
    )(*args)


def _diff_prompt_kernel(lam_init, lamv_ref, q_ref, k_ref, vt_ref, g_ref, om_ref, x_ref, wout_ref,
                        lng_ref, lnb_ref, dng_ref, y_ref, m_ref, alpha_ref, acc_ref, s_ref, p_ref, mix_ref):
    t = DIFF_TILE
    nsub = q_ref.shape[0] // t
    LEAD = PIPE_LEAD
    i = pl.program_id(1)
    heads = [slice(h * DA_V, (h + 1) * DA_V) for h in range(DA_HEADS)]
    lane = lax.broadcasted_iota(jnp.int32, (t, DA_V), 1)
    units = []
    for sub in range(nsub):
        for hs in heads:
            qh = q_ref[sub * t:(sub + 1) * t, hs]
            zero = jnp.zeros_like(qh)
            units += [(sub, hs, jnp.where(lane < DA_QK, qh, zero)), (sub, hs, jnp.where(lane >= DA_QK, qh, zero))]
    n = len(units)
    per = n // nsub
    assert n == s_ref.shape[0] and LEAD + 1 < per
    m_ref[...] = jnp.full(m_ref.shape, -jnp.inf, F32)
    acc_ref[...] = jnp.zeros(acc_ref.shape, F32)

    def exp_stage(u):
        _stage_exp(s_ref.at[u], p_ref.at[u], m_ref.at[u])

    def pv_stage(j, u):
        _stage_pv(vt_ref[0, j, units[u][1], :], p_ref.at[u], alpha_ref.at[u], acc_ref.at[u])

    def block(j, mask_of, j_before, first_unit=0):
        rows = pl.ds(pl.multiple_of(j * t, t), t)
        for idx, u in enumerate(range(first_unit, n)):
            sub, hs, qq = units[u]
            _stage_scores(qq, k_ref[0, rows, hs], mask_of(sub), s_ref.at[u], m_ref.at[u], alpha_ref.at[u])
            if idx >= LEAD:
                exp_stage(u - LEAD)
            elif j_before is not None:
                exp_stage(n - LEAD + idx)
            if idx >= LEAD + 1:
                pv_stage(j, u - LEAD - 1)
            elif j_before is not None:
                pv_stage(j_before, n - LEAD - 1 + idx)

    kc = lax.broadcasted_iota(jnp.int32, (t, t), 0) // CHUNK
    qc = lax.broadcasted_iota(jnp.int32, (t, t), 1) // CHUNK
    diag_mask = kc <= qc
    diag0 = nsub * i
    for sd in range(nsub):
        block(diag0 + sd, lambda sub, sd=sd: diag_mask if sub == sd else None, None if sd == 0 else diag0 + sd - 1,
              first_unit=sd * per)

    def trip(jj, carry):
        j0 = nsub * jj
        block(j0, lambda sub: None, jnp.where(j0 == 0, diag0 + nsub - 1, j0 - 1))
        for d in range(1, nsub):
            block(j0 + d, lambda sub: None, j0 + d - 1)
        return carry

    lax.fori_loop(0, i, trip, 0)

    last = jnp.where(i == 0, diag0 + nsub - 1, diag0 - 1)
    for u in range(n - LEAD, n):
        exp_stage(u)
        pv_stage(last, u - 1)
    pv_stage(last, n - 1)

    lam = _lambda(lamv_ref[...], lam_init)
    for sub in range(nsub):
        rows = slice(sub * t, (sub + 1) * t)
        for h, hs in enumerate(heads):
            u = sub * per + 2 * h
            o1 = acc_ref[u, :DA_V, :] / acc_ref[u, DA_V:DA_V + 1, :]
            o2 = acc_ref[u + 1, :DA_V, :] / acc_ref[u + 1, DA_V:DA_V + 1, :]
            od = (o1 - lam * o2).T
            od = (od * lax.rsqrt(jnp.mean(od * od, axis=-1, keepdims=True) + LN_EPS) * dng_ref[...]
                  * (1.0 - lam_init))
            mix_ref[rows, hs] = (od * g_ref[rows, hs].astype(F32)).astype(BF16)
    mix_ref[:, DA_W:] = om_ref[...]
    y_ref[...] = _merge(x_ref[...], mix_ref[...], wout_ref[0], lng_ref[...], lnb_ref[...])


def _diff_prompt(lam_init, lamv, q, k, vt, g, om, x, w_out, layer, ln_g, ln_b, dng, *, B, S):
    t, rows = DIFF_TILE, DIFF_ROWS
    nq, units = S // rows, 2 * DA_HEADS * (rows // t)
    row = lambda w: pl.BlockSpec((rows, w), lambda b, i: (b * nq + i, 0))
    full = lambda shape: pl.BlockSpec(shape, lambda b, i: (0,) * len(shape))
    return pl.pallas_call(
        functools.partial(_diff_prompt_kernel, lam_init),
        grid=(B, nq),
        in_specs=[full((4, DA_QK)), row(DA_W), pl.BlockSpec((1, S, DA_W), lambda b, i: (b, 0, 0)),
                  pl.BlockSpec((1, S // t, DA_W, t), lambda b, i: (b, 0, 0, 0)), row(DA_W), row(MEM_W), row(D_MODEL),
                  _resident((1, MIX_W, D_MODEL), layer, 0, 0), full((1, D_MODEL)), full((1, D_MODEL)), full((1, DA_V))],
        out_specs=row(D_MODEL),
        out_shape=jax.ShapeDtypeStruct((B * S, D_MODEL), F32),
        scratch_shapes=[pltpu.VMEM((units, 1, t), F32), pltpu.VMEM((units, 1, t), F32),
                        pltpu.VMEM((units, DA_V + SUM_ROWS, t), F32),
                        pltpu.VMEM((units, t, t), F32), pltpu.VMEM((units, t, t), BF16),
                        pltpu.VMEM((rows, MIX_W), BF16)],
        compiler_params=_params("arbitrary", "arbitrary"),
        name="diff_prompt",
    )(lamv, q, k.reshape(B, S, DA_W), vt.reshape(B, S // t, DA_W, t), g, om, x, w_out, ln_g, ln_b, dng)


def _swa_prompt_kernel(tq, sinks_ref, q_ref, k_ref, kr_ref, vt_ref, vtr_ref, g_ref, om_ref, x_ref, wout_ref,
                       lng_ref, lnb_ref, y_ref, mix_ref):
    i = pl.program_id(1)
    w, nk, nsub = SWA_TILE, 2 * SWA_TILE, tq // SWA_TILE
    low = lax.broadcasted_iota(jnp.int32, (w, LANES), 1) < SW_HD
    row_low = lax.broadcasted_iota(jnp.int32, (LANES, w), 0) < SW_HD
    sink_row = lambda hs: jnp.concatenate([jnp.full((1, w), sinks_ref[h] * LOG2E, F32) for h in hs], axis=1)
    sinks_plain, sinks_swapped = sink_row((0, 2, 5, 7)), sink_row((1, 3, 4, 6))
    kc = lax.broadcasted_iota(jnp.int32, (nk, 4 * w), 0) // CHUNK
    qc = lax.broadcasted_iota(jnp.int32, (nk, 4 * w), 1) % w // CHUNK
    firsts, scores = [], []
    for st in range(nsub):
        sub = i * nsub + st
        first = jnp.maximum(sub - 1, 0)
        back = (sub - first) * (w // CHUNK)
        mask = (kc - back <= qc) & (kc - back >= qc - WINDOW // CHUNK)
        rows = pl.ds(pl.multiple_of(first * w, w), nk)
        q = q_ref[st * w:(st + 1) * w, :]
        slabs = [q[:, p * LANES:(p + 1) * LANES] for p in range(SW_HEADS // 2)]
        even = [jnp.where(low, sl, jnp.zeros_like(sl)) for sl in slabs]
        odd = [jnp.where(low, jnp.zeros_like(sl), sl) for sl in slabs]
        firsts.append(first)
        scores.append((_window_scores(jnp.concatenate([even[0], even[1], odd[2], odd[3]], axis=0), k_ref[0, rows, :],
                                      mask),
                       _window_scores(jnp.concatenate([odd[0], odd[1], even[2], even[3]], axis=0), kr_ref[0, rows, :],
                                      mask)))
    for st, (first, (s_plain, s_swapped)) in enumerate(zip(firsts, scores)):
        vts = jnp.concatenate([vt_ref[0, first], vt_ref[0, first + 1]], axis=1)
        vtrs = jnp.concatenate([vtr_ref[0, first], vtr_ref[0, first + 1]], axis=1)
        o_plain = _window_attend(s_plain, vts, sinks_plain)
        o_swapped = _window_attend(s_swapped, vtrs, sinks_swapped)
        for p in range(SW_HEADS // 2):
            cols = slice(p * w, (p + 1) * w)
            lo, hi = (o_plain, o_swapped) if p < 2 else (o_swapped, o_plain)
            slab = jnp.where(row_low, lo[:, cols], hi[:, cols]).T
            gate = g_ref[st * w:(st + 1) * w, p * LANES:(p + 1) * LANES].astype(F32)
            mix_ref[st * w:(st + 1) * w, p * LANES:(p + 1) * LANES] = (slab * gate).astype(BF16)
    mix_ref[:, SW_W:] = om_ref[...]
    y_ref[...] = _merge(x_ref[...], mix_ref[...], wout_ref[0], lng_ref[...], lnb_ref[...])


def _swa_prompt(sinks, q, k, kr, vt, vtr, g, om, x, w_out, layer, ln_g, ln_b, *, B, S, tq):
    nq = S // tq
    row = lambda w: pl.BlockSpec((tq, w), lambda b, i: (b * nq + i, 0))
    full = lambda shape: pl.BlockSpec(shape, lambda b, i: (0,) * len(shape))
    seq = pl.BlockSpec((1, S, LANES), lambda b, i: (b, 0, 0))
    seqt = pl.BlockSpec((1, S // SWA_TILE, LANES, SWA_TILE), lambda b, i: (b, 0, 0, 0))
    r3 = lambda t: t.reshape(B, S, LANES)
    r4 = lambda t: t.reshape(B, S // SWA_TILE, LANES, SWA_TILE)
    return pl.pallas_call(
        functools.partial(_swa_prompt_kernel, tq),
        grid=(B, nq),
        in_specs=[pl.BlockSpec(memory_space=pltpu.SMEM), row(SW_W), seq, seq, seqt, seqt, row(SW_W), row(MEM_W),
                  row(D_MODEL), _resident((1, MIX_W, D_MODEL), layer, 0, 0), full((1, D_MODEL)), full((1, D_MODEL))],
        out_specs=row(D_MODEL),
        out_shape=jax.ShapeDtypeStruct((B * S, D_MODEL), F32),
        scratch_shapes=[pltpu.VMEM((tq, MIX_W), BF16)],
        compiler_params=_params("arbitrary", "arbitrary"),
        name="swa_prompt",
    )(sinks, q, r3(k), r3(kr), r4(vt), r4(vtr), g, om, x, w_out, ln_g, ln_b)


def _sample_tail(b, last, o_first, g_ref, mq_ref, gm_ref, cmk_ref, cmv_ref, x_ref, wout_ref, lng_ref, lnb_ref,
                 y_ref, mix_ref, T):
    om = _mem_attend(mq_ref[...], _head_rows(cmk_ref, MEM_HEADS, 0, 0), _head_rows(cmv_ref, MEM_HEADS, 0, 0))
    om = om * gm_ref[...].astype(F32)
    mix = jnp.concatenate([(o_first * g_ref[...].astype(F32)).astype(BF16), om.astype(BF16)], axis=1)
    mix_ref[pl.ds(pl.multiple_of(b * T, T), T), :] = mix

    @pl.when(last)
    def _():
        y_ref[...] = _merge(x_ref[...], mix_ref[...], wout_ref[0], lng_ref[...], lnb_ref[...])


def _diff_sample_body(lam_init, T, nb, nj, lamv_ref, q_ref, kn_ref, vn_ref, ckt_ref, cv_ref, g_ref, mq_ref, gm_ref,
                      cmk_ref, cmv_ref, x_ref, wout_ref, lng_ref, lnb_ref, dng_ref, y_ref,
                      m_ref, l_ref, acc_ref, mix_ref):
    b, j = pl.program_id(0) // nj, pl.program_id(0) % nj
    heads = [slice(h * DA_V, (h + 1) * DA_V) for h in range(DA_HEADS)]
    qqs = [_split_components(q_ref[:, hs]) for hs in heads]

    @pl.when(j == 0)
    def _():
        m_ref[...] = jnp.full(m_ref.shape, -jnp.inf, F32)
        l_ref[...] = jnp.zeros(l_ref.shape, F32)
        acc_ref[...] = jnp.zeros(acc_ref.shape, F32)
        _flash_update(qqs, lambda h: kn_ref[:, heads[h]], lambda h: vn_ref[:, heads[h]], False, m_ref, l_ref, acc_ref)

    _flash_update(qqs, lambda h: ckt_ref[0, heads[h], :].astype(BF16), _head_rows(cv_ref, DA_HEADS, 0), True,
                  m_ref, l_ref, acc_ref)

    @pl.when(j == nj - 1)
    def _():
        lam = _lambda(lamv_ref[...], lam_init)
        rows = [slice(h * 2 * T, (h + 1) * 2 * T) for h in range(DA_HEADS)]
        od = jnp.concatenate([_diff_finish(acc_ref[r, :], l_ref[r, :], lam, dng_ref[...], lam_init) for r in rows],
                             axis=1)
        _sample_tail(b, b == nb - 1, od, g_ref, mq_ref, gm_ref, cmk_ref, cmv_ref, x_ref, wout_ref, lng_ref,
                     lnb_ref, y_ref, mix_ref, T)


def _diff_sample_rider(lam_init, lamv, q, kn, vn, ckt, cv, g, mq, gm, cmk, cmv, layer, x, w_out, ln_g, ln_b, dng, *,
                       NB, T, tk):
    nj = ckt.shape[2] // tk
    row = lambda w: pl.BlockSpec((T, w), lambda i: (i // nj, 0))
    small = lambda shape: _resident(shape)
    memc = pl.BlockSpec((1, 1, N_MEM * MEM_HEADS, MEM_HD), lambda i: (layer, i // nj, 0, 0))
    return dict(
        steps=NB * nj,
        in_specs=[small((4, DA_QK)), row(DA_W), row(DA_W), row(DA_W),
                  pl.BlockSpec((1, DA_W, tk), lambda i: (i // nj, 0, i % nj)),
                  pl.BlockSpec((1, tk * DA_HEADS, DA_V), lambda i: (i // nj, i % nj, 0)),
                  row(DA_W), row(MEM_W), row(MEM_W), memc, memc, small((NB * T, D_MODEL)),
                  _resident((1, MIX_W, D_MODEL), layer, 0, 0), small((1, D_MODEL)), small((1, D_MODEL)),
                  small((1, DA_V))],
        args=[lamv, q, kn, vn, ckt, cv, g, mq, gm, cmk, cmv, x, w_out, ln_g, ln_b, dng],
        out_specs=[pl.BlockSpec((NB * T, D_MODEL), lambda i: (0, 0))],
        out_shape=[jax.ShapeDtypeStruct((NB * T, D_MODEL), F32)],
        scratch=[pltpu.VMEM((DA_HEADS * 2 * T, 1), F32), pltpu.VMEM((DA_HEADS * 2 * T, 1), F32),
                 pltpu.VMEM((DA_HEADS * 2 * T, DA_V), F32), pltpu.VMEM((NB * T, MIX_W), BF16)],
        body=functools.partial(_diff_sample_body, lam_init, T, NB, nj),
    )


def _swa_sample_kernel(T, sinks_ref, q_ref, kn_ref, knr_ref, vn_ref, vnr_ref, ckt_ref, cvt_ref, g_ref, mq_ref, gm_ref,
                       cmk_ref, cmv_ref, x_ref, wout_ref, lng_ref, lnb_ref, y_ref, mix_ref):
    b = pl.program_id(0)
    swap = lambda t: jnp.concatenate([t[SW_HD:], t[:SW_HD]], axis=0).astype(BF16)
    ckt, cvt = ckt_ref[0], cvt_ref[0]
    kt, ktr, vt, vtr = ckt.astype(BF16), swap(ckt), cvt.astype(BF16), swap(cvt)
    kn, knr, vn, vnr = kn_ref[...], knr_ref[...], vn_ref[...], vnr_ref[...]
    attend = lambda qq, swapped, sink: _sink_attend_cached(
        qq, ktr if swapped else kt, knr if swapped else kn, vtr if swapped else vt, vnr if swapped else vn, sink)
    o = _swa_heads(q_ref[...], attend, lambda h: sinks_ref[h] * LOG2E)
    _sample_tail(b, b == pl.num_programs(0) - 1, o, g_ref, mq_ref, gm_ref, cmk_ref, cmv_ref, x_ref, wout_ref,
                 lng_ref, lnb_ref, y_ref, mix_ref, T)


def _swa_sample(sinks, q, kn, knr, vn, vnr, ckt, cvt, g, mq, gm, cmk, cmv, layer, x, w_out, ln_g, ln_b, *, NB, T):
    W = ckt.shape[2]
    row = lambda w: pl.BlockSpec((T, w), lambda b: (b, 0))
    full = lambda shape: pl.BlockSpec(shape, lambda b: (0,) * len(shape))
    cache = pl.BlockSpec((1, LANES, W), lambda b: (b, 0, 0))
    memc = pl.BlockSpec((1, 1, N_MEM * MEM_HEADS, MEM_HD), lambda b: (layer, b, 0, 0))
    return pl.pallas_call(
        functools.partial(_swa_sample_kernel, T),
        grid=(NB,),
        in_specs=[pl.BlockSpec(memory_space=pltpu.SMEM), row(SW_W), row(LANES), row(LANES), row(LANES), row(LANES),
                  cache, cache, row(SW_W), row(MEM_W), row(MEM_W), memc, memc, full((NB * T, D_MODEL)),
                  _resident((1, MIX_W, D_MODEL), layer, 0, 0), full((1, D_MODEL)), full((1, D_MODEL))],
        out_specs=full((NB * T, D_MODEL)),
        out_shape=jax.ShapeDtypeStruct((NB * T, D_MODEL), F32),
        scratch_shapes=[pltpu.VMEM((NB * T, MIX_W), BF16)],
        compiler_params=_params("arbitrary"),
        name="swa_sample",
    )(sinks, q, kn, knr, vn, vnr, ckt, cvt, g, mq, gm, cmk, cmv, x, w_out, ln_g, ln_b)


def _rope_tables(pos):
    T = pos.shape[0]
    inv = ROPE_THETA ** (-np.arange(ROPE_HALF, dtype=np.float64) / ROPE_HALF)
    ang = pos.astype(np.float64)[:, None] * inv[None, :]
    cos, sin = np.cos(ang), np.sin(ang)
    rest = SW_HD - 2 * ROPE_HALF
    c = np.concatenate([cos, cos, np.ones((T, rest))], axis=1)
    a = np.concatenate([-sin, np.zeros((T, SW_HD - ROPE_HALF))], axis=1)
    b = np.concatenate([np.zeros((T, ROPE_HALF)), sin, np.zeros((T, rest))], axis=1)
    return tuple(jnp.asarray(np.tile(t, (1, LANES // SW_HD)), F32) for t in (c, a, b))


def _feature_major(t):
    n = t.ndim
    return jnp.transpose(t, (0,) + tuple(range(2, n)) + (1,)).reshape(t.shape[0], -1, t.shape[1])


def _rows_major(t, tail):
    n = len(tail)
    return jnp.transpose(t.reshape((t.shape[0],) + tail + (t.shape[2],)), (0, n + 1) + tuple(range(1, n + 1)))


def kernel(x_prompt, x_sample, mem_prompt, cache_diff_k, cache_diff_v, cache_swa_k, cache_swa_v, cache_mem_k,
           cache_mem_v, w_in_a, lam_q1, lam_k1, lam_q2, lam_k2, diff_norm_g, w_in_b, sinks, w_kv_shared, w_mem_kv,
           w_out, ln_g, ln_b):
    B, S, _ = x_prompt.shape
    NB, T, _ = x_sample.shape
    P = cache_diff_k.shape[2]
    assert w_in_a.shape[0] == 1 and w_in_b.shape[0] == 1 and w_out.shape[0] == DEPTH

    tabs_p = _rope_tables(np.arange(S))
    tabs_s = _rope_tables(np.tile(P + np.arange(T), NB))
    xp = x_prompt.reshape(B * S, D_MODEL)
    xs = x_sample.reshape(NB * T, D_MODEL)
    wa, wb, wo = (w_in_a[0],), (w_in_b[0], w_kv_shared), w_out
    lng = ln_g.reshape(DEPTH, 1, D_MODEL)
    lnb = ln_b.reshape(DEPTH, 1, D_MODEL)
    lamv = jnp.concatenate([lam_q1, lam_k1, lam_q2, lam_k2], axis=0)
    dng = diff_norm_g.reshape(1, DA_V)
    lam_init = 0.8 - 0.6 * math.exp(-0.3 * 0)

    mkf, mvf, mkb, mvb = _mem_kv(mem_prompt.reshape(B * N_MEM, D_MODEL), w_mem_kv)
    mem_b = lambda l: (mkb[l].reshape(B, N_MEM, MEM_W), mvb[l].reshape(B, N_MEM, MEM_W))
    head_rows = lambda t: t.reshape(t.shape[:-3] + (t.shape[-3] * t.shape[-2], t.shape[-1]))
    cmk, cmv = head_rows(cache_mem_k), head_rows(cache_mem_v)

    q, kt, vf, kb, vb, g, om = _project("a", xp, wa, tabs_p, mem_b(0), tm=PROJ_ROWS, rows_per_batch=S)
    xp1 = _diff_prompt(lam_init, lamv, q, kb, vb, g, om, xp, wo, 0, lng[0], lnb[0], dng, B=B, S=S)
    qs, kfs, vfs, kbs, vbs, gs_, mqs, gms = _project("a", xs, wa, tabs_s, tm=NB * T)
    sample_a = _diff_sample_rider(lam_init, lamv, qs, kbs, vbs, _feature_major(cache_diff_k[0]),
                                  head_rows(cache_diff_v[0]), gs_, mqs, gms, cmk, cmv, 0, xs, wo, lng[0], lnb[0], dng,
                                  NB=NB, T=T, tk=SAMPLE_KEYS)

    q, g, om, skf, svf, kb1, kr1, vb1, vr1, xs1 = _project("b", xp1, wb, tabs_p, mem_b(1), tm=PROJ_B_ROWS,
                                                          rows_per_batch=S, rider=sample_a)
    yp = _swa_prompt(sinks[0], q, kb1, kr1, vb1, vr1, g, om, xp1, wo, 1, lng[1], lnb[1], B=B, S=S, tq=SWA_ROWS)
    qs, gs_, mqs, gms, skfs, svfs, kbs1, krs1, vbs1, vrs1 = _project("b", xs1, wb, tabs_s, tm=NB * T)
    ys = _swa_sample(sinks[0], qs, kbs1, krs1, vbs1, vrs1, _feature_major(cache_swa_k), _feature_major(cache_swa_v),
                     gs_, mqs, gms, cmk, cmv, 1, xs1, wo, lng[1], lnb[1], NB=NB, T=T)

    wr_p = min(WINDOW, S)
    window = lambda t: t.reshape(B, S, LANES)[:, S - wr_p:, :].reshape(B, wr_p, SW_KV, SW_HD)
    swa_kp, swa_vp = window(skf), window(svf)
    roll = lambda c, n: jnp.concatenate([c, n.reshape(NB, T, SW_KV, SW_HD)], axis=1)[:, T:]
    return (yp.reshape(B, S, D_MODEL), ys.reshape(NB, T, D_MODEL),
            _rows_major(kt, (DA_HEADS, 2, DA_QK))[None], vf.reshape(1, B, S, DA_HEADS, DA_V),
            kfs.reshape(1, NB, T, DA_HEADS, 2, DA_QK), vfs.reshape(1, NB, T, DA_HEADS, DA_V),
            swa_kp, swa_vp, roll(cache_swa_k, skfs), roll(cache_swa_v, svfs),
            mkf.reshape(DEPTH, B, N_MEM, MEM_HEADS, MEM_HD), mvf.reshape(DEPTH, B, N_MEM, MEM_HEADS, MEM_HD))
```

```python
import functools
import math

import jax
import jax.numpy as jnp
import numpy as np
from jax import lax
from jax.experimental import pallas as pl
from jax.experimental.pallas import tpu as pltpu

D_MODEL = 1024
CHUNK = 64
N_MEM = 256
DA_HEADS = 4
DA_QK = 64
DA_V = 128
DA_W = 512
SW_HEADS = 8
SW_KV = 2
SW_HD = 64
SW_W = 512
WINDOW = 128
MEM_HEADS = 4
MEM_HD = 128
MEM_W = 512
MIX_W = 1024
ROPE_THETA = 500000.0
ROPE_HALF = 8
DEPTH = 2
DN_ALPHA = (2 * DEPTH) ** 0.25
LN_EPS = 1e-5
NEG = -1e30
LANES = 128
DIFF_TILE = 256
PIPE_LEAD = 4
DIFF_ROWS = 512
TRIP_BLOCKS = 4
SWA_TILE = 128
SUM_ROWS = 16
PROJ_ROWS = 1024
PROJ_B_ROWS = 512
MERGE_ROWS = 256
SWA_ROWS = 1024
SAMPLE_KEYS = 2048
LOG2E = math.log2(math.e)

F32 = jnp.float32
BF16 = jnp.bfloat16
VMEM_LIMIT = 48 * 1024 * 1024

_NT = (((1,), (1,)), ((), ()))


def _resident(shape, *first):
    idx = first or (0,) * len(shape)
    return pl.BlockSpec(shape, lambda *_: idx, pipeline_mode=pl.Buffered(1))


def _params(*sem):
    return pltpu.CompilerParams(dimension_semantics=sem, vmem_limit_bytes=VMEM_LIMIT)


def _rope(x, c, a, b):
    outs = []
    for i in range(x.shape[1] // LANES):
        blk = x[:, i * LANES:(i + 1) * LANES]
        outs.append(blk * c + pltpu.roll(blk, LANES - ROPE_HALF, 1) * a + pltpu.roll(blk, ROPE_HALF, 1) * b)
    return outs[0] if len(outs) == 1 else jnp.concatenate(outs, axis=1)


def _silu(g):
    return g * (1.0 / (1.0 + jnp.exp(-g)))


def _layer_norm(z, g, b):
    mu = jnp.mean(z, axis=-1, keepdims=True)
    d = z - mu
    var = jnp.mean(d * d, axis=-1, keepdims=True)
    return d * lax.rsqrt(var + LN_EPS) * g + b


def _mem_attend(mq, mk_of, mv_of):
    outs = []
    for h in range(MEM_HEADS):
        sl = slice(h * MEM_HD, (h + 1) * MEM_HD)
        s = lax.dot_general(mq[:, sl], mk_of(h), _NT, preferred_element_type=F32) * (MEM_HD ** -0.5)
        p = jnp.exp(s - jnp.max(s, axis=1, keepdims=True))
        l = jnp.sum(p, axis=1, keepdims=True)
        outs.append(jnp.dot(p.astype(BF16), mv_of(h), preferred_element_type=F32) / l)
    return jnp.concatenate(outs, axis=1)


def _head_cols(ref):
    return lambda h: ref[0, :, h * MEM_HD:(h + 1) * MEM_HD]


def _head_idx(ref, heads, h, *lead):
    return lead + (pl.ds(h, ref.shape[-2] // heads, stride=heads), slice(None))


def _head_rows(ref, heads, *lead):
    return lambda h: ref[_head_idx(ref, heads, h, *lead)].astype(BF16)


def _merge(x, mix, w_out, ln_g, ln_b):
    w = w_out.astype(BF16)
    n = max(x.shape[0] // MERGE_ROWS, 1)
    r = x.shape[0] // n
    z = lambda c: DN_ALPHA * x[c * r:(c + 1) * r] + jnp.dot(mix[c * r:(c + 1) * r], w, preferred_element_type=F32)
    outs, prev = [], z(0)
    for c in range(1, n):
        nxt = z(c)
        outs.append(_layer_norm(prev, ln_g, ln_b))
        prev = nxt
    outs.append(_layer_norm(prev, ln_g, ln_b))
    return outs[0] if n == 1 else jnp.concatenate(outs, axis=0)


def _lambda(lamv, lam_init):
    e1 = jnp.exp(jnp.sum(lamv[0:1, :] * lamv[1:2, :], axis=1, keepdims=True))
    e2 = jnp.exp(jnp.sum(lamv[2:3, :] * lamv[3:4, :], axis=1, keepdims=True))
    return e1 - e2 + lam_init


def _split_components(qh):
    lane = lax.broadcasted_iota(jnp.int32, qh.shape, 1)
    zero = jnp.zeros_like(qh)
    return jnp.concatenate([jnp.where(lane < DA_QK, qh, zero), jnp.where(lane >= DA_QK, qh, zero)], axis=0)


def _flash_update(qqs, k_of, v_of, keys_on_lanes, m_ref, l_ref, acc_ref):
    if keys_on_lanes:
        s = [jnp.dot(qq, k_of(h), preferred_element_type=F32) for h, qq in enumerate(qqs)]
    else:
        s = [lax.dot_general(qq, k_of(h), _NT, preferred_element_type=F32) for h, qq in enumerate(qqs)]
    s = jnp.concatenate(s, axis=0)
    m_old = m_ref[...]
    m_new = jnp.maximum(m_old, jnp.max(s, axis=1, keepdims=True))
    alpha = jnp.exp2(m_old - m_new)
    p = jnp.exp2(s - m_new)
    l_ref[...] = alpha * l_ref[...] + jnp.sum(p, axis=1, keepdims=True)
    pb = p.astype(BF16)
    r = qqs[0].shape[0]
    pv = [jnp.dot(pb[h * r:(h + 1) * r], v_of(h), preferred_element_type=F32) for h in range(len(qqs))]
    acc_ref[...] = alpha * acc_ref[...] + jnp.concatenate(pv, axis=0)
    m_ref[...] = m_new


def _stage_scores(qq, ks, mask, s_ref, m_ref, alpha_ref):
    s = lax.dot_general(ks, qq, _NT, preferred_element_type=F32)
    if mask is not None:
        s = jnp.where(mask, s, NEG)
    s_ref[...] = s
    m_old = m_ref[...]
    m_new = jnp.maximum(m_old, jnp.max(s, axis=0, keepdims=True))
    alpha_ref[...] = jnp.exp2(m_old - m_new)
    m_ref[...] = m_new


def _stage_exp(s_ref, p_ref, m_ref):
    p_ref[...] = jnp.exp2(s_ref[...] - m_ref[...]).astype(BF16)


def _stage_pv(vts, p_ref, alpha_ref, acc_ref):
    ones = jnp.ones((SUM_ROWS, vts.shape[1]), BF16)
    pv = jnp.dot(jnp.concatenate([vts, ones], axis=0), p_ref[...], preferred_element_type=F32)
    acc_ref[...] = alpha_ref[...] * acc_ref[...] + pv


def _diff_finish(acc, l, lam, dng, lam_init):
    rows = acc.shape[0] // 2
    o = acc / l
    od = o[:rows] - lam * o[rows:]
    return od * lax.rsqrt(jnp.mean(od * od, axis=-1, keepdims=True) + LN_EPS) * dng * (1.0 - lam_init)


def _sink_attend_cached(qq, kt, kn, vt, vn, sink):
    s_c = jnp.dot(qq, kt, preferred_element_type=F32)
    s_n = lax.dot_general(qq, kn, _NT, preferred_element_type=F32)
    m = jnp.maximum(jnp.maximum(jnp.max(s_c, axis=1, keepdims=True), jnp.max(s_n, axis=1, keepdims=True)), sink)
    p_c, p_n = jnp.exp2(s_c - m), jnp.exp2(s_n - m)
    l = jnp.sum(p_c, axis=1, keepdims=True) + jnp.sum(p_n, axis=1, keepdims=True) + jnp.exp2(sink - m)
    o = (lax.dot_general(p_c.astype(BF16), vt, _NT, preferred_element_type=F32)
         + jnp.dot(p_n.astype(BF16), vn, preferred_element_type=F32))
    return o / l


def _window_scores(qq4, ks, mask):
    return jnp.where(mask, lax.dot_general(ks, qq4, _NT, preferred_element_type=F32), NEG)


def _window_attend(s, vts, sinks4):
    m = jnp.maximum(jnp.max(s, axis=0, keepdims=True), sinks4)
    p = jnp.exp2(s - m).astype(BF16)
    ones = jnp.ones((SUM_ROWS, vts.shape[1]), BF16)
    o = jnp.dot(jnp.concatenate([vts, ones], axis=0), p, preferred_element_type=F32)
    return o[:LANES] / (o[LANES:LANES + 1] + jnp.exp2(sinks4 - m))


def _swa_heads(q, attend, sink_of):
    low = lax.broadcasted_iota(jnp.int32, (q.shape[0], LANES), 1) < SW_HD
    outs = []
    for pair in range(SW_HEADS // 2):
        kv = pair // 2
        slab = q[:, pair * LANES:(pair + 1) * LANES]
        zero = jnp.zeros_like(slab)
        o_even = attend(jnp.where(low, slab, zero), kv == 1, sink_of(2 * pair))
        o_odd = attend(jnp.where(low, zero, slab), kv == 0, sink_of(2 * pair + 1))
        outs.append(jnp.where(low, o_even, o_odd))
    return jnp.concatenate(outs, axis=1)


def _mem_kv_kernel(mem_ref, w_ref, kf_ref, vf_ref, kb_ref, vb_ref):
    h = jnp.dot(mem_ref[...].astype(BF16), w_ref[0].astype(BF16), preferred_element_type=F32)
    k, v = h[:, :MEM_W], h[:, MEM_W:]
    for hd in range(MEM_HEADS):
        kf_ref[_head_idx(kf_ref, MEM_HEADS, hd, 0)] = k[:, hd * MEM_HD:(hd + 1) * MEM_HD]
        vf_ref[_head_idx(vf_ref, MEM_HEADS, hd, 0)] = v[:, hd * MEM_HD:(hd + 1) * MEM_HD]
    kb_ref[0] = k.astype(BF16)
    vb_ref[0] = v.astype(BF16)


def _mem_kv(mem, w):
    rows = mem.shape[0]
    fshape, fspec = (DEPTH, rows * MEM_HEADS, MEM_HD), pl.BlockSpec((1, rows * MEM_HEADS, MEM_HD), lambda l: (l, 0, 0))
    bshape, bspec = (DEPTH, rows, MEM_W), pl.BlockSpec((1, rows, MEM_W), lambda l: (l, 0, 0))
    return pl.pallas_call(
        _mem_kv_kernel,
        grid=(DEPTH,),
        in_specs=[pl.BlockSpec((rows, D_MODEL), lambda l: (0, 0)),
                  pl.BlockSpec((1, D_MODEL, 2 * MEM_W), lambda l: (l, 0, 0))],
        out_specs=[fspec, fspec, bspec, bspec],
        out_shape=[jax.ShapeDtypeStruct(fshape, F32), jax.ShapeDtypeStruct(fshape, F32),
                   jax.ShapeDtypeStruct(bshape, BF16), jax.ShapeDtypeStruct(bshape, BF16)],
        compiler_params=_params("arbitrary"),
        name="mem_kv",
    )(mem, w)


def _proj_a_kernel(fuse_mem, x_ref, w_ref, c_ref, a_ref, b_ref, *refs):
    if fuse_mem:
        mk_ref, mv_ref, q_ref, kf_ref, vf_ref, kb_ref, vb_ref, g_ref, om_ref = refs
    else:
        q_ref, kf_ref, vf_ref, kb_ref, vb_ref, g_ref, mq_ref, gm_ref = refs
    x = x_ref[...].astype(BF16)
    c, a, b = c_ref[...], a_ref[...], b_ref[...]

    def cols(i):
        return jnp.dot(x, w_ref[:, i * DA_W:(i + 1) * DA_W].astype(BF16), preferred_element_type=F32)

    q_ref[...] = (_rope(cols(0), c, a, b) * (DA_QK ** -0.5 * LOG2E)).astype(BF16)
    k = _rope(cols(1), c, a, b)
    kb_ref[...] = k.astype(BF16)
    v = cols(2)
    for hd in range(DA_HEADS):
        vf_ref[_head_idx(vf_ref, DA_HEADS, hd)] = v[:, hd * DA_V:(hd + 1) * DA_V]
    if fuse_mem:
        kf_ref[0] = k.T
        for blk in range(vb_ref.shape[0]):
            vb_ref[blk] = v[blk * DIFF_TILE:(blk + 1) * DIFF_TILE, :].T.astype(BF16)
    else:
        kf_ref[...] = k
        vb_ref[...] = v.astype(BF16)
    g_ref[...] = _silu(cols(3)).astype(BF16)
    mq = cols(4).astype(BF16)
    gm = _silu(cols(5))
    if fuse_mem:
        om_ref[...] = (_mem_attend(mq, _head_cols(mk_ref), _head_cols(mv_ref)) * gm).astype(BF16)
    else:
        mq_ref[...] = mq
        gm_ref[...] = gm.astype(BF16)


def _proj_b_kernel(fuse_mem, x_ref, w_ref, wkv_ref, c_ref, a_ref, b_ref, *refs):
    if fuse_mem:
        mk_ref, mv_ref, q_ref, g_ref, om_ref, kf_ref, vf_ref, kb_ref, kr_ref, vb_ref, vr_ref = refs
    else:
        q_ref, g_ref, mq_ref, gm_ref, kf_ref, vf_ref, kb_ref, kr_ref, vb_ref, vr_ref = refs
    x = x_ref[...].astype(BF16)
    c, a, b = c_ref[...], a_ref[...], b_ref[...]

    def cols(i):
        return jnp.dot(x, w_ref[:, i * SW_W:(i + 1) * SW_W].astype(BF16), preferred_element_type=F32)

    q_ref[...] = (_rope(cols(0), c, a, b) * (SW_HD ** -0.5 * LOG2E)).astype(BF16)
    g_ref[...] = _silu(cols(1)).astype(BF16)
    mq = cols(2).astype(BF16)
    gm = _silu(cols(3))
    if fuse_mem:
        om_ref[...] = (_mem_attend(mq, _head_cols(mk_ref), _head_cols(mv_ref)) * gm).astype(BF16)
    else:
        mq_ref[...] = mq
        gm_ref[...] = gm.astype(BF16)
    kv = jnp.dot(x, wkv_ref[...].astype(BF16), preferred_element_type=F32)
    k = _rope(kv[:, :LANES], c, a, b)
    v = kv[:, LANES:]
    kf_ref[...] = k
    vf_ref[...] = v
    kb_ref[...] = k.astype(BF16)
    kr_ref[...] = pltpu.roll(k, SW_HD, 1).astype(BF16)
    vr = pltpu.roll(v, SW_HD, 1)
    if fuse_mem:
        for blk in range(vb_ref.shape[0]):
            rows = slice(blk * SWA_TILE, (blk + 1) * SWA_TILE)
            vb_ref[blk] = v[rows, :].T.astype(BF16)
            vr_ref[blk] = vr[rows, :].T.astype(BF16)
    else:
        vb_ref[...] = v.astype(BF16)
        vr_ref[...] = vr.astype(BF16)


def _with_rider(host_body, n_in, n_out, rider, *refs):
    r_in, r_out = len(rider["args"]), len(rider["out_specs"])
    o0 = n_in + r_in
    host_body(*refs[:n_in], *refs[o0:o0 + n_out])
    rider["body"](*refs[n_in:o0], *refs[o0 + n_out:])


def _project(layer, x, ws, tabs, mem=None, *, tm, rows_per_batch=None, rider=None):
    M = x.shape[0]
    fuse = mem is not None
    per = (rows_per_batch or M) // tm
    row = lambda w: pl.BlockSpec((tm, w), lambda i: (i, 0))
    tab = pl.BlockSpec((tm, LANES), lambda i: (i % per, 0))
    in_specs = [row(D_MODEL)] + [_resident(w.shape) for w in ws] + [tab, tab, tab]
    args = [x, *ws, *tabs]
    if fuse:
        mspec = pl.BlockSpec((1, N_MEM, MEM_W), lambda i: (i // per, 0, 0))
        in_specs += [mspec, mspec]
        args += list(mem)
    o = lambda w, dt: (row(w), jax.ShapeDtypeStruct((M, w), dt))
    if layer == "a":
        vt = (pl.BlockSpec((tm // DIFF_TILE, DA_W, DIFF_TILE), lambda i: (i, 0, 0)),
              jax.ShapeDtypeStruct((M // DIFF_TILE, DA_W, DIFF_TILE), BF16))
        kt = (pl.BlockSpec((1, DA_W, tm), lambda i: (i // per, 0, i % per)),
              jax.ShapeDtypeStruct((M // (per * tm), DA_W, per * tm), F32))
        vf = (pl.BlockSpec((tm * DA_HEADS, DA_V), lambda i: (i, 0)), jax.ShapeDtypeStruct((M * DA_HEADS, DA_V), F32))
        outs = [o(DA_W, BF16), kt if fuse else o(DA_W, F32), vf, o(DA_W, BF16), vt if fuse else o(DA_W, BF16),
                o(DA_W, BF16)]
        body = _proj_a_kernel
    else:
        outs = [o(SW_W, BF16), o(SW_W, BF16)]
        body = _proj_b_kernel
    mem_outs = [o(MEM_W, BF16)] if fuse else [o(MEM_W, BF16), o(MEM_W, BF16)]
    if layer == "a":
        outs = outs + mem_outs
    else:
        vt = (pl.BlockSpec((tm // SWA_TILE, LANES, SWA_TILE), lambda i: (i, 0, 0)),
              jax.ShapeDtypeStruct((M // SWA_TILE, LANES, SWA_TILE), BF16))
        values = [vt, vt] if fuse else [o(LANES, BF16)] * 2
        outs = outs + mem_outs + [o(LANES, F32), o(LANES, F32)] + [o(LANES, BF16)] * 2 + values
    kernel_body, out_specs, out_shape, scratch = functools.partial(body, fuse), [s for s, _ in outs], [t for _, t in outs], []
    if rider is not None:
        assert rider["steps"] == M // tm
        kernel_body = functools.partial(_with_rider, kernel_body, len(args), len(outs), rider)
        in_specs, args = in_specs + rider["in_specs"], args + rider["args"]
        out_specs, out_shape, scratch = out_specs + rider["out_specs"], out_shape + rider["out_shape"], rider["scratch"]
    return pl.pallas_call(
        kernel_body,
        grid=(M // tm,),
        in_specs=in_specs,
        out_specs=out_specs,
        out_shape=out_shape,
        scratch_shapes=scratch,
        compiler_params=_params("arbitrary"),
        name="proj_" + layer,
    )(*args)


def _diff_prompt_kernel(lam_init, lamv_ref, q_ref, k_ref, vt_ref, g_ref, om_ref, x_ref, wout_ref,
                        lng_ref, lnb_ref, dng_ref, y_ref, m_ref, alpha_ref, acc_ref, s_ref, p_ref, mix_ref):
    t = DIFF_TILE
    nsub = q_ref.shape[0] // t
    LEAD = PIPE_LEAD
    i = pl.program_id(1)
    heads = [slice(h * DA_V, (h + 1) * DA_V) for h in range(DA_HEADS)]
    lane = lax.broadcasted_iota(jnp.int32, (t, DA_V), 1)
    units = []
    for sub in range(nsub):
        for hs in heads:
            qh = q_ref[sub * t:(sub + 1) * t, hs]
            zero = jnp.zeros_like(qh)
            units += [(sub, hs, jnp.where(lane < DA_QK, qh, zero)), (sub, hs, jnp.where(lane >= DA_QK, qh, zero))]
    n = len(units)
    per = n // nsub
    assert n == s_ref.shape[0] and LEAD + 1 < per
    m_ref[...] = jnp.full(m_ref.shape, -jnp.inf, F32)
    acc_ref[...] = jnp.zeros(acc_ref.shape, F32)

    def exp_stage(u):
        _stage_exp(s_ref.at[u], p_ref.at[u], m_ref.at[u])

    def pv_stage(j, u):
        _stage_pv(vt_ref[0, j, units[u][1], :], p_ref.at[u], alpha_ref.at[u], acc_ref.at[u])

    def block(j, mask_of, j_before, first_unit=0):
        rows = pl.ds(pl.multiple_of(j * t, t), t)
        for idx, u in enumerate(range(first_unit, n)):
            sub, hs, qq = units[u]
            _stage_scores(qq, k_ref[0, rows, hs], mask_of(sub), s_ref.at[u], m_ref.at[u], alpha_ref.at[u])
            if idx >= LEAD:
                exp_stage(u - LEAD)
            elif j_before is not None:
                exp_stage(n - LEAD + idx)
            if idx >= LEAD + 1:
                pv_stage(j, u - LEAD - 1)
            elif j_before is not None:
                pv_stage(j_before, n - LEAD - 1 + idx)

    kc = lax.broadcasted_iota(jnp.int32, (t, t), 0) // CHUNK
    qc = lax.broadcasted_iota(jnp.int32, (t, t), 1) // CHUNK
    diag_mask = kc <= qc
    diag0 = nsub * i
    for sd in range(nsub):
        block(diag0 + sd, lambda sub, sd=sd: diag_mask if sub == sd else None, None if sd == 0 else diag0 + sd - 1,
              first_unit=sd * per)

    def run(j0, count):
        block(j0, lambda sub: None, jnp.where(j0 == 0, diag0 + nsub - 1, j0 - 1))
        for d in range(1, count):
            block(j0 + d, lambda sub: None, j0 + d - 1)

    def trip(jj, carry):
        run(TRIP_BLOCKS * jj, TRIP_BLOCKS)
        return carry

    plain = nsub * i
    lax.fori_loop(0, plain // TRIP_BLOCKS, trip, 0)

    @pl.when(plain % TRIP_BLOCKS != 0)
    def _():
        run(plain // TRIP_BLOCKS * TRIP_BLOCKS, nsub)

    last = jnp.where(i == 0, diag0 + nsub - 1, diag0 - 1)
    for u in range(n - LEAD, n):
        exp_stage(u)
        pv_stage(last, u - 1)
    pv_stage(last, n - 1)

    lam = _lambda(lamv_ref[...], lam_init)
    for sub in range(nsub):
        rows = slice(sub * t, (sub + 1) * t)
        for h, hs in enumerate(heads):
            u = sub * per + 2 * h
            o1 = acc_ref[u, :DA_V, :] / acc_ref[u, DA_V:DA_V + 1, :]
            o2 = acc_ref[u + 1, :DA_V, :] / acc_ref[u + 1, DA_V:DA_V + 1, :]
            od = (o1 - lam * o2).T
            od = (od * lax.rsqrt(jnp.mean(od * od, axis=-1, keepdims=True) + LN_EPS) * dng_ref[...]
                  * (1.0 - lam_init))
            mix_ref[rows, hs] = (od * g_ref[rows, hs].astype(F32)).astype(BF16)
    mix_ref[:, DA_W:] = om_ref[...]
    y_ref[...] = _merge(x_ref[...], mix_ref[...], wout_ref[0], lng_ref[...], lnb_ref[...])


def _diff_prompt(lam_init, lamv, q, k, vt, g, om, x, w_out, layer, ln_g, ln_b, dng, *, B, S):
    t, rows = DIFF_TILE, DIFF_ROWS
    nq, units = S // rows, 2 * DA_HEADS * (rows // t)
    row = lambda w: pl.BlockSpec((rows, w), lambda b, i: (b * nq + i, 0))
    full = lambda shape: pl.BlockSpec(shape, lambda b, i: (0,) * len(shape))
    return pl.pallas_call(
        functools.partial(_diff_prompt_kernel, lam_init),
        grid=(B, nq),
        in_specs=[full((4, DA_QK)), row(DA_W), pl.BlockSpec((1, S, DA_W), lambda b, i: (b, 0, 0)),
                  pl.BlockSpec((1, S // t, DA_W, t), lambda b, i: (b, 0, 0, 0)), row(DA_W), row(MEM_W), row(D_MODEL),
                  _resident((1, MIX_W, D_MODEL), layer, 0, 0), full((1, D_MODEL)), full((1, D_MODEL)), full((1, DA_V))],
        out_specs=row(D_MODEL),
        out_shape=jax.ShapeDtypeStruct((B * S, D_MODEL), F32),
        scratch_shapes=[pltpu.VMEM((units, 1, t), F32), pltpu.VMEM((units, 1, t), F32),
                        pltpu.VMEM((units, DA_V + SUM_ROWS, t), F32),
                        pltpu.VMEM((units, t, t), F32), pltpu.VMEM((units, t, t), BF16),
                        pltpu.VMEM((rows, MIX_W), BF16)],
        compiler_params=_params("arbitrary", "arbitrary"),
        name="diff_prompt",
    )(lamv, q, k.reshape(B, S, DA_W), vt.reshape(B, S // t, DA_W, t), g, om, x, w_out, ln_g, ln_b, dng)


def _swa_prompt_kernel(tq, sinks_ref, q_ref, k_ref, kr_ref, vt_ref, vtr_ref, g_ref, om_ref, x_ref, wout_ref,
                       lng_ref, lnb_ref, y_ref, mix_ref):
    i = pl.program_id(1)
    w, nk, nsub = SWA_TILE, 2 * SWA_TILE, tq // SWA_TILE
    low = lax.broadcasted_iota(jnp.int32, (w, LANES), 1) < SW_HD
    row_low = lax.broadcasted_iota(jnp.int32, (LANES, w), 0) < SW_HD
    sink_row = lambda hs: jnp.concatenate([jnp.full((1, w), sinks_ref[h] * LOG2E, F32) for h in hs], axis=1)
    sinks_plain, sinks_swapped = sink_row((0, 2, 5, 7)), sink_row((1, 3, 4, 6))
    kc = lax.broadcasted_iota(jnp.int32, (nk, 4 * w), 0) // CHUNK
    qc = lax.broadcasted_iota(jnp.int32, (nk, 4 * w), 1) % w // CHUNK
    firsts, scores = [], []
    for st in range(nsub):
        sub = i * nsub + st
        first = jnp.maximum(sub - 1, 0)
        back = (sub - first) * (w // CHUNK)
        mask = (kc - back <= qc) & (kc - back >= qc - WINDOW // CHUNK)
        rows = pl.ds(pl.multiple_of(first * w, w), nk)
        q = q_ref[st * w:(st + 1) * w, :]
        slabs = [q[:, p * LANES:(p + 1) * LANES] for p in range(SW_HEADS // 2)]
        even = [jnp.where(low, sl, jnp.zeros_like(sl)) for sl in slabs]
        odd = [jnp.where(low, jnp.zeros_like(sl), sl) for sl in slabs]
        firsts.append(first)
        scores.append((_window_scores(jnp.concatenate([even[0], even[1], odd[2], odd[3]], axis=0), k_ref[0, rows, :],
                                      mask),
                       _window_scores(jnp.concatenate([odd[0], odd[1], even[2], even[3]], axis=0), kr_ref[0, rows, :],
                                      mask)))
    for st, (first, (s_plain, s_swapped)) in enumerate(zip(firsts, scores)):
        vts = jnp.concatenate([vt_ref[0, first], vt_ref[0, first + 1]], axis=1)
        vtrs = jnp.concatenate([vtr_ref[0, first], vtr_ref[0, first + 1]], axis=1)
        o_plain = _window_attend(s_plain, vts, sinks_plain)
        o_swapped = _window_attend(s_swapped, vtrs, sinks_swapped)
        for p in range(SW_HEADS // 2):
            cols = slice(p * w, (p + 1) * w)
            lo, hi = (o_plain, o_swapped) if p < 2 else (o_swapped, o_plain)
            slab = jnp.where(row_low, lo[:, cols], hi[:, cols]).T
            gate = g_ref[st * w:(st + 1) * w, p * LANES:(p + 1) * LANES].astype(F32)
            mix_ref[st * w:(st + 1) * w, p * LANES:(p + 1) * LANES] = (slab * gate).astype(BF16)
    mix_ref[:, SW_W:] = om_ref[...]
    y_ref[...] = _merge(x_ref[...], mix_ref[...], wout_ref[0], lng_ref[...], lnb_ref[...])


def _swa_prompt(sinks, q, k, kr, vt, vtr, g, om, x, w_out, layer, ln_g, ln_b, *, B, S, tq):
    nq = S // tq
    row = lambda w: pl.BlockSpec((tq, w), lambda b, i: (b * nq + i, 0))
    full = lambda shape: pl.BlockSpec(shape, lambda b, i: (0,) * len(shape))
    seq = pl.BlockSpec((1, S, LANES), lambda b, i: (b, 0, 0))
    seqt = pl.BlockSpec((1, S // SWA_TILE, LANES, SWA_TILE), lambda b, i: (b, 0, 0, 0))
    r3 = lambda t: t.reshape(B, S, LANES)
    r4 = lambda t: t.reshape(B, S // SWA_TILE, LANES, SWA_TILE)
    return pl.pallas_call(
        functools.partial(_swa_prompt_kernel, tq),
        grid=(B, nq),
        in_specs=[pl.BlockSpec(memory_space=pltpu.SMEM), row(SW_W), seq, seq, seqt, seqt, row(SW_W), row(MEM_W),
                  row(D_MODEL), _resident((1, MIX_W, D_MODEL), layer, 0, 0), full((1, D_MODEL)), full((1, D_MODEL))],
        out_specs=row(D_MODEL),
        out_shape=jax.ShapeDtypeStruct((B * S, D_MODEL), F32),
        scratch_shapes=[pltpu.VMEM((tq, MIX_W), BF16)],
        compiler_params=_params("arbitrary", "arbitrary"),
        name="swa_prompt",
    )(sinks, q, r3(k), r3(kr), r4(vt), r4(vtr), g, om, x, w_out, ln_g, ln_b)


def _sample_tail(b, last, o_first, g_ref, mq_ref, gm_ref, cmk_ref, cmv_ref, x_ref, wout_ref, lng_ref, lnb_ref,
                 y_ref, mix_ref, T):
    om = _mem_attend(mq_ref[...], _head_rows(cmk_ref, MEM_HEADS, 0, 0), _head_rows(cmv_ref, MEM_HEADS, 0, 0))
    om = om * gm_ref[...].astype(F32)
    mix = jnp.concatenate([(o_first * g_ref[...].astype(F32)).astype(BF16), om.astype(BF16)], axis=1)
    mix_ref[pl.ds(pl.multiple_of(b * T, T), T), :] = mix

    @pl.when(last)
    def _():
        y_ref[...] = _merge(x_ref[...], mix_ref[...], wout_ref[0], lng_ref[...], lnb_ref[...])


def _diff_sample_body(lam_init, T, nb, nj, lamv_ref, q_ref, kn_ref, vn_ref, ckt_ref, cv_ref, g_ref, mq_ref, gm_ref,
                      cmk_ref, cmv_ref, x_ref, wout_ref, lng_ref, lnb_ref, dng_ref, y_ref,
                      m_ref, l_ref, acc_ref, mix_ref):
    b, j = pl.program_id(0) // nj, pl.program_id(0) % nj
    heads = [slice(h * DA_V, (h + 1) * DA_V) for h in range(DA_HEADS)]
    qqs = [_split_components(q_ref[:, hs]) for hs in heads]

    @pl.when(j == 0)
    def _():
        m_ref[...] = jnp.full(m_ref.shape, -jnp.inf, F32)
        l_ref[...] = jnp.zeros(l_ref.shape, F32)
        acc_ref[...] = jnp.zeros(acc_ref.shape, F32)
        _flash_update(qqs, lambda h: kn_ref[:, heads[h]], lambda h: vn_ref[:, heads[h]], False, m_ref, l_ref, acc_ref)

    _flash_update(qqs, lambda h: ckt_ref[0, heads[h], :].astype(BF16), _head_rows(cv_ref, DA_HEADS, 0), True,
                  m_ref, l_ref, acc_ref)

    @pl.when(j == nj - 1)
    def _():
        lam = _lambda(lamv_ref[...], lam_init)
        rows = [slice(h * 2 * T, (h + 1) * 2 * T) for h in range(DA_HEADS)]
        od = jnp.concatenate([_diff_finish(acc_ref[r, :], l_ref[r, :], lam, dng_ref[...], lam_init) for r in rows],
                             axis=1)
        _sample_tail(b, b == nb - 1, od, g_ref, mq_ref, gm_ref, cmk_ref, cmv_ref, x_ref, wout_ref, lng_ref,
                     lnb_ref, y_ref, mix_ref, T)


def _diff_sample_rider(lam_init, lamv, q, kn, vn, ckt, cv, g, mq, gm, cmk, cmv, layer, x, w_out, ln_g, ln_b, dng, *,
                       NB, T, tk):
    nj = ckt.shape[2] // tk
    row = lambda w: pl.BlockSpec((T, w), lambda i: (i // nj, 0))
    small = lambda shape: _resident(shape)
    memc = pl.BlockSpec((1, 1, N_MEM * MEM_HEADS, MEM_HD), lambda i: (layer, i // nj, 0, 0))
    return dict(
        steps=NB * nj,
        in_specs=[small((4, DA_QK)), row(DA_W), row(DA_W), row(DA_W),
                  pl.BlockSpec((1, DA_W, tk), lambda i: (i // nj, 0, i % nj)),
                  pl.BlockSpec((1, tk * DA_HEADS, DA_V), lambda i: (i // nj, i % nj, 0)),
                  row(DA_W), row(MEM_W), row(MEM_W), memc, memc, small((NB * T, D_MODEL)),
                  _resident((1, MIX_W, D_MODEL), layer, 0, 0), small((1, D_MODEL)), small((1, D_MODEL)),
                  small((1, DA_V))],
        args=[lamv, q, kn, vn, ckt, cv, g, mq, gm, cmk, cmv, x, w_out, ln_g, ln_b, dng],
        out_specs=[pl.BlockSpec((NB * T, D_MODEL), lambda i: (0, 0))],
        out_shape=[jax.ShapeDtypeStruct((NB * T, D_MODEL), F32)],
        scratch=[pltpu.VMEM((DA_HEADS * 2 * T, 1), F32), pltpu.VMEM((DA_HEADS * 2 * T, 1), F32),
                 pltpu.VMEM((DA_HEADS * 2 * T, DA_V), F32), pltpu.VMEM((NB * T, MIX_W), BF16)],
        body=functools.partial(_diff_sample_body, lam_init, T, NB, nj),
    )


def _swa_sample_kernel(T, sinks_ref, q_ref, kn_ref, knr_ref, vn_ref, vnr_ref, ckt_ref, cvt_ref, g_ref, mq_ref, gm_ref,
                       cmk_ref, cmv_ref, x_ref, wout_ref, lng_ref, lnb_ref, y_ref, mix_ref):
    b = pl.program_id(0)
    swap = lambda t: jnp.concatenate([t[SW_HD:], t[:SW_HD]], axis=0).astype(BF16)
    ckt, cvt = ckt_ref[0], cvt_ref[0]
    kt, ktr, vt, vtr = ckt.astype(BF16), swap(ckt), cvt.astype(BF16), swap(cvt)
    kn, knr, vn, vnr = kn_ref[...], knr_ref[...], vn_ref[...], vnr_ref[...]
    attend = lambda qq, swapped, sink: _sink_attend_cached(
        qq, ktr if swapped else kt, knr if swapped else kn, vtr if swapped else vt, vnr if swapped else vn, sink)
    o = _swa_heads(q_ref[...], attend, lambda h: sinks_ref[h] * LOG2E)
    _sample_tail(b, b == pl.num_programs(0) - 1, o, g_ref, mq_ref, gm_ref, cmk_ref, cmv_ref, x_ref, wout_ref,
                 lng_ref, lnb_ref, y_ref, mix_ref, T)


def _swa_sample(sinks, q, kn, knr, vn, vnr, ckt, cvt, g, mq, gm, cmk, cmv, layer, x, w_out, ln_g, ln_b, *, NB, T):
    W = ckt.shape[2]
    row = lambda w: pl.BlockSpec((T, w), lambda b: (b, 0))
    full = lambda shape: pl.BlockSpec(shape, lambda b: (0,) * len(shape))
    cache = pl.BlockSpec((1, LANES, W), lambda b: (b, 0, 0))
    memc = pl.BlockSpec((1, 1, N_MEM * MEM_HEADS, MEM_HD), lambda b: (layer, b, 0, 0))
    return pl.pallas_call(
        functools.partial(_swa_sample_kernel, T),
        grid=(NB,),
        in_specs=[pl.BlockSpec(memory_space=pltpu.SMEM), row(SW_W), row(LANES), row(LANES), row(LANES), row(LANES),
                  cache, cache, row(SW_W), row(MEM_W), row(MEM_W), memc, memc, full((NB * T, D_MODEL)),
                  _resident((1, MIX_W, D_MODEL), layer, 0, 0), full((1, D_MODEL)), full((1, D_MODEL))],
        out_specs=full((NB * T, D_MODEL)),
        out_shape=jax.ShapeDtypeStruct((NB * T, D_MODEL), F32),
        scratch_shapes=[pltpu.VMEM((NB * T, MIX_W), BF16)],
        compiler_params=_params("arbitrary"),
        name="swa_sample",
    )(sinks, q, kn, knr, vn, vnr, ckt, cvt, g, mq, gm, cmk, cmv, x, w_out, ln_g, ln_b)


def _rope_tables(pos):
    T = pos.shape[0]
    inv = ROPE_THETA ** (-np.arange(ROPE_HALF, dtype=np.float64) / ROPE_HALF)
    ang = pos.astype(np.float64)[:, None] * inv[None, :]
    cos, sin = np.cos(ang), np.sin(ang)
    rest = SW_HD - 2 * ROPE_HALF
    c = np.concatenate([cos, cos, np.ones((T, rest))], axis=1)
    a = np.concatenate([-sin, np.zeros((T, SW_HD - ROPE_HALF))], axis=1)
    b = np.concatenate([np.zeros((T, ROPE_HALF)), sin, np.zeros((T, rest))], axis=1)
    return tuple(jnp.asarray(np.tile(t, (1, LANES // SW_HD)), F32) for t in (c, a, b))


def _feature_major(t):
    n = t.ndim
    return jnp.transpose(t, (0,) + tuple(range(2, n)) + (1,)).reshape(t.shape[0], -1, t.shape[1])


def _rows_major(t, tail):
    n = len(tail)
    return jnp.transpose(t.reshape((t.shape[0],) + tail + (t.shape[2],)), (0, n + 1) + tuple(range(1, n + 1)))


def kernel(x_prompt, x_sample, mem_prompt, cache_diff_k, cache_diff_v, cache_swa_k, cache_swa_v, cache_mem_k,
           cache_mem_v, w_in_a, lam_q1, lam_k1, lam_q2, lam_k2, diff_norm_g, w_in_b, sinks, w_kv_shared, w_mem_kv,
           w_out, ln_g, ln_b):
    B, S, _ = x_prompt.shape
    NB, T, _ = x_sample.shape
    P = cache_diff_k.shape[2]
    assert w_in_a.shape[0] == 1 and w_in_b.shape[0] == 1 and w_out.shape[0] == DEPTH

    tabs_p = _rope_tables(np.arange(S))
    tabs_s = _rope_tables(np.tile(P + np.arange(T), NB))
    xp = x_prompt.reshape(B * S, D_MODEL)
    xs = x_sample.reshape(NB * T, D_MODEL)
    wa, wb, wo = (w_in_a[0],), (w_in_b[0], w_kv_shared), w_out
    lng = ln_g.reshape(DEPTH, 1, D_MODEL)
    lnb = ln_b.reshape(DEPTH, 1, D_MODEL)
    lamv = jnp.concatenate([lam_q1, lam_k1, lam_q2, lam_k2], axis=0)
    dng = diff_norm_g.reshape(1, DA_V)
    lam_init = 0.8 - 0.6 * math.exp(-0.3 * 0)

    mkf, mvf, mkb, mvb = _mem_kv(mem_prompt.reshape(B * N_MEM, D_MODEL), w_mem_kv)
    mem_b = lambda l: (mkb[l].reshape(B, N_MEM, MEM_W), mvb[l].reshape(B, N_MEM, MEM_W))
    head_rows = lambda t: t.reshape(t.shape[:-3] + (t.shape[-3] * t.shape[-2], t.shape[-1]))
    cmk, cmv = head_rows(cache_mem_k), head_rows(cache_mem_v)

    q, kt, vf, kb, vb, g, om = _project("a", xp, wa, tabs_p, mem_b(0), tm=PROJ_ROWS, rows_per_batch=S)
    xp1 = _diff_prompt(lam_init, lamv, q, kb, vb, g, om, xp, wo, 0, lng[0], lnb[0], dng, B=B, S=S)
    qs, kfs, vfs, kbs, vbs, gs_, mqs, gms = _project("a", xs, wa, tabs_s, tm=NB * T)
    sample_a = _diff_sample_rider(lam_init, lamv, qs, kbs, vbs, _feature_major(cache_diff_k[0]),
                                  head_rows(cache_diff_v[0]), gs_, mqs, gms, cmk, cmv, 0, xs, wo, lng[0], lnb[0], dng,
                                  NB=NB, T=T, tk=SAMPLE_KEYS)

    q, g, om, skf, svf, kb1, kr1, vb1, vr1, xs1 = _project("b", xp1, wb, tabs_p, mem_b(1), tm=PROJ_B_ROWS,
                                                          rows_per_batch=S, rider=sample_a)
    yp = _swa_prompt(sinks[0], q, kb1, kr1, vb1, vr1, g, om, xp1, wo, 1, lng[1], lnb[1], B=B, S=S, tq=SWA_ROWS)
    qs, gs_, mqs, gms, skfs, svfs, kbs1, krs1, vbs1, vrs1 = _project("b", xs1, wb, tabs_s, tm=NB * T)
    ys = _swa_sample(sinks[0], qs, kbs1, krs1, vbs1, vrs1, _feature_major(cache_swa_k), _feature_major(cache_swa_v),
                     gs_, mqs, gms, cmk, cmv, 1, xs1, wo, lng[1], lnb[1], NB=NB, T=T)

    wr_p = min(WINDOW, S)
    window = lambda t: t.reshape(B, S, LANES)[:, S - wr_p:, :].reshape(B, wr_p, SW_KV, SW_HD)
    swa_kp, swa_vp = window(skf), window(svf)
    roll = lambda c, n: jnp.concatenate([c, n.reshape(NB, T, SW_KV, SW_HD)], axis=1)[:, T:]
    return (yp.reshape(B, S, D_MODEL), ys.reshape(NB, T, D_MODEL),
            _rows_major(kt, (DA_HEADS, 2, DA_QK))[None], vf.reshape(1, B, S, DA_HEADS, DA_V),
            kfs.reshape(1, NB, T, DA_HEADS, 2, DA_QK), vfs.reshape(1, NB, T, DA_HEADS, DA_V),
            swa_kp, swa_vp, roll(cache_swa_k, skfs), roll(cache_swa_v, svfs),
            mkf.reshape(DEPTH, B, N_MEM, MEM_HEADS, MEM_HD), mvf.reshape(DEPTH, B, N_MEM, MEM_HEADS, MEM_HD))
```

```python
import functools
import inspect
import math

import jax
import jax.numpy as jnp
import numpy as np
from jax import lax
from jax.experimental import pallas as pl
from jax.experimental.pallas import tpu as pltpu

D_MODEL = 1024
CHUNK = 64
N_MEM = 256
DA_HEADS = 4
DA_QK = 64
DA_V = 128
DA_W = 512
SW_HEADS = 8
SW_KV = 2
SW_HD = 64
SW_W = 512
WINDOW = 128
MEM_HEADS = 4
MEM_HD = 128
MEM_W = 512
MIX_W = 1024
ROPE_THETA = 500000.0
ROPE_HALF = 8
DEPTH = 2
DN_ALPHA = (2 * DEPTH) ** 0.25
LN_EPS = 1e-5
NEG = -1e30
LANES = 128
DIFF_TILE = 256
PIPE_LEAD = 4
DIFF_ROWS = 512
TRIP_BLOCKS = 4
SWA_TILE = 128
SUM_ROWS = 16
PROJ_ROWS = 1024
PROJ_B_ROWS = 512
MERGE_ROWS = 256
SWA_ROWS = 1024
SAMPLE_KEYS = 2048
LOG2E = math.log2(math.e)

F32 = jnp.float32
BF16 = jnp.bfloat16
VMEM_LIMIT = 48 * 1024 * 1024

_NT = (((1,), (1,)), ((), ()))


def _resident(shape, *first):
    idx = first or (0,) * len(shape)
    return pl.BlockSpec(shape, lambda *_: idx, pipeline_mode=pl.Buffered(1))


def _params(*sem):
    return pltpu.CompilerParams(dimension_semantics=sem, vmem_limit_bytes=VMEM_LIMIT)


def _rope(x, c, a, b):
    outs = []
    for i in range(x.shape[1] // LANES):
        blk = x[:, i * LANES:(i + 1) * LANES]
        outs.append(blk * c + pltpu.roll(blk, LANES - ROPE_HALF, 1) * a + pltpu.roll(blk, ROPE_HALF, 1) * b)
    return outs[0] if len(outs) == 1 else jnp.concatenate(outs, axis=1)


def _silu(g):
    return g * (1.0 / (1.0 + jnp.exp(-g)))


def _layer_norm(z, g, b):
    mu = jnp.mean(z, axis=-1, keepdims=True)
    d = z - mu
    var = jnp.mean(d * d, axis=-1, keepdims=True)
    return d * lax.rsqrt(var + LN_EPS) * g + b


def _mem_attend(mq, mk_of, mv_of):
    outs = []
    for h in range(MEM_HEADS):
        sl = slice(h * MEM_HD, (h + 1) * MEM_HD)
        s = lax.dot_general(mq[:, sl], mk_of(h), _NT, preferred_element_type=F32) * (MEM_HD ** -0.5)
        p = jnp.exp(s - jnp.max(s, axis=1, keepdims=True))
        l = jnp.sum(p, axis=1, keepdims=True)
        outs.append(jnp.dot(p.astype(BF16), mv_of(h), preferred_element_type=F32) / l)
    return jnp.concatenate(outs, axis=1)


def _head_cols(ref):
    return lambda h: ref[0, :, h * MEM_HD:(h + 1) * MEM_HD]


def _head_idx(ref, heads, h, *lead):
    return lead + (pl.ds(h, ref.shape[-2] // heads, stride=heads), slice(None))


def _head_rows(ref, heads, *lead):
    return lambda h: ref[_head_idx(ref, heads, h, *lead)].astype(BF16)


def _merge(x, mix, w_out, ln_g, ln_b):
    w = w_out.astype(BF16)
    n = max(x.shape[0] // MERGE_ROWS, 1)
    r = x.shape[0] // n
    z = lambda c: DN_ALPHA * x[c * r:(c + 1) * r] + jnp.dot(mix[c * r:(c + 1) * r], w, preferred_element_type=F32)
    outs, prev = [], z(0)
    for c in range(1, n):
        nxt = z(c)
        outs.append(_layer_norm(prev, ln_g, ln_b))
        prev = nxt
    outs.append(_layer_norm(prev, ln_g, ln_b))
    return outs[0] if n == 1 else jnp.concatenate(outs, axis=0)


def _lambda(lamv, lam_init):
    e1 = jnp.exp(jnp.sum(lamv[0:1, :] * lamv[1:2, :], axis=1, keepdims=True))
    e2 = jnp.exp(jnp.sum(lamv[2:3, :] * lamv[3:4, :], axis=1, keepdims=True))
    return e1 - e2 + lam_init


def _split_components(qh):
    lane = lax.broadcasted_iota(jnp.int32, qh.shape, 1)
    zero = jnp.zeros_like(qh)
    return jnp.concatenate([jnp.where(lane < DA_QK, qh, zero), jnp.where(lane >= DA_QK, qh, zero)], axis=0)


def _flash_stages(qqs, k_of, v_of, keys_on_lanes, m_ref, l_ref, acc_ref):
    if keys_on_lanes:
        s = [jnp.dot(qq, k_of(h), preferred_element_type=F32) for h, qq in enumerate(qqs)]
    else:
        s = [lax.dot_general(qq, k_of(h), _NT, preferred_element_type=F32) for h, qq in enumerate(qqs)]
    s = jnp.concatenate(s, axis=0)
    yield
    m_old = m_ref[...]
    m_new = jnp.maximum(m_old, jnp.max(s, axis=1, keepdims=True))
    alpha = jnp.exp2(m_old - m_new)
    p = jnp.exp2(s - m_new)
    l_ref[...] = alpha * l_ref[...] + jnp.sum(p, axis=1, keepdims=True)
    pb = p.astype(BF16)
    yield
    r = qqs[0].shape[0]
    pv = [jnp.dot(pb[h * r:(h + 1) * r], v_of(h), preferred_element_type=F32) for h in range(len(qqs))]
    acc_ref[...] = alpha * acc_ref[...] + jnp.concatenate(pv, axis=0)
    m_ref[...] = m_new


def _drain(stages):
    if inspect.isgenerator(stages):
        for _ in stages:
            pass


def _stage_scores(qq, ks, mask, s_ref, m_ref, alpha_ref):
    s = lax.dot_general(ks, qq, _NT, preferred_element_type=F32)
    if mask is not None:
        s = jnp.where(mask, s, NEG)
    s_ref[...] = s
    m_old = m_ref[...]
    m_new = jnp.maximum(m_old, jnp.max(s, axis=0, keepdims=True))
    alpha_ref[...] = jnp.exp2(m_old - m_new)
    m_ref[...] = m_new


def _stage_exp(s_ref, p_ref, m_ref):
    p_ref[...] = jnp.exp2(s_ref[...] - m_ref[...]).astype(BF16)


def _stage_pv(vts, p_ref, alpha_ref, acc_ref):
    ones = jnp.ones((SUM_ROWS, vts.shape[1]), BF16)
    pv = jnp.dot(jnp.concatenate([vts, ones], axis=0), p_ref[...], preferred_element_type=F32)
    acc_ref[...] = alpha_ref[...] * acc_ref[...] + pv


def _diff_finish(acc, l, lam, dng, lam_init):
    rows = acc.shape[0] // 2
    o = acc / l
    od = o[:rows] - lam * o[rows:]
    return od * lax.rsqrt(jnp.mean(od * od, axis=-1, keepdims=True) + LN_EPS) * dng * (1.0 - lam_init)


def _sink_attend_cached(qq, kt, kn, vt, vn, sink):
    s_c = jnp.dot(qq, kt, preferred_element_type=F32)
    s_n = lax.dot_general(qq, kn, _NT, preferred_element_type=F32)
    m = jnp.maximum(jnp.maximum(jnp.max(s_c, axis=1, keepdims=True), jnp.max(s_n, axis=1, keepdims=True)), sink)
    p_c, p_n = jnp.exp2(s_c - m), jnp.exp2(s_n - m)
    l = jnp.sum(p_c, axis=1, keepdims=True) + jnp.sum(p_n, axis=1, keepdims=True) + jnp.exp2(sink - m)
    o = (lax.dot_general(p_c.astype(BF16), vt, _NT, preferred_element_type=F32)
         + jnp.dot(p_n.astype(BF16), vn, preferred_element_type=F32))
    return o / l


def _window_scores(qq4, ks, mask):
    return jnp.where(mask, lax.dot_general(ks, qq4, _NT, preferred_element_type=F32), NEG)


def _window_attend(s, vts, sinks4):
    m = jnp.maximum(jnp.max(s, axis=0, keepdims=True), sinks4)
    p = jnp.exp2(s - m).astype(BF16)
    ones = jnp.ones((SUM_ROWS, vts.shape[1]), BF16)
    o = jnp.dot(jnp.concatenate([vts, ones], axis=0), p, preferred_element_type=F32)
    return o[:LANES] / (o[LANES:LANES + 1] + jnp.exp2(sinks4 - m))


def _swa_heads(q, attend, sink_of):
    low = lax.broadcasted_iota(jnp.int32, (q.shape[0], LANES), 1) < SW_HD
    outs = []
    for pair in range(SW_HEADS // 2):
        kv = pair // 2
        slab = q[:, pair * LANES:(pair + 1) * LANES]
        zero = jnp.zeros_like(slab)
        o_even = attend(jnp.where(low, slab, zero), kv == 1, sink_of(2 * pair))
        o_odd = attend(jnp.where(low, zero, slab), kv == 0, sink_of(2 * pair + 1))
        outs.append(jnp.where(low, o_even, o_odd))
    return jnp.concatenate(outs, axis=1)


def _mem_kv_kernel(mem_ref, w_ref, kf_ref, vf_ref, kb_ref, vb_ref):
    h = jnp.dot(mem_ref[...].astype(BF16), w_ref[0].astype(BF16), preferred_element_type=F32)
    k, v = h[:, :MEM_W], h[:, MEM_W:]
    for hd in range(MEM_HEADS):
        kf_ref[_head_idx(kf_ref, MEM_HEADS, hd, 0)] = k[:, hd * MEM_HD:(hd + 1) * MEM_HD]
        vf_ref[_head_idx(vf_ref, MEM_HEADS, hd, 0)] = v[:, hd * MEM_HD:(hd + 1) * MEM_HD]
    kb_ref[0] = k.astype(BF16)
    vb_ref[0] = v.astype(BF16)


def _mem_kv(mem, w):
    rows = mem.shape[0]
    fshape, fspec = (DEPTH, rows * MEM_HEADS, MEM_HD), pl.BlockSpec((1, rows * MEM_HEADS, MEM_HD), lambda l: (l, 0, 0))
    bshape, bspec = (DEPTH, rows, MEM_W), pl.BlockSpec((1, rows, MEM_W), lambda l: (l, 0, 0))
    return pl.pallas_call(
        _mem_kv_kernel,
        grid=(DEPTH,),
        in_specs=[pl.BlockSpec((rows, D_MODEL), lambda l: (0, 0)),
                  pl.BlockSpec((1, D_MODEL, 2 * MEM_W), lambda l: (l, 0, 0))],
        out_specs=[fspec, fspec, bspec, bspec],
        out_shape=[jax.ShapeDtypeStruct(fshape, F32), jax.ShapeDtypeStruct(fshape, F32),
                   jax.ShapeDtypeStruct(bshape, BF16), jax.ShapeDtypeStruct(bshape, BF16)],
        compiler_params=_params("arbitrary"),
        name="mem_kv",
    )(mem, w)


def _proj_a_kernel(fuse_mem, x_ref, w_ref, c_ref, a_ref, b_ref, *refs):
    if fuse_mem:
        mk_ref, mv_ref, q_ref, kf_ref, vf_ref, kb_ref, vb_ref, g_ref, om_ref = refs
    else:
        q_ref, kf_ref, vf_ref, kb_ref, vb_ref, g_ref, mq_ref, gm_ref = refs
    x = x_ref[...].astype(BF16)
    c, a, b = c_ref[...], a_ref[...], b_ref[...]

    def cols(i):
        return jnp.dot(x, w_ref[:, i * DA_W:(i + 1) * DA_W].astype(BF16), preferred_element_type=F32)

    q_ref[...] = (_rope(cols(0), c, a, b) * (DA_QK ** -0.5 * LOG2E)).astype(BF16)
    k = _rope(cols(1), c, a, b)
    kb_ref[...] = k.astype(BF16)
    v = cols(2)
    for hd in range(DA_HEADS):
        vf_ref[_head_idx(vf_ref, DA_HEADS, hd)] = v[:, hd * DA_V:(hd + 1) * DA_V]
    if fuse_mem:
        kf_ref[0] = k.T
        for blk in range(vb_ref.shape[0]):
            vb_ref[blk] = v[blk * DIFF_TILE:(blk + 1) * DIFF_TILE, :].T.astype(BF16)
    else:
        kf_ref[...] = k
        vb_ref[...] = v.astype(BF16)
    g_ref[...] = _silu(cols(3)).astype(BF16)
    mq = cols(4).astype(BF16)
    gm = _silu(cols(5))
    if fuse_mem:
        om_ref[...] = (_mem_attend(mq, _head_cols(mk_ref), _head_cols(mv_ref)) * gm).astype(BF16)
    else:
        mq_ref[...] = mq
        gm_ref[...] = gm.astype(BF16)


def _proj_b_kernel(fuse_mem, x_ref, w_ref, wkv_ref, c_ref, a_ref, b_ref, *refs):
    if fuse_mem:
        mk_ref, mv_ref, q_ref, g_ref, om_ref, kf_ref, vf_ref, kb_ref, kr_ref, vb_ref, vr_ref = refs
    else:
        q_ref, g_ref, mq_ref, gm_ref, kf_ref, vf_ref, kb_ref, kr_ref, vb_ref, vr_ref = refs
    x = x_ref[...].astype(BF16)
    c, a, b = c_ref[...], a_ref[...], b_ref[...]

    def cols(i):
        return jnp.dot(x, w_ref[:, i * SW_W:(i + 1) * SW_W].astype(BF16), preferred_element_type=F32)

    q_ref[...] = (_rope(cols(0), c, a, b) * (SW_HD ** -0.5 * LOG2E)).astype(BF16)
    yield
    g_ref[...] = _silu(cols(1)).astype(BF16)
    yield
    mq = cols(2).astype(BF16)
    gm = _silu(cols(3))
    if fuse_mem:
        om_ref[...] = (_mem_attend(mq, _head_cols(mk_ref), _head_cols(mv_ref)) * gm).astype(BF16)
    else:
        mq_ref[...] = mq
        gm_ref[...] = gm.astype(BF16)
    yield
    kv = jnp.dot(x, wkv_ref[...].astype(BF16), preferred_element_type=F32)
    k = _rope(kv[:, :LANES], c, a, b)
    v = kv[:, LANES:]
    kf_ref[...] = k
    vf_ref[...] = v
    kb_ref[...] = k.astype(BF16)
    kr_ref[...] = pltpu.roll(k, SW_HD, 1).astype(BF16)
    vr = pltpu.roll(v, SW_HD, 1)
    if fuse_mem:
        for blk in range(vb_ref.shape[0]):
            rows = slice(blk * SWA_TILE, (blk + 1) * SWA_TILE)
            vb_ref[blk] = v[rows, :].T.astype(BF16)
            vr_ref[blk] = vr[rows, :].T.astype(BF16)
    else:
        vb_ref[...] = v.astype(BF16)
        vr_ref[...] = vr.astype(BF16)


def _with_rider(host_body, n_in, n_out, rider, *refs):
    r_in, r_out = len(rider["args"]), len(rider["out_specs"])
    o0 = n_in + r_in
    ride = rider["body"](*refs[n_in:o0], *refs[o0 + n_out:])
    mark = next(ride, None)
    host = host_body(*refs[:n_in], *refs[o0:o0 + n_out])
    if inspect.isgenerator(host):
        for _ in host:
            if mark == "stage":
                mark = next(ride, None)
    _drain(ride)


def _project(layer, x, ws, tabs, mem=None, *, tm, rows_per_batch=None, rider=None):
    M = x.shape[0]
    fuse = mem is not None
    per = (rows_per_batch or M) // tm
    row = lambda w: pl.BlockSpec((tm, w), lambda i: (i, 0))
    tab = pl.BlockSpec((tm, LANES), lambda i: (i % per, 0))
    in_specs = [row(D_MODEL)] + [_resident(w.shape) for w in ws] + [tab, tab, tab]
    args = [x, *ws, *tabs]
    if fuse:
        mspec = pl.BlockSpec((1, N_MEM, MEM_W), lambda i: (i // per, 0, 0))
        in_specs += [mspec, mspec]
        args += list(mem)
    o = lambda w, dt: (row(w), jax.ShapeDtypeStruct((M, w), dt))
    if layer == "a":
        vt = (pl.BlockSpec((tm // DIFF_TILE, DA_W, DIFF_TILE), lambda i: (i, 0, 0)),
              jax.ShapeDtypeStruct((M // DIFF_TILE, DA_W, DIFF_TILE), BF16))
        kt = (pl.BlockSpec((1, DA_W, tm), lambda i: (i // per, 0, i % per)),
              jax.ShapeDtypeStruct((M // (per * tm), DA_W, per * tm), F32))
        vf = (pl.BlockSpec((tm * DA_HEADS, DA_V), lambda i: (i, 0)), jax.ShapeDtypeStruct((M * DA_HEADS, DA_V), F32))
        outs = [o(DA_W, BF16), kt if fuse else o(DA_W, F32), vf, o(DA_W, BF16), vt if fuse else o(DA_W, BF16),
                o(DA_W, BF16)]
        body = _proj_a_kernel
    else:
        outs = [o(SW_W, BF16), o(SW_W, BF16)]
        body = _proj_b_kernel
    mem_outs = [o(MEM_W, BF16)] if fuse else [o(MEM_W, BF16), o(MEM_W, BF16)]
    if layer == "a":
        outs = outs + mem_outs
    else:
        vt = (pl.BlockSpec((tm // SWA_TILE, LANES, SWA_TILE), lambda i: (i, 0, 0)),
              jax.ShapeDtypeStruct((M // SWA_TILE, LANES, SWA_TILE), BF16))
        values = [vt, vt] if fuse else [o(LANES, BF16)] * 2
        outs = outs + mem_outs + [o(LANES, F32), o(LANES, F32)] + [o(LANES, BF16)] * 2 + values
    kernel_body = functools.partial(lambda run, *refs: _drain(run(*refs)), functools.partial(body, fuse))
    out_specs, out_shape, scratch = [s for s, _ in outs], [t for _, t in outs], []
    if rider is not None:
        assert rider["steps"] == M // tm
        kernel_body = functools.partial(_with_rider, functools.partial(body, fuse), len(args), len(outs), rider)
        in_specs, args = in_specs + rider["in_specs"], args + rider["args"]
        out_specs, out_shape, scratch = out_specs + rider["out_specs"], out_shape + rider["out_shape"], rider["scratch"]
    return pl.pallas_call(
        kernel_body,
        grid=(M // tm,),
        in_specs=in_specs,
        out_specs=out_specs,
        out_shape=out_shape,
        scratch_shapes=scratch,
        compiler_params=_params("arbitrary"),
        name="proj_" + layer,
    )(*args)


def _diff_prompt_kernel(lam_init, lamv_ref, q_ref, k_ref, vt_ref, g_ref, om_ref, x_ref, wout_ref,
                        lng_ref, lnb_ref, dng_ref, y_ref, m_ref, alpha_ref, acc_ref, s_ref, p_ref, mix_ref):
    t = DIFF_TILE
    nsub = q_ref.shape[0] // t
    LEAD = PIPE_LEAD
    i = pl.program_id(1)
    heads = [slice(h * DA_V, (h + 1) * DA_V) for h in range(DA_HEADS)]
    lane = lax.broadcasted_iota(jnp.int32, (t, DA_V), 1)
    units = []
    for sub in range(nsub):
        for hs in heads:
            qh = q_ref[sub * t:(sub + 1) * t, hs]
            zero = jnp.zeros_like(qh)
            units += [(sub, hs, jnp.where(lane < DA_QK, qh, zero)), (sub, hs, jnp.where(lane >= DA_QK, qh, zero))]
    n = len(units)
    per = n // nsub
    assert n == s_ref.shape[0] and LEAD + 1 < per
    m_ref[...] = jnp.full(m_ref.shape, -jnp.inf, F32)
    acc_ref[...] = jnp.zeros(acc_ref.shape, F32)

    def exp_stage(u):
        _stage_exp(s_ref.at[u], p_ref.at[u], m_ref.at[u])

    def pv_stage(j, u):
        _stage_pv(vt_ref[0, j, units[u][1], :], p_ref.at[u], alpha_ref.at[u], acc_ref.at[u])

    def block(j, mask_of, j_before, first_unit=0):
        rows = pl.ds(pl.multiple_of(j * t, t), t)
        for idx, u in enumerate(range(first_unit, n)):
            sub, hs, qq = units[u]
            _stage_scores(qq, k_ref[0, rows, hs], mask_of(sub), s_ref.at[u], m_ref.at[u], alpha_ref.at[u])
            if idx >= LEAD:
                exp_stage(u - LEAD)
            elif j_before is not None:
                exp_stage(n - LEAD + idx)
            if idx >= LEAD + 1:
                pv_stage(j, u - LEAD - 1)
            elif j_before is not None:
                pv_stage(j_before, n - LEAD - 1 + idx)

    kc = lax.broadcasted_iota(jnp.int32, (t, t), 0) // CHUNK
    qc = lax.broadcasted_iota(jnp.int32, (t, t), 1) // CHUNK
    diag_mask = kc <= qc
    diag0 = nsub * i
    for sd in range(nsub):
        block(diag0 + sd, lambda sub, sd=sd: diag_mask if sub == sd else None, None if sd == 0 else diag0 + sd - 1,
              first_unit=sd * per)

    def run(j0, count):
        block(j0, lambda sub: None, jnp.where(j0 == 0, diag0 + nsub - 1, j0 - 1))
        for d in range(1, count):
            block(j0 + d, lambda sub: None, j0 + d - 1)

    def trip(jj, carry):
        run(TRIP_BLOCKS * jj, TRIP_BLOCKS)
        return carry

    plain = nsub * i
    lax.fori_loop(0, plain // TRIP_BLOCKS, trip, 0)

    @pl.when(plain % TRIP_BLOCKS != 0)
    def _():
        run(plain // TRIP_BLOCKS * TRIP_BLOCKS, nsub)

    last = jnp.where(i == 0, diag0 + nsub - 1, diag0 - 1)
    for u in range(n - LEAD, n):
        exp_stage(u)
        pv_stage(last, u - 1)
    pv_stage(last, n - 1)

    lam = _lambda(lamv_ref[...], lam_init)
    for sub in range(nsub):
        rows = slice(sub * t, (sub + 1) * t)
        for h, hs in enumerate(heads):
            u = sub * per + 2 * h
            o1 = acc_ref[u, :DA_V, :] / acc_ref[u, DA_V:DA_V + 1, :]
            o2 = acc_ref[u + 1, :DA_V, :] / acc_ref[u + 1, DA_V:DA_V + 1, :]
            od = (o1 - lam * o2).T
            od = (od * lax.rsqrt(jnp.mean(od * od, axis=-1, keepdims=True) + LN_EPS) * dng_ref[...]
                  * (1.0 - lam_init))
            mix_ref[rows, hs] = (od * g_ref[rows, hs].astype(F32)).astype(BF16)
    mix_ref[:, DA_W:] = om_ref[...]
    y_ref[...] = _merge(x_ref[...], mix_ref[...], wout_ref[0], lng_ref[...], lnb_ref[...])


def _diff_prompt(lam_init, lamv, q, k, vt, g, om, x, w_out, layer, ln_g, ln_b, dng, *, B, S):
    t, rows = DIFF_TILE, DIFF_ROWS
    nq, units = S // rows, 2 * DA_HEADS * (rows // t)
    row = lambda w: pl.BlockSpec((rows, w), lambda b, i: (b * nq + i, 0))
    full = lambda shape: pl.BlockSpec(shape, lambda b, i: (0,) * len(shape))
    return pl.pallas_call(
        functools.partial(_diff_prompt_kernel, lam_init),
        grid=(B, nq),
        in_specs=[full((4, DA_QK)), row(DA_W), pl.BlockSpec((1, S, DA_W), lambda b, i: (b, 0, 0)),
                  pl.BlockSpec((1, S // t, DA_W, t), lambda b, i: (b, 0, 0, 0)), row(DA_W), row(MEM_W), row(D_MODEL),
                  _resident((1, MIX_W, D_MODEL), layer, 0, 0), full((1, D_MODEL)), full((1, D_MODEL)), full((1, DA_V))],
        out_specs=row(D_MODEL),
        out_shape=jax.ShapeDtypeStruct((B * S, D_MODEL), F32),
        scratch_shapes=[pltpu.VMEM((units, 1, t), F32), pltpu.VMEM((units, 1, t), F32),
                        pltpu.VMEM((units, DA_V + SUM_ROWS, t), F32),
                        pltpu.VMEM((units, t, t), F32), pltpu.VMEM((units, t, t), BF16),
                        pltpu.VMEM((rows, MIX_W), BF16)],
        compiler_params=_params("arbitrary", "arbitrary"),
        name="diff_prompt",
    )(lamv, q, k.reshape(B, S, DA_W), vt.reshape(B, S // t, DA_W, t), g, om, x, w_out, ln_g, ln_b, dng)


def _swa_prompt_kernel(tq, sinks_ref, q_ref, k_ref, kr_ref, vt_ref, vtr_ref, g_ref, om_ref, x_ref, wout_ref,
                       lng_ref, lnb_ref, y_ref, mix_ref):
    i = pl.program_id(1)
    w, nk, nsub = SWA_TILE, 2 * SWA_TILE, tq // SWA_TILE
    low = lax.broadcasted_iota(jnp.int32, (w, LANES), 1) < SW_HD
    row_low = lax.broadcasted_iota(jnp.int32, (LANES, w), 0) < SW_HD
    sink_row = lambda hs: jnp.concatenate([jnp.full((1, w), sinks_ref[h] * LOG2E, F32) for h in hs], axis=1)
    sinks_plain, sinks_swapped = sink_row((0, 2, 5, 7)), sink_row((1, 3, 4, 6))
    kc = lax.broadcasted_iota(jnp.int32, (nk, 4 * w), 0) // CHUNK
    qc = lax.broadcasted_iota(jnp.int32, (nk, 4 * w), 1) % w // CHUNK
    firsts, scores = [], []
    for st in range(nsub):
        sub = i * nsub + st
        first = jnp.maximum(sub - 1, 0)
        back = (sub - first) * (w // CHUNK)
        mask = (kc - back <= qc) & (kc - back >= qc - WINDOW // CHUNK)
        rows = pl.ds(pl.multiple_of(first * w, w), nk)
        q = q_ref[st * w:(st + 1) * w, :]
        slabs = [q[:, p * LANES:(p + 1) * LANES] for p in range(SW_HEADS // 2)]
        even = [jnp.where(low, sl, jnp.zeros_like(sl)) for sl in slabs]
        odd = [jnp.where(low, jnp.zeros_like(sl), sl) for sl in slabs]
        firsts.append(first)
        scores.append((_window_scores(jnp.concatenate([even[0], even[1], odd[2], odd[3]], axis=0), k_ref[0, rows, :],
                                      mask),
                       _window_scores(jnp.concatenate([odd[0], odd[1], even[2], even[3]], axis=0), kr_ref[0, rows, :],
                                      mask)))
    for st, (first, (s_plain, s_swapped)) in enumerate(zip(firsts, scores)):
        vts = jnp.concatenate([vt_ref[0, first], vt_ref[0, first + 1]], axis=1)
        vtrs = jnp.concatenate([vtr_ref[0, first], vtr_ref[0, first + 1]], axis=1)
        o_plain = _window_attend(s_plain, vts, sinks_plain)
        o_swapped = _window_attend(s_swapped, vtrs, sinks_swapped)
        for p in range(SW_HEADS // 2):
            cols = slice(p * w, (p + 1) * w)
            lo, hi = (o_plain, o_swapped) if p < 2 else (o_swapped, o_plain)
            slab = jnp.where(row_low, lo[:, cols], hi[:, cols]).T
            gate = g_ref[st * w:(st + 1) * w, p * LANES:(p + 1) * LANES].astype(F32)
            mix_ref[st * w:(st + 1) * w, p * LANES:(p + 1) * LANES] = (slab * gate).astype(BF16)
    mix_ref[:, SW_W:] = om_ref[...]
    y_ref[...] = _merge(x_ref[...], mix_ref[...], wout_ref[0], lng_ref[...], lnb_ref[...])


def _swa_prompt(sinks, q, k, kr, vt, vtr, g, om, x, w_out, layer, ln_g, ln_b, *, B, S, tq):
    nq = S // tq
    row = lambda w: pl.BlockSpec((tq, w), lambda b, i: (b * nq + i, 0))
    full = lambda shape: pl.BlockSpec(shape, lambda b, i: (0,) * len(shape))
    seq = pl.BlockSpec((1, S, LANES), lambda b, i: (b, 0, 0))
    seqt = pl.BlockSpec((1, S // SWA_TILE, LANES, SWA_TILE), lambda b, i: (b, 0, 0, 0))
    r3 = lambda t: t.reshape(B, S, LANES)
    r4 = lambda t: t.reshape(B, S // SWA_TILE, LANES, SWA_TILE)
    return pl.pallas_call(
        functools.partial(_swa_prompt_kernel, tq),
        grid=(B, nq),
        in_specs=[pl.BlockSpec(memory_space=pltpu.SMEM), row(SW_W), seq, seq, seqt, seqt, row(SW_W), row(MEM_W),
                  row(D_MODEL), _resident((1, MIX_W, D_MODEL), layer, 0, 0), full((1, D_MODEL)), full((1, D_MODEL))],
        out_specs=row(D_MODEL),
        out_shape=jax.ShapeDtypeStruct((B * S, D_MODEL), F32),
        scratch_shapes=[pltpu.VMEM((tq, MIX_W), BF16)],
        compiler_params=_params("arbitrary", "arbitrary"),
        name="swa_prompt",
    )(sinks, q, r3(k), r3(kr), r4(vt), r4(vtr), g, om, x, w_out, ln_g, ln_b)


def _sample_tail(b, last, o_first, g_ref, mq_ref, gm_ref, cmk_ref, cmv_ref, x_ref, wout_ref, lng_ref, lnb_ref,
                 y_ref, mix_ref, T):
    om = _mem_attend(mq_ref[...], _head_rows(cmk_ref, MEM_HEADS, 0, 0), _head_rows(cmv_ref, MEM_HEADS, 0, 0))
    om = om * gm_ref[...].astype(F32)
    mix = jnp.concatenate([(o_first * g_ref[...].astype(F32)).astype(BF16), om.astype(BF16)], axis=1)
    mix_ref[pl.ds(pl.multiple_of(b * T, T), T), :] = mix

    @pl.when(last)
    def _():
        y_ref[...] = _merge(x_ref[...], mix_ref[...], wout_ref[0], lng_ref[...], lnb_ref[...])


def _diff_sample_body(lam_init, T, nb, nj, lamv_ref, q_ref, kn_ref, vn_ref, ckt_ref, cv_ref, g_ref, mq_ref, gm_ref,
                      cmk_ref, cmv_ref, x_ref, wout_ref, lng_ref, lnb_ref, dng_ref, y_ref,
                      m_ref, l_ref, acc_ref, mix_ref):
    b, j = pl.program_id(0) // nj, pl.program_id(0) % nj
    heads = [slice(h * DA_V, (h + 1) * DA_V) for h in range(DA_HEADS)]
    qqs = [_split_components(q_ref[:, hs]) for hs in heads]

    @pl.when(j == 0)
    def _():
        m_ref[...] = jnp.full(m_ref.shape, -jnp.inf, F32)
        l_ref[...] = jnp.zeros(l_ref.shape, F32)
        acc_ref[...] = jnp.zeros(acc_ref.shape, F32)
        _drain(_flash_stages(qqs, lambda h: kn_ref[:, heads[h]], lambda h: vn_ref[:, heads[h]], False, m_ref, l_ref,
                             acc_ref))

    yield "stage"
    for _ in _flash_stages(qqs, lambda h: ckt_ref[0, heads[h], :].astype(BF16), _head_rows(cv_ref, DA_HEADS, 0), True,
                           m_ref, l_ref, acc_ref):
        yield "stage"
    yield "tail"

    @pl.when(j == nj - 1)
    def _():
        lam = _lambda(lamv_ref[...], lam_init)
        rows = [slice(h * 2 * T, (h + 1) * 2 * T) for h in range(DA_HEADS)]
        od = jnp.concatenate([_diff_finish(acc_ref[r, :], l_ref[r, :], lam, dng_ref[...], lam_init) for r in rows],
                             axis=1)
        _sample_tail(b, b == nb - 1, od, g_ref, mq_ref, gm_ref, cmk_ref, cmv_ref, x_ref, wout_ref, lng_ref,
                     lnb_ref, y_ref, mix_ref, T)


def _diff_sample_rider(lam_init, lamv, q, kn, vn, ckt, cv, g, mq, gm, cmk, cmv, layer, x, w_out, ln_g, ln_b, dng, *,
                       NB, T, tk):
    nj = ckt.shape[2] // tk
    row = lambda w: pl.BlockSpec((T, w), lambda i: (i // nj, 0))
    small = lambda shape: _resident(shape)
    memc = pl.BlockSpec((1, 1, N_MEM * MEM_HEADS, MEM_HD), lambda i: (layer, i // nj, 0, 0),
                        pipeline_mode=pl.Buffered(1))
    return dict(
        steps=NB * nj,
        in_specs=[small((4, DA_QK)), row(DA_W), row(DA_W), row(DA_W),
                  pl.BlockSpec((1, DA_W, tk), lambda i: (i // nj, 0, i % nj)),
                  pl.BlockSpec((1, tk * DA_HEADS, DA_V), lambda i: (i // nj, i % nj, 0)),
                  row(DA_W), row(MEM_W), row(MEM_W), memc, memc, small((NB * T, D_MODEL)),
                  _resident((1, MIX_W, D_MODEL), layer, 0, 0), small((1, D_MODEL)), small((1, D_MODEL)),
                  small((1, DA_V))],
        args=[lamv, q, kn, vn, ckt, cv, g, mq, gm, cmk, cmv, x, w_out, ln_g, ln_b, dng],
        out_specs=[pl.BlockSpec((NB * T, D_MODEL), lambda i: (0, 0))],
        out_shape=[jax.ShapeDtypeStruct((NB * T, D_MODEL), F32)],
        scratch=[pltpu.VMEM((DA_HEADS * 2 * T, 1), F32), pltpu.VMEM((DA_HEADS * 2 * T, 1), F32),
                 pltpu.VMEM((DA_HEADS * 2 * T, DA_V), F32), pltpu.VMEM((NB * T, MIX_W), BF16)],
        body=functools.partial(_diff_sample_body, lam_init, T, NB, nj),
    )


def _swa_sample_kernel(T, sinks_ref, q_ref, kn_ref, knr_ref, vn_ref, vnr_ref, ckt_ref, cvt_ref, g_ref, mq_ref, gm_ref,
                       cmk_ref, cmv_ref, x_ref, wout_ref, lng_ref, lnb_ref, y_ref, mix_ref):
    b = pl.program_id(0)
    swap = lambda t: jnp.concatenate([t[SW_HD:], t[:SW_HD]], axis=0).astype(BF16)
    ckt, cvt = ckt_ref[0], cvt_ref[0]
    kt, ktr, vt, vtr = ckt.astype(BF16), swap(ckt), cvt.astype(BF16), swap(cvt)
    kn, knr, vn, vnr = kn_ref[...], knr_ref[...], vn_ref[...], vnr_ref[...]
    attend = lambda qq, swapped, sink: _sink_attend_cached(
        qq, ktr if swapped else kt, knr if swapped else kn, vtr if swapped else vt, vnr if swapped else vn, sink)
    o = _swa_heads(q_ref[...], attend, lambda h: sinks_ref[h] * LOG2E)
    _sample_tail(b, b == pl.num_programs(0) - 1, o, g_ref, mq_ref, gm_ref, cmk_ref, cmv_ref, x_ref, wout_ref,
                 lng_ref, lnb_ref, y_ref, mix_ref, T)


def _swa_sample(sinks, q, kn, knr, vn, vnr, ckt, cvt, g, mq, gm, cmk, cmv, layer, x, w_out, ln_g, ln_b, *, NB, T):
    W = ckt.shape[2]
    row = lambda w: pl.BlockSpec((T, w), lambda b: (b, 0))
    full = lambda shape: pl.BlockSpec(shape, lambda b: (0,) * len(shape))
    cache = pl.BlockSpec((1, LANES, W), lambda b: (b, 0, 0))
    memc = pl.BlockSpec((1, 1, N_MEM * MEM_HEADS, MEM_HD), lambda b: (layer, b, 0, 0))
    return pl.pallas_call(
        functools.partial(_swa_sample_kernel, T),
        grid=(NB,),
        in_specs=[pl.BlockSpec(memory_space=pltpu.SMEM), row(SW_W), row(LANES), row(LANES), row(LANES), row(LANES),
                  cache, cache, row(SW_W), row(MEM_W), row(MEM_W), memc, memc, full((NB * T, D_MODEL)),
                  _resident((1, MIX_W, D_MODEL), layer, 0, 0), full((1, D_MODEL)), full((1, D_MODEL))],
        out_specs=full((NB * T, D_MODEL)),
        out_shape=jax.ShapeDtypeStruct((NB * T, D_MODEL), F32),
        scratch_shapes=[pltpu.VMEM((NB * T, MIX_W), BF16)],
        compiler_params=_params("arbitrary"),
        name="swa_sample",
    )(sinks, q, kn, knr, vn, vnr, ckt, cvt, g, mq, gm, cmk, cmv, x, w_out, ln_g, ln_b)


def _rope_tables(pos):
    T = pos.shape[0]
    inv = ROPE_THETA ** (-np.arange(ROPE_HALF, dtype=np.float64) / ROPE_HALF)
    ang = pos.astype(np.float64)[:, None] * inv[None, :]
    cos, sin = np.cos(ang), np.sin(ang)
    rest = SW_HD - 2 * ROPE_HALF
    c = np.concatenate([cos, cos, np.ones((T, rest))], axis=1)
    a = np.concatenate([-sin, np.zeros((T, SW_HD - ROPE_HALF))], axis=1)
    b = np.concatenate([np.zeros((T, ROPE_HALF)), sin, np.zeros((T, rest))], axis=1)
    return tuple(jnp.asarray(np.tile(t, (1, LANES // SW_HD)), F32) for t in (c, a, b))


def _feature_major(t):
    n = t.ndim
    return jnp.transpose(t, (0,) + tuple(range(2, n)) + (1,)).reshape(t.shape[0], -1, t.shape[1])


def _rows_major(t, tail):
    n = len(tail)
    return jnp.transpose(t.reshape((t.shape[0],) + tail + (t.shape[2],)), (0, n + 1) + tuple(range(1, n + 1)))


def kernel(x_prompt, x_sample, mem_prompt, cache_diff_k, cache_diff_v, cache_swa_k, cache_swa_v, cache_mem_k,
           cache_mem_v, w_in_a, lam_q1, lam_k1, lam_q2, lam_k2, diff_norm_g, w_in_b, sinks, w_kv_shared, w_mem_kv,
           w_out, ln_g, ln_b):
    B, S, _ = x_prompt.shape
    NB, T, _ = x_sample.shape
    P = cache_diff_k.shape[2]
    assert w_in_a.shape[0] == 1 and w_in_b.shape[0] == 1 and w_out.shape[0] == DEPTH

    tabs_p = _rope_tables(np.arange(S))
    tabs_s = _rope_tables(np.tile(P + np.arange(T), NB))
    xp = x_prompt.reshape(B * S, D_MODEL)
    xs = x_sample.reshape(NB * T, D_MODEL)
    wa, wb, wo = (w_in_a[0],), (w_in_b[0], w_kv_shared), w_out
    lng = ln_g.reshape(DEPTH, 1, D_MODEL)
    lnb = ln_b.reshape(DEPTH, 1, D_MODEL)
    lamv = jnp.concatenate([lam_q1, lam_k1, lam_q2, lam_k2], axis=0)
    dng = diff_norm_g.reshape(1, DA_V)
    lam_init = 0.8 - 0.6 * math.exp(-0.3 * 0)

    mkf, mvf, mkb, mvb = _mem_kv(mem_prompt.reshape(B * N_MEM, D_MODEL), w_mem_kv)
    mem_b = lambda l: (mkb[l].reshape(B, N_MEM, MEM_W), mvb[l].reshape(B, N_MEM, MEM_W))
    head_rows = lambda t: t.reshape(t.shape[:-3] + (t.shape[-3] * t.shape[-2], t.shape[-1]))
    cmk, cmv = head_rows(cache_mem_k), head_rows(cache_mem_v)

    q, kt, vf, kb, vb, g, om = _project("a", xp, wa, tabs_p, mem_b(0), tm=PROJ_ROWS, rows_per_batch=S)
    xp1 = _diff_prompt(lam_init, lamv, q, kb, vb, g, om, xp, wo, 0, lng[0], lnb[0], dng, B=B, S=S)
    qs, kfs, vfs, kbs, vbs, gs_, mqs, gms = _project("a", xs, wa, tabs_s, tm=NB * T)
    sample_a = _diff_sample_rider(lam_init, lamv, qs, kbs, vbs, _feature_major(cache_diff_k[0]),
                                  head_rows(cache_diff_v[0]), gs_, mqs, gms, cmk, cmv, 0, xs, wo, lng[0], lnb[0], dng,
                                  NB=NB, T=T, tk=SAMPLE_KEYS)

    q, g, om, skf, svf, kb1, kr1, vb1, vr1, xs1 = _project("b", xp1, wb, tabs_p, mem_b(1), tm=PROJ_B_ROWS,
                                                          rows_per_batch=S, rider=sample_a)
    yp = _swa_prompt(sinks[0], q, kb1, kr1, vb1, vr1, g, om, xp1, wo, 1, lng[1], lnb[1], B=B, S=S, tq=SWA_ROWS)
    qs, gs_, mqs, gms, skfs, svfs, kbs1, krs1, vbs1, vrs1 = _project("b", xs1, wb, tabs_s, tm=NB * T)
    ys = _swa_sample(sinks[0], qs, kbs1, krs1, vbs1, vrs1, _feature_major(cache_swa_k), _feature_major(cache_swa_v),
                     gs_, mqs, gms, cmk, cmv, 1, xs1, wo, lng[1], lnb[1], NB=NB, T=T)

    wr_p = min(WINDOW, S)
    window = lambda t: t.reshape(B, S, LANES)[:, S - wr_p:, :].reshape(B, wr_p, SW_KV, SW_HD)
    swa_kp, swa_vp = window(skf), window(svf)
    roll = lambda c, n: jnp.concatenate([c, n.reshape(NB, T, SW_KV, SW_HD)], axis=1)[:, T:]
    return (yp.reshape(B, S, D_MODEL), ys.reshape(NB, T, D_MODEL),
            _rows_major(kt, (DA_HEADS, 2, DA_QK))[None], vf.reshape(1, B, S, DA_HEADS, DA_V),
            kfs.reshape(1, NB, T, DA_HEADS, 2, DA_QK), vfs.reshape(1, NB, T, DA_HEADS, DA_V),
            swa_kp, swa_vp, roll(cache_swa_k, skfs), roll(cache_swa_v, svfs),
            mkf.reshape(DEPTH, B, N_MEM, MEM_HEADS, MEM_HD), mvf.reshape(DEPTH, B, N_MEM, MEM_HEADS, MEM_HD))
```

```python
import functools
import inspect
import math

import jax
import jax.numpy as jnp
import numpy as np
from jax import lax
from jax.experimental import pallas as pl
from jax.experimental.pallas import tpu as pltpu

D_MODEL = 1024
CHUNK = 64
N_MEM = 256
DA_HEADS = 4
DA_QK = 64
DA_V = 128
DA_W = 512
SW_HEADS = 8
SW_KV = 2
SW_HD = 64
SW_W = 512
WINDOW = 128
MEM_HEADS = 4
MEM_HD = 128
MEM_W = 512
MIX_W = 1024
ROPE_THETA = 500000.0
ROPE_HALF = 8
DEPTH = 2
DN_ALPHA = (2 * DEPTH) ** 0.25
LN_EPS = 1e-5
NEG = -1e30
LANES = 128
DIFF_TILE = 256
PIPE_LEAD = 4
DIFF_ROWS = 512
TRIP_BLOCKS = 4
SWA_TILE = 128
SUM_ROWS = 16
PROJ_ROWS = 1024
PROJ_B_ROWS = 512
MERGE_ROWS = 256
SWA_ROWS = 1024
SAMPLE_KEYS = 2048
LOG2E = math.log2(math.e)

F32 = jnp.float32
BF16 = jnp.bfloat16
VMEM_LIMIT = 50 * 1024 * 1024

_NT = (((1,), (1,)), ((), ()))


def _resident(shape, *first):
    idx = first or (0,) * len(shape)
    return pl.BlockSpec(shape, lambda *_: idx, pipeline_mode=pl.Buffered(1))


def _params(*sem):
    return pltpu.CompilerParams(dimension_semantics=sem, vmem_limit_bytes=VMEM_LIMIT)


def _rope(x, c, a, b):
    outs = []
    for i in range(x.shape[1] // LANES):
        blk = x[:, i * LANES:(i + 1) * LANES]
        outs.append(blk * c + pltpu.roll(blk, LANES - ROPE_HALF, 1) * a + pltpu.roll(blk, ROPE_HALF, 1) * b)
    return outs[0] if len(outs) == 1 else jnp.concatenate(outs, axis=1)


def _silu(g):
    return g * (1.0 / (1.0 + jnp.exp(-g)))


def _layer_norm(z, g, b):
    mu = jnp.mean(z, axis=-1, keepdims=True)
    d = z - mu
    var = jnp.mean(d * d, axis=-1, keepdims=True)
    return d * lax.rsqrt(var + LN_EPS) * g + b


def _mem_attend(mq, mk_of, mv_of):
    outs = []
    for h in range(MEM_HEADS):
        sl = slice(h * MEM_HD, (h + 1) * MEM_HD)
        s = lax.dot_general(mq[:, sl], mk_of(h), _NT, preferred_element_type=F32) * (MEM_HD ** -0.5)
        p = jnp.exp(s - jnp.max(s, axis=1, keepdims=True))
        l = jnp.sum(p, axis=1, keepdims=True)
        outs.append(jnp.dot(p.astype(BF16), mv_of(h), preferred_element_type=F32) / l)
    return jnp.concatenate(outs, axis=1)


def _head_cols(ref):
    return lambda h: ref[0, :, h * MEM_HD:(h + 1) * MEM_HD]


def _head_idx(ref, heads, h, *lead):
    return lead + (pl.ds(h, ref.shape[-2] // heads, stride=heads), slice(None))


def _head_rows(ref, heads, *lead):
    return lambda h: ref[_head_idx(ref, heads, h, *lead)].astype(BF16)


def _merge(x, mix, w_out, ln_g, ln_b):
    w = w_out.astype(BF16)
    n = max(x.shape[0] // MERGE_ROWS, 1)
    r = x.shape[0] // n
    z = lambda c: DN_ALPHA * x[c * r:(c + 1) * r] + jnp.dot(mix[c * r:(c + 1) * r], w, preferred_element_type=F32)
    outs, prev = [], z(0)
    for c in range(1, n):
        nxt = z(c)
        outs.append(_layer_norm(prev, ln_g, ln_b))
        prev = nxt
    outs.append(_layer_norm(prev, ln_g, ln_b))
    return outs[0] if n == 1 else jnp.concatenate(outs, axis=0)


def _lambda(lamv, lam_init):
    e1 = jnp.exp(jnp.sum(lamv[0:1, :] * lamv[1:2, :], axis=1, keepdims=True))
    e2 = jnp.exp(jnp.sum(lamv[2:3, :] * lamv[3:4, :], axis=1, keepdims=True))
    return e1 - e2 + lam_init


def _split_components(qh):
    lane = lax.broadcasted_iota(jnp.int32, qh.shape, 1)
    zero = jnp.zeros_like(qh)
    return jnp.concatenate([jnp.where(lane < DA_QK, qh, zero), jnp.where(lane >= DA_QK, qh, zero)], axis=0)


def _flash_stages(qqs, k_of, v_of, keys_on_lanes, m_ref, l_ref, acc_ref):
    if keys_on_lanes:
        s = [jnp.dot(qq, k_of(h), preferred_element_type=F32) for h, qq in enumerate(qqs)]
    else:
        s = [lax.dot_general(qq, k_of(h), _NT, preferred_element_type=F32) for h, qq in enumerate(qqs)]
    s = jnp.concatenate(s, axis=0)
    yield
    m_old = m_ref[...]
    m_new = jnp.maximum(m_old, jnp.max(s, axis=1, keepdims=True))
    alpha = jnp.exp2(m_old - m_new)
    p = jnp.exp2(s - m_new)
    l_ref[...] = alpha * l_ref[...] + jnp.sum(p, axis=1, keepdims=True)
    pb = p.astype(BF16)
    yield
    r = qqs[0].shape[0]
    pv = [jnp.dot(pb[h * r:(h + 1) * r], v_of(h), preferred_element_type=F32) for h in range(len(qqs))]
    acc_ref[...] = alpha * acc_ref[...] + jnp.concatenate(pv, axis=0)
    m_ref[...] = m_new


def _drain(stages):
    if inspect.isgenerator(stages):
        for _ in stages:
            pass


def _stage_scores(qq, ks, mask, s_ref, m_ref, alpha_ref):
    s = lax.dot_general(ks, qq, _NT, preferred_element_type=F32)
    if mask is not None:
        s = jnp.where(mask, s, NEG)
    s_ref[...] = s
    m_old = m_ref[...]
    m_new = jnp.maximum(m_old, jnp.max(s, axis=0, keepdims=True))
    alpha_ref[...] = jnp.exp2(m_old - m_new)
    m_ref[...] = m_new


def _stage_exp(s_ref, p_ref, m_ref):
    p_ref[...] = jnp.exp2(s_ref[...] - m_ref[...]).astype(BF16)


def _stage_pv(vts, p_ref, alpha_ref, acc_ref):
    ones = jnp.ones((SUM_ROWS, vts.shape[1]), BF16)
    pv = jnp.dot(jnp.concatenate([vts, ones], axis=0), p_ref[...], preferred_element_type=F32)
    acc_ref[...] = alpha_ref[...] * acc_ref[...] + pv


def _diff_finish(acc, l, lam, dng, lam_init):
    rows = acc.shape[0] // 2
    o = acc / l
    od = o[:rows] - lam * o[rows:]
    return od * lax.rsqrt(jnp.mean(od * od, axis=-1, keepdims=True) + LN_EPS) * dng * (1.0 - lam_init)


def _sink_attend_cached(qq, kt, kn, vt, vn, sink):
    s_c = jnp.dot(qq, kt, preferred_element_type=F32)
    s_n = lax.dot_general(qq, kn, _NT, preferred_element_type=F32)
    m = jnp.maximum(jnp.maximum(jnp.max(s_c, axis=1, keepdims=True), jnp.max(s_n, axis=1, keepdims=True)), sink)
    p_c, p_n = jnp.exp2(s_c - m), jnp.exp2(s_n - m)
    l = jnp.sum(p_c, axis=1, keepdims=True) + jnp.sum(p_n, axis=1, keepdims=True) + jnp.exp2(sink - m)
    o = (lax.dot_general(p_c.astype(BF16), vt, _NT, preferred_element_type=F32)
         + jnp.dot(p_n.astype(BF16), vn, preferred_element_type=F32))
    return o / l


def _window_scores(qq4, ks, mask):
    return jnp.where(mask, lax.dot_general(ks, qq4, _NT, preferred_element_type=F32), NEG)


def _window_attend(s, vts, sinks4):
    m = jnp.maximum(jnp.max(s, axis=0, keepdims=True), sinks4)
    p = jnp.exp2(s - m).astype(BF16)
    ones = jnp.ones((SUM_ROWS, vts.shape[1]), BF16)
    o = jnp.dot(jnp.concatenate([vts, ones], axis=0), p, preferred_element_type=F32)
    return o[:LANES] / (o[LANES:LANES + 1] + jnp.exp2(sinks4 - m))


def _swa_heads(q, attend, sink_of):
    low = lax.broadcasted_iota(jnp.int32, (q.shape[0], LANES), 1) < SW_HD
    outs = []
    for pair in range(SW_HEADS // 2):
        kv = pair // 2
        slab = q[:, pair * LANES:(pair + 1) * LANES]
        zero = jnp.zeros_like(slab)
        o_even = attend(jnp.where(low, slab, zero), kv == 1, sink_of(2 * pair))
        o_odd = attend(jnp.where(low, zero, slab), kv == 0, sink_of(2 * pair + 1))
        outs.append(jnp.where(low, o_even, o_odd))
    return jnp.concatenate(outs, axis=1)


def _mem_kv_kernel(mem_ref, w_ref, kf_ref, vf_ref, kb_ref, vb_ref):
    h = jnp.dot(mem_ref[...].astype(BF16), w_ref[0].astype(BF16), preferred_element_type=F32)
    k, v = h[:, :MEM_W], h[:, MEM_W:]
    for hd in range(MEM_HEADS):
        kf_ref[_head_idx(kf_ref, MEM_HEADS, hd, 0)] = k[:, hd * MEM_HD:(hd + 1) * MEM_HD]
        vf_ref[_head_idx(vf_ref, MEM_HEADS, hd, 0)] = v[:, hd * MEM_HD:(hd + 1) * MEM_HD]
    kb_ref[0] = k.astype(BF16)
    vb_ref[0] = v.astype(BF16)


def _mem_kv(mem, w):
    rows = mem.shape[0]
    fshape, fspec = (DEPTH, rows * MEM_HEADS, MEM_HD), pl.BlockSpec((1, rows * MEM_HEADS, MEM_HD), lambda l: (l, 0, 0))
    bshape, bspec = (DEPTH, rows, MEM_W), pl.BlockSpec((1, rows, MEM_W), lambda l: (l, 0, 0))
    return pl.pallas_call(
        _mem_kv_kernel,
        grid=(DEPTH,),
        in_specs=[pl.BlockSpec((rows, D_MODEL), lambda l: (0, 0)),
                  pl.BlockSpec((1, D_MODEL, 2 * MEM_W), lambda l: (l, 0, 0))],
        out_specs=[fspec, fspec, bspec, bspec],
        out_shape=[jax.ShapeDtypeStruct(fshape, F32), jax.ShapeDtypeStruct(fshape, F32),
                   jax.ShapeDtypeStruct(bshape, BF16), jax.ShapeDtypeStruct(bshape, BF16)],
        compiler_params=_params("arbitrary"),
        name="mem_kv",
    )(mem, w)


def _proj_a_kernel(fuse_mem, x_ref, w_ref, c_ref, a_ref, b_ref, *refs):
    if fuse_mem:
        mk_ref, mv_ref, q_ref, kf_ref, vf_ref, kb_ref, vb_ref, g_ref, om_ref = refs
    else:
        q_ref, kf_ref, vf_ref, kb_ref, vb_ref, g_ref, mq_ref, gm_ref = refs
    x = x_ref[...].astype(BF16)
    c, a, b = c_ref[...], a_ref[...], b_ref[...]

    def cols(i):
        return jnp.dot(x, w_ref[:, i * DA_W:(i + 1) * DA_W].astype(BF16), preferred_element_type=F32)

    q_ref[...] = (_rope(cols(0), c, a, b) * (DA_QK ** -0.5 * LOG2E)).astype(BF16)
    k = _rope(cols(1), c, a, b)
    kb_ref[...] = k.astype(BF16)
    v = cols(2)
    for hd in range(DA_HEADS):
        vf_ref[_head_idx(vf_ref, DA_HEADS, hd)] = v[:, hd * DA_V:(hd + 1) * DA_V]
    if fuse_mem:
        kf_ref[0] = k.T
        for blk in range(vb_ref.shape[0]):
            vb_ref[blk] = v[blk * DIFF_TILE:(blk + 1) * DIFF_TILE, :].T.astype(BF16)
    else:
        kf_ref[...] = k
        vb_ref[...] = v.astype(BF16)
    g_ref[...] = _silu(cols(3)).astype(BF16)
    mq = cols(4).astype(BF16)
    gm = _silu(cols(5))
    if fuse_mem:
        om_ref[...] = (_mem_attend(mq, _head_cols(mk_ref), _head_cols(mv_ref)) * gm).astype(BF16)
    else:
        mq_ref[...] = mq
        gm_ref[...] = gm.astype(BF16)


def _proj_b_kernel(fuse_mem, x_ref, w_ref, wkv_ref, c_ref, a_ref, b_ref, *refs):
    if fuse_mem:
        mk_ref, mv_ref, q_ref, g_ref, om_ref, kf_ref, vf_ref, kb_ref, kr_ref, vb_ref, vr_ref = refs
    else:
        q_ref, g_ref, mq_ref, gm_ref, kf_ref, vf_ref, kb_ref, kr_ref, vb_ref, vr_ref = refs
    x = x_ref[...].astype(BF16)
    c, a, b = c_ref[...], a_ref[...], b_ref[...]

    def cols(i):
        return jnp.dot(x, w_ref[:, i * SW_W:(i + 1) * SW_W].astype(BF16), preferred_element_type=F32)

    q_ref[...] = (_rope(cols(0), c, a, b) * (SW_HD ** -0.5 * LOG2E)).astype(BF16)
    yield
    g_ref[...] = _silu(cols(1)).astype(BF16)
    yield
    mq = cols(2).astype(BF16)
    gm = _silu(cols(3))
    if fuse_mem:
        om_ref[...] = (_mem_attend(mq, _head_cols(mk_ref), _head_cols(mv_ref)) * gm).astype(BF16)
    else:
        mq_ref[...] = mq
        gm_ref[...] = gm.astype(BF16)
    yield
    kv = jnp.dot(x, wkv_ref[...].astype(BF16), preferred_element_type=F32)
    k = _rope(kv[:, :LANES], c, a, b)
    v = kv[:, LANES:]
    kf_ref[...] = k
    vf_ref[...] = v
    kb_ref[...] = k.astype(BF16)
    kr_ref[...] = pltpu.roll(k, SW_HD, 1).astype(BF16)
    vr = pltpu.roll(v, SW_HD, 1)
    if fuse_mem:
        for blk in range(vb_ref.shape[0]):
            rows = slice(blk * SWA_TILE, (blk + 1) * SWA_TILE)
            vb_ref[blk] = v[rows, :].T.astype(BF16)
            vr_ref[blk] = vr[rows, :].T.astype(BF16)
    else:
        vb_ref[...] = v.astype(BF16)
        vr_ref[...] = vr.astype(BF16)


def _with_rider(host_body, n_in, n_out, rider, *refs):
    r_in, r_out = len(rider["args"]), len(rider["out_specs"])
    o0 = n_in + r_in
    ride = rider["body"](*refs[n_in:o0], *refs[o0 + n_out:])
    mark = next(ride, None)
    host = host_body(*refs[:n_in], *refs[o0:o0 + n_out])
    if inspect.isgenerator(host):
        for _ in host:
            if mark == "stage":
                mark = next(ride, None)
    _drain(ride)


def _project(layer, x, ws, tabs, mem=None, *, tm, rows_per_batch=None, rider=None):
    M = x.shape[0]
    fuse = mem is not None
    per = (rows_per_batch or M) // tm
    row = lambda w: pl.BlockSpec((tm, w), lambda i: (i, 0))
    tab = pl.BlockSpec((tm, LANES), lambda i: (i % per, 0))
    in_specs = [row(D_MODEL)] + [_resident(w.shape) for w in ws] + [tab, tab, tab]
    args = [x, *ws, *tabs]
    if fuse:
        mspec = pl.BlockSpec((1, N_MEM, MEM_W), lambda i: (i // per, 0, 0))
        in_specs += [mspec, mspec]
        args += list(mem)
    o = lambda w, dt: (row(w), jax.ShapeDtypeStruct((M, w), dt))
    if layer == "a":
        vt = (pl.BlockSpec((tm // DIFF_TILE, DA_W, DIFF_TILE), lambda i: (i, 0, 0)),
              jax.ShapeDtypeStruct((M // DIFF_TILE, DA_W, DIFF_TILE), BF16))
        kt = (pl.BlockSpec((1, DA_W, tm), lambda i: (i // per, 0, i % per)),
              jax.ShapeDtypeStruct((M // (per * tm), DA_W, per * tm), F32))
        vf = (pl.BlockSpec((tm * DA_HEADS, DA_V), lambda i: (i, 0)), jax.ShapeDtypeStruct((M * DA_HEADS, DA_V), F32))
        outs = [o(DA_W, BF16), kt if fuse else o(DA_W, F32), vf, o(DA_W, BF16), vt if fuse else o(DA_W, BF16),
                o(DA_W, BF16)]
        body = _proj_a_kernel
    else:
        outs = [o(SW_W, BF16), o(SW_W, BF16)]
        body = _proj_b_kernel
    mem_outs = [o(MEM_W, BF16)] if fuse else [o(MEM_W, BF16), o(MEM_W, BF16)]
    if layer == "a":
        outs = outs + mem_outs
    else:
        vt = (pl.BlockSpec((tm // SWA_TILE, LANES, SWA_TILE), lambda i: (i, 0, 0)),
              jax.ShapeDtypeStruct((M // SWA_TILE, LANES, SWA_TILE), BF16))
        values = [vt, vt] if fuse else [o(LANES, BF16)] * 2
        outs = outs + mem_outs + [o(LANES, F32), o(LANES, F32)] + [o(LANES, BF16)] * 2 + values
    kernel_body = functools.partial(lambda run, *refs: _drain(run(*refs)), functools.partial(body, fuse))
    out_specs, out_shape, scratch = [s for s, _ in outs], [t for _, t in outs], []
    if rider is not None:
        assert rider["steps"] == M // tm
        kernel_body = functools.partial(_with_rider, functools.partial(body, fuse), len(args), len(outs), rider)
        in_specs, args = in_specs + rider["in_specs"], args + rider["args"]
        out_specs, out_shape, scratch = out_specs + rider["out_specs"], out_shape + rider["out_shape"], rider["scratch"]
    return pl.pallas_call(
        kernel_body,
        grid=(M // tm,),
        in_specs=in_specs,
        out_specs=out_specs,
        out_shape=out_shape,
        scratch_shapes=scratch,
        compiler_params=_params("arbitrary"),
        name="proj_" + layer,
    )(*args)


def _diff_prompt_kernel(lam_init, lamv_ref, q_ref, k_ref, vt_ref, g_ref, om_ref, x_ref, wout_ref,
                        lng_ref, lnb_ref, dng_ref, y_ref, m_ref, alpha_ref, acc_ref, s_ref, p_ref, mix_ref):
    t = DIFF_TILE
    nsub = q_ref.shape[0] // t
    LEAD = PIPE_LEAD
    i = pl.program_id(1)
    heads = [slice(h * DA_V, (h + 1) * DA_V) for h in range(DA_HEADS)]
    lane = lax.broadcasted_iota(jnp.int32, (t, DA_V), 1)
    units = []
    for sub in range(nsub):
        for hs in heads:
            qh = q_ref[sub * t:(sub + 1) * t, hs]
            zero = jnp.zeros_like(qh)
            units += [(sub, hs, jnp.where(lane < DA_QK, qh, zero)), (sub, hs, jnp.where(lane >= DA_QK, qh, zero))]
    n = len(units)
    per = n // nsub
    assert n == s_ref.shape[0] and LEAD + 1 < per
    m_ref[...] = jnp.full(m_ref.shape, -jnp.inf, F32)
    acc_ref[...] = jnp.zeros(acc_ref.shape, F32)

    def exp_stage(u):
        _stage_exp(s_ref.at[u], p_ref.at[u], m_ref.at[u])

    def pv_stage(j, u):
        _stage_pv(vt_ref[0, j, units[u][1], :], p_ref.at[u], alpha_ref.at[u], acc_ref.at[u])

    def block(j, mask_of, j_before, first_unit=0):
        rows = pl.ds(pl.multiple_of(j * t, t), t)
        for idx, u in enumerate(range(first_unit, n)):
            sub, hs, qq = units[u]
            _stage_scores(qq, k_ref[0, rows, hs], mask_of(sub), s_ref.at[u], m_ref.at[u], alpha_ref.at[u])
            if idx >= LEAD:
                exp_stage(u - LEAD)
            elif j_before is not None:
                exp_stage(n - LEAD + idx)
            if idx >= LEAD + 1:
                pv_stage(j, u - LEAD - 1)
            elif j_before is not None:
                pv_stage(j_before, n - LEAD - 1 + idx)

    kc = lax.broadcasted_iota(jnp.int32, (t, t), 0) // CHUNK
    qc = lax.broadcasted_iota(jnp.int32, (t, t), 1) // CHUNK
    diag_mask = kc <= qc
    diag0 = nsub * i
    for sd in range(nsub):
        block(diag0 + sd, lambda sub, sd=sd: diag_mask if sub == sd else None, None if sd == 0 else diag0 + sd - 1,
              first_unit=sd * per)

    def run(j0, count):
        block(j0, lambda sub: None, jnp.where(j0 == 0, diag0 + nsub - 1, j0 - 1))
        for d in range(1, count):
            block(j0 + d, lambda sub: None, j0 + d - 1)

    def trip(jj, carry):
        run(TRIP_BLOCKS * jj, TRIP_BLOCKS)
        return carry

    plain = nsub * i
    lax.fori_loop(0, plain // TRIP_BLOCKS, trip, 0)

    @pl.when(plain % TRIP_BLOCKS != 0)
    def _():
        run(plain // TRIP_BLOCKS * TRIP_BLOCKS, nsub)

    last = jnp.where(i == 0, diag0 + nsub - 1, diag0 - 1)
    for u in range(n - LEAD, n):
        exp_stage(u)
        pv_stage(last, u - 1)
    pv_stage(last, n - 1)

    lam = _lambda(lamv_ref[...], lam_init)
    for sub in range(nsub):
        rows = slice(sub * t, (sub + 1) * t)
        for h, hs in enumerate(heads):
            u = sub * per + 2 * h
            o1 = acc_ref[u, :DA_V, :] / acc_ref[u, DA_V:DA_V + 1, :]
            o2 = acc_ref[u + 1, :DA_V, :] / acc_ref[u + 1, DA_V:DA_V + 1, :]
            od = (o1 - lam * o2).T
            od = (od * lax.rsqrt(jnp.mean(od * od, axis=-1, keepdims=True) + LN_EPS) * dng_ref[...]
                  * (1.0 - lam_init))
            mix_ref[rows, hs] = (od * g_ref[rows, hs].astype(F32)).astype(BF16)
    mix_ref[:, DA_W:] = om_ref[...]
    y_ref[...] = _merge(x_ref[...], mix_ref[...], wout_ref[0], lng_ref[...], lnb_ref[...])


def _diff_prompt(lam_init, lamv, q, k, vt, g, om, x, w_out, layer, ln_g, ln_b, dng, *, B, S):
    t, rows = DIFF_TILE, DIFF_ROWS
    nq, units = S // rows, 2 * DA_HEADS * (rows // t)
    row = lambda w: pl.BlockSpec((rows, w), lambda b, i: (b * nq + i, 0))
    full = lambda shape: pl.BlockSpec(shape, lambda b, i: (0,) * len(shape))
    return pl.pallas_call(
        functools.partial(_diff_prompt_kernel, lam_init),
        grid=(B, nq),
        in_specs=[full((4, DA_QK)), row(DA_W), pl.BlockSpec((1, S, DA_W), lambda b, i: (b, 0, 0)),
                  pl.BlockSpec((1, S // t, DA_W, t), lambda b, i: (b, 0, 0, 0)), row(DA_W), row(MEM_W), row(D_MODEL),
                  _resident((1, MIX_W, D_MODEL), layer, 0, 0), full((1, D_MODEL)), full((1, D_MODEL)), full((1, DA_V))],
        out_specs=row(D_MODEL),
        out_shape=jax.ShapeDtypeStruct((B * S, D_MODEL), F32),
        scratch_shapes=[pltpu.VMEM((units, 1, t), F32), pltpu.VMEM((units, 1, t), F32),
                        pltpu.VMEM((units, DA_V + SUM_ROWS, t), F32),
                        pltpu.VMEM((units, t, t), F32), pltpu.VMEM((units, t, t), BF16),
                        pltpu.VMEM((rows, MIX_W), BF16)],
        compiler_params=_params("arbitrary", "arbitrary"),
        name="diff_prompt",
    )(lamv, q, k.reshape(B, S, DA_W), vt.reshape(B, S // t, DA_W, t), g, om, x, w_out, ln_g, ln_b, dng)


def _swa_prompt_kernel(tq, sinks_ref, q_ref, k_ref, kr_ref, vt_ref, vtr_ref, g_ref, om_ref, x_ref, wout_ref,
                       lng_ref, lnb_ref, y_ref, mix_ref):
    i = pl.program_id(1)
    w, nk, nsub = SWA_TILE, 2 * SWA_TILE, tq // SWA_TILE
    low = lax.broadcasted_iota(jnp.int32, (w, LANES), 1) < SW_HD
    row_low = lax.broadcasted_iota(jnp.int32, (LANES, w), 0) < SW_HD
    sink_row = lambda hs: jnp.concatenate([jnp.full((1, w), sinks_ref[h] * LOG2E, F32) for h in hs], axis=1)
    sinks_plain, sinks_swapped = sink_row((0, 2, 5, 7)), sink_row((1, 3, 4, 6))
    kc = lax.broadcasted_iota(jnp.int32, (nk, 4 * w), 0) // CHUNK
    qc = lax.broadcasted_iota(jnp.int32, (nk, 4 * w), 1) % w // CHUNK
    firsts, scores = [], []
    for st in range(nsub):
        sub = i * nsub + st
        first = jnp.maximum(sub - 1, 0)
        back = (sub - first) * (w // CHUNK)
        mask = (kc - back <= qc) & (kc - back >= qc - WINDOW // CHUNK)
        rows = pl.ds(pl.multiple_of(first * w, w), nk)
        q = q_ref[st * w:(st + 1) * w, :]
        slabs = [q[:, p * LANES:(p + 1) * LANES] for p in range(SW_HEADS // 2)]
        even = [jnp.where(low, sl, jnp.zeros_like(sl)) for sl in slabs]
        odd = [jnp.where(low, jnp.zeros_like(sl), sl) for sl in slabs]
        firsts.append(first)
        scores.append((_window_scores(jnp.concatenate([even[0], even[1], odd[2], odd[3]], axis=0), k_ref[0, rows, :],
                                      mask),
                       _window_scores(jnp.concatenate([odd[0], odd[1], even[2], even[3]], axis=0), kr_ref[0, rows, :],
                                      mask)))
    for st, (first, (s_plain, s_swapped)) in enumerate(zip(firsts, scores)):
        vts = jnp.concatenate([vt_ref[0, first], vt_ref[0, first + 1]], axis=1)
        vtrs = jnp.concatenate([vtr_ref[0, first], vtr_ref[0, first + 1]], axis=1)
        o_plain = _window_attend(s_plain, vts, sinks_plain)
        o_swapped = _window_attend(s_swapped, vtrs, sinks_swapped)
        for p in range(SW_HEADS // 2):
            cols = slice(p * w, (p + 1) * w)
            lo, hi = (o_plain, o_swapped) if p < 2 else (o_swapped, o_plain)
            slab = jnp.where(row_low, lo[:, cols], hi[:, cols]).T
            gate = g_ref[st * w:(st + 1) * w, p * LANES:(p + 1) * LANES].astype(F32)
            mix_ref[st * w:(st + 1) * w, p * LANES:(p + 1) * LANES] = (slab * gate).astype(BF16)
    mix_ref[:, SW_W:] = om_ref[...]
    y_ref[...] = _merge(x_ref[...], mix_ref[...], wout_ref[0], lng_ref[...], lnb_ref[...])


def _swa_prompt(sinks, q, k, kr, vt, vtr, g, om, x, w_out, layer, ln_g, ln_b, *, B, S, tq):
    nq = S // tq
    row = lambda w: pl.BlockSpec((tq, w), lambda b, i: (b * nq + i, 0))
    full = lambda shape: pl.BlockSpec(shape, lambda b, i: (0,) * len(shape))
    seq = pl.BlockSpec((1, S, LANES), lambda b, i: (b, 0, 0))
    seqt = pl.BlockSpec((1, S // SWA_TILE, LANES, SWA_TILE), lambda b, i: (b, 0, 0, 0))
    r3 = lambda t: t.reshape(B, S, LANES)
    r4 = lambda t: t.reshape(B, S // SWA_TILE, LANES, SWA_TILE)
    return pl.pallas_call(
        functools.partial(_swa_prompt_kernel, tq),
        grid=(B, nq),
        in_specs=[pl.BlockSpec(memory_space=pltpu.SMEM), row(SW_W), seq, seq, seqt, seqt, row(SW_W), row(MEM_W),
                  row(D_MODEL), _resident((1, MIX_W, D_MODEL), layer, 0, 0), full((1, D_MODEL)), full((1, D_MODEL))],
        out_specs=row(D_MODEL),
        out_shape=jax.ShapeDtypeStruct((B * S, D_MODEL), F32),
        scratch_shapes=[pltpu.VMEM((tq, MIX_W), BF16)],
        compiler_params=_params("arbitrary", "arbitrary"),
        name="swa_prompt",
    )(sinks, q, r3(k), r3(kr), r4(vt), r4(vtr), g, om, x, w_out, ln_g, ln_b)


def _sample_tail(b, last, o_first, g_ref, mq_ref, gm_ref, cmk_ref, cmv_ref, x_ref, wout_ref, lng_ref, lnb_ref,
                 y_ref, mix_ref, T):
    om = _mem_attend(mq_ref[...], _head_rows(cmk_ref, MEM_HEADS, 0, 0), _head_rows(cmv_ref, MEM_HEADS, 0, 0))
    om = om * gm_ref[...].astype(F32)
    mix = jnp.concatenate([(o_first * g_ref[...].astype(F32)).astype(BF16), om.astype(BF16)], axis=1)
    mix_ref[pl.ds(pl.multiple_of(b * T, T), T), :] = mix

    @pl.when(last)
    def _():
        y_ref[...] = _merge(x_ref[...], mix_ref[...], wout_ref[0], lng_ref[...], lnb_ref[...])


def _diff_sample_body(lam_init, T, nb, nj, lamv_ref, q_ref, kn_ref, vn_ref, ckt_ref, cv_ref, g_ref, mq_ref, gm_ref,
                      cmk_ref, cmv_ref, x_ref, wout_ref, lng_ref, lnb_ref, dng_ref, y_ref,
                      m_ref, l_ref, acc_ref, mix_ref):
    b, j = pl.program_id(0) // nj, pl.program_id(0) % nj
    heads = [slice(h * DA_V, (h + 1) * DA_V) for h in range(DA_HEADS)]
    qqs = [_split_components(q_ref[:, hs]) for hs in heads]

    @pl.when(j == 0)
    def _():
        m_ref[...] = jnp.full(m_ref.shape, -jnp.inf, F32)
        l_ref[...] = jnp.zeros(l_ref.shape, F32)
        acc_ref[...] = jnp.zeros(acc_ref.shape, F32)
        _drain(_flash_stages(qqs, lambda h: kn_ref[:, heads[h]], lambda h: vn_ref[:, heads[h]], False, m_ref, l_ref,
                             acc_ref))

    yield "stage"
    for _ in _flash_stages(qqs, lambda h: ckt_ref[0, heads[h], :].astype(BF16), _head_rows(cv_ref, DA_HEADS, 0), True,
                           m_ref, l_ref, acc_ref):
        yield "stage"
    yield "tail"

    @pl.when(j == nj - 1)
    def _():
        lam = _lambda(lamv_ref[...], lam_init)
        rows = [slice(h * 2 * T, (h + 1) * 2 * T) for h in range(DA_HEADS)]
        od = jnp.concatenate([_diff_finish(acc_ref[r, :], l_ref[r, :], lam, dng_ref[...], lam_init) for r in rows],
                             axis=1)
        _sample_tail(b, b == nb - 1, od, g_ref, mq_ref, gm_ref, cmk_ref, cmv_ref, x_ref, wout_ref, lng_ref,
                     lnb_ref, y_ref, mix_ref, T)


def _diff_sample_rider(lam_init, lamv, q, kn, vn, ckt, cv, g, mq, gm, cmk, cmv, layer, x, w_out, ln_g, ln_b, dng, *,
                       NB, T, tk):
    nj = ckt.shape[2] // tk
    row = lambda w: pl.BlockSpec((T, w), lambda i: (i // nj, 0))
    small = lambda shape: _resident(shape)
    memc = pl.BlockSpec((1, 1, N_MEM * MEM_HEADS, MEM_HD), lambda i: (layer, i // nj, 0, 0))
    return dict(
        steps=NB * nj,
        in_specs=[small((4, DA_QK)), row(DA_W), row(DA_W), row(DA_W),
                  pl.BlockSpec((1, DA_W, tk), lambda i: (i // nj, 0, i % nj)),
                  pl.BlockSpec((1, tk * DA_HEADS, DA_V), lambda i: (i // nj, i % nj, 0)),
                  row(DA_W), row(MEM_W), row(MEM_W), memc, memc, small((NB * T, D_MODEL)),
                  _resident((1, MIX_W, D_MODEL), layer, 0, 0), small((1, D_MODEL)), small((1, D_MODEL)),
                  small((1, DA_V))],
        args=[lamv, q, kn, vn, ckt, cv, g, mq, gm, cmk, cmv, x, w_out, ln_g, ln_b, dng],
        out_specs=[pl.BlockSpec((NB * T, D_MODEL), lambda i: (0, 0))],
        out_shape=[jax.ShapeDtypeStruct((NB * T, D_MODEL), F32)],
        scratch=[pltpu.VMEM((DA_HEADS * 2 * T, 1), F32), pltpu.VMEM((DA_HEADS * 2 * T, 1), F32),
                 pltpu.VMEM((DA_HEADS * 2 * T, DA_V), F32), pltpu.VMEM((NB * T, MIX_W), BF16)],
        body=functools.partial(_diff_sample_body, lam_init, T, NB, nj),
    )


def _swa_sample_kernel(T, sinks_ref, q_ref, kn_ref, knr_ref, vn_ref, vnr_ref, ckt_ref, cvt_ref, g_ref, mq_ref, gm_ref,
                       cmk_ref, cmv_ref, x_ref, wout_ref, lng_ref, lnb_ref, y_ref, mix_ref):
    b = pl.program_id(0)
    swap = lambda t: jnp.concatenate([t[SW_HD:], t[:SW_HD]], axis=0).astype(BF16)
    ckt, cvt = ckt_ref[0], cvt_ref[0]
    kt, ktr, vt, vtr = ckt.astype(BF16), swap(ckt), cvt.astype(BF16), swap(cvt)
    kn, knr, vn, vnr = kn_ref[...], knr_ref[...], vn_ref[...], vnr_ref[...]
    attend = lambda qq, swapped, sink: _sink_attend_cached(
        qq, ktr if swapped else kt, knr if swapped else kn, vtr if swapped else vt, vnr if swapped else vn, sink)
    o = _swa_heads(q_ref[...], attend, lambda h: sinks_ref[h] * LOG2E)
    _sample_tail(b, b == pl.num_programs(0) - 1, o, g_ref, mq_ref, gm_ref, cmk_ref, cmv_ref, x_ref, wout_ref,
                 lng_ref, lnb_ref, y_ref, mix_ref, T)


def _swa_sample(sinks, q, kn, knr, vn, vnr, ckt, cvt, g, mq, gm, cmk, cmv, layer, x, w_out, ln_g, ln_b, *, NB, T):
    W = ckt.shape[2]
    row = lambda w: pl.BlockSpec((T, w), lambda b: (b, 0))
    full = lambda shape: pl.BlockSpec(shape, lambda b: (0,) * len(shape))
    cache = pl.BlockSpec((1, LANES, W), lambda b: (b, 0, 0))
    memc = pl.BlockSpec((1, 1, N_MEM * MEM_HEADS, MEM_HD), lambda b: (layer, b, 0, 0))
    return pl.pallas_call(
        functools.partial(_swa_sample_kernel, T),
        grid=(NB,),
        in_specs=[pl.BlockSpec(memory_space=pltpu.SMEM), row(SW_W), row(LANES), row(LANES), row(LANES), row(LANES),
                  cache, cache, row(SW_W), row(MEM_W), row(MEM_W), memc, memc, full((NB * T, D_MODEL)),
                  _resident((1, MIX_W, D_MODEL), layer, 0, 0), full((1, D_MODEL)), full((1, D_MODEL))],
        out_specs=full((NB * T, D_MODEL)),
        out_shape=jax.ShapeDtypeStruct((NB * T, D_MODEL), F32),
        scratch_shapes=[pltpu.VMEM((NB * T, MIX_W), BF16)],
        compiler_params=_params("arbitrary"),
        name="swa_sample",
    )(sinks, q, kn, knr, vn, vnr, ckt, cvt, g, mq, gm, cmk, cmv, x, w_out, ln_g, ln_b)


def _rope_tables(pos):
    T = pos.shape[0]
    inv = ROPE_THETA ** (-np.arange(ROPE_HALF, dtype=np.float64) / ROPE_HALF)
    ang = pos.astype(np.float64)[:, None] * inv[None, :]
    cos, sin = np.cos(ang), np.sin(ang)
    rest = SW_HD - 2 * ROPE_HALF
    c = np.concatenate([cos, cos, np.ones((T, rest))], axis=1)
    a = np.concatenate([-sin, np.zeros((T, SW_HD - ROPE_HALF))], axis=1)
    b = np.concatenate([np.zeros((T, ROPE_HALF)), sin, np.zeros((T, rest))], axis=1)
    return tuple(jnp.asarray(np.tile(t, (1, LANES // SW_HD)), F32) for t in (c, a, b))


def _feature_major(t):
    n = t.ndim
    return jnp.transpose(t, (0,) + tuple(range(2, n)) + (1,)).reshape(t.shape[0], -1, t.shape[1])


def _rows_major(t, tail):
    n = len(tail)
    return jnp.transpose(t.reshape((t.shape[0],) + tail + (t.shape[2],)), (0, n + 1) + tuple(range(1, n + 1)))


def kernel(x_prompt, x_sample, mem_prompt, cache_diff_k, cache_diff_v, cache_swa_k, cache_swa_v, cache_mem_k,
           cache_mem_v, w_in_a, lam_q1, lam_k1, lam_q2, lam_k2, diff_norm_g, w_in_b, sinks, w_kv_shared, w_mem_kv,
           w_out, ln_g, ln_b):
    B, S, _ = x_prompt.shape
    NB, T, _ = x_sample.shape
    P = cache_diff_k.shape[2]
    assert w_in_a.shape[0] == 1 and w_in_b.shape[0] == 1 and w_out.shape[0] == DEPTH

    tabs_p = _rope_tables(np.arange(S))
    tabs_s = _rope_tables(np.tile(P + np.arange(T), NB))
    xp = x_prompt.reshape(B * S, D_MODEL)
    xs = x_sample.reshape(NB * T, D_MODEL)
    wa, wb, wo = (w_in_a[0],), (w_in_b[0], w_kv_shared), w_out
    lng = ln_g.reshape(DEPTH, 1, D_MODEL)
    lnb = ln_b.reshape(DEPTH, 1, D_MODEL)
    lamv = jnp.concatenate([lam_q1, lam_k1, lam_q2, lam_k2], axis=0)
    dng = diff_norm_g.reshape(1, DA_V)
    lam_init = 0.8 - 0.6 * math.exp(-0.3 * 0)

    mkf, mvf, mkb, mvb = _mem_kv(mem_prompt.reshape(B * N_MEM, D_MODEL), w_mem_kv)
    mem_b = lambda l: (mkb[l].reshape(B, N_MEM, MEM_W), mvb[l].reshape(B, N_MEM, MEM_W))
    head_rows = lambda t: t.reshape(t.shape[:-3] + (t.shape[-3] * t.shape[-2], t.shape[-1]))
    cmk, cmv = head_rows(cache_mem_k), head_rows(cache_mem_v)

    q, kt, vf, kb, vb, g, om = _project("a", xp, wa, tabs_p, mem_b(0), tm=PROJ_ROWS, rows_per_batch=S)
    xp1 = _diff_prompt(lam_init, lamv, q, kb, vb, g, om, xp, wo, 0, lng[0], lnb[0], dng, B=B, S=S)
    qs, kfs, vfs, kbs, vbs, gs_, mqs, gms = _project("a", xs, wa, tabs_s, tm=NB * T)
    sample_a = _diff_sample_rider(lam_init, lamv, qs, kbs, vbs, _feature_major(cache_diff_k[0]),
                                  head_rows(cache_diff_v[0]), gs_, mqs, gms, cmk, cmv, 0, xs, wo, lng[0], lnb[0], dng,
                                  NB=NB, T=T, tk=SAMPLE_KEYS)

    q, g, om, skf, svf, kb1, kr1, vb1, vr1, xs1 = _project("b", xp1, wb, tabs_p, mem_b(1), tm=PROJ_B_ROWS,
                                                          rows_per_batch=S, rider=sample_a)
    yp = _swa_prompt(sinks[0], q, kb1, kr1, vb1, vr1, g, om, xp1, wo, 1, lng[1], lnb[1], B=B, S=S, tq=SWA_ROWS)
    qs, gs_, mqs, gms, skfs, svfs, kbs1, krs1, vbs1, vrs1 = _project("b", xs1, wb, tabs_s, tm=NB * T)
    ys = _swa_sample(sinks[0], qs, kbs1, krs1, vbs1, vrs1, _feature_major(cache_swa_k), _feature_major(cache_swa_v),
                     gs_, mqs, gms, cmk, cmv, 1, xs1, wo, lng[1], lnb[1], NB=NB, T=T)

    wr_p = min(WINDOW, S)
    window = lambda t: t.reshape(B, S, LANES)[:, S - wr_p:, :].reshape(B, wr_p, SW_KV, SW_HD)
    swa_kp, swa_vp = window(skf), window(svf)
    roll = lambda c, n: jnp.concatenate([c, n.reshape(NB, T, SW_KV, SW_HD)], axis=1)[:, T:]
    return (yp.reshape(B, S, D_MODEL), ys.reshape(NB, T, D_MODEL),
            _rows_major(kt, (DA_HEADS, 2, DA_QK))[None], vf.reshape(1, B, S, DA_HEADS, DA_V),
            kfs.reshape(1, NB, T, DA_HEADS, 2, DA_QK), vfs.reshape(1, NB, T, DA_HEADS, DA_V),
            swa_kp, swa_vp, roll(cache_swa_k, skfs), roll(cache_swa_v, svfs),
            mkf.reshape(DEPTH, B, N_MEM, MEM_HEADS, MEM_HD), mvf.reshape(DEPTH, B, N_MEM, MEM_HEADS, MEM_HD))
```

```python
import functools
import inspect
import math

import jax
import jax.numpy as jnp
import numpy as np
from jax import lax
from jax.experimental import pallas as pl
from jax.experimental.pallas import tpu as pltpu

D_MODEL = 1024
CHUNK = 64
N_MEM = 256
DA_HEADS = 4
DA_QK = 64
DA_V = 128
DA_W = 512
SW_HEADS = 8
SW_KV = 2
SW_HD = 64
SW_W = 512
WINDOW = 128
MEM_HEADS = 4
MEM_HD = 128
MEM_W = 512
MIX_W = 1024
ROPE_THETA = 500000.0
ROPE_HALF = 8
DEPTH = 2
DN_ALPHA = (2 * DEPTH) ** 0.25
LN_EPS = 1e-5
NEG = -1e30
LANES = 128
DIFF_TILE = 256
PIPE_LEAD = 4
DIFF_ROWS = 512
TRIP_BLOCKS = 4
SWA_LEAD = 1
SWA_TILE = 128
SUM_ROWS = 16
PROJ_ROWS = 1024
PROJ_B_ROWS = 512
MERGE_ROWS = 256
SWA_ROWS = 1024
SAMPLE_KEYS = 2048
LOG2E = math.log2(math.e)

F32 = jnp.float32
BF16 = jnp.bfloat16
VMEM_LIMIT = 50 * 1024 * 1024

_NT = (((1,), (1,)), ((), ()))


def _resident(shape, *first):
    idx = first or (0,) * len(shape)
    return pl.BlockSpec(shape, lambda *_: idx, pipeline_mode=pl.Buffered(1))


def _params(*sem):
    return pltpu.CompilerParams(dimension_semantics=sem, vmem_limit_bytes=VMEM_LIMIT)


def _rope(x, c, a, b):
    outs = []
    for i in range(x.shape[1] // LANES):
        blk = x[:, i * LANES:(i + 1) * LANES]
        outs.append(blk * c + pltpu.roll(blk, LANES - ROPE_HALF, 1) * a + pltpu.roll(blk, ROPE_HALF, 1) * b)
    return outs[0] if len(outs) == 1 else jnp.concatenate(outs, axis=1)


def _silu(g):
    return g * (1.0 / (1.0 + jnp.exp(-g)))


def _layer_norm(z, g, b):
    mu = jnp.mean(z, axis=-1, keepdims=True)
    d = z - mu
    var = jnp.mean(d * d, axis=-1, keepdims=True)
    return d * lax.rsqrt(var + LN_EPS) * g + b


def _mem_attend(mq, mk_of, mv_of):
    outs = []
    for h in range(MEM_HEADS):
        sl = slice(h * MEM_HD, (h + 1) * MEM_HD)
        s = lax.dot_general(mq[:, sl], mk_of(h), _NT, preferred_element_type=F32) * (MEM_HD ** -0.5)
        p = jnp.exp(s - jnp.max(s, axis=1, keepdims=True))
        l = jnp.sum(p, axis=1, keepdims=True)
        outs.append(jnp.dot(p.astype(BF16), mv_of(h), preferred_element_type=F32) / l)
    return jnp.concatenate(outs, axis=1)


def _head_cols(ref):
    return lambda h: ref[0, :, h * MEM_HD:(h + 1) * MEM_HD]


def _head_idx(ref, heads, h, *lead):
    return lead + (pl.ds(h, ref.shape[-2] // heads, stride=heads), slice(None))


def _head_rows(ref, heads, *lead):
    return lambda h: ref[_head_idx(ref, heads, h, *lead)].astype(BF16)


def _merge(x, mix, w_out, ln_g, ln_b):
    w = w_out.astype(BF16)
    n = max(x.shape[0] // MERGE_ROWS, 1)
    r = x.shape[0] // n
    z = lambda c: DN_ALPHA * x[c * r:(c + 1) * r] + jnp.dot(mix[c * r:(c + 1) * r], w, preferred_element_type=F32)
    outs, prev = [], z(0)
    for c in range(1, n):
        nxt = z(c)
        outs.append(_layer_norm(prev, ln_g, ln_b))
        prev = nxt
    outs.append(_layer_norm(prev, ln_g, ln_b))
    return outs[0] if n == 1 else jnp.concatenate(outs, axis=0)


def _lambda(lamv, lam_init):
    e1 = jnp.exp(jnp.sum(lamv[0:1, :] * lamv[1:2, :], axis=1, keepdims=True))
    e2 = jnp.exp(jnp.sum(lamv[2:3, :] * lamv[3:4, :], axis=1, keepdims=True))
    return e1 - e2 + lam_init


def _split_components(qh):
    lane = lax.broadcasted_iota(jnp.int32, qh.shape, 1)
    zero = jnp.zeros_like(qh)
    return jnp.concatenate([jnp.where(lane < DA_QK, qh, zero), jnp.where(lane >= DA_QK, qh, zero)], axis=0)


def _flash_stages(qqs, k_of, v_of, keys_on_lanes, m_ref, l_ref, acc_ref):
    if keys_on_lanes:
        s = [jnp.dot(qq, k_of(h), preferred_element_type=F32) for h, qq in enumerate(qqs)]
    else:
        s = [lax.dot_general(qq, k_of(h), _NT, preferred_element_type=F32) for h, qq in enumerate(qqs)]
    s = jnp.concatenate(s, axis=0)
    yield
    m_old = m_ref[...]
    m_new = jnp.maximum(m_old, jnp.max(s, axis=1, keepdims=True))
    alpha = jnp.exp2(m_old - m_new)
    p = jnp.exp2(s - m_new)
    l_ref[...] = alpha * l_ref[...] + jnp.sum(p, axis=1, keepdims=True)
    pb = p.astype(BF16)
    yield
    r = qqs[0].shape[0]
    pv = [jnp.dot(pb[h * r:(h + 1) * r], v_of(h), preferred_element_type=F32) for h in range(len(qqs))]
    acc_ref[...] = alpha * acc_ref[...] + jnp.concatenate(pv, axis=0)
    m_ref[...] = m_new


def _drain(stages):
    if inspect.isgenerator(stages):
        for _ in stages:
            pass


def _stage_scores(qq, ks, mask, s_ref, m_ref, alpha_ref):
    s = lax.dot_general(ks, qq, _NT, preferred_element_type=F32)
    if mask is not None:
        s = jnp.where(mask, s, NEG)
    s_ref[...] = s
    m_old = m_ref[...]
    m_new = jnp.maximum(m_old, jnp.max(s, axis=0, keepdims=True))
    alpha_ref[...] = jnp.exp2(m_old - m_new)
    m_ref[...] = m_new


def _stage_exp(s_ref, p_ref, m_ref):
    p_ref[...] = jnp.exp2(s_ref[...] - m_ref[...]).astype(BF16)


def _stage_pv(vts, p_ref, alpha_ref, acc_ref):
    ones = jnp.ones((SUM_ROWS, vts.shape[1]), BF16)
    pv = jnp.dot(jnp.concatenate([vts, ones], axis=0), p_ref[...], preferred_element_type=F32)
    acc_ref[...] = alpha_ref[...] * acc_ref[...] + pv


def _diff_finish(acc, l, lam, dng, lam_init):
    rows = acc.shape[0] // 2
    o = acc / l
    od = o[:rows] - lam * o[rows:]
    return od * lax.rsqrt(jnp.mean(od * od, axis=-1, keepdims=True) + LN_EPS) * dng * (1.0 - lam_init)


def _sink_attend_cached(qq, kt, kn, vt, vn, sink):
    s_c = jnp.dot(qq, kt, preferred_element_type=F32)
    s_n = lax.dot_general(qq, kn, _NT, preferred_element_type=F32)
    m = jnp.maximum(jnp.maximum(jnp.max(s_c, axis=1, keepdims=True), jnp.max(s_n, axis=1, keepdims=True)), sink)
    p_c, p_n = jnp.exp2(s_c - m), jnp.exp2(s_n - m)
    l = jnp.sum(p_c, axis=1, keepdims=True) + jnp.sum(p_n, axis=1, keepdims=True) + jnp.exp2(sink - m)
    o = (lax.dot_general(p_c.astype(BF16), vt, _NT, preferred_element_type=F32)
         + jnp.dot(p_n.astype(BF16), vn, preferred_element_type=F32))
    return o / l


def _window_scores(qq4, ks, mask):
    s = lax.dot_general(ks, qq4, _NT, preferred_element_type=F32)
    w = mask.shape[1]
    return jnp.concatenate([jnp.where(mask, s[:, c * w:(c + 1) * w], NEG) for c in range(s.shape[1] // w)], axis=1)


def _window_attend(s, vts, sinks4):
    m = jnp.maximum(jnp.max(s, axis=0, keepdims=True), sinks4)
    p = jnp.exp2(s - m).astype(BF16)
    ones = jnp.ones((SUM_ROWS, vts.shape[1]), BF16)
    o = jnp.dot(jnp.concatenate([vts, ones], axis=0), p, preferred_element_type=F32)
    return o[:LANES] / (o[LANES:LANES + 1] + jnp.exp2(sinks4 - m))


def _swa_heads(q, attend, sink_of):
    low = lax.broadcasted_iota(jnp.int32, (q.shape[0], LANES), 1) < SW_HD
    outs = []
    for pair in range(SW_HEADS // 2):
        kv = pair // 2
        slab = q[:, pair * LANES:(pair + 1) * LANES]
        zero = jnp.zeros_like(slab)
        o_even = attend(jnp.where(low, slab, zero), kv == 1, sink_of(2 * pair))
        o_odd = attend(jnp.where(low, zero, slab), kv == 0, sink_of(2 * pair + 1))
        outs.append(jnp.where(low, o_even, o_odd))
    return jnp.concatenate(outs, axis=1)


def _mem_kv_kernel(mem_ref, w_ref, kf_ref, vf_ref, kb_ref, vb_ref):
    h = jnp.dot(mem_ref[...].astype(BF16), w_ref[0].astype(BF16), preferred_element_type=F32)
    k, v = h[:, :MEM_W], h[:, MEM_W:]
    for hd in range(MEM_HEADS):
        kf_ref[_head_idx(kf_ref, MEM_HEADS, hd, 0)] = k[:, hd * MEM_HD:(hd + 1) * MEM_HD]
        vf_ref[_head_idx(vf_ref, MEM_HEADS, hd, 0)] = v[:, hd * MEM_HD:(hd + 1) * MEM_HD]
    kb_ref[0] = k.astype(BF16)
    vb_ref[0] = v.astype(BF16)


def _mem_kv(mem, w):
    rows = mem.shape[0]
    fshape, fspec = (DEPTH, rows * MEM_HEADS, MEM_HD), pl.BlockSpec((1, rows * MEM_HEADS, MEM_HD), lambda l: (l, 0, 0))
    bshape, bspec = (DEPTH, rows, MEM_W), pl.BlockSpec((1, rows, MEM_W), lambda l: (l, 0, 0))
    return pl.pallas_call(
        _mem_kv_kernel,
        grid=(DEPTH,),
        in_specs=[pl.BlockSpec((rows, D_MODEL), lambda l: (0, 0)),
                  pl.BlockSpec((1, D_MODEL, 2 * MEM_W), lambda l: (l, 0, 0))],
        out_specs=[fspec, fspec, bspec, bspec],
        out_shape=[jax.ShapeDtypeStruct(fshape, F32), jax.ShapeDtypeStruct(fshape, F32),
                   jax.ShapeDtypeStruct(bshape, BF16), jax.ShapeDtypeStruct(bshape, BF16)],
        compiler_params=_params("arbitrary"),
        name="mem_kv",
    )(mem, w)


def _proj_a_kernel(fuse_mem, x_ref, w_ref, c_ref, a_ref, b_ref, *refs):
    if fuse_mem:
        mk_ref, mv_ref, q_ref, kf_ref, vf_ref, kb_ref, vb_ref, g_ref, om_ref = refs
    else:
        q_ref, kf_ref, vf_ref, kb_ref, vb_ref, g_ref, mq_ref, gm_ref = refs
    x = x_ref[...].astype(BF16)
    c, a, b = c_ref[...], a_ref[...], b_ref[...]

    def cols(i):
        return jnp.dot(x, w_ref[:, i * DA_W:(i + 1) * DA_W].astype(BF16), preferred_element_type=F32)

    q_ref[...] = (_rope(cols(0), c, a, b) * (DA_QK ** -0.5 * LOG2E)).astype(BF16)
    k = _rope(cols(1), c, a, b)
    kb_ref[...] = k.astype(BF16)
    v = cols(2)
    for hd in range(DA_HEADS):
        vf_ref[_head_idx(vf_ref, DA_HEADS, hd)] = v[:, hd * DA_V:(hd + 1) * DA_V]
    if fuse_mem:
        kf_ref[0] = k.T
        for blk in range(vb_ref.shape[0]):
            vb_ref[blk] = v[blk * DIFF_TILE:(blk + 1) * DIFF_TILE, :].T.astype(BF16)
    else:
        kf_ref[...] = k
        vb_ref[...] = v.astype(BF16)
    g_ref[...] = _silu(cols(3)).astype(BF16)
    mq = cols(4).astype(BF16)
    gm = _silu(cols(5))
    if fuse_mem:
        om_ref[...] = (_mem_attend(mq, _head_cols(mk_ref), _head_cols(mv_ref)) * gm).astype(BF16)
    else:
        mq_ref[...] = mq
        gm_ref[...] = gm.astype(BF16)


def _proj_b_kernel(fuse_mem, x_ref, w_ref, wkv_ref, c_ref, a_ref, b_ref, *refs):
    if fuse_mem:
        mk_ref, mv_ref, q_ref, g_ref, om_ref, kf_ref, vf_ref, kb_ref, kr_ref, vb_ref, vr_ref = refs
    else:
        q_ref, g_ref, mq_ref, gm_ref, kf_ref, vf_ref, kb_ref, kr_ref, vb_ref, vr_ref = refs
    x = x_ref[...].astype(BF16)
    c, a, b = c_ref[...], a_ref[...], b_ref[...]

    def cols(i):
        return jnp.dot(x, w_ref[:, i * SW_W:(i + 1) * SW_W].astype(BF16), preferred_element_type=F32)

    q_ref[...] = (_rope(cols(0), c, a, b) * (SW_HD ** -0.5 * LOG2E)).astype(BF16)
    yield
    g_ref[...] = _silu(cols(1)).astype(BF16)
    yield
    mq = cols(2).astype(BF16)
    gm = _silu(cols(3))
    if fuse_mem:
        om_ref[...] = (_mem_attend(mq, _head_cols(mk_ref), _head_cols(mv_ref)) * gm).astype(BF16)
    else:
        mq_ref[...] = mq
        gm_ref[...] = gm.astype(BF16)
    yield
    kv = jnp.dot(x, wkv_ref[...].astype(BF16), preferred_element_type=F32)
    k = _rope(kv[:, :LANES], c, a, b)
    v = kv[:, LANES:]
    kf_ref[...] = k
    vf_ref[...] = v
    kb_ref[...] = k.astype(BF16)
    kr_ref[...] = pltpu.roll(k, SW_HD, 1).astype(BF16)
    vr = pltpu.roll(v, SW_HD, 1)
    if fuse_mem:
        for blk in range(vb_ref.shape[0]):
            rows = slice(blk * SWA_TILE, (blk + 1) * SWA_TILE)
            vb_ref[blk] = v[rows, :].T.astype(BF16)
            vr_ref[blk] = vr[rows, :].T.astype(BF16)
    else:
        vb_ref[...] = v.astype(BF16)
        vr_ref[...] = vr.astype(BF16)


def _with_rider(host_body, n_in, n_out, rider, *refs):
    r_in, r_out = len(rider["args"]), len(rider["out_specs"])
    o0 = n_in + r_in
    ride = rider["body"](*refs[n_in:o0], *refs[o0 + n_out:])
    mark = next(ride, None)
    host = host_body(*refs[:n_in], *refs[o0:o0 + n_out])
    if inspect.isgenerator(host):
        for _ in host:
            if mark == "stage":
                mark = next(ride, None)
    _drain(ride)


def _project(layer, x, ws, tabs, mem=None, *, tm, rows_per_batch=None, rider=None):
    M = x.shape[0]
    fuse = mem is not None
    per = (rows_per_batch or M) // tm
    row = lambda w: pl.BlockSpec((tm, w), lambda i: (i, 0))
    tab = pl.BlockSpec((tm, LANES), lambda i: (i % per, 0))
    in_specs = [row(D_MODEL)] + [_resident(w.shape) for w in ws] + [tab, tab, tab]
    args = [x, *ws, *tabs]
    if fuse:
        mspec = pl.BlockSpec((1, N_MEM, MEM_W), lambda i: (i // per, 0, 0))
        in_specs += [mspec, mspec]
        args += list(mem)
    o = lambda w, dt: (row(w), jax.ShapeDtypeStruct((M, w), dt))
    if layer == "a":
        vt = (pl.BlockSpec((tm // DIFF_TILE, DA_W, DIFF_TILE), lambda i: (i, 0, 0)),
              jax.ShapeDtypeStruct((M // DIFF_TILE, DA_W, DIFF_TILE), BF16))
        kt = (pl.BlockSpec((1, DA_W, tm), lambda i: (i // per, 0, i % per)),
              jax.ShapeDtypeStruct((M // (per * tm), DA_W, per * tm), F32))
        vf = (pl.BlockSpec((tm * DA_HEADS, DA_V), lambda i: (i, 0)), jax.ShapeDtypeStruct((M * DA_HEADS, DA_V), F32))
        outs = [o(DA_W, BF16), kt if fuse else o(DA_W, F32), vf, o(DA_W, BF16), vt if fuse else o(DA_W, BF16),
                o(DA_W, BF16)]
        body = _proj_a_kernel
    else:
        outs = [o(SW_W, BF16), o(SW_W, BF16)]
        body = _proj_b_kernel
    mem_outs = [o(MEM_W, BF16)] if fuse else [o(MEM_W, BF16), o(MEM_W, BF16)]
    if layer == "a":
        outs = outs + mem_outs
    else:
        vt = (pl.BlockSpec((tm // SWA_TILE, LANES, SWA_TILE), lambda i: (i, 0, 0)),
              jax.ShapeDtypeStruct((M // SWA_TILE, LANES, SWA_TILE), BF16))
        values = [vt, vt] if fuse else [o(LANES, BF16)] * 2
        outs = outs + mem_outs + [o(LANES, F32), o(LANES, F32)] + [o(LANES, BF16)] * 2 + values
    kernel_body = functools.partial(lambda run, *refs: _drain(run(*refs)), functools.partial(body, fuse))
    out_specs, out_shape, scratch = [s for s, _ in outs], [t for _, t in outs], []
    if rider is not None:
        assert rider["steps"] == M // tm
        kernel_body = functools.partial(_with_rider, functools.partial(body, fuse), len(args), len(outs), rider)
        in_specs, args = in_specs + rider["in_specs"], args + rider["args"]
        out_specs, out_shape, scratch = out_specs + rider["out_specs"], out_shape + rider["out_shape"], rider["scratch"]
    return pl.pallas_call(
        kernel_body,
        grid=(M // tm,),
        in_specs=in_specs,
        out_specs=out_specs,
        out_shape=out_shape,
        scratch_shapes=scratch,
        compiler_params=_params("arbitrary"),
        name="proj_" + layer,
    )(*args)


def _diff_prompt_kernel(lam_init, lamv_ref, q_ref, k_ref, vt_ref, g_ref, om_ref, x_ref, wout_ref,
                        lng_ref, lnb_ref, dng_ref, y_ref, m_ref, alpha_ref, acc_ref, s_ref, p_ref, mix_ref):
    t = DIFF_TILE
    nsub = q_ref.shape[0] // t
    LEAD = PIPE_LEAD
    i = pl.program_id(1)
    heads = [slice(h * DA_V, (h + 1) * DA_V) for h in range(DA_HEADS)]
    lane = lax.broadcasted_iota(jnp.int32, (t, DA_V), 1)
    units = []
    for sub in range(nsub):
        for hs in heads:
            qh = q_ref[sub * t:(sub + 1) * t, hs]
            zero = jnp.zeros_like(qh)
            units += [(sub, hs, jnp.where(lane < DA_QK, qh, zero)), (sub, hs, jnp.where(lane >= DA_QK, qh, zero))]
    n = len(units)
    per = n // nsub
    assert n == s_ref.shape[0] and LEAD + 1 < per
    m_ref[...] = jnp.full(m_ref.shape, -jnp.inf, F32)
    acc_ref[...] = jnp.zeros(acc_ref.shape, F32)

    def exp_stage(u):
        _stage_exp(s_ref.at[u], p_ref.at[u], m_ref.at[u])

    def pv_stage(j, u):
        _stage_pv(vt_ref[0, j, units[u][1], :], p_ref.at[u], alpha_ref.at[u], acc_ref.at[u])

    def block(j, mask_of, j_before, first_unit=0):
        rows = pl.ds(pl.multiple_of(j * t, t), t)
        for idx, u in enumerate(range(first_unit, n)):
            sub, hs, qq = units[u]
            _stage_scores(qq, k_ref[0, rows, hs], mask_of(sub), s_ref.at[u], m_ref.at[u], alpha_ref.at[u])
            if idx >= LEAD:
                exp_stage(u - LEAD)
            elif j_before is not None:
                exp_stage(n - LEAD + idx)
            if idx >= LEAD + 1:
                pv_stage(j, u - LEAD - 1)
            elif j_before is not None:
                pv_stage(j_before, n - LEAD - 1 + idx)

    kc = lax.broadcasted_iota(jnp.int32, (t, t), 0) // CHUNK
    qc = lax.broadcasted_iota(jnp.int32, (t, t), 1) // CHUNK
    diag_mask = kc <= qc
    diag0 = nsub * i
    for sd in range(nsub):
        block(diag0 + sd, lambda sub, sd=sd: diag_mask if sub == sd else None, None if sd == 0 else diag0 + sd - 1,
              first_unit=sd * per)

    def run(j0, count):
        block(j0, lambda sub: None, jnp.where(j0 == 0, diag0 + nsub - 1, j0 - 1))
        for d in range(1, count):
            block(j0 + d, lambda sub: None, j0 + d - 1)

    def trip(jj, carry):
        run(TRIP_BLOCKS * jj, TRIP_BLOCKS)
        return carry

    plain = nsub * i
    lax.fori_loop(0, plain // TRIP_BLOCKS, trip, 0)

    @pl.when(plain % TRIP_BLOCKS != 0)
    def _():
        run(plain // TRIP_BLOCKS * TRIP_BLOCKS, nsub)

    last = jnp.where(i == 0, diag0 + nsub - 1, diag0 - 1)
    for u in range(n - LEAD, n):
        exp_stage(u)
        pv_stage(last, u - 1)
    pv_stage(last, n - 1)

    lam = _lambda(lamv_ref[...], lam_init)
    for sub in range(nsub):
        rows = slice(sub * t, (sub + 1) * t)
        for h, hs in enumerate(heads):
            u = sub * per + 2 * h
            o1 = acc_ref[u, :DA_V, :] / acc_ref[u, DA_V:DA_V + 1, :]
            o2 = acc_ref[u + 1, :DA_V, :] / acc_ref[u + 1, DA_V:DA_V + 1, :]
            od = (o1 - lam * o2).T
            od = (od * lax.rsqrt(jnp.mean(od * od, axis=-1, keepdims=True) + LN_EPS) * dng_ref[...]
                  * (1.0 - lam_init))
            mix_ref[rows, hs] = (od * g_ref[rows, hs].astype(F32)).astype(BF16)
    mix_ref[:, DA_W:] = om_ref[...]
    y_ref[...] = _merge(x_ref[...], mix_ref[...], wout_ref[0], lng_ref[...], lnb_ref[...])


def _diff_prompt(lam_init, lamv, q, k, vt, g, om, x, w_out, layer, ln_g, ln_b, dng, *, B, S):
    t, rows = DIFF_TILE, DIFF_ROWS
    nq, units = S // rows, 2 * DA_HEADS * (rows // t)
    row = lambda w: pl.BlockSpec((rows, w), lambda b, i: (b * nq + i, 0))
    full = lambda shape: pl.BlockSpec(shape, lambda b, i: (0,) * len(shape))
    return pl.pallas_call(
        functools.partial(_diff_prompt_kernel, lam_init),
        grid=(B, nq),
        in_specs=[full((4, DA_QK)), row(DA_W), pl.BlockSpec((1, S, DA_W), lambda b, i: (b, 0, 0)),
                  pl.BlockSpec((1, S // t, DA_W, t), lambda b, i: (b, 0, 0, 0)), row(DA_W), row(MEM_W), row(D_MODEL),
                  _resident((1, MIX_W, D_MODEL), layer, 0, 0), full((1, D_MODEL)), full((1, D_MODEL)), full((1, DA_V))],
        out_specs=row(D_MODEL),
        out_shape=jax.ShapeDtypeStruct((B * S, D_MODEL), F32),
        scratch_shapes=[pltpu.VMEM((units, 1, t), F32), pltpu.VMEM((units, 1, t), F32),
                        pltpu.VMEM((units, DA_V + SUM_ROWS, t), F32),
                        pltpu.VMEM((units, t, t), F32), pltpu.VMEM((units, t, t), BF16),
                        pltpu.VMEM((rows, MIX_W), BF16)],
        compiler_params=_params("arbitrary", "arbitrary"),
        name="diff_prompt",
    )(lamv, q, k.reshape(B, S, DA_W), vt.reshape(B, S // t, DA_W, t), g, om, x, w_out, ln_g, ln_b, dng)


def _swa_prompt_kernel(tq, sinks_ref, q_ref, k_ref, kr_ref, vt_ref, vtr_ref, g_ref, om_ref, x_ref, wout_ref,
                       lng_ref, lnb_ref, y_ref, mix_ref):
    i = pl.program_id(1)
    w, nk, nsub = SWA_TILE, 2 * SWA_TILE, tq // SWA_TILE
    low = lax.broadcasted_iota(jnp.int32, (w, LANES), 1) < SW_HD
    row_low = lax.broadcasted_iota(jnp.int32, (LANES, w), 0) < SW_HD
    sink_row = lambda hs: jnp.concatenate([jnp.full((1, w), sinks_ref[h] * LOG2E, F32) for h in hs], axis=1)
    sinks_plain, sinks_swapped = sink_row((0, 2, 5, 7)), sink_row((1, 3, 4, 6))
    kc = lax.broadcasted_iota(jnp.int32, (nk, w), 0) // CHUNK
    qc = lax.broadcasted_iota(jnp.int32, (nk, w), 1) // CHUNK
    firsts, scores = [], []

    def score(st):
        sub = i * nsub + st
        first = jnp.maximum(sub - 1, 0)
        back = (sub - first) * (w // CHUNK)
        mask = (kc - back <= qc) & (kc - back >= qc - WINDOW // CHUNK)
        rows = pl.ds(pl.multiple_of(first * w, w), nk)
        q = q_ref[st * w:(st + 1) * w, :]
        slabs = [q[:, p * LANES:(p + 1) * LANES] for p in range(SW_HEADS // 2)]
        even = [jnp.where(low, sl, jnp.zeros_like(sl)) for sl in slabs]
        odd = [jnp.where(low, jnp.zeros_like(sl), sl) for sl in slabs]
        firsts.append(first)
        scores.append((_window_scores(jnp.concatenate([even[0], even[1], odd[2], odd[3]], axis=0), k_ref[0, rows, :],
                                      mask),
                       _window_scores(jnp.concatenate([odd[0], odd[1], even[2], even[3]], axis=0), kr_ref[0, rows, :],
                                      mask)))
    def attend(st):
        first, (s_plain, s_swapped) = firsts[st], scores[st]
        vts = jnp.concatenate([vt_ref[0, first], vt_ref[0, first + 1]], axis=1)
        vtrs = jnp.concatenate([vtr_ref[0, first], vtr_ref[0, first + 1]], axis=1)
        o_plain = _window_attend(s_plain, vts, sinks_plain)
        o_swapped = _window_attend(s_swapped, vtrs, sinks_swapped)
        for p in range(SW_HEADS // 2):
            cols = slice(p * w, (p + 1) * w)
            lo, hi = (o_plain, o_swapped) if p < 2 else (o_swapped, o_plain)
            slab = jnp.where(row_low, lo[:, cols], hi[:, cols]).T
            gate = g_ref[st * w:(st + 1) * w, p * LANES:(p + 1) * LANES].astype(F32)
            mix_ref[st * w:(st + 1) * w, p * LANES:(p + 1) * LANES] = (slab * gate).astype(BF16)
    for st in range(min(SWA_LEAD, nsub)):
        score(st)
    for st in range(nsub):
        if st + SWA_LEAD < nsub:
            score(st + SWA_LEAD)
        attend(st)
    mix_ref[:, SW_W:] = om_ref[...]
    y_ref[...] = _merge(x_ref[...], mix_ref[...], wout_ref[0], lng_ref[...], lnb_ref[...])


def _swa_prompt(sinks, q, k, kr, vt, vtr, g, om, x, w_out, layer, ln_g, ln_b, *, B, S, tq):
    nq = S // tq
    row = lambda w: pl.BlockSpec((tq, w), lambda b, i: (b * nq + i, 0))
    full = lambda shape: pl.BlockSpec(shape, lambda b, i: (0,) * len(shape))
    seq = pl.BlockSpec((1, S, LANES), lambda b, i: (b, 0, 0))
    seqt = pl.BlockSpec((1, S // SWA_TILE, LANES, SWA_TILE), lambda b, i: (b, 0, 0, 0))
    r3 = lambda t: t.reshape(B, S, LANES)
    r4 = lambda t: t.reshape(B, S // SWA_TILE, LANES, SWA_TILE)
    return pl.pallas_call(
        functools.partial(_swa_prompt_kernel, tq),
        grid=(B, nq),
        in_specs=[pl.BlockSpec(memory_space=pltpu.SMEM), row(SW_W), seq, seq, seqt, seqt, row(SW_W), row(MEM_W),
                  row(D_MODEL), _resident((1, MIX_W, D_MODEL), layer, 0, 0), full((1, D_MODEL)), full((1, D_MODEL))],
        out_specs=row(D_MODEL),
        out_shape=jax.ShapeDtypeStruct((B * S, D_MODEL), F32),
        scratch_shapes=[pltpu.VMEM((tq, MIX_W), BF16)],
        compiler_params=_params("arbitrary", "arbitrary"),
        name="swa_prompt",
    )(sinks, q, r3(k), r3(kr), r4(vt), r4(vtr), g, om, x, w_out, ln_g, ln_b)


def _sample_tail(b, last, o_first, g_ref, mq_ref, gm_ref, cmk_ref, cmv_ref, x_ref, wout_ref, lng_ref, lnb_ref,
                 y_ref, mix_ref, T):
    om = _mem_attend(mq_ref[...], _head_rows(cmk_ref, MEM_HEADS, 0, 0), _head_rows(cmv_ref, MEM_HEADS, 0, 0))
    om = om * gm_ref[...].astype(F32)
    mix = jnp.concatenate([(o_first * g_ref[...].astype(F32)).astype(BF16), om.astype(BF16)], axis=1)
    mix_ref[pl.ds(pl.multiple_of(b * T, T), T), :] = mix

    @pl.when(last)
    def _():
        y_ref[...] = _merge(x_ref[...], mix_ref[...], wout_ref[0], lng_ref[...], lnb_ref[...])


def _diff_sample_body(lam_init, T, nb, nj, lamv_ref, q_ref, kn_ref, vn_ref, ckt_ref, cv_ref, g_ref, mq_ref, gm_ref,
                      cmk_ref, cmv_ref, x_ref, wout_ref, lng_ref, lnb_ref, dng_ref, y_ref,
                      m_ref, l_ref, acc_ref, mix_ref):
    b, j = pl.program_id(0) // nj, pl.program_id(0) % nj
    heads = [slice(h * DA_V, (h + 1) * DA_V) for h in range(DA_HEADS)]
    qqs = [_split_components(q_ref[:, hs]) for hs in heads]

    @pl.when(j == 0)
    def _():
        m_ref[...] = jnp.full(m_ref.shape, -jnp.inf, F32)
        l_ref[...] = jnp.zeros(l_ref.shape, F32)
        acc_ref[...] = jnp.zeros(acc_ref.shape, F32)
        _drain(_flash_stages(qqs, lambda h: kn_ref[:, heads[h]], lambda h: vn_ref[:, heads[h]], False, m_ref, l_ref,
                             acc_ref))

    yield "stage"
    for _ in _flash_stages(qqs, lambda h: ckt_ref[0, heads[h], :].astype(BF16), _head_rows(cv_ref, DA_HEADS, 0), True,
                           m_ref, l_ref, acc_ref):
        yield "stage"
    yield "tail"

    @pl.when(j == nj - 1)
    def _():
        lam = _lambda(lamv_ref[...], lam_init)
        rows = [slice(h * 2 * T, (h + 1) * 2 * T) for h in range(DA_HEADS)]
        od = jnp.concatenate([_diff_finish(acc_ref[r, :], l_ref[r, :], lam, dng_ref[...], lam_init) for r in rows],
                             axis=1)
        _sample_tail(b, b == nb - 1, od, g_ref, mq_ref, gm_ref, cmk_ref, cmv_ref, x_ref, wout_ref, lng_ref,
                     lnb_ref, y_ref, mix_ref, T)


def _diff_sample_rider(lam_init, lamv, q, kn, vn, ckt, cv, g, mq, gm, cmk, cmv, layer, x, w_out, ln_g, ln_b, dng, *,
                       NB, T, tk):
    nj = ckt.shape[2] // tk
    row = lambda w: pl.BlockSpec((T, w), lambda i: (i // nj, 0))
    small = lambda shape: _resident(shape)
    memc = pl.BlockSpec((1, 1, N_MEM * MEM_HEADS, MEM_HD), lambda i: (layer, i // nj, 0, 0))
    return dict(
        steps=NB * nj,
        in_specs=[small((4, DA_QK)), row(DA_W), row(DA_W), row(DA_W),
                  pl.BlockSpec((1, DA_W, tk), lambda i: (i // nj, 0, i % nj)),
                  pl.BlockSpec((1, tk * DA_HEADS, DA_V), lambda i: (i // nj, i % nj, 0)),
                  row(DA_W), row(MEM_W), row(MEM_W), memc, memc, small((NB * T, D_MODEL)),
                  _resident((1, MIX_W, D_MODEL), layer, 0, 0), small((1, D_MODEL)), small((1, D_MODEL)),
                  small((1, DA_V))],
        args=[lamv, q, kn, vn, ckt, cv, g, mq, gm, cmk, cmv, x, w_out, ln_g, ln_b, dng],
        out_specs=[pl.BlockSpec((NB * T, D_MODEL), lambda i: (0, 0))],
        out_shape=[jax.ShapeDtypeStruct((NB * T, D_MODEL), F32)],
        scratch=[pltpu.VMEM((DA_HEADS * 2 * T, 1), F32), pltpu.VMEM((DA_HEADS * 2 * T, 1), F32),
                 pltpu.VMEM((DA_HEADS * 2 * T, DA_V), F32), pltpu.VMEM((NB * T, MIX_W), BF16)],
        body=functools.partial(_diff_sample_body, lam_init, T, NB, nj),
    )


def _swa_sample_kernel(T, sinks_ref, q_ref, kn_ref, knr_ref, vn_ref, vnr_ref, ckt_ref, cvt_ref, g_ref, mq_ref, gm_ref,
                       cmk_ref, cmv_ref, x_ref, wout_ref, lng_ref, lnb_ref, y_ref, mix_ref):
    b = pl.program_id(0)
    swap = lambda t: jnp.concatenate([t[SW_HD:], t[:SW_HD]], axis=0).astype(BF16)
    ckt, cvt = ckt_ref[0], cvt_ref[0]
    kt, ktr, vt, vtr = ckt.astype(BF16), swap(ckt), cvt.astype(BF16), swap(cvt)
    kn, knr, vn, vnr = kn_ref[...], knr_ref[...], vn_ref[...], vnr_ref[...]
    attend = lambda qq, swapped, sink: _sink_attend_cached(
        qq, ktr if swapped else kt, knr if swapped else kn, vtr if swapped else vt, vnr if swapped else vn, sink)
    o = _swa_heads(q_ref[...], attend, lambda h: sinks_ref[h] * LOG2E)
    _sample_tail(b, b == pl.num_programs(0) - 1, o, g_ref, mq_ref, gm_ref, cmk_ref, cmv_ref, x_ref, wout_ref,
                 lng_ref, lnb_ref, y_ref, mix_ref, T)


def _swa_sample(sinks, q, kn, knr, vn, vnr, ckt, cvt, g, mq, gm, cmk, cmv, layer, x, w_out, ln_g, ln_b, *, NB, T):
    W = ckt.shape[2]
    row = lambda w: pl.BlockSpec((T, w), lambda b: (b, 0))
    full = lambda shape: pl.BlockSpec(shape, lambda b: (0,) * len(shape))
    cache = pl.BlockSpec((1, LANES, W), lambda b: (b, 0, 0))
    memc = pl.BlockSpec((1, 1, N_MEM * MEM_HEADS, MEM_HD), lambda b: (layer, b, 0, 0))
    return pl.pallas_call(
        functools.partial(_swa_sample_kernel, T),
        grid=(NB,),
        in_specs=[pl.BlockSpec(memory_space=pltpu.SMEM), row(SW_W), row(LANES), row(LANES), row(LANES), row(LANES),
                  cache, cache, row(SW_W), row(MEM_W), row(MEM_W), memc, memc, full((NB * T, D_MODEL)),
                  _resident((1, MIX_W, D_MODEL), layer, 0, 0), full((1, D_MODEL)), full((1, D_MODEL))],
        out_specs=full((NB * T, D_MODEL)),
        out_shape=jax.ShapeDtypeStruct((NB * T, D_MODEL), F32),
        scratch_shapes=[pltpu.VMEM((NB * T, MIX_W), BF16)],
        compiler_params=_params("arbitrary"),
        name="swa_sample",
    )(sinks, q, kn, knr, vn, vnr, ckt, cvt, g, mq, gm, cmk, cmv, x, w_out, ln_g, ln_b)


def _rope_tables(pos):
    T = pos.shape[0]
    inv = ROPE_THETA ** (-np.arange(ROPE_HALF, dtype=np.float64) / ROPE_HALF)
    ang = pos.astype(np.float64)[:, None] * inv[None, :]
    cos, sin = np.cos(ang), np.sin(ang)
    rest = SW_HD - 2 * ROPE_HALF
    c = np.concatenate([cos, cos, np.ones((T, rest))], axis=1)
    a = np.concatenate([-sin, np.zeros((T, SW_HD - ROPE_HALF))], axis=1)
    b = np.concatenate([np.zeros((T, ROPE_HALF)), sin, np.zeros((T, rest))], axis=1)
    return tuple(jnp.asarray(np.tile(t, (1, LANES // SW_HD)), F32) for t in (c, a, b))


def _feature_major(t):
    n = t.ndim
    return jnp.transpose(t, (0,) + tuple(range(2, n)) + (1,)).reshape(t.shape[0], -1, t.shape[1])


def _rows_major(t, tail):
    n = len(tail)
    return jnp.transpose(t.reshape((t.shape[0],) + tail + (t.shape[2],)), (0, n + 1) + tuple(range(1, n + 1)))


def kernel(x_prompt, x_sample, mem_prompt, cache_diff_k, cache_diff_v, cache_swa_k, cache_swa_v, cache_mem_k,
           cache_mem_v, w_in_a, lam_q1, lam_k1, lam_q2, lam_k2, diff_norm_g, w_in_b, sinks, w_kv_shared, w_mem_kv,
           w_out, ln_g, ln_b):
    B, S, _ = x_prompt.shape
    NB, T, _ = x_sample.shape
    P = cache_diff_k.shape[2]
    assert w_in_a.shape[0] == 1 and w_in_b.shape[0] == 1 and w_out.shape[0] == DEPTH

    tabs_p = _rope_tables(np.arange(S))
    tabs_s = _rope_tables(np.tile(P + np.arange(T), NB))
    xp = x_prompt.reshape(B * S, D_MODEL)
    xs = x_sample.reshape(NB * T, D_MODEL)
    wa, wb, wo = (w_in_a[0],), (w_in_b[0], w_kv_shared), w_out
    lng = ln_g.reshape(DEPTH, 1, D_MODEL)
    lnb = ln_b.reshape(DEPTH, 1, D_MODEL)
    lamv = jnp.concatenate([lam_q1, lam_k1, lam_q2, lam_k2], axis=0)
    dng = diff_norm_g.reshape(1, DA_V)
    lam_init = 0.8 - 0.6 * math.exp(-0.3 * 0)

    mkf, mvf, mkb, mvb = _mem_kv(mem_prompt.reshape(B * N_MEM, D_MODEL), w_mem_kv)
    mem_b = lambda l: (mkb[l].reshape(B, N_MEM, MEM_W), mvb[l].reshape(B, N_MEM, MEM_W))
    head_rows = lambda t: t.reshape(t.shape[:-3] + (t.shape[-3] * t.shape[-2], t.shape[-1]))
    cmk, cmv = head_rows(cache_mem_k), head_rows(cache_mem_v)

    q, kt, vf, kb, vb, g, om = _project("a", xp, wa, tabs_p, mem_b(0), tm=PROJ_ROWS, rows_per_batch=S)
    xp1 = _diff_prompt(lam_init, lamv, q, kb, vb, g, om, xp, wo, 0, lng[0], lnb[0], dng, B=B, S=S)
    qs, kfs, vfs, kbs, vbs, gs_, mqs, gms = _project("a", xs, wa, tabs_s, tm=NB * T)
    sample_a = _diff_sample_rider(lam_init, lamv, qs, kbs, vbs, _feature_major(cache_diff_k[0]),
                                  head_rows(cache_diff_v[0]), gs_, mqs, gms, cmk, cmv, 0, xs, wo, lng[0], lnb[0], dng,
                                  NB=NB, T=T, tk=SAMPLE_KEYS)

    q, g, om, skf, svf, kb1, kr1, vb1, vr1, xs1 = _project("b", xp1, wb, tabs_p, mem_b(1), tm=PROJ_B_ROWS,
                                                          rows_per_batch=S, rider=sample_a)
    yp = _swa_prompt(sinks[0], q, kb1, kr1, vb1, vr1, g, om, xp1, wo, 1, lng[1], lnb[1], B=B, S=S, tq=SWA_ROWS)
    qs, gs_, mqs, gms, skfs, svfs, kbs1, krs1, vbs1, vrs1 = _project("b", xs1, wb, tabs_s, tm=NB * T)
    ys = _swa_sample(sinks[0], qs, kbs1, krs1, vbs1, vrs1, _feature_major(cache_swa_k), _feature_major(cache_swa_v),
                     gs_, mqs, gms, cmk, cmv, 1, xs1, wo, lng[1], lnb[1], NB=NB, T=T)

    wr_p = min(WINDOW, S)
    window = lambda t: t.reshape(B, S, LANES)[:, S - wr_p:, :].reshape(B, wr_p, SW_KV, SW_HD)
    swa_kp, swa_vp = window(skf), window(svf)
    roll = lambda c, n: jnp.concatenate([c, n.reshape(NB, T, SW_KV, SW_HD)], axis=1)[:, T:]
    return (yp.reshape(B, S, D_MODEL), ys.reshape(NB, T, D_MODEL),
            _rows_major(kt, (DA_HEADS, 2, DA_QK))[None], vf.reshape(1, B, S, DA_HEADS, DA_V),
            kfs.reshape(1, NB, T, DA_HEADS, 2, DA_QK), vfs.reshape(1, NB, T, DA_HEADS, DA_V),
            swa_kp, swa_vp, roll(cache_swa_k, skfs), roll(cache_swa_v, svfs),
            mkf.reshape(DEPTH, B, N_MEM, MEM_HEADS, MEM_HD), mvf.reshape(DEPTH, B, N_MEM, MEM_HEADS, MEM_HD))
```

```python
import functools
import inspect
import math

import jax
import jax.numpy as jnp
import numpy as np
from jax import lax
from jax.experimental import pallas as pl
from jax.experimental.pallas import tpu as pltpu

D_MODEL = 1024
CHUNK = 64
N_MEM = 256
DA_HEADS = 4
DA_QK = 64
DA_V = 128
DA_W = 512
SW_HEADS = 8
SW_KV = 2
SW_HD = 64
SW_W = 512
WINDOW = 128
MEM_HEADS = 4
MEM_HD = 128
MEM_W = 512
MIX_W = 1024
ROPE_THETA = 500000.0
ROPE_HALF = 8
DEPTH = 2
DN_ALPHA = (2 * DEPTH) ** 0.25
LN_EPS = 1e-5
NEG = -1e30
LANES = 128
DIFF_TILE = 256
PIPE_LEAD = 4
DIFF_ROWS = 512
TRIP_BLOCKS = 4
SWA_LEAD = 1
SWA_TILE = 128
SUM_ROWS = 16
PROJ_ROWS = 1024
PROJ_B_ROWS = 512
MERGE_ROWS = 256
SWA_ROWS = 1024
SAMPLE_KEYS = 2048
LOG2E = math.log2(math.e)

F32 = jnp.float32
BF16 = jnp.bfloat16
VMEM_LIMIT = 50 * 1024 * 1024

_NT = (((1,), (1,)), ((), ()))


def _resident(shape, *first):
    idx = first or (0,) * len(shape)
    return pl.BlockSpec(shape, lambda *_: idx, pipeline_mode=pl.Buffered(1))


def _params(*sem):
    return pltpu.CompilerParams(dimension_semantics=sem, vmem_limit_bytes=VMEM_LIMIT)


def _rope(x, c, a, b):
    outs = []
    for i in range(x.shape[1] // LANES):
        blk = x[:, i * LANES:(i + 1) * LANES]
        outs.append(blk * c + pltpu.roll(blk, LANES - ROPE_HALF, 1) * a + pltpu.roll(blk, ROPE_HALF, 1) * b)
    return outs[0] if len(outs) == 1 else jnp.concatenate(outs, axis=1)


def _silu(g):
    return g * (1.0 / (1.0 + jnp.exp(-g)))


def _layer_norm(z, g, b):
    mu = jnp.mean(z, axis=-1, keepdims=True)
    d = z - mu
    var = jnp.mean(d * d, axis=-1, keepdims=True)
    return d * lax.rsqrt(var + LN_EPS) * g + b


def _mem_attend(mq, mk_of, mv_of):
    outs = []
    for h in range(MEM_HEADS):
        sl = slice(h * MEM_HD, (h + 1) * MEM_HD)
        s = lax.dot_general(mq[:, sl], mk_of(h), _NT, preferred_element_type=F32) * (MEM_HD ** -0.5)
        p = jnp.exp(s - jnp.max(s, axis=1, keepdims=True))
        l = jnp.sum(p, axis=1, keepdims=True)
        outs.append(jnp.dot(p.astype(BF16), mv_of(h), preferred_element_type=F32) / l)
    return jnp.concatenate(outs, axis=1)


def _head_cols(ref):
    return lambda h: ref[0, :, h * MEM_HD:(h + 1) * MEM_HD]


def _head_idx(ref, heads, h, *lead):
    return lead + (pl.ds(h, ref.shape[-2] // heads, stride=heads), slice(None))


def _head_rows(ref, heads, *lead):
    return lambda h: ref[_head_idx(ref, heads, h, *lead)].astype(BF16)


def _merge(x, mix, w_out, ln_g, ln_b):
    w = w_out.astype(BF16)
    n = max(x.shape[0] // MERGE_ROWS, 1)
    r = x.shape[0] // n
    z = lambda c: DN_ALPHA * x[c * r:(c + 1) * r] + jnp.dot(mix[c * r:(c + 1) * r], w, preferred_element_type=F32)
    outs, prev = [], z(0)
    for c in range(1, n):
        nxt = z(c)
        outs.append(_layer_norm(prev, ln_g, ln_b))
        prev = nxt
    outs.append(_layer_norm(prev, ln_g, ln_b))
    return outs[0] if n == 1 else jnp.concatenate(outs, axis=0)


def _lambda(lamv, lam_init):
    e1 = jnp.exp(jnp.sum(lamv[0:1, :] * lamv[1:2, :], axis=1, keepdims=True))
    e2 = jnp.exp(jnp.sum(lamv[2:3, :] * lamv[3:4, :], axis=1, keepdims=True))
    return e1 - e2 + lam_init


def _split_components(qh):
    lane = lax.broadcasted_iota(jnp.int32, qh.shape, 1)
    zero = jnp.zeros_like(qh)
    return jnp.concatenate([jnp.where(lane < DA_QK, qh, zero), jnp.where(lane >= DA_QK, qh, zero)], axis=0)


def _flash_stages(qqs, k_of, v_of, keys_on_lanes, m_ref, l_ref, acc_ref):
    if keys_on_lanes:
        s = [jnp.dot(qq, k_of(h), preferred_element_type=F32) for h, qq in enumerate(qqs)]
    else:
        s = [lax.dot_general(qq, k_of(h), _NT, preferred_element_type=F32) for h, qq in enumerate(qqs)]
    s = jnp.concatenate(s, axis=0)
    yield
    m_old = m_ref[...]
    m_new = jnp.maximum(m_old, jnp.max(s, axis=1, keepdims=True))
    alpha = jnp.exp2(m_old - m_new)
    p = jnp.exp2(s - m_new)
    l_ref[...] = alpha * l_ref[...] + jnp.sum(p, axis=1, keepdims=True)
    pb = p.astype(BF16)
    yield
    r = qqs[0].shape[0]
    pv = [jnp.dot(pb[h * r:(h + 1) * r], v_of(h), preferred_element_type=F32) for h in range(len(qqs))]
    acc_ref[...] = alpha * acc_ref[...] + jnp.concatenate(pv, axis=0)
    m_ref[...] = m_new


def _drain(stages):
    if inspect.isgenerator(stages):
        for _ in stages:
            pass


def _stage_scores(qq, ks, mask, s_ref, m_ref, alpha_ref):
    s = lax.dot_general(ks, qq, _NT, preferred_element_type=F32)
    if mask is not None:
        s = jnp.where(mask, s, NEG)
    s_ref[...] = s
    m_old = m_ref[...]
    m_new = jnp.maximum(m_old, jnp.max(s, axis=0, keepdims=True))
    alpha_ref[...] = jnp.exp2(m_old - m_new)
    m_ref[...] = m_new


def _stage_exp(s_ref, p_ref, m_ref):
    p_ref[...] = jnp.exp2(s_ref[...] - m_ref[...]).astype(BF16)


def _stage_pv(vts, p_ref, alpha_ref, acc_ref):
    ones = jnp.ones((SUM_ROWS, vts.shape[1]), BF16)
    pv = jnp.dot(jnp.concatenate([vts, ones], axis=0), p_ref[...], preferred_element_type=F32)
    acc_ref[...] = alpha_ref[...] * acc_ref[...] + pv


def _diff_finish(acc, l, lam, dng, lam_init):
    rows = acc.shape[0] // 2
    o = acc / l
    od = o[:rows] - lam * o[rows:]
    return od * lax.rsqrt(jnp.mean(od * od, axis=-1, keepdims=True) + LN_EPS) * dng * (1.0 - lam_init)


def _sink_attend_cached(qq, kt, kn, vt, vn, sink):
    s_c = jnp.dot(qq, kt, preferred_element_type=F32)
    s_n = lax.dot_general(qq, kn, _NT, preferred_element_type=F32)
    m = jnp.maximum(jnp.maximum(jnp.max(s_c, axis=1, keepdims=True), jnp.max(s_n, axis=1, keepdims=True)), sink)
    p_c, p_n = jnp.exp2(s_c - m), jnp.exp2(s_n - m)
    l = jnp.sum(p_c, axis=1, keepdims=True) + jnp.sum(p_n, axis=1, keepdims=True) + jnp.exp2(sink - m)
    o = (lax.dot_general(p_c.astype(BF16), vt, _NT, preferred_element_type=F32)
         + jnp.dot(p_n.astype(BF16), vn, preferred_element_type=F32))
    return o / l


def _window_scores(qq4, ks, masks):
    s = lax.dot_general(ks, qq4, _NT, preferred_element_type=F32)
    n, groups = s.shape[0] // len(masks), []
    for g, mask in enumerate(masks):
        sg = s[g * n:(g + 1) * n]
        if mask is not None:
            w = mask.shape[1]
            sg = jnp.concatenate([jnp.where(mask, sg[:, c * w:(c + 1) * w], NEG) for c in range(sg.shape[1] // w)],
                                 axis=1)
        groups.append(sg)
    return jnp.concatenate(groups, axis=0)


def _window_attend(s, vts, sinks4):
    m = jnp.maximum(jnp.max(s, axis=0, keepdims=True), sinks4)
    p = jnp.exp2(s - m).astype(BF16)
    ones = jnp.ones((SUM_ROWS, vts.shape[1]), BF16)
    o = jnp.dot(jnp.concatenate([vts, ones], axis=0), p, preferred_element_type=F32)
    return o[:LANES] / (o[LANES:LANES + 1] + jnp.exp2(sinks4 - m))


def _swa_heads(q, attend, sink_of):
    low = lax.broadcasted_iota(jnp.int32, (q.shape[0], LANES), 1) < SW_HD
    outs = []
    for pair in range(SW_HEADS // 2):
        kv = pair // 2
        slab = q[:, pair * LANES:(pair + 1) * LANES]
        zero = jnp.zeros_like(slab)
        o_even = attend(jnp.where(low, slab, zero), kv == 1, sink_of(2 * pair))
        o_odd = attend(jnp.where(low, zero, slab), kv == 0, sink_of(2 * pair + 1))
        outs.append(jnp.where(low, o_even, o_odd))
    return jnp.concatenate(outs, axis=1)


def _mem_kv_kernel(mem_ref, w_ref, kf_ref, vf_ref, kb_ref, vb_ref):
    h = jnp.dot(mem_ref[...].astype(BF16), w_ref[0].astype(BF16), preferred_element_type=F32)
    k, v = h[:, :MEM_W], h[:, MEM_W:]
    for hd in range(MEM_HEADS):
        kf_ref[_head_idx(kf_ref, MEM_HEADS, hd, 0)] = k[:, hd * MEM_HD:(hd + 1) * MEM_HD]
        vf_ref[_head_idx(vf_ref, MEM_HEADS, hd, 0)] = v[:, hd * MEM_HD:(hd + 1) * MEM_HD]
    kb_ref[0] = k.astype(BF16)
    vb_ref[0] = v.astype(BF16)


def _mem_kv(mem, w):
    rows = mem.shape[0]
    fshape, fspec = (DEPTH, rows * MEM_HEADS, MEM_HD), pl.BlockSpec((1, rows * MEM_HEADS, MEM_HD), lambda l: (l, 0, 0))
    bshape, bspec = (DEPTH, rows, MEM_W), pl.BlockSpec((1, rows, MEM_W), lambda l: (l, 0, 0))
    return pl.pallas_call(
        _mem_kv_kernel,
        grid=(DEPTH,),
        in_specs=[pl.BlockSpec((rows, D_MODEL), lambda l: (0, 0)),
                  pl.BlockSpec((1, D_MODEL, 2 * MEM_W), lambda l: (l, 0, 0))],
        out_specs=[fspec, fspec, bspec, bspec],
        out_shape=[jax.ShapeDtypeStruct(fshape, F32), jax.ShapeDtypeStruct(fshape, F32),
                   jax.ShapeDtypeStruct(bshape, BF16), jax.ShapeDtypeStruct(bshape, BF16)],
        compiler_params=_params("arbitrary"),
        name="mem_kv",
    )(mem, w)


def _proj_a_kernel(fuse_mem, x_ref, w_ref, c_ref, a_ref, b_ref, *refs):
    if fuse_mem:
        mk_ref, mv_ref, q_ref, kf_ref, vf_ref, kb_ref, vb_ref, g_ref, om_ref = refs
    else:
        q_ref, kf_ref, vf_ref, kb_ref, vb_ref, g_ref, mq_ref, gm_ref = refs
    x = x_ref[...].astype(BF16)
    c, a, b = c_ref[...], a_ref[...], b_ref[...]

    def cols(i):
        return jnp.dot(x, w_ref[:, i * DA_W:(i + 1) * DA_W].astype(BF16), preferred_element_type=F32)

    q_ref[...] = (_rope(cols(0), c, a, b) * (DA_QK ** -0.5 * LOG2E)).astype(BF16)
    k = _rope(cols(1), c, a, b)
    kb_ref[...] = k.astype(BF16)
    v = cols(2)
    for hd in range(DA_HEADS):
        vf_ref[_head_idx(vf_ref, DA_HEADS, hd)] = v[:, hd * DA_V:(hd + 1) * DA_V]
    if fuse_mem:
        kf_ref[0] = k.T
        for blk in range(vb_ref.shape[0]):
            vb_ref[blk] = v[blk * DIFF_TILE:(blk + 1) * DIFF_TILE, :].T.astype(BF16)
    else:
        kf_ref[...] = k
        vb_ref[...] = v.astype(BF16)
    g_ref[...] = _silu(cols(3)).astype(BF16)
    mq = cols(4).astype(BF16)
    gm = _silu(cols(5))
    if fuse_mem:
        om_ref[...] = (_mem_attend(mq, _head_cols(mk_ref), _head_cols(mv_ref)) * gm).astype(BF16)
    else:
        mq_ref[...] = mq
        gm_ref[...] = gm.astype(BF16)


def _proj_b_kernel(fuse_mem, x_ref, w_ref, wkv_ref, c_ref, a_ref, b_ref, *refs):
    if fuse_mem:
        mk_ref, mv_ref, q_ref, g_ref, om_ref, kf_ref, vf_ref, kb_ref, kr_ref, vb_ref, vr_ref = refs
    else:
        q_ref, g_ref, mq_ref, gm_ref, kf_ref, vf_ref, kb_ref, kr_ref, vb_ref, vr_ref = refs
    x = x_ref[...].astype(BF16)
    c, a, b = c_ref[...], a_ref[...], b_ref[...]

    def cols(i):
        return jnp.dot(x, w_ref[:, i * SW_W:(i + 1) * SW_W].astype(BF16), preferred_element_type=F32)

    q_ref[...] = (_rope(cols(0), c, a, b) * (SW_HD ** -0.5 * LOG2E)).astype(BF16)
    yield
    g_ref[...] = _silu(cols(1)).astype(BF16)
    yield
    mq = cols(2).astype(BF16)
    gm = _silu(cols(3))
    if fuse_mem:
        om_ref[...] = (_mem_attend(mq, _head_cols(mk_ref), _head_cols(mv_ref)) * gm).astype(BF16)
    else:
        mq_ref[...] = mq
        gm_ref[...] = gm.astype(BF16)
    yield
    kv = jnp.dot(x, wkv_ref[...].astype(BF16), preferred_element_type=F32)
    k = _rope(kv[:, :LANES], c, a, b)
    v = kv[:, LANES:]
    kf_ref[...] = k
    vf_ref[...] = v
    kb_ref[...] = k.astype(BF16)
    kr_ref[...] = pltpu.roll(k, SW_HD, 1).astype(BF16)
    vr = pltpu.roll(v, SW_HD, 1)
    if fuse_mem:
        for blk in range(vb_ref.shape[0]):
            rows = slice(blk * SWA_TILE, (blk + 1) * SWA_TILE)
            vb_ref[blk] = v[rows, :].T.astype(BF16)
            vr_ref[blk] = vr[rows, :].T.astype(BF16)
    else:
        vb_ref[...] = v.astype(BF16)
        vr_ref[...] = vr.astype(BF16)


def _with_rider(host_body, n_in, n_out, rider, *refs):
    r_in, r_out = len(rider["args"]), len(rider["out_specs"])
    o0 = n_in + r_in
    ride = rider["body"](*refs[n_in:o0], *refs[o0 + n_out:])
    mark = next(ride, None)
    host = host_body(*refs[:n_in], *refs[o0:o0 + n_out])
    if inspect.isgenerator(host):
        for _ in host:
            if mark == "stage":
                mark = next(ride, None)
    _drain(ride)


def _project(layer, x, ws, tabs, mem=None, *, tm, rows_per_batch=None, rider=None):
    M = x.shape[0]
    fuse = mem is not None
    per = (rows_per_batch or M) // tm
    row = lambda w: pl.BlockSpec((tm, w), lambda i: (i, 0))
    tab = pl.BlockSpec((tm, LANES), lambda i: (i % per, 0))
    in_specs = [row(D_MODEL)] + [_resident(w.shape) for w in ws] + [tab, tab, tab]
    args = [x, *ws, *tabs]
    if fuse:
        mspec = pl.BlockSpec((1, N_MEM, MEM_W), lambda i: (i // per, 0, 0))
        in_specs += [mspec, mspec]
        args += list(mem)
    o = lambda w, dt: (row(w), jax.ShapeDtypeStruct((M, w), dt))
    if layer == "a":
        vt = (pl.BlockSpec((tm // DIFF_TILE, DA_W, DIFF_TILE), lambda i: (i, 0, 0)),
              jax.ShapeDtypeStruct((M // DIFF_TILE, DA_W, DIFF_TILE), BF16))
        kt = (pl.BlockSpec((1, DA_W, tm), lambda i: (i // per, 0, i % per)),
              jax.ShapeDtypeStruct((M // (per * tm), DA_W, per * tm), F32))
        vf = (pl.BlockSpec((tm * DA_HEADS, DA_V), lambda i: (i, 0)), jax.ShapeDtypeStruct((M * DA_HEADS, DA_V), F32))
        outs = [o(DA_W, BF16), kt if fuse else o(DA_W, F32), vf, o(DA_W, BF16), vt if fuse else o(DA_W, BF16),
                o(DA_W, BF16)]
        body = _proj_a_kernel
    else:
        outs = [o(SW_W, BF16), o(SW_W, BF16)]
        body = _proj_b_kernel
    mem_outs = [o(MEM_W, BF16)] if fuse else [o(MEM_W, BF16), o(MEM_W, BF16)]
    if layer == "a":
        outs = outs + mem_outs
    else:
        vt = (pl.BlockSpec((tm // SWA_TILE, LANES, SWA_TILE), lambda i: (i, 0, 0)),
              jax.ShapeDtypeStruct((M // SWA_TILE, LANES, SWA_TILE), BF16))
        values = [vt, vt] if fuse else [o(LANES, BF16)] * 2
        outs = outs + mem_outs + [o(LANES, F32), o(LANES, F32)] + [o(LANES, BF16)] * 2 + values
    kernel_body = functools.partial(lambda run, *refs: _drain(run(*refs)), functools.partial(body, fuse))
    out_specs, out_shape, scratch = [s for s, _ in outs], [t for _, t in outs], []
    if rider is not None:
        assert rider["steps"] == M // tm
        kernel_body = functools.partial(_with_rider, functools.partial(body, fuse), len(args), len(outs), rider)
        in_specs, args = in_specs + rider["in_specs"], args + rider["args"]
        out_specs, out_shape, scratch = out_specs + rider["out_specs"], out_shape + rider["out_shape"], rider["scratch"]
    return pl.pallas_call(
        kernel_body,
        grid=(M // tm,),
        in_specs=in_specs,
        out_specs=out_specs,
        out_shape=out_shape,
        scratch_shapes=scratch,
        compiler_params=_params("arbitrary"),
        name="proj_" + layer,
    )(*args)


def _diff_prompt_kernel(lam_init, lamv_ref, q_ref, k_ref, vt_ref, g_ref, om_ref, x_ref, wout_ref,
                        lng_ref, lnb_ref, dng_ref, y_ref, m_ref, alpha_ref, acc_ref, s_ref, p_ref, mix_ref):
    t = DIFF_TILE
    nsub = q_ref.shape[0] // t
    LEAD = PIPE_LEAD
    i = pl.program_id(1)
    heads = [slice(h * DA_V, (h + 1) * DA_V) for h in range(DA_HEADS)]
    lane = lax.broadcasted_iota(jnp.int32, (t, DA_V), 1)
    units = []
    for sub in range(nsub):
        for hs in heads:
            qh = q_ref[sub * t:(sub + 1) * t, hs]
            zero = jnp.zeros_like(qh)
            units += [(sub, hs, jnp.where(lane < DA_QK, qh, zero)), (sub, hs, jnp.where(lane >= DA_QK, qh, zero))]
    n = len(units)
    per = n // nsub
    assert n == s_ref.shape[0] and LEAD + 1 < per
    m_ref[...] = jnp.full(m_ref.shape, -jnp.inf, F32)
    acc_ref[...] = jnp.zeros(acc_ref.shape, F32)

    def exp_stage(u):
        _stage_exp(s_ref.at[u], p_ref.at[u], m_ref.at[u])

    def pv_stage(j, u):
        _stage_pv(vt_ref[0, j, units[u][1], :], p_ref.at[u], alpha_ref.at[u], acc_ref.at[u])

    def block(j, mask_of, j_before, first_unit=0):
        rows = pl.ds(pl.multiple_of(j * t, t), t)
        for idx, u in enumerate(range(first_unit, n)):
            sub, hs, qq = units[u]
            _stage_scores(qq, k_ref[0, rows, hs], mask_of(sub), s_ref.at[u], m_ref.at[u], alpha_ref.at[u])
            if idx >= LEAD:
                exp_stage(u - LEAD)
            elif j_before is not None:
                exp_stage(n - LEAD + idx)
            if idx >= LEAD + 1:
                pv_stage(j, u - LEAD - 1)
            elif j_before is not None:
                pv_stage(j_before, n - LEAD - 1 + idx)

    kc = lax.broadcasted_iota(jnp.int32, (t, t), 0) // CHUNK
    qc = lax.broadcasted_iota(jnp.int32, (t, t), 1) // CHUNK
    diag_mask = kc <= qc
    diag0 = nsub * i
    for sd in range(nsub):
        block(diag0 + sd, lambda sub, sd=sd: diag_mask if sub == sd else None, None if sd == 0 else diag0 + sd - 1,
              first_unit=sd * per)

    def run(j0, count):
        block(j0, lambda sub: None, jnp.where(j0 == 0, diag0 + nsub - 1, j0 - 1))
        for d in range(1, count):
            block(j0 + d, lambda sub: None, j0 + d - 1)

    def trip(jj, carry):
        run(TRIP_BLOCKS * jj, TRIP_BLOCKS)
        return carry

    plain = nsub * i
    lax.fori_loop(0, plain // TRIP_BLOCKS, trip, 0)

    @pl.when(plain % TRIP_BLOCKS != 0)
    def _():
        run(plain // TRIP_BLOCKS * TRIP_BLOCKS, nsub)

    last = jnp.where(i == 0, diag0 + nsub - 1, diag0 - 1)
    for u in range(n - LEAD, n):
        exp_stage(u)
        pv_stage(last, u - 1)
    pv_stage(last, n - 1)

    lam = _lambda(lamv_ref[...], lam_init)
    for sub in range(nsub):
        rows = slice(sub * t, (sub + 1) * t)
        for h, hs in enumerate(heads):
            u = sub * per + 2 * h
            o1 = acc_ref[u, :DA_V, :] / acc_ref[u, DA_V:DA_V + 1, :]
            o2 = acc_ref[u + 1, :DA_V, :] / acc_ref[u + 1, DA_V:DA_V + 1, :]
            od = (o1 - lam * o2).T
            od = (od * lax.rsqrt(jnp.mean(od * od, axis=-1, keepdims=True) + LN_EPS) * dng_ref[...]
                  * (1.0 - lam_init))
            mix_ref[rows, hs] = (od * g_ref[rows, hs].astype(F32)).astype(BF16)
    mix_ref[:, DA_W:] = om_ref[...]
    y_ref[...] = _merge(x_ref[...], mix_ref[...], wout_ref[0], lng_ref[...], lnb_ref[...])


def _diff_prompt(lam_init, lamv, q, k, vt, g, om, x, w_out, layer, ln_g, ln_b, dng, *, B, S):
    t, rows = DIFF_TILE, DIFF_ROWS
    nq, units = S // rows, 2 * DA_HEADS * (rows // t)
    row = lambda w: pl.BlockSpec((rows, w), lambda b, i: (b * nq + i, 0))
    full = lambda shape: pl.BlockSpec(shape, lambda b, i: (0,) * len(shape))
    return pl.pallas_call(
        functools.partial(_diff_prompt_kernel, lam_init),
        grid=(B, nq),
        in_specs=[full((4, DA_QK)), row(DA_W), pl.BlockSpec((1, S, DA_W), lambda b, i: (b, 0, 0)),
                  pl.BlockSpec((1, S // t, DA_W, t), lambda b, i: (b, 0, 0, 0)), row(DA_W), row(MEM_W), row(D_MODEL),
                  _resident((1, MIX_W, D_MODEL), layer, 0, 0), full((1, D_MODEL)), full((1, D_MODEL)), full((1, DA_V))],
        out_specs=row(D_MODEL),
        out_shape=jax.ShapeDtypeStruct((B * S, D_MODEL), F32),
        scratch_shapes=[pltpu.VMEM((units, 1, t), F32), pltpu.VMEM((units, 1, t), F32),
                        pltpu.VMEM((units, DA_V + SUM_ROWS, t), F32),
                        pltpu.VMEM((units, t, t), F32), pltpu.VMEM((units, t, t), BF16),
                        pltpu.VMEM((rows, MIX_W), BF16)],
        compiler_params=_params("arbitrary", "arbitrary"),
        name="diff_prompt",
    )(lamv, q, k.reshape(B, S, DA_W), vt.reshape(B, S // t, DA_W, t), g, om, x, w_out, ln_g, ln_b, dng)


def _swa_prompt_kernel(tq, sinks_ref, q_ref, k_ref, kr_ref, vt_ref, vtr_ref, g_ref, om_ref, x_ref, wout_ref,
                       lng_ref, lnb_ref, y_ref, mix_ref):
    i = pl.program_id(1)
    w, nk, nsub = SWA_TILE, 2 * SWA_TILE, tq // SWA_TILE
    low = lax.broadcasted_iota(jnp.int32, (w, LANES), 1) < SW_HD
    row_low = lax.broadcasted_iota(jnp.int32, (LANES, w), 0) < SW_HD
    sink_row = lambda hs: jnp.concatenate([jnp.full((1, w), sinks_ref[h] * LOG2E, F32) for h in hs], axis=1)
    sinks_plain, sinks_swapped = sink_row((0, 2, 5, 7)), sink_row((1, 3, 4, 6))
    kc = lax.broadcasted_iota(jnp.int32, (nk, w), 0) // CHUNK
    qc = lax.broadcasted_iota(jnp.int32, (nk, w), 1) // CHUNK
    visible = lambda kch, qch, back: (kch - back <= qch) & (kch - back >= qch - WINDOW // CHUNK)
    qc1, full = lax.broadcasted_iota(jnp.int32, (CHUNK, w), 1) // CHUNK, w // CHUNK
    inner = [None if all(visible(kch, qch, full) for qch in range(full)) else visible(kch, qc1, full)
             for kch in range(nk // CHUNK)]
    firsts, scores = [], []

    def score(st):
        sub = i * nsub + st
        if st == 0:
            first = jnp.maximum(sub - 1, 0)
            masks = [visible(kc, qc, (sub - first) * full)]
        else:
            first, masks = sub - 1, inner
        rows = pl.ds(pl.multiple_of(first * w, w), nk)
        q = q_ref[st * w:(st + 1) * w, :]
        slabs = [q[:, p * LANES:(p + 1) * LANES] for p in range(SW_HEADS // 2)]
        even = [jnp.where(low, sl, jnp.zeros_like(sl)) for sl in slabs]
        odd = [jnp.where(low, jnp.zeros_like(sl), sl) for sl in slabs]
        firsts.append(first)
        scores.append((_window_scores(jnp.concatenate([even[0], even[1], odd[2], odd[3]], axis=0), k_ref[0, rows, :],
                                      masks),
                       _window_scores(jnp.concatenate([odd[0], odd[1], even[2], even[3]], axis=0), kr_ref[0, rows, :],
                                      masks)))
    def attend(st):
        first, (s_plain, s_swapped) = firsts[st], scores[st]
        vts = jnp.concatenate([vt_ref[0, first], vt_ref[0, first + 1]], axis=1)
        vtrs = jnp.concatenate([vtr_ref[0, first], vtr_ref[0, first + 1]], axis=1)
        o_plain = _window_attend(s_plain, vts, sinks_plain)
        o_swapped = _window_attend(s_swapped, vtrs, sinks_swapped)
        for p in range(SW_HEADS // 2):
            cols = slice(p * w, (p + 1) * w)
            lo, hi = (o_plain, o_swapped) if p < 2 else (o_swapped, o_plain)
            slab = jnp.where(row_low, lo[:, cols], hi[:, cols]).T
            gate = g_ref[st * w:(st + 1) * w, p * LANES:(p + 1) * LANES].astype(F32)
            mix_ref[st * w:(st + 1) * w, p * LANES:(p + 1) * LANES] = (slab * gate).astype(BF16)
    for st in range(min(SWA_LEAD, nsub)):
        score(st)
    for st in range(nsub):
        if st + SWA_LEAD < nsub:
            score(st + SWA_LEAD)
        attend(st)
    mix_ref[:, SW_W:] = om_ref[...]
    y_ref[...] = _merge(x_ref[...], mix_ref[...], wout_ref[0], lng_ref[...], lnb_ref[...])


def _swa_prompt(sinks, q, k, kr, vt, vtr, g, om, x, w_out, layer, ln_g, ln_b, *, B, S, tq):
    nq = S // tq
    row = lambda w: pl.BlockSpec((tq, w), lambda b, i: (b * nq + i, 0))
    full = lambda shape: pl.BlockSpec(shape, lambda b, i: (0,) * len(shape))
    seq = pl.BlockSpec((1, S, LANES), lambda b, i: (b, 0, 0))
    seqt = pl.BlockSpec((1, S // SWA_TILE, LANES, SWA_TILE), lambda b, i: (b, 0, 0, 0))
    r3 = lambda t: t.reshape(B, S, LANES)
    r4 = lambda t: t.reshape(B, S // SWA_TILE, LANES, SWA_TILE)
    return pl.pallas_call(
        functools.partial(_swa_prompt_kernel, tq),
        grid=(B, nq),
        in_specs=[pl.BlockSpec(memory_space=pltpu.SMEM), row(SW_W), seq, seq, seqt, seqt, row(SW_W), row(MEM_W),
                  row(D_MODEL), _resident((1, MIX_W, D_MODEL), layer, 0, 0), full((1, D_MODEL)), full((1, D_MODEL))],
        out_specs=row(D_MODEL),
        out_shape=jax.ShapeDtypeStruct((B * S, D_MODEL), F32),
        scratch_shapes=[pltpu.VMEM((tq, MIX_W), BF16)],
        compiler_params=_params("arbitrary", "arbitrary"),
        name="swa_prompt",
    )(sinks, q, r3(k), r3(kr), r4(vt), r4(vtr), g, om, x, w_out, ln_g, ln_b)


def _sample_tail(b, last, o_first, g_ref, mq_ref, gm_ref, cmk_ref, cmv_ref, x_ref, wout_ref, lng_ref, lnb_ref,
                 y_ref, mix_ref, T):
    om = _mem_attend(mq_ref[...], _head_rows(cmk_ref, MEM_HEADS, 0, 0), _head_rows(cmv_ref, MEM_HEADS, 0, 0))
    om = om * gm_ref[...].astype(F32)
    mix = jnp.concatenate([(o_first * g_ref[...].astype(F32)).astype(BF16), om.astype(BF16)], axis=1)
    mix_ref[pl.ds(pl.multiple_of(b * T, T), T), :] = mix

    @pl.when(last)
    def _():
        y_ref[...] = _merge(x_ref[...], mix_ref[...], wout_ref[0], lng_ref[...], lnb_ref[...])


def _diff_sample_body(lam_init, T, nb, nj, lamv_ref, q_ref, kn_ref, vn_ref, ckt_ref, cv_ref, g_ref, mq_ref, gm_ref,
                      cmk_ref, cmv_ref, x_ref, wout_ref, lng_ref, lnb_ref, dng_ref, y_ref,
                      m_ref, l_ref, acc_ref, mix_ref):
    b, j = pl.program_id(0) // nj, pl.program_id(0) % nj
    heads = [slice(h * DA_V, (h + 1) * DA_V) for h in range(DA_HEADS)]
    qqs = [_split_components(q_ref[:, hs]) for hs in heads]

    @pl.when(j == 0)
    def _():
        m_ref[...] = jnp.full(m_ref.shape, -jnp.inf, F32)
        l_ref[...] = jnp.zeros(l_ref.shape, F32)
        acc_ref[...] = jnp.zeros(acc_ref.shape, F32)
        _drain(_flash_stages(qqs, lambda h: kn_ref[:, heads[h]], lambda h: vn_ref[:, heads[h]], False, m_ref, l_ref,
                             acc_ref))

    yield "stage"
    for _ in _flash_stages(qqs, lambda h: ckt_ref[0, heads[h], :].astype(BF16), _head_rows(cv_ref, DA_HEADS, 0), True,
                           m_ref, l_ref, acc_ref):
        yield "stage"
    yield "tail"

    @pl.when(j == nj - 1)
    def _():
        lam = _lambda(lamv_ref[...], lam_init)
        rows = [slice(h * 2 * T, (h + 1) * 2 * T) for h in range(DA_HEADS)]
        od = jnp.concatenate([_diff_finish(acc_ref[r, :], l_ref[r, :], lam, dng_ref[...], lam_init) for r in rows],
                             axis=1)
        _sample_tail(b, b == nb - 1, od, g_ref, mq_ref, gm_ref, cmk_ref, cmv_ref, x_ref, wout_ref, lng_ref,
                     lnb_ref, y_ref, mix_ref, T)


def _diff_sample_rider(lam_init, lamv, q, kn, vn, ckt, cv, g, mq, gm, cmk, cmv, layer, x, w_out, ln_g, ln_b, dng, *,
                       NB, T, tk):
    nj = ckt.shape[2] // tk
    row = lambda w: pl.BlockSpec((T, w), lambda i: (i // nj, 0))
    small = lambda shape: _resident(shape)
    memc = pl.BlockSpec((1, 1, N_MEM * MEM_HEADS, MEM_HD), lambda i: (layer, i // nj, 0, 0))
    return dict(
        steps=NB * nj,
        in_specs=[small((4, DA_QK)), row(DA_W), row(DA_W), row(DA_W),
                  pl.BlockSpec((1, DA_W, tk), lambda i: (i // nj, 0, i % nj)),
                  pl.BlockSpec((1, tk * DA_HEADS, DA_V), lambda i: (i // nj, i % nj, 0)),
                  row(DA_W), row(MEM_W), row(MEM_W), memc, memc, small((NB * T, D_MODEL)),
                  _resident((1, MIX_W, D_MODEL), layer, 0, 0), small((1, D_MODEL)), small((1, D_MODEL)),
                  small((1, DA_V))],
        args=[lamv, q, kn, vn, ckt, cv, g, mq, gm, cmk, cmv, x, w_out, ln_g, ln_b, dng],
        out_specs=[pl.BlockSpec((NB * T, D_MODEL), lambda i: (0, 0))],
        out_shape=[jax.ShapeDtypeStruct((NB * T, D_MODEL), F32)],
        scratch=[pltpu.VMEM((DA_HEADS * 2 * T, 1), F32), pltpu.VMEM((DA_HEADS * 2 * T, 1), F32),
                 pltpu.VMEM((DA_HEADS * 2 * T, DA_V), F32), pltpu.VMEM((NB * T, MIX_W), BF16)],
        body=functools.partial(_diff_sample_body, lam_init, T, NB, nj),
    )


def _swa_sample_kernel(T, sinks_ref, q_ref, kn_ref, knr_ref, vn_ref, vnr_ref, ckt_ref, cvt_ref, g_ref, mq_ref, gm_ref,
                       cmk_ref, cmv_ref, x_ref, wout_ref, lng_ref, lnb_ref, y_ref, mix_ref):
    b = pl.program_id(0)
    swap = lambda t: jnp.concatenate([t[SW_HD:], t[:SW_HD]], axis=0).astype(BF16)
    ckt, cvt = ckt_ref[0], cvt_ref[0]
    kt, ktr, vt, vtr = ckt.astype(BF16), swap(ckt), cvt.astype(BF16), swap(cvt)
    kn, knr, vn, vnr = kn_ref[...], knr_ref[...], vn_ref[...], vnr_ref[...]
    attend = lambda qq, swapped, sink: _sink_attend_cached(
        qq, ktr if swapped else kt, knr if swapped else kn, vtr if swapped else vt, vnr if swapped else vn, sink)
    o = _swa_heads(q_ref[...], attend, lambda h: sinks_ref[h] * LOG2E)
    _sample_tail(b, b == pl.num_programs(0) - 1, o, g_ref, mq_ref, gm_ref, cmk_ref, cmv_ref, x_ref, wout_ref,
                 lng_ref, lnb_ref, y_ref, mix_ref, T)


def _swa_sample(sinks, q, kn, knr, vn, vnr, ckt, cvt, g, mq, gm, cmk, cmv, layer, x, w_out, ln_g, ln_b, *, NB, T):
    W = ckt.shape[2]
    row = lambda w: pl.BlockSpec((T, w), lambda b: (b, 0))
    full = lambda shape: pl.BlockSpec(shape, lambda b: (0,) * len(shape))
    cache = pl.BlockSpec((1, LANES, W), lambda b: (b, 0, 0))
    memc = pl.BlockSpec((1, 1, N_MEM * MEM_HEADS, MEM_HD), lambda b: (layer, b, 0, 0))
    return pl.pallas_call(
        functools.partial(_swa_sample_kernel, T),
        grid=(NB,),
        in_specs=[pl.BlockSpec(memory_space=pltpu.SMEM), row(SW_W), row(LANES), row(LANES), row(LANES), row(LANES),
                  cache, cache, row(SW_W), row(MEM_W), row(MEM_W), memc, memc, full((NB * T, D_MODEL)),
                  _resident((1, MIX_W, D_MODEL), layer, 0, 0), full((1, D_MODEL)), full((1, D_MODEL))],
        out_specs=full((NB * T, D_MODEL)),
        out_shape=jax.ShapeDtypeStruct((NB * T, D_MODEL), F32),
        scratch_shapes=[pltpu.VMEM((NB * T, MIX_W), BF16)],
        compiler_params=_params("arbitrary"),
        name="swa_sample",
    )(sinks, q, kn, knr, vn, vnr, ckt, cvt, g, mq, gm, cmk, cmv, x, w_out, ln_g, ln_b)


def _rope_tables(pos):
    T = pos.shape[0]
    inv = ROPE_THETA ** (-np.arange(ROPE_HALF, dtype=np.float64) / ROPE_HALF)
    ang = pos.astype(np.float64)[:, None] * inv[None, :]
    cos, sin = np.cos(ang), np.sin(ang)
    rest = SW_HD - 2 * ROPE_HALF
    c = np.concatenate([cos, cos, np.ones((T, rest))], axis=1)
    a = np.concatenate([-sin, np.zeros((T, SW_HD - ROPE_HALF))], axis=1)
    b = np.concatenate([np.zeros((T, ROPE_HALF)), sin, np.zeros((T, rest))], axis=1)
    return tuple(jnp.asarray(np.tile(t, (1, LANES // SW_HD)), F32) for t in (c, a, b))


def _feature_major(t):
    n = t.ndim
    return jnp.transpose(t, (0,) + tuple(range(2, n)) + (1,)).reshape(t.shape[0], -1, t.shape[1])


def _rows_major(t, tail):
    n = len(tail)
    return jnp.transpose(t.reshape((t.shape[0],) + tail + (t.shape[2],)), (0, n + 1) + tuple(range(1, n + 1)))


def kernel(x_prompt, x_sample, mem_prompt, cache_diff_k, cache_diff_v, cache_swa_k, cache_swa_v, cache_mem_k,
           cache_mem_v, w_in_a, lam_q1, lam_k1, lam_q2, lam_k2, diff_norm_g, w_in_b, sinks, w_kv_shared, w_mem_kv,
           w_out, ln_g, ln_b):
    B, S, _ = x_prompt.shape
    NB, T, _ = x_sample.shape
    P = cache_diff_k.shape[2]
    assert w_in_a.shape[0] == 1 and w_in_b.shape[0] == 1 and w_out.shape[0] == DEPTH

    tabs_p = _rope_tables(np.arange(S))
    tabs_s = _rope_tables(np.tile(P + np.arange(T), NB))
    xp = x_prompt.reshape(B * S, D_MODEL)
    xs = x_sample.reshape(NB * T, D_MODEL)
    wa, wb, wo = (w_in_a[0],), (w_in_b[0], w_kv_shared), w_out
    lng = ln_g.reshape(DEPTH, 1, D_MODEL)
    lnb = ln_b.reshape(DEPTH, 1, D_MODEL)
    lamv = jnp.concatenate([lam_q1, lam_k1, lam_q2, lam_k2], axis=0)
    dng = diff_norm_g.reshape(1, DA_V)
    lam_init = 0.8 - 0.6 * math.exp(-0.3 * 0)

    mkf, mvf, mkb, mvb = _mem_kv(mem_prompt.reshape(B * N_MEM, D_MODEL), w_mem_kv)
    mem_b = lambda l: (mkb[l].reshape(B, N_MEM, MEM_W), mvb[l].reshape(B, N_MEM, MEM_W))
    head_rows = lambda t: t.reshape(t.shape[:-3] + (t.shape[-3] * t.shape[-2], t.shape[-1]))
    cmk, cmv = head_rows(cache_mem_k), head_rows(cache_mem_v)

    q, kt, vf, kb, vb, g, om = _project("a", xp, wa, tabs_p, mem_b(0), tm=PROJ_ROWS, rows_per_batch=S)
    xp1 = _diff_prompt(lam_init, lamv, q, kb, vb, g, om, xp, wo, 0, lng[0], lnb[0], dng, B=B, S=S)
    qs, kfs, vfs, kbs, vbs, gs_, mqs, gms = _project("a", xs, wa, tabs_s, tm=NB * T)
    sample_a = _diff_sample_rider(lam_init, lamv, qs, kbs, vbs, _feature_major(cache_diff_k[0]),
                                  head_rows(cache_diff_v[0]), gs_, mqs, gms, cmk, cmv, 0, xs, wo, lng[0], lnb[0], dng,
                                  NB=NB, T=T, tk=SAMPLE_KEYS)

    q, g, om, skf, svf, kb1, kr1, vb1, vr1, xs1 = _project("b", xp1, wb, tabs_p, mem_b(1), tm=PROJ_B_ROWS,
                                                          rows_per_batch=S, rider=sample_a)
    yp = _swa_prompt(sinks[0], q, kb1, kr1, vb1, vr1, g, om, xp1, wo, 1, lng[1], lnb[1], B=B, S=S, tq=SWA_ROWS)
    qs, gs_, mqs, gms, skfs, svfs, kbs1, krs1, vbs1, vrs1 = _project("b", xs1, wb, tabs_s, tm=NB * T)
    ys = _swa_sample(sinks[0], qs, kbs1, krs1, vbs1, vrs1, _feature_major(cache_swa_k), _feature_major(cache_swa_v),
                     gs_, mqs, gms, cmk, cmv, 1, xs1, wo, lng[1], lnb[1], NB=NB, T=T)

    wr_p = min(WINDOW, S)
    window = lambda t: t.reshape(B, S, LANES)[:, S - wr_p:, :].reshape(B, wr_p, SW_KV, SW_HD)
    swa_kp, swa_vp = window(skf), window(svf)
    roll = lambda c, n: jnp.concatenate([c, n.reshape(NB, T, SW_KV, SW_HD)], axis=1)[:, T:]
    return (yp.reshape(B, S, D_MODEL), ys.reshape(NB, T, D_MODEL),
            _rows_major(kt, (DA_HEADS, 2, DA_QK))[None], vf.reshape(1, B, S, DA_HEADS, DA_V),
            kfs.reshape(1, NB, T, DA_HEADS, 2, DA_QK), vfs.reshape(1, NB, T, DA_HEADS, DA_V),
            swa_kp, swa_vp, roll(cache_swa_k, skfs), roll(cache_swa_v, svfs),
            mkf.reshape(DEPTH, B, N_MEM, MEM_HEADS, MEM_HD), mvf.reshape(DEPTH, B, N_MEM, MEM_HEADS, MEM_HD))
```

```python
import functools
import inspect
import math

import jax
import jax.numpy as jnp
import numpy as np
from jax import lax
from jax.experimental import pallas as pl
from jax.experimental.pallas import tpu as pltpu

D_MODEL = 1024
CHUNK = 64
N_MEM = 256
DA_HEADS = 4
DA_QK = 64
DA_V = 128
DA_W = 512
SW_HEADS = 8
SW_KV = 2
SW_HD = 64
SW_W = 512
WINDOW = 128
MEM_HEADS = 4
MEM_HD = 128
MEM_W = 512
MIX_W = 1024
ROPE_THETA = 500000.0
ROPE_HALF = 8
DEPTH = 2
DN_ALPHA = (2 * DEPTH) ** 0.25
LN_EPS = 1e-5
NEG = -1e30
LANES = 128
DIFF_TILE = 256
PIPE_LEAD = 4
DIFF_ROWS = 512
TRIP_BLOCKS = 4
SWA_LEAD = 1
SWA_TILE = 128
SUM_ROWS = 16
PROJ_ROWS = 1024
PROJ_B_ROWS = 512
MERGE_ROWS = 256
SWA_ROWS = 1024
SAMPLE_KEYS = 2048
LOG2E = math.log2(math.e)

F32 = jnp.float32
BF16 = jnp.bfloat16
VMEM_LIMIT = 50 * 1024 * 1024

_NT = (((1,), (1,)), ((), ()))


def _resident(shape, *first):
    idx = first or (0,) * len(shape)
    return pl.BlockSpec(shape, lambda *_: idx, pipeline_mode=pl.Buffered(1))


def _params(*sem):
    return pltpu.CompilerParams(dimension_semantics=sem, vmem_limit_bytes=VMEM_LIMIT)


def _rope(x, c, a, b):
    outs = []
    for i in range(x.shape[1] // LANES):
        blk = x[:, i * LANES:(i + 1) * LANES]
        outs.append(blk * c + pltpu.roll(blk, LANES - ROPE_HALF, 1) * a + pltpu.roll(blk, ROPE_HALF, 1) * b)
    return outs[0] if len(outs) == 1 else jnp.concatenate(outs, axis=1)


def _silu(g):
    return g * (1.0 / (1.0 + jnp.exp(-g)))


def _layer_norm(z, g, b):
    mu = jnp.mean(z, axis=-1, keepdims=True)
    d = z - mu
    var = jnp.mean(d * d, axis=-1, keepdims=True)
    return d * lax.rsqrt(var + LN_EPS) * g + b


def _mem_attend(mq, mk_of, mv_of):
    outs = []
    for h in range(MEM_HEADS):
        sl = slice(h * MEM_HD, (h + 1) * MEM_HD)
        s = lax.dot_general(mq[:, sl], mk_of(h), _NT, preferred_element_type=F32) * (MEM_HD ** -0.5)
        p = jnp.exp(s - jnp.max(s, axis=1, keepdims=True))
        l = jnp.sum(p, axis=1, keepdims=True)
        outs.append(jnp.dot(p.astype(BF16), mv_of(h), preferred_element_type=F32) / l)
    return jnp.concatenate(outs, axis=1)


def _head_cols(ref):
    return lambda h: ref[0, :, h * MEM_HD:(h + 1) * MEM_HD]


def _head_idx(ref, heads, h, *lead):
    return lead + (pl.ds(h, ref.shape[-2] // heads, stride=heads), slice(None))


def _head_rows(ref, heads, *lead):
    return lambda h: ref[_head_idx(ref, heads, h, *lead)].astype(BF16)


def _merge(x, mix, w_out, ln_g, ln_b):
    w = w_out.astype(BF16)
    n = max(x.shape[0] // MERGE_ROWS, 1)
    r = x.shape[0] // n
    z = lambda c: DN_ALPHA * x[c * r:(c + 1) * r] + jnp.dot(mix[c * r:(c + 1) * r], w, preferred_element_type=F32)
    outs, prev = [], z(0)
    for c in range(1, n):
        nxt = z(c)
        outs.append(_layer_norm(prev, ln_g, ln_b))
        prev = nxt
    outs.append(_layer_norm(prev, ln_g, ln_b))
    return outs[0] if n == 1 else jnp.concatenate(outs, axis=0)


def _lambda(lamv, lam_init):
    e1 = jnp.exp(jnp.sum(lamv[0:1, :] * lamv[1:2, :], axis=1, keepdims=True))
    e2 = jnp.exp(jnp.sum(lamv[2:3, :] * lamv[3:4, :], axis=1, keepdims=True))
    return e1 - e2 + lam_init


def _split_components(qh):
    lane = lax.broadcasted_iota(jnp.int32, qh.shape, 1)
    zero = jnp.zeros_like(qh)
    return jnp.concatenate([jnp.where(lane < DA_QK, qh, zero), jnp.where(lane >= DA_QK, qh, zero)], axis=0)


def _flash_stages(qqs, k_of, v_of, keys_on_lanes, m_ref, l_ref, acc_ref):
    if keys_on_lanes:
        s = [jnp.dot(qq, k_of(h), preferred_element_type=F32) for h, qq in enumerate(qqs)]
    else:
        s = [lax.dot_general(qq, k_of(h), _NT, preferred_element_type=F32) for h, qq in enumerate(qqs)]
    s = jnp.concatenate(s, axis=0)
    yield
    m_old = m_ref[...]
    m_new = jnp.maximum(m_old, jnp.max(s, axis=1, keepdims=True))
    alpha = jnp.exp2(m_old - m_new)
    p = jnp.exp2(s - m_new)
    l_ref[...] = alpha * l_ref[...] + jnp.sum(p, axis=1, keepdims=True)
    pb = p.astype(BF16)
    yield
    r = qqs[0].shape[0]
    pv = [jnp.dot(pb[h * r:(h + 1) * r], v_of(h), preferred_element_type=F32) for h in range(len(qqs))]
    acc_ref[...] = alpha * acc_ref[...] + jnp.concatenate(pv, axis=0)
    m_ref[...] = m_new


def _drain(stages):
    if inspect.isgenerator(stages):
        for _ in stages:
            pass


def _stage_scores(qq, ks, mask, s_ref, m_ref, alpha_ref):
    s = lax.dot_general(ks, qq, _NT, preferred_element_type=F32)
    if mask is not None:
        s = jnp.where(mask, s, NEG)
    s_ref[...] = s
    m_old = m_ref[...]
    m_new = jnp.maximum(m_old, jnp.max(s, axis=0, keepdims=True))
    alpha_ref[...] = jnp.exp2(m_old - m_new)
    m_ref[...] = m_new


def _stage_exp(s_ref, p_ref, m_ref):
    p_ref[...] = jnp.exp2(s_ref[...] - m_ref[...]).astype(BF16)


def _stage_pv(vts, p_ref, alpha_ref, acc_ref):
    ones = jnp.ones((SUM_ROWS, vts.shape[1]), BF16)
    pv = jnp.dot(jnp.concatenate([vts, ones], axis=0), p_ref[...], preferred_element_type=F32)
    acc_ref[...] = alpha_ref[...] * acc_ref[...] + pv


def _diff_finish(acc, l, lam, dng, lam_init):
    rows = acc.shape[0] // 2
    o = acc / l
    od = o[:rows] - lam * o[rows:]
    return od * lax.rsqrt(jnp.mean(od * od, axis=-1, keepdims=True) + LN_EPS) * dng * (1.0 - lam_init)


def _sink_attend_cached(qq, kt, kn, vt, vn, sink):
    s_c = jnp.dot(qq, kt, preferred_element_type=F32)
    s_n = lax.dot_general(qq, kn, _NT, preferred_element_type=F32)
    m = jnp.maximum(jnp.maximum(jnp.max(s_c, axis=1, keepdims=True), jnp.max(s_n, axis=1, keepdims=True)), sink)
    p_c, p_n = jnp.exp2(s_c - m), jnp.exp2(s_n - m)
    l = jnp.sum(p_c, axis=1, keepdims=True) + jnp.sum(p_n, axis=1, keepdims=True) + jnp.exp2(sink - m)
    o = (lax.dot_general(p_c.astype(BF16), vt, _NT, preferred_element_type=F32)
         + jnp.dot(p_n.astype(BF16), vn, preferred_element_type=F32))
    return o / l


def _window_scores(qq4, ks, masks):
    s = lax.dot_general(ks, qq4, _NT, preferred_element_type=F32)
    n, groups = s.shape[0] // len(masks), []
    for g, mask in enumerate(masks):
        sg = s[g * n:(g + 1) * n]
        if mask is not None:
            w = mask.shape[1]
            sg = jnp.concatenate([jnp.where(mask, sg[:, c * w:(c + 1) * w], NEG) for c in range(sg.shape[1] // w)],
                                 axis=1)
        groups.append(sg)
    return jnp.concatenate(groups, axis=0)


def _window_attend(s, vts, sinks4):
    m = jnp.maximum(jnp.max(s, axis=0, keepdims=True), sinks4)
    p = jnp.exp2(s - m).astype(BF16)
    ones = jnp.ones((SUM_ROWS, vts.shape[1]), BF16)
    o = jnp.dot(jnp.concatenate([vts, ones], axis=0), p, preferred_element_type=F32)
    return o[:LANES] / (o[LANES:LANES + 1] + jnp.exp2(sinks4 - m))


def _swa_heads(q, attend, sink_of):
    low = lax.broadcasted_iota(jnp.int32, (q.shape[0], LANES), 1) < SW_HD
    outs = []
    for pair in range(SW_HEADS // 2):
        kv = pair // 2
        slab = q[:, pair * LANES:(pair + 1) * LANES]
        zero = jnp.zeros_like(slab)
        o_even = attend(jnp.where(low, slab, zero), kv == 1, sink_of(2 * pair))
        o_odd = attend(jnp.where(low, zero, slab), kv == 0, sink_of(2 * pair + 1))
        outs.append(jnp.where(low, o_even, o_odd))
    return jnp.concatenate(outs, axis=1)


def _mem_kv_kernel(mem_ref, w_ref, kf_ref, vf_ref, kb_ref, vb_ref):
    h = jnp.dot(mem_ref[...].astype(BF16), w_ref[0].astype(BF16), preferred_element_type=F32)
    k, v = h[:, :MEM_W], h[:, MEM_W:]
    for hd in range(MEM_HEADS):
        kf_ref[_head_idx(kf_ref, MEM_HEADS, hd, 0)] = k[:, hd * MEM_HD:(hd + 1) * MEM_HD]
        vf_ref[_head_idx(vf_ref, MEM_HEADS, hd, 0)] = v[:, hd * MEM_HD:(hd + 1) * MEM_HD]
    kb_ref[0] = k.astype(BF16)
    vb_ref[0] = v.astype(BF16)


def _mem_kv(mem, w):
    rows = mem.shape[0]
    fshape, fspec = (DEPTH, rows * MEM_HEADS, MEM_HD), pl.BlockSpec((1, rows * MEM_HEADS, MEM_HD), lambda l: (l, 0, 0))
    bshape, bspec = (DEPTH, rows, MEM_W), pl.BlockSpec((1, rows, MEM_W), lambda l: (l, 0, 0))
    return pl.pallas_call(
        _mem_kv_kernel,
        grid=(DEPTH,),
        in_specs=[pl.BlockSpec((rows, D_MODEL), lambda l: (0, 0)),
                  pl.BlockSpec((1, D_MODEL, 2 * MEM_W), lambda l: (l, 0, 0))],
        out_specs=[fspec, fspec, bspec, bspec],
        out_shape=[jax.ShapeDtypeStruct(fshape, F32), jax.ShapeDtypeStruct(fshape, F32),
                   jax.ShapeDtypeStruct(bshape, BF16), jax.ShapeDtypeStruct(bshape, BF16)],
        compiler_params=_params("arbitrary"),
        name="mem_kv",
    )(mem, w)


def _proj_a_kernel(fuse_mem, x_ref, w_ref, c_ref, a_ref, b_ref, *refs):
    if fuse_mem:
        mk_ref, mv_ref, q_ref, kf_ref, vf_ref, kb_ref, vb_ref, g_ref, om_ref = refs
    else:
        q_ref, kf_ref, vf_ref, kb_ref, vb_ref, g_ref, mq_ref, gm_ref = refs
    x = x_ref[...].astype(BF16)
    c, a, b = c_ref[...], a_ref[...], b_ref[...]

    def cols(i):
        return jnp.dot(x, w_ref[:, i * DA_W:(i + 1) * DA_W].astype(BF16), preferred_element_type=F32)

    q_ref[...] = (_rope(cols(0), c, a, b) * (DA_QK ** -0.5 * LOG2E)).astype(BF16)
    k = _rope(cols(1), c, a, b)
    kb_ref[...] = k.astype(BF16)
    v = cols(2)
    for hd in range(DA_HEADS):
        vf_ref[_head_idx(vf_ref, DA_HEADS, hd)] = v[:, hd * DA_V:(hd + 1) * DA_V]
    if fuse_mem:
        kf_ref[0] = k.T
        for blk in range(vb_ref.shape[0]):
            vb_ref[blk] = v[blk * DIFF_TILE:(blk + 1) * DIFF_TILE, :].T.astype(BF16)
    else:
        kf_ref[...] = k
        vb_ref[...] = v.astype(BF16)
    g_ref[...] = _silu(cols(3)).astype(BF16)
    mq = cols(4).astype(BF16)
    gm = _silu(cols(5))
    if fuse_mem:
        om_ref[...] = (_mem_attend(mq, _head_cols(mk_ref), _head_cols(mv_ref)) * gm).astype(BF16)
    else:
        mq_ref[...] = mq
        gm_ref[...] = gm.astype(BF16)


def _proj_b_kernel(fuse_mem, x_ref, w_ref, wkv_ref, c_ref, a_ref, b_ref, *refs):
    if fuse_mem:
        mk_ref, mv_ref, q_ref, g_ref, om_ref, kf_ref, vf_ref, kb_ref, kr_ref, vb_ref, vr_ref = refs
    else:
        q_ref, g_ref, mq_ref, gm_ref, kf_ref, vf_ref, kb_ref, kr_ref, vb_ref, vr_ref = refs
    x = x_ref[...].astype(BF16)
    c, a, b = c_ref[...], a_ref[...], b_ref[...]

    def cols(i):
        return jnp.dot(x, w_ref[:, i * SW_W:(i + 1) * SW_W].astype(BF16), preferred_element_type=F32)

    q_ref[...] = (_rope(cols(0), c, a, b) * (SW_HD ** -0.5 * LOG2E)).astype(BF16)
    yield
    g_ref[...] = _silu(cols(1)).astype(BF16)
    yield
    mq = cols(2).astype(BF16)
    gm = _silu(cols(3))
    if fuse_mem:
        om_ref[...] = (_mem_attend(mq, _head_cols(mk_ref), _head_cols(mv_ref)) * gm).astype(BF16)
    else:
        mq_ref[...] = mq
        gm_ref[...] = gm.astype(BF16)
    yield
    kv = jnp.dot(x, wkv_ref[...].astype(BF16), preferred_element_type=F32)
    k = _rope(kv[:, :LANES], c, a, b)
    v = kv[:, LANES:]
    kf_ref[...] = k
    vf_ref[...] = v
    kb_ref[...] = k.astype(BF16)
    kr_ref[...] = pltpu.roll(k, SW_HD, 1).astype(BF16)
    vr = pltpu.roll(v, SW_HD, 1)
    if fuse_mem:
        for blk in range(vb_ref.shape[0]):
            rows = slice(blk * SWA_TILE, (blk + 1) * SWA_TILE)
            vb_ref[blk] = v[rows, :].T.astype(BF16)
            vr_ref[blk] = vr[rows, :].T.astype(BF16)
    else:
        vb_ref[...] = v.astype(BF16)
        vr_ref[...] = vr.astype(BF16)


def _with_rider(host_body, n_in, n_out, rider, *refs):
    r_in, r_out = len(rider["args"]), len(rider["out_specs"])
    o0 = n_in + r_in
    ride = rider["body"](*refs[n_in:o0], *refs[o0 + n_out:])
    mark = next(ride, None)
    host = host_body(*refs[:n_in], *refs[o0:o0 + n_out])
    if inspect.isgenerator(host):
        for _ in host:
            if mark == "stage":
                mark = next(ride, None)
    _drain(ride)


def _project(layer, x, ws, tabs, mem=None, *, tm, rows_per_batch=None, rider=None):
    M = x.shape[0]
    fuse = mem is not None
    per = (rows_per_batch or M) // tm
    row = lambda w: pl.BlockSpec((tm, w), lambda i: (i, 0))
    tab = pl.BlockSpec((tm, LANES), lambda i: (i % per, 0))
    in_specs = [row(D_MODEL)] + [_resident(w.shape) for w in ws] + [tab, tab, tab]
    args = [x, *ws, *tabs]
    if fuse:
        mspec = pl.BlockSpec((1, N_MEM, MEM_W), lambda i: (i // per, 0, 0))
        in_specs += [mspec, mspec]
        args += list(mem)
    o = lambda w, dt: (row(w), jax.ShapeDtypeStruct((M, w), dt))
    if layer == "a":
        vt = (pl.BlockSpec((tm // DIFF_TILE, DA_W, DIFF_TILE), lambda i: (i, 0, 0)),
              jax.ShapeDtypeStruct((M // DIFF_TILE, DA_W, DIFF_TILE), BF16))
        kt = (pl.BlockSpec((1, DA_W, tm), lambda i: (i // per, 0, i % per)),
              jax.ShapeDtypeStruct((M // (per * tm), DA_W, per * tm), F32))
        vf = (pl.BlockSpec((tm * DA_HEADS, DA_V), lambda i: (i, 0)), jax.ShapeDtypeStruct((M * DA_HEADS, DA_V), F32))
        outs = [o(DA_W, BF16), kt if fuse else o(DA_W, F32), vf, o(DA_W, BF16), vt if fuse else o(DA_W, BF16),
                o(DA_W, BF16)]
        body = _proj_a_kernel
    else:
        outs = [o(SW_W, BF16), o(SW_W, BF16)]
        body = _proj_b_kernel
    mem_outs = [o(MEM_W, BF16)] if fuse else [o(MEM_W, BF16), o(MEM_W, BF16)]
    if layer == "a":
        outs = outs + mem_outs
    else:
        vt = (pl.BlockSpec((tm // SWA_TILE, LANES, SWA_TILE), lambda i: (i, 0, 0)),
              jax.ShapeDtypeStruct((M // SWA_TILE, LANES, SWA_TILE), BF16))
        values = [vt, vt] if fuse else [o(LANES, BF16)] * 2
        outs = outs + mem_outs + [o(LANES, F32), o(LANES, F32)] + [o(LANES, BF16)] * 2 + values
    kernel_body = functools.partial(lambda run, *refs: _drain(run(*refs)), functools.partial(body, fuse))
    out_specs, out_shape, scratch = [s for s, _ in outs], [t for _, t in outs], []
    if rider is not None:
        assert rider["steps"] == M // tm
        kernel_body = functools.partial(_with_rider, functools.partial(body, fuse), len(args), len(outs), rider)
        in_specs, args = in_specs + rider["in_specs"], args + rider["args"]
        out_specs, out_shape, scratch = out_specs + rider["out_specs"], out_shape + rider["out_shape"], rider["scratch"]
    return pl.pallas_call(
        kernel_body,
        grid=(M // tm,),
        in_specs=in_specs,
        out_specs=out_specs,
        out_shape=out_shape,
        scratch_shapes=scratch,
        compiler_params=_params("arbitrary"),
        name="proj_" + layer,
    )(*args)


def _diff_prompt_kernel(lam_init, lamv_ref, q_ref, k_ref, vt_ref, g_ref, om_ref, x_ref, wout_ref,
                        lng_ref, lnb_ref, dng_ref, y_ref, m_ref, alpha_ref, acc_ref, s_ref, p_ref, mix_ref):
    t = DIFF_TILE
    nsub = q_ref.shape[0] // t
    LEAD = PIPE_LEAD
    i = pl.program_id(1)
    heads = [slice(h * DA_V, (h + 1) * DA_V) for h in range(DA_HEADS)]
    lane = lax.broadcasted_iota(jnp.int32, (t, DA_V), 1)
    units = []
    for sub in range(nsub):
        for hs in heads:
            qh = q_ref[sub * t:(sub + 1) * t, hs]
            zero = jnp.zeros_like(qh)
            units += [(sub, hs, jnp.where(lane < DA_QK, qh, zero)), (sub, hs, jnp.where(lane >= DA_QK, qh, zero))]
    n = len(units)
    per = n // nsub
    assert n == s_ref.shape[0] and LEAD + 1 < per
    m_ref[...] = jnp.full(m_ref.shape, -jnp.inf, F32)
    acc_ref[...] = jnp.zeros(acc_ref.shape, F32)

    def exp_stage(u):
        _stage_exp(s_ref.at[u], p_ref.at[u], m_ref.at[u])

    def pv_stage(j, u):
        _stage_pv(vt_ref[0, j, units[u][1], :], p_ref.at[u], alpha_ref.at[u], acc_ref.at[u])

    def block(j, mask_of, j_before, first_unit=0):
        rows = pl.ds(pl.multiple_of(j * t, t), t)
        for idx, u in enumerate(range(first_unit, n)):
            sub, hs, qq = units[u]
            _stage_scores(qq, k_ref[0, rows, hs], mask_of(sub), s_ref.at[u], m_ref.at[u], alpha_ref.at[u])
            if idx >= LEAD:
                exp_stage(u - LEAD)
            elif j_before is not None:
                exp_stage(n - LEAD + idx)
            if idx >= LEAD + 1:
                pv_stage(j, u - LEAD - 1)
            elif j_before is not None:
                pv_stage(j_before, n - LEAD - 1 + idx)

    kc = lax.broadcasted_iota(jnp.int32, (t, t), 0) // CHUNK
    qc = lax.broadcasted_iota(jnp.int32, (t, t), 1) // CHUNK
    diag_mask = kc <= qc
    diag0 = nsub * i
    for sd in range(nsub):
        block(diag0 + sd, lambda sub, sd=sd: diag_mask if sub == sd else None, None if sd == 0 else diag0 + sd - 1,
              first_unit=sd * per)

    def run(j0, count):
        block(j0, lambda sub: None, jnp.where(j0 == 0, diag0 + nsub - 1, j0 - 1))
        for d in range(1, count):
            block(j0 + d, lambda sub: None, j0 + d - 1)

    def trip(jj, carry):
        run(TRIP_BLOCKS * jj, TRIP_BLOCKS)
        return carry

    plain = nsub * i
    lax.fori_loop(0, plain // TRIP_BLOCKS, trip, 0)

    @pl.when(plain % TRIP_BLOCKS != 0)
    def _():
        run(plain // TRIP_BLOCKS * TRIP_BLOCKS, nsub)

    last = jnp.where(i == 0, diag0 + nsub - 1, diag0 - 1)
    for u in range(n - LEAD, n):
        exp_stage(u)
        pv_stage(last, u - 1)
    pv_stage(last, n - 1)

    lam = _lambda(lamv_ref[...], lam_init)
    for sub in range(nsub):
        rows = slice(sub * t, (sub + 1) * t)
        for h, hs in enumerate(heads):
            u = sub * per + 2 * h
            o1 = acc_ref[u, :DA_V, :] / acc_ref[u, DA_V:DA_V + 1, :]
            o2 = acc_ref[u + 1, :DA_V, :] / acc_ref[u + 1, DA_V:DA_V + 1, :]
            od = (o1 - lam * o2).T
            od = (od * lax.rsqrt(jnp.mean(od * od, axis=-1, keepdims=True) + LN_EPS) * dng_ref[...]
                  * (1.0 - lam_init))
            mix_ref[rows, hs] = (od * g_ref[rows, hs].astype(F32)).astype(BF16)
    mix_ref[:, DA_W:] = om_ref[...]
    y_ref[...] = _merge(x_ref[...], mix_ref[...], wout_ref[0], lng_ref[...], lnb_ref[...])


def _diff_prompt(lam_init, lamv, q, k, vt, g, om, x, w_out, layer, ln_g, ln_b, dng, *, B, S):
    t, rows = DIFF_TILE, DIFF_ROWS
    nq, units = S // rows, 2 * DA_HEADS * (rows // t)
    row = lambda w: pl.BlockSpec((rows, w), lambda b, i: (b * nq + i, 0))
    full = lambda shape: pl.BlockSpec(shape, lambda b, i: (0,) * len(shape))
    return pl.pallas_call(
        functools.partial(_diff_prompt_kernel, lam_init),
        grid=(B, nq),
        in_specs=[full((4, DA_QK)), row(DA_W), pl.BlockSpec((1, S, DA_W), lambda b, i: (b, 0, 0)),
                  pl.BlockSpec((1, S // t, DA_W, t), lambda b, i: (b, 0, 0, 0)), row(DA_W), row(MEM_W), row(D_MODEL),
                  _resident((1, MIX_W, D_MODEL), layer, 0, 0), full((1, D_MODEL)), full((1, D_MODEL)), full((1, DA_V))],
        out_specs=row(D_MODEL),
        out_shape=jax.ShapeDtypeStruct((B * S, D_MODEL), F32),
        scratch_shapes=[pltpu.VMEM((units, 1, t), F32), pltpu.VMEM((units, 1, t), F32),
                        pltpu.VMEM((units, DA_V + SUM_ROWS, t), F32),
                        pltpu.VMEM((units, t, t), F32), pltpu.VMEM((units, t, t), BF16),
                        pltpu.VMEM((rows, MIX_W), BF16)],
        compiler_params=_params("arbitrary", "arbitrary"),
        name="diff_prompt",
    )(lamv, q, k.reshape(B, S, DA_W), vt.reshape(B, S // t, DA_W, t), g, om, x, w_out, ln_g, ln_b, dng)


def _swa_prompt_kernel(tq, sinks_ref, q_ref, k_ref, kr_ref, vt_ref, vtr_ref, g_ref, om_ref, x_ref, wout_ref,
                       lng_ref, lnb_ref, y_ref, mix_ref):
    i = pl.program_id(1)
    w, nk, nsub = SWA_TILE, 2 * SWA_TILE, tq // SWA_TILE
    low = lax.broadcasted_iota(jnp.int32, (w, LANES), 1) < SW_HD
    row_low = lax.broadcasted_iota(jnp.int32, (LANES, w), 0) < SW_HD
    sink_row = lambda hs: jnp.concatenate([jnp.full((1, w), sinks_ref[h] * LOG2E, F32) for h in hs], axis=1)
    sinks_plain, sinks_swapped = sink_row((0, 2, 5, 7)), sink_row((1, 3, 4, 6))
    kc = lax.broadcasted_iota(jnp.int32, (nk, w), 0) // CHUNK
    qc = lax.broadcasted_iota(jnp.int32, (nk, w), 1) // CHUNK
    visible = lambda kch, qch, back: (kch - back <= qch) & (kch - back >= qch - WINDOW // CHUNK)
    qc1, full = lax.broadcasted_iota(jnp.int32, (CHUNK, w), 1) // CHUNK, w // CHUNK
    inner = [None if all(visible(kch, qch, full) for qch in range(full)) else visible(kch, qc1, full)
             for kch in range(nk // CHUNK)]
    firsts, scores = [], []

    def score(st):
        sub = i * nsub + st
        if st == 0:
            first = jnp.maximum(sub - 1, 0)
            masks = [visible(kc, qc, (sub - first) * full)]
        else:
            first, masks = sub - 1, inner
        rows = pl.ds(pl.multiple_of(first * w, w), nk)
        q = q_ref[st * w:(st + 1) * w, :]
        slabs = [q[:, p * LANES:(p + 1) * LANES] for p in range(SW_HEADS // 2)]
        even = [jnp.where(low, sl, jnp.zeros_like(sl)) for sl in slabs]
        odd = [jnp.where(low, jnp.zeros_like(sl), sl) for sl in slabs]
        firsts.append(first)
        half = lambda x, y, ks: _window_scores(jnp.concatenate([x, y], axis=0), ks, masks)
        k, kr = k_ref[0, rows, :], kr_ref[0, rows, :]
        scores.append(((half(even[0], even[1], k), half(odd[2], odd[3], k)),
                       (half(odd[0], odd[1], kr), half(even[2], even[3], kr))))
    def attend(st):
        first, (s_plain, s_swapped) = firsts[st], scores[st]
        vts = jnp.concatenate([vt_ref[0, first], vt_ref[0, first + 1]], axis=1)
        vtrs = jnp.concatenate([vtr_ref[0, first], vtr_ref[0, first + 1]], axis=1)
        halves = lambda ss, v, sk: jnp.concatenate(
            [_window_attend(sh, v, sk[:, c * 2 * w:(c + 1) * 2 * w]) for c, sh in enumerate(ss)], axis=1)
        o_plain = halves(s_plain, vts, sinks_plain)
        o_swapped = halves(s_swapped, vtrs, sinks_swapped)
        for p in range(SW_HEADS // 2):
            cols = slice(p * w, (p + 1) * w)
            lo, hi = (o_plain, o_swapped) if p < 2 else (o_swapped, o_plain)
            slab = jnp.where(row_low, lo[:, cols], hi[:, cols]).T
            gate = g_ref[st * w:(st + 1) * w, p * LANES:(p + 1) * LANES].astype(F32)
            mix_ref[st * w:(st + 1) * w, p * LANES:(p + 1) * LANES] = (slab * gate).astype(BF16)
    for st in range(min(SWA_LEAD, nsub)):
        score(st)
    for st in range(nsub):
        if st + SWA_LEAD < nsub:
            score(st + SWA_LEAD)
        attend(st)
    mix_ref[:, SW_W:] = om_ref[...]
    y_ref[...] = _merge(x_ref[...], mix_ref[...], wout_ref[0], lng_ref[...], lnb_ref[...])


def _swa_prompt(sinks, q, k, kr, vt, vtr, g, om, x, w_out, layer, ln_g, ln_b, *, B, S, tq):
    nq = S // tq
    row = lambda w: pl.BlockSpec((tq, w), lambda b, i: (b * nq + i, 0))
    full = lambda shape: pl.BlockSpec(shape, lambda b, i: (0,) * len(shape))
    seq = pl.BlockSpec((1, S, LANES), lambda b, i: (b, 0, 0))
    seqt = pl.BlockSpec((1, S // SWA_TILE, LANES, SWA_TILE), lambda b, i: (b, 0, 0, 0))
    r3 = lambda t: t.reshape(B, S, LANES)
    r4 = lambda t: t.reshape(B, S // SWA_TILE, LANES, SWA_TILE)
    return pl.pallas_call(
        functools.partial(_swa_prompt_kernel, tq),
        grid=(B, nq),
        in_specs=[pl.BlockSpec(memory_space=pltpu.SMEM), row(SW_W), seq, seq, seqt, seqt, row(SW_W), row(MEM_W),
                  row(D_MODEL), _resident((1, MIX_W, D_MODEL), layer, 0, 0), full((1, D_MODEL)), full((1, D_MODEL))],
        out_specs=row(D_MODEL),
        out_shape=jax.ShapeDtypeStruct((B * S, D_MODEL), F32),
        scratch_shapes=[pltpu.VMEM((tq, MIX_W), BF16)],
        compiler_params=_params("arbitrary", "arbitrary"),
        name="swa_prompt",
    )(sinks, q, r3(k), r3(kr), r4(vt), r4(vtr), g, om, x, w_out, ln_g, ln_b)


def _sample_tail(b, last, o_first, g_ref, mq_ref, gm_ref, cmk_ref, cmv_ref, x_ref, wout_ref, lng_ref, lnb_ref,
                 y_ref, mix_ref, T):
    om = _mem_attend(mq_ref[...], _head_rows(cmk_ref, MEM_HEADS, 0, 0), _head_rows(cmv_ref, MEM_HEADS, 0, 0))
    om = om * gm_ref[...].astype(F32)
    mix = jnp.concatenate([(o_first * g_ref[...].astype(F32)).astype(BF16), om.astype(BF16)], axis=1)
    mix_ref[pl.ds(pl.multiple_of(b * T, T), T), :] = mix

    @pl.when(last)
    def _():
        y_ref[...] = _merge(x_ref[...], mix_ref[...], wout_ref[0], lng_ref[...], lnb_ref[...])


def _diff_sample_body(lam_init, T, nb, nj, lamv_ref, q_ref, kn_ref, vn_ref, ckt_ref, cv_ref, g_ref, mq_ref, gm_ref,
                      cmk_ref, cmv_ref, x_ref, wout_ref, lng_ref, lnb_ref, dng_ref, y_ref,
                      m_ref, l_ref, acc_ref, mix_ref):
    b, j = pl.program_id(0) // nj, pl.program_id(0) % nj
    heads = [slice(h * DA_V, (h + 1) * DA_V) for h in range(DA_HEADS)]
    qqs = [_split_components(q_ref[:, hs]) for hs in heads]

    @pl.when(j == 0)
    def _():
        m_ref[...] = jnp.full(m_ref.shape, -jnp.inf, F32)
        l_ref[...] = jnp.zeros(l_ref.shape, F32)
        acc_ref[...] = jnp.zeros(acc_ref.shape, F32)
        _drain(_flash_stages(qqs, lambda h: kn_ref[:, heads[h]], lambda h: vn_ref[:, heads[h]], False, m_ref, l_ref,
                             acc_ref))

    yield "stage"
    for _ in _flash_stages(qqs, lambda h: ckt_ref[0, heads[h], :].astype(BF16), _head_rows(cv_ref, DA_HEADS, 0), True,
                           m_ref, l_ref, acc_ref):
        yield "stage"
    yield "tail"

    @pl.when(j == nj - 1)
    def _():
        lam = _lambda(lamv_ref[...], lam_init)
        rows = [slice(h * 2 * T, (h + 1) * 2 * T) for h in range(DA_HEADS)]
        od = jnp.concatenate([_diff_finish(acc_ref[r, :], l_ref[r, :], lam, dng_ref[...], lam_init) for r in rows],
                             axis=1)
        _sample_tail(b, b == nb - 1, od, g_ref, mq_ref, gm_ref, cmk_ref, cmv_ref, x_ref, wout_ref, lng_ref,
                     lnb_ref, y_ref, mix_ref, T)


def _diff_sample_rider(lam_init, lamv, q, kn, vn, ckt, cv, g, mq, gm, cmk, cmv, layer, x, w_out, ln_g, ln_b, dng, *,
                       NB, T, tk):
    nj = ckt.shape[2] // tk
    row = lambda w: pl.BlockSpec((T, w), lambda i: (i // nj, 0))
    small = lambda shape: _resident(shape)
    memc = pl.BlockSpec((1, 1, N_MEM * MEM_HEADS, MEM_HD), lambda i: (layer, i // nj, 0, 0))
    return dict(
        steps=NB * nj,
        in_specs=[small((4, DA_QK)), row(DA_W), row(DA_W), row(DA_W),
                  pl.BlockSpec((1, DA_W, tk), lambda i: (i // nj, 0, i % nj)),
                  pl.BlockSpec((1, tk * DA_HEADS, DA_V), lambda i: (i // nj, i % nj, 0)),
                  row(DA_W), row(MEM_W), row(MEM_W), memc, memc, small((NB * T, D_MODEL)),
                  _resident((1, MIX_W, D_MODEL), layer, 0, 0), small((1, D_MODEL)), small((1, D_MODEL)),
                  small((1, DA_V))],
        args=[lamv, q, kn, vn, ckt, cv, g, mq, gm, cmk, cmv, x, w_out, ln_g, ln_b, dng],
        out_specs=[pl.BlockSpec((NB * T, D_MODEL), lambda i: (0, 0))],
        out_shape=[jax.ShapeDtypeStruct((NB * T, D_MODEL), F32)],
        scratch=[pltpu.VMEM((DA_HEADS * 2 * T, 1), F32), pltpu.VMEM((DA_HEADS * 2 * T, 1), F32),
                 pltpu.VMEM((DA_HEADS * 2 * T, DA_V), F32), pltpu.VMEM((NB * T, MIX_W), BF16)],
        body=functools.partial(_diff_sample_body, lam_init, T, NB, nj),
    )


def _swa_sample_kernel(T, sinks_ref, q_ref, kn_ref, knr_ref, vn_ref, vnr_ref, ckt_ref, cvt_ref, g_ref, mq_ref, gm_ref,
                       cmk_ref, cmv_ref, x_ref, wout_ref, lng_ref, lnb_ref, y_ref, mix_ref):
    b = pl.program_id(0)
    swap = lambda t: jnp.concatenate([t[SW_HD:], t[:SW_HD]], axis=0).astype(BF16)
    ckt, cvt = ckt_ref[0], cvt_ref[0]
    kt, ktr, vt, vtr = ckt.astype(BF16), swap(ckt), cvt.astype(BF16), swap(cvt)
    kn, knr, vn, vnr = kn_ref[...], knr_ref[...], vn_ref[...], vnr_ref[...]
    attend = lambda qq, swapped, sink: _sink_attend_cached(
        qq, ktr if swapped else kt, knr if swapped else kn, vtr if swapped else vt, vnr if swapped else vn, sink)
    o = _swa_heads(q_ref[...], attend, lambda h: sinks_ref[h] * LOG2E)
    _sample_tail(b, b == pl.num_programs(0) - 1, o, g_ref, mq_ref, gm_ref, cmk_ref, cmv_ref, x_ref, wout_ref,
                 lng_ref, lnb_ref, y_ref, mix_ref, T)


def _swa_sample(sinks, q, kn, knr, vn, vnr, ckt, cvt, g, mq, gm, cmk, cmv, layer, x, w_out, ln_g, ln_b, *, NB, T):
    W = ckt.shape[2]
    row = lambda w: pl.BlockSpec((T, w), lambda b: (b, 0))
    full = lambda shape: pl.BlockSpec(shape, lambda b: (0,) * len(shape))
    cache = pl.BlockSpec((1, LANES, W), lambda b: (b, 0, 0))
    memc = pl.BlockSpec((1, 1, N_MEM * MEM_HEADS, MEM_HD), lambda b: (layer, b, 0, 0))
    return pl.pallas_call(
        functools.partial(_swa_sample_kernel, T),
        grid=(NB,),
        in_specs=[pl.BlockSpec(memory_space=pltpu.SMEM), row(SW_W), row(LANES), row(LANES), row(LANES), row(LANES),
                  cache, cache, row(SW_W), row(MEM_W), row(MEM_W), memc, memc, full((NB * T, D_MODEL)),
                  _resident((1, MIX_W, D_MODEL), layer, 0, 0), full((1, D_MODEL)), full((1, D_MODEL))],
        out_specs=full((NB * T, D_MODEL)),
        out_shape=jax.ShapeDtypeStruct((NB * T, D_MODEL), F32),
        scratch_shapes=[pltpu.VMEM((NB * T, MIX_W), BF16)],
        compiler_params=_params("arbitrary"),
        name="swa_sample",
    )(sinks, q, kn, knr, vn, vnr, ckt, cvt, g, mq, gm, cmk, cmv, x, w_out, ln_g, ln_b)


def _rope_tables(pos):
    T = pos.shape[0]
    inv = ROPE_THETA ** (-np.arange(ROPE_HALF, dtype=np.float64) / ROPE_HALF)
    ang = pos.astype(np.float64)[:, None] * inv[None, :]
    cos, sin = np.cos(ang), np.sin(ang)
    rest = SW_HD - 2 * ROPE_HALF
    c = np.concatenate([cos, cos, np.ones((T, rest))], axis=1)
    a = np.concatenate([-sin, np.zeros((T, SW_HD - ROPE_HALF))], axis=1)
    b = np.concatenate([np.zeros((T, ROPE_HALF)), sin, np.zeros((T, rest))], axis=1)
    return tuple(jnp.asarray(np.tile(t, (1, LANES // SW_HD)), F32) for t in (c, a, b))


def _feature_major(t):
    n = t.ndim
    return jnp.transpose(t, (0,) + tuple(range(2, n)) + (1,)).reshape(t.shape[0], -1, t.shape[1])


def _rows_major(t, tail):
    n = len(tail)
    return jnp.transpose(t.reshape((t.shape[0],) + tail + (t.shape[2],)), (0, n + 1) + tuple(range(1, n + 1)))


def kernel(x_prompt, x_sample, mem_prompt, cache_diff_k, cache_diff_v, cache_swa_k, cache_swa_v, cache_mem_k,
           cache_mem_v, w_in_a, lam_q1, lam_k1, lam_q2, lam_k2, diff_norm_g, w_in_b, sinks, w_kv_shared, w_mem_kv,
           w_out, ln_g, ln_b):
    B, S, _ = x_prompt.shape
    NB, T, _ = x_sample.shape
    P = cache_diff_k.shape[2]
    assert w_in_a.shape[0] == 1 and w_in_b.shape[0] == 1 and w_out.shape[0] == DEPTH

    tabs_p = _rope_tables(np.arange(S))
    tabs_s = _rope_tables(np.tile(P + np.arange(T), NB))
    xp = x_prompt.reshape(B * S, D_MODEL)
    xs = x_sample.reshape(NB * T, D_MODEL)
    wa, wb, wo = (w_in_a[0],), (w_in_b[0], w_kv_shared), w_out
    lng = ln_g.reshape(DEPTH, 1, D_MODEL)
    lnb = ln_b.reshape(DEPTH, 1, D_MODEL)
    lamv = jnp.concatenate([lam_q1, lam_k1, lam_q2, lam_k2], axis=0)
    dng = diff_norm_g.reshape(1, DA_V)
    lam_init = 0.8 - 0.6 * math.exp(-0.3 * 0)

    mkf, mvf, mkb, mvb = _mem_kv(mem_prompt.reshape(B * N_MEM, D_MODEL), w_mem_kv)
    mem_b = lambda l: (mkb[l].reshape(B, N_MEM, MEM_W), mvb[l].reshape(B, N_MEM, MEM_W))
    head_rows = lambda t: t.reshape(t.shape[:-3] + (t.shape[-3] * t.shape[-2], t.shape[-1]))
    cmk, cmv = head_rows(cache_mem_k), head_rows(cache_mem_v)

    q, kt, vf, kb, vb, g, om = _project("a", xp, wa, tabs_p, mem_b(0), tm=PROJ_ROWS, rows_per_batch=S)
    xp1 = _diff_prompt(lam_init, lamv, q, kb, vb, g, om, xp, wo, 0, lng[0], lnb[0], dng, B=B, S=S)
    qs, kfs, vfs, kbs, vbs, gs_, mqs, gms = _project("a", xs, wa, tabs_s, tm=NB * T)
    sample_a = _diff_sample_rider(lam_init, lamv, qs, kbs, vbs, _feature_major(cache_diff_k[0]),
                                  head_rows(cache_diff_v[0]), gs_, mqs, gms, cmk, cmv, 0, xs, wo, lng[0], lnb[0], dng,
                                  NB=NB, T=T, tk=SAMPLE_KEYS)

    q, g, om, skf, svf, kb1, kr1, vb1, vr1, xs1 = _project("b", xp1, wb, tabs_p, mem_b(1), tm=PROJ_B_ROWS,
                                                          rows_per_batch=S, rider=sample_a)
    yp = _swa_prompt(sinks[0], q, kb1, kr1, vb1, vr1, g, om, xp1, wo, 1, lng[1], lnb[1], B=B, S=S, tq=SWA_ROWS)
    qs, gs_, mqs, gms, skfs, svfs, kbs1, krs1, vbs1, vrs1 = _project("b", xs1, wb, tabs_s, tm=NB * T)
    ys = _swa_sample(sinks[0], qs, kbs1, krs1, vbs1, vrs1, _feature_major(cache_swa_k), _feature_major(cache_swa_v),
                     gs_, mqs, gms, cmk, cmv, 1, xs1, wo, lng[1], lnb[1], NB=NB, T=T)

    wr_p = min(WINDOW, S)
    window = lambda t: t.reshape(B, S, LANES)[:, S - wr_p:, :].reshape(B, wr_p, SW_KV, SW_HD)
    swa_kp, swa_vp = window(skf), window(svf)
    roll = lambda c, n: jnp.concatenate([c, n.reshape(NB, T, SW_KV, SW_HD)], axis=1)[:, T:]
    return (yp.reshape(B, S, D_MODEL), ys.reshape(NB, T, D_MODEL),
            _rows_major(kt, (DA_HEADS, 2, DA_QK))[None], vf.reshape(1, B, S, DA_HEADS, DA_V),
            kfs.reshape(1, NB, T, DA_HEADS, 2, DA_QK), vfs.reshape(1, NB, T, DA_HEADS, DA_V),
            swa_kp, swa_vp, roll(cache_swa_k, skfs), roll(cache_swa_v, svfs),
            mkf.reshape(DEPTH, B, N_MEM, MEM_HEADS, MEM_HD), mvf.reshape(DEPTH, B, N_MEM, MEM_HEADS, MEM_HD))
```
